```python
import jax, jax.numpy as jnp
from jax import lax
import numpy as np

D_MODEL = 1024
BATCH = 32
SEQ = 256
DEPTH = 2
DEC_BATCH = 2
DEC_SEQ = 1024
PAST_LEN = 256

GRID_W = 64
HEAD_DIM = 64
MLA_HEADS = 8
MLA_Q_LORA = 384
MLA_KV_LORA = 256
MLA_NOPE = 64
MLA_ROPE = 32
MLA_V = 64
SWA_HEADS = 8
SWA_KV_HEADS = 2
SWA_GROUP = SWA_HEADS // SWA_KV_HEADS
SWA_WINDOW = 128
SWA_BLOCK = 128
NA_HEADS = 16
NA_WIN_ROWS = 8
NA_WIN_COLS = 16
D_FF = 2816
N_EXPERTS = 8
TOP_K = 2
E_FF = 3584
Q_BLOCK = 128
ROPE_THETA = 10000.0
LN_EPS = 1e-5
RMS_EPS = 1e-6
NEG = -1e30
MIX_WIDTH_AB = MLA_HEADS * MLA_V + SWA_HEADS * HEAD_DIM
L0_IN = MLA_Q_LORA + MLA_KV_LORA + MLA_ROPE + SWA_HEADS * HEAD_DIM + 2 * SWA_KV_HEADS * HEAD_DIM
L0_SPLITS = [MLA_Q_LORA,
             MLA_Q_LORA + MLA_KV_LORA,
             MLA_Q_LORA + MLA_KV_LORA + MLA_ROPE,
             MLA_Q_LORA + MLA_KV_LORA + MLA_ROPE + SWA_HEADS * HEAD_DIM,
             MLA_Q_LORA + MLA_KV_LORA + MLA_ROPE + SWA_HEADS * HEAD_DIM + SWA_KV_HEADS * HEAD_DIM]
NA_WIDTH = NA_HEADS * HEAD_DIM

kernel_name = 'hybrid_dit_mla_swa_natten_moe_step'


def layer_norm(x, g, b):
    xf = x.astype(jnp.float32)
    mu = jnp.mean(xf, -1, keepdims=True)
    var = jnp.mean(jnp.square(xf - mu), -1, keepdims=True)
    y = (xf - mu) * lax.rsqrt(var + LN_EPS)
    return (y * g.astype(jnp.float32) + b.astype(jnp.float32)).astype(x.dtype)


def rms_norm(x, g):
    xf = x.astype(jnp.float32)
    y = xf * lax.rsqrt(jnp.mean(xf * xf, -1, keepdims=True) + RMS_EPS)
    return (y * g.astype(jnp.float32)).astype(x.dtype)


def _rotate(x, pos):
    nf = x.shape[-1] // 2
    inv = ROPE_THETA ** (-jnp.arange(nf, dtype=jnp.float32) / nf)
    ang = pos.astype(jnp.float32)[:, None] * inv[None, :]
    cos = jnp.cos(ang)[None, :, None, :].astype(x.dtype)
    sin = jnp.sin(ang)[None, :, None, :].astype(x.dtype)
    x1, x2 = x[..., :nf], x[..., nf:]
    return jnp.concatenate([x1 * cos - x2 * sin, x1 * sin + x2 * cos], -1)


def axial_rope(x):
    t = jnp.arange(x.shape[1])
    half = x.shape[-1] // 2
    return jnp.concatenate([_rotate(x[..., :half], t // GRID_W), _rotate(x[..., half:], t % GRID_W)], -1)


def adaln(cond, ada_w, ada_b):
    m = jnp.einsum('bd,de->be', jax.nn.silu(cond), ada_w) + ada_b
    s = jnp.split(m, 6, axis=-1)
    return (s[0][:, None], s[1][:, None], s[2][:, None], s[3][:, None], s[4][:, None], s[5][:, None])


def dense_attention(q, k, v, sink=None):
    b, lq, g, r, dk = q.shape
    nb = lq // Q_BLOCK
    scale = dk ** -0.5
    qb = q.reshape(b, nb, Q_BLOCK, g, r, dk).transpose(1, 0, 2, 3, 4, 5)

    def one_block(qblk):
        s = jnp.einsum('bqgrd,bkgd->bgrqk', qblk, k).astype(jnp.float32) * scale
        if sink is None:
            p = jax.nn.softmax(s, -1)
        else:
            sk = jnp.broadcast_to(sink.astype(jnp.float32)[None, :, :, None, None], s.shape[:-1] + (1,))
            p = jax.nn.softmax(jnp.concatenate([s, sk], -1), -1)[..., :-1]
        return jnp.einsum('bgrqk,bkgd->bqgrd', p.astype(v.dtype), v)

    out = lax.map(one_block, qb)
    return out.transpose(1, 0, 2, 3, 4, 5).reshape(b, lq, g, r, v.shape[-1])


def banded_window_attention(q, k, v, k_ctx, v_ctx, sink):
    b, n, g, r, d = q.shape
    nb = n // SWA_BLOCK
    span = SWA_BLOCK + 2 * SWA_WINDOW
    pad = SWA_WINDOW
    kp = jnp.pad(k, ((0, 0), (pad, pad), (0, 0), (0, 0)))
    vp = jnp.pad(v, ((0, 0), (pad, pad), (0, 0), (0, 0)))
    idx = jnp.arange(nb)[:, None] * SWA_BLOCK + jnp.arange(span)[None, :]
    kb = kp[:, idx]
    vb = vp[:, idx]
    kpos = idx - pad
    qpos = jnp.arange(n).reshape(nb, SWA_BLOCK)
    mask = ((kpos[:, None, :] >= 0) & (kpos[:, None, :] < n)
            & (jnp.abs(qpos[:, :, None] - kpos[:, None, :]) <= SWA_WINDOW))
    qb = q.reshape(b, nb, SWA_BLOCK, g, r, d)
    scale = d ** -0.5
    s_loc = jnp.einsum('bnqgrd,bnkgd->bgrnqk', qb, kb).astype(jnp.float32) * scale
    s_loc = jnp.where(mask[None, None, None], s_loc, NEG)
    s_ctx = jnp.einsum('bnqgrd,bkgd->bgrnqk', qb, k_ctx).astype(jnp.float32) * scale
    s_sink = jnp.broadcast_to(sink.astype(jnp.float32)[None, :, :, None, None, None], s_loc.shape[:-1] + (1,))
    p = jax.nn.softmax(jnp.concatenate([s_loc, s_ctx, s_sink], -1), -1).astype(v.dtype)
    nc = k_ctx.shape[1]
    out = (jnp.einsum('bgrnqk,bnkgd->bnqgrd', p[..., :span], vb)
           + jnp.einsum('bgrnqk,bkgd->bnqgrd', p[..., span:span + nc], v_ctx))
    return out.reshape(b, n, g, r, d)


def neighbourhood_attention(q, k, v, k_ctx, v_ctx, rel_bias):
    b, n, h, d = q.shape
    rows = n // GRID_W
    wr = min(NA_WIN_ROWS, rows)
    wc = NA_WIN_COLS
    r_idx = jnp.arange(rows)
    rs = jnp.clip(r_idx - wr // 2, 0, rows - wr)
    row_idx = rs[:, None] + jnp.arange(wr)[None, :]
    cq = jnp.arange(GRID_W)
    cs = jnp.clip(cq - wc // 2, 0, GRID_W - wc)
    kg = k.reshape(b, rows, GRID_W, h, d)[:, row_idx].reshape(b, rows, wr * GRID_W, h, d)
    vg = v.reshape(b, rows, GRID_W, h, d)[:, row_idx].reshape(b, rows, wr * GRID_W, h, d)
    qg = q.reshape(b, rows, GRID_W, h, d)
    col_mask = (cq[None, :] >= cs[:, None]) & (cq[None, :] < cs[:, None] + wc)
    mask = jnp.broadcast_to(col_mask[:, None, :], (GRID_W, wr, GRID_W)).reshape(GRID_W, wr * GRID_W)
    dr = row_idx - r_idx[:, None] + (NA_WIN_ROWS - 1)
    dc = jnp.clip(cq[None, :] - cq[:, None], -(wc - 1), wc - 1) + (wc - 1)
    bias = rel_bias[:, dr[:, None, :, None], dc[None, :, None, :]]
    bias = bias.reshape(h, rows, GRID_W, wr * GRID_W).astype(jnp.float32)
    scale = d ** -0.5
    s_loc = jnp.einsum('brqhd,brkhd->bhrqk', qg, kg).astype(jnp.float32) * scale + bias[None]
    s_loc = jnp.where(mask[None, None, None], s_loc, NEG)
    s_ctx = jnp.einsum('brqhd,bkhd->bhrqk', qg, k_ctx).astype(jnp.float32) * scale
    p = jax.nn.softmax(jnp.concatenate([s_loc, s_ctx], -1), -1).astype(v.dtype)
    nl = wr * GRID_W
    out = (jnp.einsum('bhrqk,brkhd->brqhd', p[..., :nl], vg)
           + jnp.einsum('bhrqk,bkhd->brqhd', p[..., nl:], v_ctx))
    return out.reshape(b, n, h, d)


def mla_expand(c_kv, k_rope, w_kv_up):
    b, n, _ = c_kv.shape
    kv = jnp.einsum('bnc,ce->bne', c_kv, w_kv_up).reshape(b, n, MLA_HEADS, MLA_NOPE + MLA_V)
    k_pe = jnp.broadcast_to(k_rope[:, :, None, :], (b, n, MLA_HEADS, MLA_ROPE))
    return jnp.concatenate([kv[..., :MLA_NOPE], k_pe], -1), kv[..., MLA_NOPE:]


def mixer_ab_project(h, p):
    b, n, _ = h.shape
    z = jnp.einsum('bnd,de->bne', h, p['w_in'])
    q_lat, kv_lat, k_rope, sq, sk, sv = jnp.split(z, L0_SPLITS, axis=-1)
    q = jnp.einsum('bnc,ce->bne', rms_norm(q_lat, p['q_norm']), p['w_q_up'])
    q = q.reshape(b, n, MLA_HEADS, MLA_NOPE + MLA_ROPE)
    c_kv = rms_norm(kv_lat, p['kv_norm'])
    sq = sq.reshape(b, n, SWA_KV_HEADS, SWA_GROUP, HEAD_DIM)
    sk = sk.reshape(b, n, SWA_KV_HEADS, HEAD_DIM)
    sv = sv.reshape(b, n, SWA_KV_HEADS, HEAD_DIM)
    return q, c_kv, k_rope, sq, sk, sv


def mixer_ab_output(o_mla, o_swa, w_out):
    b, n = o_mla.shape[:2]
    o = jnp.concatenate([o_mla.reshape(b, n, -1), o_swa.reshape(b, n, -1)], -1)
    return jnp.einsum('bne,ed->bnd', o, w_out)


def mixer_ab_context(h, p):
    q, c_kv, k_rope, sq, sk, sv = mixer_ab_project(h, p)
    k_mla, v_mla = mla_expand(c_kv, k_rope, p['w_kv_up'])
    o_mla = dense_attention(q[:, :, :, None, :], k_mla, v_mla)
    o_swa = dense_attention(sq, sk, sv, sink=p['sink'].reshape(SWA_KV_HEADS, SWA_GROUP))
    return mixer_ab_output(o_mla, o_swa, p['w_out']), (c_kv, k_rope, sk, sv)


def mixer_ab_latent(h, cache, p):
    ckv_ctx, krope_ctx, sk_ctx, sv_ctx = cache
    b, n, _ = h.shape
    q, c_kv, k_rope, sq, sk, sv = mixer_ab_project(h, p)
    q = jnp.concatenate([q[..., :MLA_NOPE], axial_rope(q[..., MLA_NOPE:])], -1)
    k_rope = axial_rope(k_rope[:, :, None, :])[:, :, 0, :]
    k_lat, v_lat = mla_expand(c_kv, k_rope, p['w_kv_up'])
    k_ctx, v_ctx = mla_expand(ckv_ctx, krope_ctx, p['w_kv_up'])
    o_mla = dense_attention(q[:, :, :, None, :], jnp.concatenate([k_ctx, k_lat], 1),
                            jnp.concatenate([v_ctx, v_lat], 1))
    sq = axial_rope(sq.reshape(b, n, SWA_HEADS, HEAD_DIM)).reshape(b, n, SWA_KV_HEADS, SWA_GROUP, HEAD_DIM)
    sk = axial_rope(sk)
    o_swa = banded_window_attention(sq, sk, sv, sk_ctx, sv_ctx, p['sink'].reshape(SWA_KV_HEADS, SWA_GROUP))
    return mixer_ab_output(o_mla, o_swa, p['w_out'])


def mixer_c_project(h, p):
    b, n, _ = h.shape
    z = jnp.einsum('bnd,de->bne', h, p['w_in'])
    q, k, v = jnp.split(z, 3, axis=-1)
    return (q.reshape(b, n, NA_HEADS, HEAD_DIM), k.reshape(b, n, NA_HEADS, HEAD_DIM),
            v.reshape(b, n, NA_HEADS, HEAD_DIM))


def mixer_c_context(h, p):
    b, n, _ = h.shape
    q, k, v = mixer_c_project(h, p)
    o = dense_attention(q[:, :, :, None, :], k, v).reshape(b, n, NA_WIDTH)
    return jnp.einsum('bne,ed->bnd', o, p['w_out']), (k, v)


def mixer_c_latent(h, cache, p):
    k_ctx, v_ctx = cache
    b, n, _ = h.shape
    q, k, v = mixer_c_project(h, p)
    o = neighbourhood_attention(q, k, v, k_ctx, v_ctx, p['rel_bias']).reshape(b, n, NA_WIDTH)
    return jnp.einsum('bne,ed->bnd', o, p['w_out'])


def swiglu(h, w1, w3, w2):
    a = jnp.einsum('bnd,df->bnf', h, w1)
    g = jnp.einsum('bnd,df->bnf', h, w3)
    return jnp.einsum('bnf,fd->bnd', jax.nn.silu(a) * g, w2)


def moe_swiglu(h, router_w, router_b, w1, w3, w2):
    logits = (jnp.einsum('bnd,de->bne', h, router_w) + router_b).astype(jnp.float32)
    top_v, top_i = lax.top_k(logits, TOP_K)
    wts = jax.nn.softmax(top_v, -1)
    gate = jnp.sum(jax.nn.one_hot(top_i, N_EXPERTS, dtype=jnp.float32) * wts[..., None], axis=-2).astype(h.dtype)
    y = jnp.zeros_like(h)
    for e in range(N_EXPERTS):
        y = y + gate[..., e:e + 1] * swiglu(h, w1[e], w3[e], w2[e])
    return y


def setup_inputs(seed: int = 0) -> dict:
    key = jax.random.key(seed)
    ks = iter(jax.random.split(key, 64))

    def nrm(shape, scale=1.0):
        return jax.random.normal(next(ks), shape, jnp.float32) * scale

    d = D_MODEL
    beta = (8.0 * DEPTH) ** -0.25
    inp = {}
    inp['x_prompt'] = nrm((BATCH, SEQ, d))
    inp['x_sample'] = nrm((DEC_BATCH, DEC_SEQ, d))
    inp['cache_l0_mla_ckv'] = nrm((DEC_BATCH, PAST_LEN, MLA_KV_LORA))
    inp['cache_l0_mla_krope'] = nrm((DEC_BATCH, PAST_LEN, MLA_ROPE))
    inp['cache_l0_swa_k'] = nrm((DEC_BATCH, PAST_LEN, SWA_KV_HEADS, HEAD_DIM))
    inp['cache_l0_swa_v'] = nrm((DEC_BATCH, PAST_LEN, SWA_KV_HEADS, HEAD_DIM))
    inp['cache_l1_na_k'] = nrm((DEC_BATCH, PAST_LEN, NA_HEADS, HEAD_DIM))
    inp['cache_l1_na_v'] = nrm((DEC_BATCH, PAST_LEN, NA_HEADS, HEAD_DIM))
    inp['c'] = nrm((DEC_BATCH, d))
    inp['c_ctx'] = nrm((d,))
    inp['l0_ada_w'] = nrm((d, 6 * d), 0.5 * d ** -0.5)
    inp['l0_ada_b'] = nrm((6 * d,), 0.01)
    inp['l0_w_in'] = nrm((d, L0_IN), d ** -0.5)
    inp['l0_mla_q_norm'] = 1.0 + nrm((MLA_Q_LORA,), 0.01)
    inp['l0_mla_w_q_up'] = nrm((MLA_Q_LORA, MLA_HEADS * (MLA_NOPE + MLA_ROPE)), MLA_Q_LORA ** -0.5)
    inp['l0_mla_kv_norm'] = 1.0 + nrm((MLA_KV_LORA,), 0.01)
    inp['l0_mla_w_kv_up'] = nrm((MLA_KV_LORA, MLA_HEADS * (MLA_NOPE + MLA_V)), MLA_KV_LORA ** -0.5)
    inp['l0_swa_sink'] = nrm((SWA_HEADS,), 0.5)
    inp['l0_w_out'] = nrm((MIX_WIDTH_AB, d), beta * MIX_WIDTH_AB ** -0.5)
    inp['l0_ln1_g'] = 1.0 + nrm((d,), 0.01)
    inp['l0_ln1_b'] = nrm((d,), 0.01)
    inp['l0_ffn_w1'] = nrm((d, D_FF), d ** -0.5)
    inp['l0_ffn_w3'] = nrm((d, D_FF), d ** -0.5)
    inp['l0_ffn_w2'] = nrm((D_FF, d), beta * D_FF ** -0.5)
    inp['l0_ln2_g'] = 1.0 + nrm((d,), 0.01)
    inp['l0_ln2_b'] = nrm((d,), 0.01)
    inp['l1_ada_w'] = nrm((d, 6 * d), 0.5 * d ** -0.5)
    inp['l1_ada_b'] = nrm((6 * d,), 0.01)
    inp['l1_w_in'] = nrm((d, 3 * NA_WIDTH), d ** -0.5)
    inp['l1_na_rel_bias'] = nrm((NA_HEADS, 2 * NA_WIN_ROWS - 1, 2 * NA_WIN_COLS - 1), 0.1)
    inp['l1_w_out'] = nrm((NA_WIDTH, d), beta * NA_WIDTH ** -0.5)
    inp['l1_ln1_g'] = 1.0 + nrm((d,), 0.01)
    inp['l1_ln1_b'] = nrm((d,), 0.01)
    inp['l1_moe_router_w'] = nrm((d, N_EXPERTS), d ** -0.5)
    inp['l1_moe_router_b'] = nrm((N_EXPERTS,), 0.01)
    inp['l1_moe_w1'] = nrm((N_EXPERTS, d, E_FF), d ** -0.5)
    inp['l1_moe_w3'] = nrm((N_EXPERTS, d, E_FF), d ** -0.5)
    inp['l1_moe_w2'] = nrm((N_EXPERTS, E_FF, d), beta * E_FF ** -0.5)
    inp['l1_ln2_g'] = 1.0 + nrm((d,), 0.01)
    inp['l1_ln2_b'] = nrm((d,), 0.01)
    return inp


def reference(x_prompt, x_sample, cache_l0_mla_ckv, cache_l0_mla_krope, cache_l0_swa_k, cache_l0_swa_v,
              cache_l1_na_k, cache_l1_na_v, c, c_ctx,
              l0_ada_w, l0_ada_b, l0_w_in, l0_mla_q_norm, l0_mla_w_q_up, l0_mla_kv_norm, l0_mla_w_kv_up,
              l0_swa_sink, l0_w_out, l0_ln1_g, l0_ln1_b, l0_ffn_w1, l0_ffn_w3, l0_ffn_w2, l0_ln2_g, l0_ln2_b,
              l1_ada_w, l1_ada_b, l1_w_in, l1_na_rel_bias, l1_w_out, l1_ln1_g, l1_ln1_b,
              l1_moe_router_w, l1_moe_router_b, l1_moe_w1, l1_moe_w3, l1_moe_w2, l1_ln2_g, l1_ln2_b):
    alpha = (2.0 * DEPTH) ** 0.25
    layer_params = [
        dict(ada_w=l0_ada_w, ada_b=l0_ada_b, w_in=l0_w_in, q_norm=l0_mla_q_norm, w_q_up=l0_mla_w_q_up,
             kv_norm=l0_mla_kv_norm, w_kv_up=l0_mla_w_kv_up, sink=l0_swa_sink, w_out=l0_w_out,
             ln1_g=l0_ln1_g, ln1_b=l0_ln1_b, ffn_w1=l0_ffn_w1, ffn_w3=l0_ffn_w3, ffn_w2=l0_ffn_w2,
             ln2_g=l0_ln2_g, ln2_b=l0_ln2_b),
        dict(ada_w=l1_ada_w, ada_b=l1_ada_b, w_in=l1_w_in, rel_bias=l1_na_rel_bias, w_out=l1_w_out,
             ln1_g=l1_ln1_g, ln1_b=l1_ln1_b, router_w=l1_moe_router_w, router_b=l1_moe_router_b,
             moe_w1=l1_moe_w1, moe_w3=l1_moe_w3, moe_w2=l1_moe_w2, ln2_g=l1_ln2_g, ln2_b=l1_ln2_b),
    ]
    layer_caches = [(cache_l0_mla_ckv, cache_l0_mla_krope, cache_l0_swa_k, cache_l0_swa_v),
                    (cache_l1_na_k, cache_l1_na_v)]
    y_prompt = x_prompt
    y_sample = x_sample
    states = []
    for layer in range(DEPTH):
        p = layer_params[layer]
        sh1_p, sc1_p, g1_p, sh2_p, sc2_p, g2_p = adaln(c_ctx[None, :], p['ada_w'], p['ada_b'])
        sh1_s, sc1_s, g1_s, sh2_s, sc2_s, g2_s = adaln(c, p['ada_w'], p['ada_b'])
        h_p = y_prompt * (1 + sc1_p) + sh1_p
        h_s = y_sample * (1 + sc1_s) + sh1_s
        if layer % 2 == 0:
            o_p, st = mixer_ab_context(h_p, p)
            o_s = mixer_ab_latent(h_s, layer_caches[layer], p)
        else:
            o_p, st = mixer_c_context(h_p, p)
            o_s = mixer_c_latent(h_s, layer_caches[layer], p)
        states.append(st)
        y_prompt = layer_norm(alpha * y_prompt + g1_p * o_p, p['ln1_g'], p['ln1_b'])
        y_sample = layer_norm(alpha * y_sample + g1_s * o_s, p['ln1_g'], p['ln1_b'])
        h_p = y_prompt * (1 + sc2_p) + sh2_p
        h_s = y_sample * (1 + sc2_s) + sh2_s
        if layer % 2 == 0:
            f_p = swiglu(h_p, p['ffn_w1'], p['ffn_w3'], p['ffn_w2'])
            f_s = swiglu(h_s, p['ffn_w1'], p['ffn_w3'], p['ffn_w2'])
        else:
            f_p = moe_swiglu(h_p, p['router_w'], p['router_b'], p['moe_w1'], p['moe_w3'], p['moe_w2'])
            f_s = moe_swiglu(h_s, p['router_w'], p['router_b'], p['moe_w1'], p['moe_w3'], p['moe_w2'])
        y_prompt = layer_norm(alpha * y_prompt + g2_p * f_p, p['ln2_g'], p['ln2_b'])
        y_sample = layer_norm(alpha * y_sample + g2_s * f_s, p['ln2_g'], p['ln2_b'])
    new_l0_mla_ckv, new_l0_mla_krope, new_l0_swa_k, new_l0_swa_v = states[0]
    new_l1_na_k, new_l1_na_v = states[1]
    return (y_prompt, y_sample, new_l0_mla_ckv, new_l0_mla_krope, new_l0_swa_k, new_l0_swa_v, new_l1_na_k, new_l1_na_v)
```

```python
import functools

import jax
import jax.numpy as jnp
from jax import lax
from jax.experimental import pallas as pl
from jax.experimental.pallas import tpu as pltpu

F32 = jnp.float32
BF16 = jnp.bfloat16

D = 1024
BATCH, SEQ = 32, 256
DEC_BATCH, DEC_SEQ = 2, 1024
PAST = 256
GRID_W = 64
T_P = BATCH * SEQ
T_S = DEC_BATCH * DEC_SEQ
T = T_P + T_S
N_COND = 1 + DEC_BATCH

MLA_HEADS, MLA_Q_LORA, MLA_KV_LORA, MLA_NOPE, MLA_ROPE, MLA_V = 8, 384, 256, 64, 32, 64
SWA_HEADS, SWA_KV_HEADS, SWA_WINDOW, HEAD_DIM = 8, 2, 128, 64
NA_HEADS, NA_WIN_ROWS, NA_WIN_COLS = 16, 8, 16
D_FF, N_EXPERTS, E_FF = 2816, 8, 3584
ROPE_THETA = 10000.0
LN_EPS, RMS_EPS = 1e-5, 1e-6
NEG = -1e30
ALPHA = 4.0 ** 0.25

LANES = 128
TM = 512
NT = T // TM
NP_TILES = T_P // TM
TILES_PER_SAMPLE = DEC_SEQ // TM
TG = 512
P_ROWS = 2 * T + N_EXPERTS * TG
NG = P_ROWS // TG
F_CHUNK = 896
VMEM_LIMIT = 56 * 1024 * 1024


def _params(*sem):
    return pltpu.CompilerParams(dimension_semantics=sem, vmem_limit_bytes=VMEM_LIMIT)


def _const_spec(shape, single_buffer=False):
    if single_buffer:
        return pl.BlockSpec(shape, lambda *_: (0,) * len(shape), pipeline_mode=pl.Buffered(1))
    return pl.BlockSpec(shape, lambda *_: (0,) * len(shape))


ROW_TILE = D // LANES


def _store_rows_tiled(ref, x):
    n = x.shape[0]
    for k in range(ROW_TILE):
        ref[pl.ds(k, n, stride=ROW_TILE), :] = x[:, k * LANES:(k + 1) * LANES]


def _load_rows_tiled(ref, n):
    return jnp.concatenate([ref[pl.ds(k, n, stride=ROW_TILE), :] for k in range(ROW_TILE)], axis=1)


def _cond_of_tile(i):
    return jnp.maximum((i - (NP_TILES - TILES_PER_SAMPLE)) // TILES_PER_SAMPLE, 0)


def _mod_spec():
    return pl.BlockSpec((1, 6, D), lambda i: (_cond_of_tile(i), 0, 0))


def _row_spec(width, rows=TM):
    return pl.BlockSpec((rows, width), lambda i: (i, 0))


def _dot(a, b):
    return jnp.dot(a, b, preferred_element_type=F32)


def _dot_nt(a, b):
    return lax.dot_general(a, b, (((1,), (1,)), ((), ())), preferred_element_type=F32)


def _layer_norm(r, g, b):
    mu = jnp.mean(r, axis=-1, keepdims=True)
    d = r - mu
    var = jnp.mean(d * d, axis=-1, keepdims=True)
    return d * lax.rsqrt(var + LN_EPS) * g + b


def _rms_norm(x, g):
    return x * lax.rsqrt(jnp.mean(x * x, axis=-1, keepdims=True) + RMS_EPS) * g


def _silu(x):
    return x * jax.nn.sigmoid(x)


def _ada_kernel(c_ref, w_ref, b_ref, o_ref):
    s = _silu(c_ref[...]).astype(BF16)
    o_ref[...] = _dot(s, w_ref[...].astype(BF16)) + b_ref[...]


def _adaln(cond, ada_w, ada_b):
    nb = 1536
    out = pl.pallas_call(
        _ada_kernel,
        out_shape=jax.ShapeDtypeStruct((8, 6 * D), F32),
        grid=(6 * D // nb,),
        in_specs=[_const_spec((8, D)), pl.BlockSpec((D, nb), lambda j: (0, j)),
                  pl.BlockSpec((1, nb), lambda j: (0, j))],
        out_specs=pl.BlockSpec((8, nb), lambda j: (0, j)),
        compiler_params=_params("arbitrary"),
        name="adaln",
    )(cond, ada_w, ada_b.reshape(1, 6 * D))
    return out[:N_COND].reshape(N_COND, 6, D)


def _rope_tables(head_dim, width):
    half = head_dim // 2
    nf = half // 2
    lane = jnp.arange(width)
    d = lane % head_dim
    dd = d % half
    f = dd % nf
    inv = ROPE_THETA ** (-f.astype(F32) / nf)
    t = jnp.arange(DEC_SEQ)
    pos = jnp.where((d // half)[None, :] == 0, (t // GRID_W)[:, None], (t % GRID_W)[:, None])
    ang = pos.astype(F32) * inv[None, :]
    cos, sin = jnp.cos(ang), jnp.sin(ang)
    first = (dd < nf)[None, :]
    return cos, jnp.where(first, -sin, 0.0), jnp.where(first, 0.0, sin)


def _rope(x, cos, sin_up, sin_dn, nf):
    w = x.shape[-1]
    return x * cos + pltpu.roll(x, w - nf, 1) * sin_up + pltpu.roll(x, nf, 1) * sin_dn


L0_COLS = MLA_Q_LORA + MLA_KV_LORA + 512 + 128 + 128 + 128


def _l0_in_kernel(x_ref, mod_ref, win_ref, qn_ref, wq_ref, kvn_ref, wkv_ref,
                  c8_ref, su8_ref, sd8_ref, c16_ref, su16_ref, sd16_ref,
                  qnope_o, qrope_o, ckv_o, knope_o, vmla_o, kr4_o, sq_o, sk_o, sv_o):
    i = pl.program_id(0)
    m = mod_ref[0]
    h = (x_ref[...] * (1.0 + m[1:2]) + m[0:1]).astype(BF16)
    z = _dot(h, win_ref[...])
    q_lat = z[:, 0:384]
    kv_lat = z[:, 384:640]
    sq = z[:, 640:1152]
    sk = z[:, 1152:1280]
    sv = z[:, 1280:1408]
    kr4 = z[:, 1408:1536]
    q = _dot(_rms_norm(q_lat, qn_ref[...]).astype(BF16), wq_ref[...])
    c_kv = _rms_norm(kv_lat, kvn_ref[...])
    kv = _dot(c_kv.astype(BF16), wkv_ref[...])
    qnope_o[...] = q[:, 0:512].astype(BF16)
    ckv_o[...] = c_kv
    knope_o[...] = kv[:, 0:512].astype(BF16)
    vmla_o[...] = kv[:, 512:1024].astype(BF16)
    sv_o[...] = sv
    q_rope = q[:, 512:768]

    @pl.when(i < NP_TILES)
    def _():
        qrope_o[...] = q_rope.astype(BF16)
        kr4_o[...] = kr4
        sq_o[...] = sq.astype(BF16)
        sk_o[...] = sk

    @pl.when(i >= NP_TILES)
    def _():
        c8, su8, sd8 = c8_ref[...], su8_ref[...], sd8_ref[...]
        c16, su16, sd16 = c16_ref[...], su16_ref[...], sd16_ref[...]
        qrope_o[...] = _rope(q_rope, c8, su8, sd8, 8).astype(BF16)
        kr4_o[...] = _rope(kr4, c8[:, :128], su8[:, :128], sd8[:, :128], 8)
        sq_o[...] = _rope(sq, c16, su16, sd16, 16).astype(BF16)
        sk_o[...] = _rope(sk, c16[:, :128], su16[:, :128], sd16[:, :128], 16)


def _l0_in(x, mods, w_in_r, q_norm, w_q_up_r, kv_norm, w_kv_up_r, tabs8, tabs16):
    def tab_spec(width):
        return pl.BlockSpec((TM, width), lambda i: (jnp.maximum(i - NP_TILES, 0) % TILES_PER_SAMPLE, 0))

    out_shape = [
        jax.ShapeDtypeStruct((T, 512), BF16),
        jax.ShapeDtypeStruct((T, 256), BF16),
        jax.ShapeDtypeStruct((T, 256), F32),
        jax.ShapeDtypeStruct((T, 512), BF16),
        jax.ShapeDtypeStruct((T, 512), BF16),
        jax.ShapeDtypeStruct((T, 128), F32),
        jax.ShapeDtypeStruct((T, 512), BF16),
        jax.ShapeDtypeStruct((T, 128), F32),
        jax.ShapeDtypeStruct((T, 128), F32),
    ]
    return pl.pallas_call(
        _l0_in_kernel,
        out_shape=out_shape,
        grid=(NT,),
        in_specs=[_row_spec(D), _mod_spec(), _const_spec((D, L0_COLS)),
                  _const_spec((1, MLA_Q_LORA)), _const_spec((MLA_Q_LORA, 768)),
                  _const_spec((1, MLA_KV_LORA)), _const_spec((MLA_KV_LORA, 1024)),
                  tab_spec(256), tab_spec(256), tab_spec(256),
                  tab_spec(512), tab_spec(512), tab_spec(512)],
        out_specs=[_row_spec(s.shape[1]) for s in out_shape],
        compiler_params=_params("arbitrary"),
        name="l0_in_proj",
    )(x, mods, w_in_r, q_norm.reshape(1, -1), w_q_up_r, kv_norm.reshape(1, -1), w_kv_up_r,
      *tabs8, *tabs16)


def _kv_up_kernel(c_ref, w_ref, k_o, v_o):
    kv = _dot(c_ref[...].astype(BF16), w_ref[...])
    k_o[...] = kv[:, 0:512].astype(BF16)
    v_o[...] = kv[:, 512:1024].astype(BF16)


def _kv_up(ckv, w_kv_up_r):
    n = ckv.shape[0]
    return pl.pallas_call(
        _kv_up_kernel,
        out_shape=[jax.ShapeDtypeStruct((n, 512), BF16)] * 2,
        grid=(1,),
        in_specs=[_const_spec((n, MLA_KV_LORA)), _const_spec((MLA_KV_LORA, 1024))],
        out_specs=[_const_spec((n, 512))] * 2,
        compiler_params=_params("arbitrary"),
        name="l0_ctx_kv_up",
    )(ckv, w_kv_up_r)


def _lane():
    return lax.broadcasted_iota(jnp.int32, (1, LANES), 1)


def _attend(scores, values, extra_logit=None):
    m = jnp.max(scores[0], axis=1, keepdims=True)
    for s in scores[1:]:
        m = jnp.maximum(m, jnp.max(s, axis=1, keepdims=True))
    if extra_logit is not None:
        m = jnp.maximum(m, extra_logit)
    den = None
    acc = None
    for s, v in zip(scores, values):
        e = jnp.exp(s - m)
        d = jnp.sum(e, axis=1, keepdims=True)
        a = _dot(e.astype(BF16), v)
        den = d if den is None else den + d
        acc = a if acc is None else acc + a
    if extra_logit is not None:
        den = den + jnp.exp(extra_logit - m)
    return acc / den


def _head_pair(q2, keys, values, scale, masks=None, biases=None, sinks=None, q_extra=None):
    lo = _lane() < 64
    zero = jnp.zeros_like(q2)
    outs = []
    for hh in range(2):
        qm = jnp.where(lo if hh == 0 else jnp.logical_not(lo), q2, zero)
        if q_extra is not None:
            qm = jnp.concatenate([qm, q_extra[hh]], axis=1)
        scores = []
        for n, k in enumerate(keys):
            s = _dot_nt(qm, k) * scale
            if biases is not None and biases[n] is not None:
                s = s + biases[n][hh]
            if masks is not None and masks[n] is not None:
                s = jnp.where(masks[n], s, NEG)
            scores.append(s)
        outs.append(_attend(scores, values, None if sinks is None else sinks[hh]))
    return jnp.where(lo, outs[0], outs[1])


def _dup_halves(x):
    lo = _lane() < 64
    sw = pltpu.roll(x, 64, 1)
    return jnp.where(lo, x, sw), jnp.where(lo, sw, x)


MLA_SCALE = (MLA_NOPE + MLA_ROPE) ** -0.5
HD_SCALE = HEAD_DIM ** -0.5


def _mla_pairs(qn_ref, qr_ref, key_sets, o_ref):
    lane = _lane()
    for j in range(MLA_HEADS // 2):
        cols = slice(128 * j, 128 * (j + 1))
        qr = qr_ref[:, 128 * (j // 2):128 * (j // 2 + 1)]
        zero = jnp.zeros_like(qr)
        q_extra = [jnp.where((lane // MLA_ROPE) == ((2 * j + hh) % 4), qr, zero) for hh in range(2)]
        keys = [jnp.concatenate([kn[:, cols], kr4], axis=1) for kn, kr4, _ in key_sets]
        values = [v[:, cols] for _, _, v in key_sets]
        o = _head_pair(qn_ref[:, cols], keys, values, MLA_SCALE, q_extra=q_extra)
        o_ref[:, cols] = o.astype(BF16)


def _swa_pairs(sink_ref, sq_ref, key_sets, masks, o_ref):
    kd = [[a.astype(BF16) for a in _dup_halves(k)] for k, _ in key_sets]
    vd = [[a.astype(BF16) for a in _dup_halves(v)] for _, v in key_sets]
    for g in range(SWA_KV_HEADS):
        for u in range(2):
            c = 2 * g + u
            cols = slice(128 * c, 128 * (c + 1))
            sinks = [sink_ref[2 * c + hh] for hh in range(2)]
            o = _head_pair(sq_ref[:, cols], [k[g] for k in kd], [v[g] for v in vd], HD_SCALE,
                           masks=masks, sinks=sinks)
            o_ref[:, 512 + 128 * c:512 + 128 * (c + 1)] = o.astype(BF16)


def _l0_attn_prompt_kernel(sink_ref, qn_ref, qr_ref, kn_ref, v_ref, kr4_ref, sq_ref, sk_ref, sv_ref, o_ref):
    _mla_pairs(qn_ref, qr_ref, [(kn_ref, kr4_ref[...].astype(BF16), v_ref)], o_ref)
    _swa_pairs(sink_ref, sq_ref, [(sk_ref[...], sv_ref[...])], None, o_ref)


def _l0_attn_prompt(sink, qn, qr, kn, vm, kr4, sq, sk, sv):
    spec = lambda w: pl.BlockSpec((SEQ, w), lambda b, *_: (b, 0))
    return pl.pallas_call(
        _l0_attn_prompt_kernel,
        out_shape=jax.ShapeDtypeStruct((T_P, D), BF16),
        grid_spec=pltpu.PrefetchScalarGridSpec(
            num_scalar_prefetch=1, grid=(BATCH,),
            in_specs=[spec(512), spec(256), spec(512), spec(512), spec(128), spec(512), spec(128), spec(128)],
            out_specs=spec(D)),
        compiler_params=_params("arbitrary"),
        name="l0_attn_prompt",
    )(sink, qn, qr, kn, vm, kr4, sq, sk, sv)


TQ_S = 256


def _l0_attn_sample_kernel(sink_ref, qn_ref, qr_ref, sq_ref, kn_ref, v_ref, kr4_ref, sk_ref, sv_ref,
                           knc_ref, vc_ref, kr4c_ref, skc_ref, svc_ref, o_ref):
    i = pl.program_id(1)
    _mla_pairs(qn_ref, qr_ref,
               [(knc_ref, kr4c_ref[...].astype(BF16), vc_ref), (kn_ref, kr4_ref[...].astype(BF16), v_ref)],
               o_ref)
    qpos = i * TQ_S + lax.broadcasted_iota(jnp.int32, (TQ_S, DEC_SEQ), 0)
    kpos = lax.broadcasted_iota(jnp.int32, (TQ_S, DEC_SEQ), 1)
    band = jnp.abs(qpos - kpos) <= SWA_WINDOW
    _swa_pairs(sink_ref, sq_ref, [(skc_ref[...], svc_ref[...]), (sk_ref[...], sv_ref[...])],
               [None, band], o_ref)


def _l0_attn_sample(sink, qn, qr, kn, vm, kr4, sq, sk, sv, knc, vc, kr4c, skc, svc):
    nq = DEC_SEQ // TQ_S
    qspec = lambda w: pl.BlockSpec((TQ_S, w), lambda b, i, *_: (T_P // TQ_S + b * nq + i, 0))
    kspec = lambda w: pl.BlockSpec((DEC_SEQ, w), lambda b, i, *_: (T_P // DEC_SEQ + b, 0))
    cspec = lambda w: pl.BlockSpec((PAST, w), lambda b, i, *_: (b, 0))
    return pl.pallas_call(
        _l0_attn_sample_kernel,
        out_shape=jax.ShapeDtypeStruct((T_S, D), BF16),
        grid_spec=pltpu.PrefetchScalarGridSpec(
            num_scalar_prefetch=1, grid=(DEC_BATCH, nq),
            in_specs=[qspec(512), qspec(256), qspec(512),
                      kspec(512), kspec(512), kspec(128), kspec(128), kspec(128),
                      cspec(512), cspec(512), cspec(128), cspec(128), cspec(128)],
            out_specs=pl.BlockSpec((TQ_S, D), lambda b, i, *_: (b * nq + i, 0))),
        compiler_params=_params("arbitrary", "arbitrary"),
        name="l0_attn_sample",
    )(sink, qn, qr, sq, kn, vm, kr4, sk, sv, knc, vc, kr4c, skc, svc)


def _out_ln_kernel(op_ref, os_ref, w_ref, y_ref, mod_ref, g_ref, b_ref, out_ref):
    i = pl.program_id(0)
    m = mod_ref[0]

    def finish(o):
        r = ALPHA * y_ref[...] + m[2:3] * _dot(o, w_ref[...])
        out_ref[...] = _layer_norm(r, g_ref[...], b_ref[...])

    @pl.when(i < NP_TILES)
    def _():
        finish(op_ref[...])

    @pl.when(i >= NP_TILES)
    def _():
        finish(os_ref[...])


def _out_ln(o_prompt, o_sample, w_out, y, mods, ln_g, ln_b):
    return pl.pallas_call(
        _out_ln_kernel,
        out_shape=jax.ShapeDtypeStruct((T, D), F32),
        grid=(NT,),
        in_specs=[pl.BlockSpec((TM, D), lambda i: (jnp.minimum(i, NP_TILES - 1), 0)),
                  pl.BlockSpec((TM, D), lambda i: (jnp.maximum(i - NP_TILES, 0), 0)),
                  _const_spec((D, D)), _row_spec(D), _mod_spec(),
                  _const_spec((1, D)), _const_spec((1, D))],
        out_specs=_row_spec(D),
        compiler_params=_params("arbitrary"),
        name="out_proj_ln",
    )(o_prompt, o_sample, w_out, y, mods, ln_g.reshape(1, D), ln_b.reshape(1, D))


FFN_CHUNK = D_FF // 2


def _ffn_ln_kernel(y_ref, mod_ref, w1_ref, w3_ref, w2_ref, g_ref, b_ref, out_ref):
    m = mod_ref[0]
    y = y_ref[...]
    h = (y * (1.0 + m[4:5]) + m[3:4]).astype(BF16)
    acc = None
    for c in range(D_FF // FFN_CHUNK):
        cols = slice(c * FFN_CHUNK, (c + 1) * FFN_CHUNK)
        a = _dot(h, w1_ref[:, cols])
        g = _dot(h, w3_ref[:, cols])
        part = _dot((_silu(a) * g).astype(BF16), w2_ref[cols, :])
        acc = part if acc is None else acc + part
    out_ref[...] = _layer_norm(ALPHA * y + m[5:6] * acc, g_ref[...], b_ref[...])


def _ffn_ln(y, mods, w1, w3, w2, ln_g, ln_b):
    return pl.pallas_call(
        _ffn_ln_kernel,
        out_shape=jax.ShapeDtypeStruct((T, D), F32),
        grid=(NT,),
        in_specs=[_row_spec(D), _mod_spec(), _const_spec((D, D_FF), True), _const_spec((D, D_FF), True),
                  _const_spec((D_FF, D), True), _const_spec((1, D)), _const_spec((1, D))],
        out_specs=_row_spec(D),
        compiler_params=_params("arbitrary"),
        name="ffn_ln",
    )(y, mods, w1, w3, w2, ln_g.reshape(1, D), ln_b.reshape(1, D))


def _l1_in_kernel(y_ref, mod_ref, w_ref, q_o, k_o, v_o):
    m = mod_ref[0]
    h = (y_ref[...] * (1.0 + m[1:2]) + m[0:1]).astype(BF16)
    z = _dot(h, w_ref[...])
    q_o[...] = z[:, 0:D].astype(BF16)
    k_o[...] = z[:, D:2 * D]
    v_o[...] = z[:, 2 * D:3 * D]


def _l1_in(y, mods, w_in):
    return pl.pallas_call(
        _l1_in_kernel,
        out_shape=[jax.ShapeDtypeStruct((T, D), BF16), jax.ShapeDtypeStruct((T, D), F32),
                   jax.ShapeDtypeStruct((T, D), F32)],
        grid=(NT,),
        in_specs=[_row_spec(D), _mod_spec(), _const_spec((D, 3 * D))],
        out_specs=[_row_spec(D)] * 3,
        compiler_params=_params("arbitrary"),
        name="l1_in_proj",
    )(y, mods, w_in)


def _l1_attn_prompt_kernel(q_ref, k_ref, v_ref, o_ref):
    for j in range(NA_HEADS // 2):
        cols = slice(128 * j, 128 * (j + 1))
        o = _head_pair(q_ref[:, cols], [k_ref[:, cols].astype(BF16)], [v_ref[:, cols].astype(BF16)], HD_SCALE)
        o_ref[:, cols] = o.astype(BF16)


def _l1_attn_prompt(q, k, v):
    spec = pl.BlockSpec((SEQ, D), lambda b: (b, 0))
    return pl.pallas_call(
        _l1_attn_prompt_kernel,
        out_shape=jax.ShapeDtypeStruct((T_P, D), BF16),
        grid=(BATCH,),
        in_specs=[spec, spec, spec],
        out_specs=spec,
        compiler_params=_params("arbitrary"),
        name="l1_attn_prompt",
    )(q, k, v)


NA_ROWS = DEC_SEQ // GRID_W
NA_KEYS = NA_WIN_ROWS * GRID_W


def _l1_attn_sample_kernel(q_ref, k_ref, v_ref, kc_ref, vc_ref, bias_ref, o_ref):
    kc = kc_ref[0].astype(BF16)
    vc = vc_ref[0].astype(BF16)

    def row(r, carry):
        rs = jnp.clip(r - NA_WIN_ROWS // 2, 0, NA_ROWS - NA_WIN_ROWS)
        d0 = rs - r + (NA_WIN_ROWS - 1)
        q2 = q_ref[pl.ds(pl.multiple_of(r * GRID_W, GRID_W), GRID_W), :]
        win = pl.ds(pl.multiple_of(rs * GRID_W, GRID_W), NA_KEYS)
        kw = k_ref[win, :].astype(BF16)
        vw = v_ref[win, :].astype(BF16)
        bias = [jnp.concatenate([bias_ref[hh, d0 + 2 * u] for u in range(NA_WIN_ROWS // 2)], axis=1)
                for hh in range(2)]
        o = _head_pair(q2, [kw, kc], [vw, vc], HD_SCALE, biases=[bias, None])
        o_ref[pl.ds(pl.multiple_of(r * GRID_W, GRID_W), GRID_W), :] = o.astype(BF16)
        return carry

    lax.fori_loop(0, NA_ROWS, row, 0)


def _l1_attn_sample(q, k, v, kc, vc, bias_pairs):
    lat = pl.BlockSpec((DEC_SEQ, LANES), lambda j, b: (T_P // DEC_SEQ + b, j))
    ctx = pl.BlockSpec((1, PAST, LANES), lambda j, b: (b, 0, j))
    return pl.pallas_call(
        _l1_attn_sample_kernel,
        out_shape=jax.ShapeDtypeStruct((T_S, D), BF16),
        grid=(NA_HEADS // 2, DEC_BATCH),
        in_specs=[lat, lat, lat, ctx, ctx,
                  pl.BlockSpec((2, 2 * NA_WIN_ROWS - 2, GRID_W, LANES), lambda j, b: (j, 0, 0, 0))],
        out_specs=pl.BlockSpec((DEC_SEQ, LANES), lambda j, b: (b, j)),
        compiler_params=_params("arbitrary", "arbitrary"),
        name="l1_attn_sample",
    )(q, k, v, kc, vc, bias_pairs)


def _na_bias_pairs(rel_bias):
    cq = jnp.arange(GRID_W)
    cs = jnp.clip(cq - NA_WIN_COLS // 2, 0, GRID_W - NA_WIN_COLS)
    col_mask = (cq[None, :] >= cs[:, None]) & (cq[None, :] < cs[:, None] + NA_WIN_COLS)
    dc = jnp.clip(cq[None, :] - cq[:, None], -(NA_WIN_COLS - 1), NA_WIN_COLS - 1) + (NA_WIN_COLS - 1)
    b = jnp.where(col_mask[None, None], rel_bias[:, :, dc].astype(F32), NEG)
    return jnp.concatenate([b[:, :-1], b[:, 1:]], axis=-1)


def _router_kernel(y_ref, mod_ref, rw_ref, rb_ref, h_o, meta_o, cnt_o, carry_ref):
    i = pl.program_id(0)

    @pl.when(i == 0)
    def _():
        carry_ref[...] = jnp.zeros_like(carry_ref)

    m = mod_ref[0]
    h = y_ref[...] * (1.0 + m[4:5]) + m[3:4]
    _store_rows_tiled(h_o, h)
    logits = jnp.dot(h, rw_ref[...], preferred_element_type=F32, precision=lax.Precision.HIGHEST) + rb_ref[...]
    lane = lax.broadcasted_iota(jnp.int32, (TM, LANES), 1).astype(F32)
    m1 = jnp.max(logits, axis=1, keepdims=True)
    i1 = jnp.min(jnp.where(logits == m1, lane, float(LANES)), axis=1, keepdims=True)
    sel1 = lane == i1
    rest = jnp.where(sel1, -jnp.inf, logits)
    m2 = jnp.max(rest, axis=1, keepdims=True)
    i2 = jnp.min(jnp.where(rest == m2, lane, float(LANES)), axis=1, keepdims=True)
    sel2 = lane == i2
    e2 = jnp.exp(m2 - m1)
    w1 = 1.0 / (1.0 + e2)
    w2 = e2 / (1.0 + e2)
    sel = jnp.logical_or(sel1, sel2)
    rr = lax.broadcasted_iota(jnp.int32, (TM, TM), 0)
    cc = lax.broadcasted_iota(jnp.int32, (TM, TM), 1)
    tri = jnp.where(cc < rr, 1.0, 0.0).astype(BF16)
    ahead = _dot(tri, jnp.where(sel, 1.0, 0.0).astype(BF16)) + carry_ref[...]
    r1 = jnp.sum(jnp.where(sel1, ahead, 0.0), axis=1, keepdims=True)
    r2 = jnp.sum(jnp.where(sel2, ahead, 0.0), axis=1, keepdims=True)
    meta = jnp.where(lane == 0, i1, 0.0)
    meta = jnp.where(lane == 1, i2, meta)
    meta = jnp.where(lane == 2, r1, meta)
    meta = jnp.where(lane == 3, r2, meta)
    meta = jnp.where(lane == 4, w1, meta)
    meta = jnp.where(lane == 5, w2, meta)
    meta_o[...] = meta
    carry_ref[...] = carry_ref[...] + jnp.sum(jnp.where(sel, 1.0, 0.0), axis=0, keepdims=True)
    cnt_o[...] = carry_ref[...]


def _router(y, mods, router_w, router_b):
    rw = jnp.zeros((D, LANES), F32).at[:, :N_EXPERTS].set(router_w)
    rb = jnp.full((1, LANES), NEG, F32).at[0, :N_EXPERTS].set(router_b)
    return pl.pallas_call(
        _router_kernel,
        out_shape=[jax.ShapeDtypeStruct((T * ROW_TILE, LANES), F32),
                   jax.ShapeDtypeStruct((T, LANES), F32),
                   jax.ShapeDtypeStruct((1, LANES), F32)],
        grid=(NT,),
        in_specs=[_row_spec(D), _mod_spec(), _const_spec((D, LANES)), _const_spec((1, LANES))],
        out_specs=[_row_spec(LANES, TM * ROW_TILE), _row_spec(LANES), _const_spec((1, LANES))],
        scratch_shapes=[pltpu.VMEM((1, LANES), F32)],
        compiler_params=_params("arbitrary"),
        name="moe_router",
    )(y, mods, rw, rb)


def _row_copy(src_ref, src_row, dst_ref, dst_row, sem):
    return pltpu.make_async_copy(src_ref.at[pl.ds(pl.multiple_of(src_row * ROW_TILE, ROW_TILE), ROW_TILE)],
                                 dst_ref.at[pl.ds(pl.multiple_of(dst_row * ROW_TILE, ROW_TILE), ROW_TILE)], sem)


def _dispatch_kernel(pos_ref, h_ref, xg_in_ref, xg_ref, sem):
    del xg_in_ref
    i = pl.program_id(0)

    def issue(r, carry):
        t = i * TM + r
        _row_copy(h_ref, r, xg_ref, pos_ref[2 * t], sem).start()
        _row_copy(h_ref, r, xg_ref, pos_ref[2 * t + 1], sem).start()
        return carry

    lax.fori_loop(0, TM, issue, 0)

    def drain(r, carry):
        _row_copy(h_ref, 0, xg_ref, 0, sem).wait()
        _row_copy(h_ref, 0, xg_ref, 0, sem).wait()
        return carry

    lax.fori_loop(0, TM, drain, 0)


def _dispatch(pos, h_tiled, xg0):
    return pl.pallas_call(
        _dispatch_kernel,
        out_shape=jax.ShapeDtypeStruct((P_ROWS * ROW_TILE, LANES), F32),
        grid_spec=pltpu.PrefetchScalarGridSpec(
            num_scalar_prefetch=1, grid=(NT,),
            in_specs=[pl.BlockSpec((TM * ROW_TILE, LANES), lambda i, *_: (i, 0)),
                      pl.BlockSpec(memory_space=pl.ANY)],
            out_specs=pl.BlockSpec(memory_space=pl.ANY),
            scratch_shapes=[pltpu.SemaphoreType.DMA(())]),
        input_output_aliases={2: 0},
        compiler_params=_params("arbitrary"),
        name="moe_dispatch",
    )(pos, h_tiled, xg0)


def _expert_ffn_kernel(te_ref, na_ref, x_ref, w1_ref, w3_ref, w2_ref, o_ref, acc_ref):
    g = pl.program_id(0)
    c = pl.program_id(1)

    @pl.when(g < na_ref[0])
    def _():
        x = _load_rows_tiled(x_ref, TG).astype(BF16)
        a = _dot(x, w1_ref[0])
        b = _dot(x, w3_ref[0])
        part = _dot((_silu(a) * b).astype(BF16), w2_ref[0])

        @pl.when(c == 0)
        def _():
            acc_ref[...] = part

        @pl.when(c > 0)
        def _():
            acc_ref[...] += part

        @pl.when(c == E_FF // F_CHUNK - 1)
        def _():
            _store_rows_tiled(o_ref, acc_ref[...])

    @pl.when(jnp.logical_and(g >= na_ref[0], c == 0))
    def _():
        o_ref[...] = jnp.zeros_like(o_ref)


def _expert_ffn(tile_expert, n_active, xg, w1, w3, w2):
    nc = E_FF // F_CHUNK
    rows = pl.BlockSpec((TG * ROW_TILE, LANES), lambda g, c, te, na: (g, 0))
    return pl.pallas_call(
        _expert_ffn_kernel,
        out_shape=jax.ShapeDtypeStruct((P_ROWS * ROW_TILE, LANES), F32),
        grid_spec=pltpu.PrefetchScalarGridSpec(
            num_scalar_prefetch=2, grid=(NG, nc),
            in_specs=[rows,
                      pl.BlockSpec((1, D, F_CHUNK), lambda g, c, te, na: (te[g], 0, c)),
                      pl.BlockSpec((1, D, F_CHUNK), lambda g, c, te, na: (te[g], 0, c)),
                      pl.BlockSpec((1, F_CHUNK, D), lambda g, c, te, na: (te[g], c, 0))],
            out_specs=rows,
            scratch_shapes=[pltpu.VMEM((TG, D), F32)]),
        compiler_params=_params("arbitrary", "arbitrary"),
        name="moe_expert_ffn",
    )(tile_expert, n_active, xg, w1, w3, w2)


def _combine_ln_kernel(pos_ref, y_ref, mod_ref, meta_ref, g_ref, b_ref, eo_ref, out_ref, buf1, buf2, sem):
    i = pl.program_id(0)

    def issue(r, carry):
        t = i * TM + r
        _row_copy(eo_ref, pos_ref[2 * t], buf1, r, sem).start()
        _row_copy(eo_ref, pos_ref[2 * t + 1], buf2, r, sem).start()
        return carry

    lax.fori_loop(0, TM, issue, 0)

    def drain(r, carry):
        _row_copy(eo_ref, 0, buf1, 0, sem).wait()
        _row_copy(eo_ref, 0, buf2, 0, sem).wait()
        return carry

    lax.fori_loop(0, TM, drain, 0)
    m = mod_ref[0]
    meta = meta_ref[...]
    f = meta[:, 4:5] * _load_rows_tiled(buf1, TM) + meta[:, 5:6] * _load_rows_tiled(buf2, TM)
    out_ref[...] = _layer_norm(ALPHA * y_ref[...] + m[5:6] * f, g_ref[...], b_ref[...])


def _combine_ln(pos, y, mods, meta, ln_g, ln_b, eo):
    return pl.pallas_call(
        _combine_ln_kernel,
        out_shape=jax.ShapeDtypeStruct((T, D), F32),
        grid_spec=pltpu.PrefetchScalarGridSpec(
            num_scalar_prefetch=1, grid=(NT,),
            in_specs=[pl.BlockSpec((TM, D), lambda i, *_: (i, 0)),
                      pl.BlockSpec((1, 6, D), lambda i, *_: (_cond_of_tile(i), 0, 0)),
                      pl.BlockSpec((TM, LANES), lambda i, *_: (i, 0)),
                      pl.BlockSpec((1, D), lambda i, *_: (0, 0)),
                      pl.BlockSpec((1, D), lambda i, *_: (0, 0)),
                      pl.BlockSpec(memory_space=pl.ANY)],
            out_specs=pl.BlockSpec((TM, D), lambda i, *_: (i, 0)),
            scratch_shapes=[pltpu.VMEM((TM * ROW_TILE, LANES), F32),
                            pltpu.VMEM((TM * ROW_TILE, LANES), F32),
                            pltpu.SemaphoreType.DMA(())]),
        compiler_params=_params("arbitrary"),
        name="moe_combine_ln",
    )(pos, y, mods, meta, ln_g.reshape(1, D), ln_b.reshape(1, D), eo)


def _moe_ln(y, mods, router_w, router_b, w1, w3, w2, ln_g, ln_b):
    h_tiled, meta, counts = _router(y, mods, router_w, router_b)
    cnt = counts[0, :N_EXPERTS].astype(jnp.int32)
    tiles = (cnt + TG - 1) // TG
    tile_end = jnp.cumsum(tiles)
    offs = (tile_end - tiles) * TG
    e1 = meta[:, 0].astype(jnp.int32)
    e2 = meta[:, 1].astype(jnp.int32)
    pos = jnp.stack([offs[e1] + meta[:, 2].astype(jnp.int32), offs[e2] + meta[:, 3].astype(jnp.int32)], axis=1)
    pos = pos.reshape(2 * T)
    tile_expert = jnp.minimum(jnp.searchsorted(tile_end, jnp.arange(NG), side="right"), N_EXPERTS - 1)
    tile_expert = tile_expert.astype(jnp.int32)
    n_active = tile_end[-1:].astype(jnp.int32)
    xg = _dispatch(pos, h_tiled, jnp.zeros((P_ROWS * ROW_TILE, LANES), F32))
    eo = _expert_ffn(tile_expert, n_active, xg, w1.astype(BF16), w3.astype(BF16), w2.astype(BF16))
    return _combine_ln(pos, y, mods, meta, ln_g, ln_b, eo)


def _l0_weight_layouts(w_in, w_q_up, w_kv_up):
    a, b, c = MLA_Q_LORA, MLA_Q_LORA + MLA_KV_LORA, MLA_Q_LORA + MLA_KV_LORA + MLA_ROPE
    k_rope = w_in[:, b:c]
    w_in_r = jnp.concatenate([w_in[:, :b], w_in[:, c:], k_rope, k_rope, k_rope, k_rope], axis=1)
    wq = w_q_up.reshape(MLA_Q_LORA, MLA_HEADS, MLA_NOPE + MLA_ROPE)
    w_q_up_r = jnp.concatenate([wq[:, :, :MLA_NOPE].reshape(MLA_Q_LORA, -1),
                                wq[:, :, MLA_NOPE:].reshape(MLA_Q_LORA, -1)], axis=1)
    wkv = w_kv_up.reshape(MLA_KV_LORA, MLA_HEADS, MLA_NOPE + MLA_V)
    w_kv_up_r = jnp.concatenate([wkv[:, :, :MLA_NOPE].reshape(MLA_KV_LORA, -1),
                                 wkv[:, :, MLA_NOPE:].reshape(MLA_KV_LORA, -1)], axis=1)
    return w_in_r.astype(BF16), w_q_up_r.astype(BF16), w_kv_up_r.astype(BF16)


def kernel(x_prompt, x_sample, cache_l0_mla_ckv, cache_l0_mla_krope, cache_l0_swa_k, cache_l0_swa_v,
           cache_l1_na_k, cache_l1_na_v, c, c_ctx,
           l0_ada_w, l0_ada_b, l0_w_in, l0_mla_q_norm, l0_mla_w_q_up, l0_mla_kv_norm, l0_mla_w_kv_up,
           l0_swa_sink, l0_w_out, l0_ln1_g, l0_ln1_b, l0_ffn_w1, l0_ffn_w3, l0_ffn_w2, l0_ln2_g, l0_ln2_b,
           l1_ada_w, l1_ada_b, l1_w_in, l1_na_rel_bias, l1_w_out, l1_ln1_g, l1_ln1_b,
           l1_moe_router_w, l1_moe_router_b, l1_moe_w1, l1_moe_w3, l1_moe_w2, l1_ln2_g, l1_ln2_b):
    y = jnp.concatenate([x_prompt.reshape(T_P, D), x_sample.reshape(T_S, D)], axis=0)
    cond = jnp.concatenate([c_ctx[None, :], c, jnp.zeros((8 - N_COND, D), F32)], axis=0)

    mods = _adaln(cond, l0_ada_w, l0_ada_b)
    w_in_r, w_q_up_r, w_kv_up_r = _l0_weight_layouts(l0_w_in, l0_mla_w_q_up, l0_mla_w_kv_up)
    qn, qr, ckv, kn, vm, kr4, sq, sk, sv = _l0_in(
        y, mods, w_in_r, l0_mla_q_norm, w_q_up_r, l0_mla_kv_norm, w_kv_up_r,
        _rope_tables(MLA_ROPE, 256), _rope_tables(HEAD_DIM, 512))
    knc, vc = _kv_up(cache_l0_mla_ckv.reshape(DEC_BATCH * PAST, MLA_KV_LORA), w_kv_up_r)
    kr4c = jnp.tile(cache_l0_mla_krope.reshape(DEC_BATCH * PAST, MLA_ROPE), (1, 4))
    skc = cache_l0_swa_k.reshape(DEC_BATCH * PAST, SWA_KV_HEADS * HEAD_DIM)
    svc = cache_l0_swa_v.reshape(DEC_BATCH * PAST, SWA_KV_HEADS * HEAD_DIM)
    o_p = _l0_attn_prompt(l0_swa_sink, qn, qr, kn, vm, kr4, sq, sk, sv)
    o_s = _l0_attn_sample(l0_swa_sink, qn, qr, kn, vm, kr4, sq, sk, sv, knc, vc, kr4c, skc, svc)
    y = _out_ln(o_p, o_s, l0_w_out.astype(BF16), y, mods, l0_ln1_g, l0_ln1_b)
    y = _ffn_ln(y, mods, l0_ffn_w1.astype(BF16), l0_ffn_w3.astype(BF16), l0_ffn_w2.astype(BF16),
                l0_ln2_g, l0_ln2_b)
    new_ckv = ckv[:T_P].reshape(BATCH, SEQ, MLA_KV_LORA)
    new_krope = kr4[:T_P, :MLA_ROPE].reshape(BATCH, SEQ, MLA_ROPE)
    new_sk = sk[:T_P].reshape(BATCH, SEQ, SWA_KV_HEADS, HEAD_DIM)
    new_sv = sv[:T_P].reshape(BATCH, SEQ, SWA_KV_HEADS, HEAD_DIM)

    mods = _adaln(cond, l1_ada_w, l1_ada_b)
    q, k, v = _l1_in(y, mods, l1_w_in.astype(BF16))
    o_p = _l1_attn_prompt(q, k, v)
    o_s = _l1_attn_sample(q, k, v, cache_l1_na_k.reshape(DEC_BATCH, PAST, D),
                          cache_l1_na_v.reshape(DEC_BATCH, PAST, D), _na_bias_pairs(l1_na_rel_bias))
    y = _out_ln(o_p, o_s, l1_w_out.astype(BF16), y, mods, l1_ln1_g, l1_ln1_b)
    y = _moe_ln(y, mods, l1_moe_router_w, l1_moe_router_b, l1_moe_w1, l1_moe_w3, l1_moe_w2, l1_ln2_g, l1_ln2_b)
    new_k = k[:T_P].reshape(BATCH, SEQ, NA_HEADS, HEAD_DIM)
    new_v = v[:T_P].reshape(BATCH, SEQ, NA_HEADS, HEAD_DIM)

    return (y[:T_P].reshape(BATCH, SEQ, D), y[T_P:].reshape(DEC_BATCH, DEC_SEQ, D),
            new_ckv, new_krope, new_sk, new_sv, new_k, new_v)
```

```python
import functools

import jax
import jax.numpy as jnp
from jax import lax
from jax.experimental import pallas as pl
from jax.experimental.pallas import tpu as pltpu

F32 = jnp.float32
BF16 = jnp.bfloat16

D = 1024
BATCH, SEQ = 32, 256
DEC_BATCH, DEC_SEQ = 2, 1024
PAST = 256
GRID_W = 64
T_P = BATCH * SEQ
T_S = DEC_BATCH * DEC_SEQ
T = T_P + T_S
N_COND = 1 + DEC_BATCH

MLA_HEADS, MLA_Q_LORA, MLA_KV_LORA, MLA_NOPE, MLA_ROPE, MLA_V = 8, 384, 256, 64, 32, 64
SWA_HEADS, SWA_KV_HEADS, SWA_WINDOW, HEAD_DIM = 8, 2, 128, 64
NA_HEADS, NA_WIN_ROWS, NA_WIN_COLS = 16, 8, 16
D_FF, N_EXPERTS, E_FF = 2816, 8, 3584
ROPE_THETA = 10000.0
LN_EPS, RMS_EPS = 1e-5, 1e-6
NEG = -1e30
ALPHA = 4.0 ** 0.25

LANES = 128
TM = 512
NT = T // TM
NP_TILES = T_P // TM
TILES_PER_SAMPLE = DEC_SEQ // TM
TG = 512
P_ROWS = 2 * T + N_EXPERTS * TG
NG = P_ROWS // TG
F_CHUNK = 896
VMEM_LIMIT = 56 * 1024 * 1024


def _params(*sem):
    return pltpu.CompilerParams(dimension_semantics=sem, vmem_limit_bytes=VMEM_LIMIT)


def _const_spec(shape, single_buffer=False):
    if single_buffer:
        return pl.BlockSpec(shape, lambda *_: (0,) * len(shape), pipeline_mode=pl.Buffered(1))
    return pl.BlockSpec(shape, lambda *_: (0,) * len(shape))


ROW_TILE = D // LANES


def _store_rows_tiled(ref, x):
    n = x.shape[0]
    for k in range(ROW_TILE):
        ref[pl.ds(k, n, stride=ROW_TILE), :] = x[:, k * LANES:(k + 1) * LANES]


def _load_rows_tiled(ref, n):
    return jnp.concatenate([ref[pl.ds(k, n, stride=ROW_TILE), :] for k in range(ROW_TILE)], axis=1)


def _cond_of_tile(i):
    return jnp.maximum((i - (NP_TILES - TILES_PER_SAMPLE)) // TILES_PER_SAMPLE, 0)


def _mod_spec():
    return pl.BlockSpec((1, 6, D), lambda i: (_cond_of_tile(i), 0, 0))


def _row_spec(width, rows=TM):
    return pl.BlockSpec((rows, width), lambda i: (i, 0))


def _dot(a, b):
    return jnp.dot(a, b, preferred_element_type=F32)


def _dot_nt(a, b):
    return lax.dot_general(a, b, (((1,), (1,)), ((), ())), preferred_element_type=F32)


def _layer_norm(r, g, b):
    mu = jnp.mean(r, axis=-1, keepdims=True)
    d = r - mu
    var = jnp.mean(d * d, axis=-1, keepdims=True)
    return d * lax.rsqrt(var + LN_EPS) * g + b


def _rms_norm(x, g):
    return x * lax.rsqrt(jnp.mean(x * x, axis=-1, keepdims=True) + RMS_EPS) * g


def _silu(x):
    return x * jax.nn.sigmoid(x)


def _ada_kernel(c_ref, w_ref, b_ref, o_ref):
    s = _silu(c_ref[...]).astype(BF16)
    o_ref[...] = _dot(s, w_ref[...].astype(BF16)) + b_ref[...]


def _adaln(cond, ada_w, ada_b):
    nb = 1536
    out = pl.pallas_call(
        _ada_kernel,
        out_shape=jax.ShapeDtypeStruct((8, 6 * D), F32),
        grid=(6 * D // nb,),
        in_specs=[_const_spec((8, D)), pl.BlockSpec((D, nb), lambda j: (0, j)),
                  pl.BlockSpec((1, nb), lambda j: (0, j))],
        out_specs=pl.BlockSpec((8, nb), lambda j: (0, j)),
        compiler_params=_params("arbitrary"),
        name="adaln",
    )(cond, ada_w, ada_b.reshape(1, 6 * D))
    return out[:N_COND].reshape(N_COND, 6, D)


def _rope_tables(head_dim, width):
    half = head_dim // 2
    nf = half // 2
    lane = jnp.arange(width)
    d = lane % head_dim
    dd = d % half
    f = dd % nf
    inv = ROPE_THETA ** (-f.astype(F32) / nf)
    t = jnp.arange(DEC_SEQ)
    pos = jnp.where((d // half)[None, :] == 0, (t // GRID_W)[:, None], (t % GRID_W)[:, None])
    ang = pos.astype(F32) * inv[None, :]
    cos, sin = jnp.cos(ang), jnp.sin(ang)
    first = (dd < nf)[None, :]
    return cos, jnp.where(first, -sin, 0.0), jnp.where(first, 0.0, sin)


def _rope(x, cos, sin_up, sin_dn, nf):
    w = x.shape[-1]
    return x * cos + pltpu.roll(x, w - nf, 1) * sin_up + pltpu.roll(x, nf, 1) * sin_dn


L0_COLS = MLA_Q_LORA + MLA_KV_LORA + 512 + 128 + 128 + 128


def _l0_in_kernel(xp_ref, xs_ref, mod_ref, win_ref, qn_ref, wq_ref, kvn_ref, wkv_ref,
                  c8_ref, su8_ref, sd8_ref, c16_ref, su16_ref, sd16_ref,
                  y_o, qnope_o, qrope_o, ckv_o, knope_o, vmla_o, kr4_o, sq_o, sk_o, sv_o):
    i = pl.program_id(0)
    m = mod_ref[0]
    x = jnp.where(i < NP_TILES, xp_ref[...], xs_ref[...])
    y_o[...] = x
    h = (x * (1.0 + m[1:2]) + m[0:1]).astype(BF16)
    z = _dot(h, win_ref[...])
    q_lat = z[:, 0:384]
    kv_lat = z[:, 384:640]
    sq = z[:, 640:1152]
    sk = z[:, 1152:1280]
    sv = z[:, 1280:1408]
    kr4 = z[:, 1408:1536]
    q = _dot(_rms_norm(q_lat, qn_ref[...]).astype(BF16), wq_ref[...])
    c_kv = _rms_norm(kv_lat, kvn_ref[...])
    kv = _dot(c_kv.astype(BF16), wkv_ref[...])
    qnope_o[...] = q[:, 0:512].astype(BF16)
    ckv_o[...] = c_kv
    knope_o[...] = kv[:, 0:512].astype(BF16)
    vmla_o[...] = kv[:, 512:1024].astype(BF16)
    sv_o[...] = sv
    q_rope = q[:, 512:768]

    @pl.when(i < NP_TILES)
    def _():
        qrope_o[...] = q_rope.astype(BF16)
        kr4_o[...] = kr4
        sq_o[...] = sq.astype(BF16)
        sk_o[...] = sk

    @pl.when(i >= NP_TILES)
    def _():
        c8, su8, sd8 = c8_ref[...], su8_ref[...], sd8_ref[...]
        c16, su16, sd16 = c16_ref[...], su16_ref[...], sd16_ref[...]
        qrope_o[...] = _rope(q_rope, c8, su8, sd8, 8).astype(BF16)
        kr4_o[...] = _rope(kr4, c8[:, :128], su8[:, :128], sd8[:, :128], 8)
        sq_o[...] = _rope(sq, c16, su16, sd16, 16).astype(BF16)
        sk_o[...] = _rope(sk, c16[:, :128], su16[:, :128], sd16[:, :128], 16)


def _prompt_tile_spec(width):
    return pl.BlockSpec((TM, width), lambda i, *_: (jnp.minimum(i, NP_TILES - 1), 0))


def _sample_tile_spec(width):
    return pl.BlockSpec((TM, width), lambda i, *_: (jnp.maximum(i - NP_TILES, 0), 0))


def _l0_in(x_prompt, x_sample, mods, w_in_r, q_norm, w_q_up_r, kv_norm, w_kv_up_r, tabs8, tabs16):
    def tab_spec(width):
        return pl.BlockSpec((TM, width), lambda i: (jnp.maximum(i - NP_TILES, 0) % TILES_PER_SAMPLE, 0))

    out_shape = [
        jax.ShapeDtypeStruct((T, D), F32),
        jax.ShapeDtypeStruct((T, 512), BF16),
        jax.ShapeDtypeStruct((T, 256), BF16),
        jax.ShapeDtypeStruct((T, 256), F32),
        jax.ShapeDtypeStruct((T, 512), BF16),
        jax.ShapeDtypeStruct((T, 512), BF16),
        jax.ShapeDtypeStruct((T, 128), F32),
        jax.ShapeDtypeStruct((T, 512), BF16),
        jax.ShapeDtypeStruct((T, 128), F32),
        jax.ShapeDtypeStruct((T, 128), F32),
    ]
    return pl.pallas_call(
        _l0_in_kernel,
        out_shape=out_shape,
        grid=(NT,),
        in_specs=[_prompt_tile_spec(D), _sample_tile_spec(D), _mod_spec(), _const_spec((D, L0_COLS)),
                  _const_spec((1, MLA_Q_LORA)), _const_spec((MLA_Q_LORA, 768)),
                  _const_spec((1, MLA_KV_LORA)), _const_spec((MLA_KV_LORA, 1024)),
                  tab_spec(256), tab_spec(256), tab_spec(256),
                  tab_spec(512), tab_spec(512), tab_spec(512)],
        out_specs=[_row_spec(s.shape[1]) for s in out_shape],
        compiler_params=_params("arbitrary"),
        name="l0_in_proj",
    )(x_prompt, x_sample, mods, w_in_r, q_norm.reshape(1, -1), w_q_up_r, kv_norm.reshape(1, -1), w_kv_up_r,
      *tabs8, *tabs16)


def _kv_up_kernel(c_ref, w_ref, k_o, v_o):
    kv = _dot(c_ref[...].astype(BF16), w_ref[...])
    k_o[...] = kv[:, 0:512].astype(BF16)
    v_o[...] = kv[:, 512:1024].astype(BF16)


def _kv_up(ckv, w_kv_up_r):
    n = ckv.shape[0]
    return pl.pallas_call(
        _kv_up_kernel,
        out_shape=[jax.ShapeDtypeStruct((n, 512), BF16)] * 2,
        grid=(1,),
        in_specs=[_const_spec((n, MLA_KV_LORA)), _const_spec((MLA_KV_LORA, 1024))],
        out_specs=[_const_spec((n, 512))] * 2,
        compiler_params=_params("arbitrary"),
        name="l0_ctx_kv_up",
    )(ckv, w_kv_up_r)


def _lane():
    return lax.broadcasted_iota(jnp.int32, (1, LANES), 1)


def _attend(scores, values, extra_logit=None):
    m = jnp.max(scores[0], axis=1, keepdims=True)
    for s in scores[1:]:
        m = jnp.maximum(m, jnp.max(s, axis=1, keepdims=True))
    if extra_logit is not None:
        m = jnp.maximum(m, extra_logit)
    den = None
    acc = None
    for s, v in zip(scores, values):
        e = jnp.exp(s - m)
        d = jnp.sum(e, axis=1, keepdims=True)
        a = _dot(e.astype(BF16), v)
        den = d if den is None else den + d
        acc = a if acc is None else acc + a
    if extra_logit is not None:
        den = den + jnp.exp(extra_logit - m)
    return acc / den


def _head_pair(q2, keys, values, scale, masks=None, biases=None, sinks=None, q_extra=None):
    lo = _lane() < 64
    zero = jnp.zeros_like(q2)
    outs = []
    for hh in range(2):
        qm = jnp.where(lo if hh == 0 else jnp.logical_not(lo), q2, zero)
        if q_extra is not None:
            qm = jnp.concatenate([qm, q_extra[hh]], axis=1)
        scores = []
        for n, k in enumerate(keys):
            s = _dot_nt(qm, k) * scale
            if biases is not None and biases[n] is not None:
                s = s + biases[n][hh]
            if masks is not None and masks[n] is not None:
                s = jnp.where(masks[n], s, NEG)
            scores.append(s)
        outs.append(_attend(scores, values, None if sinks is None else sinks[hh]))
    return jnp.where(lo, outs[0], outs[1])


def _dup_halves(x):
    lo = _lane() < 64
    sw = pltpu.roll(x, 64, 1)
    return jnp.where(lo, x, sw), jnp.where(lo, sw, x)


MLA_SCALE = (MLA_NOPE + MLA_ROPE) ** -0.5
HD_SCALE = HEAD_DIM ** -0.5


def _mla_pairs(qn_ref, qr_ref, key_sets, o_ref):
    lane = _lane()
    for j in range(MLA_HEADS // 2):
        cols = slice(128 * j, 128 * (j + 1))
        qr = qr_ref[:, 128 * (j // 2):128 * (j // 2 + 1)]
        zero = jnp.zeros_like(qr)
        q_extra = [jnp.where((lane // MLA_ROPE) == ((2 * j + hh) % 4), qr, zero) for hh in range(2)]
        keys = [jnp.concatenate([kn[:, cols], kr4], axis=1) for kn, kr4, _ in key_sets]
        values = [v[:, cols] for _, _, v in key_sets]
        o = _head_pair(qn_ref[:, cols], keys, values, MLA_SCALE, q_extra=q_extra)
        o_ref[:, cols] = o.astype(BF16)


def _swa_pairs(sink_ref, sq_ref, key_sets, masks, o_ref):
    kd = [[a.astype(BF16) for a in _dup_halves(k)] for k, _ in key_sets]
    vd = [[a.astype(BF16) for a in _dup_halves(v)] for _, v in key_sets]
    for g in range(SWA_KV_HEADS):
        for u in range(2):
            c = 2 * g + u
            cols = slice(128 * c, 128 * (c + 1))
            sinks = [sink_ref[2 * c + hh] for hh in range(2)]
            o = _head_pair(sq_ref[:, cols], [k[g] for k in kd], [v[g] for v in vd], HD_SCALE,
                           masks=masks, sinks=sinks)
            o_ref[:, 512 + 128 * c:512 + 128 * (c + 1)] = o.astype(BF16)


def _l0_attn_prompt_kernel(sink_ref, qn_ref, qr_ref, kn_ref, v_ref, kr4_ref, sq_ref, sk_ref, sv_ref, o_ref):
    _mla_pairs(qn_ref, qr_ref, [(kn_ref, kr4_ref[...].astype(BF16), v_ref)], o_ref)
    _swa_pairs(sink_ref, sq_ref, [(sk_ref[...], sv_ref[...])], None, o_ref)


def _l0_attn_prompt(sink, qn, qr, kn, vm, kr4, sq, sk, sv):
    spec = lambda w: pl.BlockSpec((SEQ, w), lambda b, *_: (b, 0))
    return pl.pallas_call(
        _l0_attn_prompt_kernel,
        out_shape=jax.ShapeDtypeStruct((T_P, D), BF16),
        grid_spec=pltpu.PrefetchScalarGridSpec(
            num_scalar_prefetch=1, grid=(BATCH,),
            in_specs=[spec(512), spec(256), spec(512), spec(512), spec(128), spec(512), spec(128), spec(128)],
            out_specs=spec(D)),
        compiler_params=_params("arbitrary"),
        name="l0_attn_prompt",
    )(sink, qn, qr, kn, vm, kr4, sq, sk, sv)


TQ_S = 256


def _l0_attn_sample_kernel(sink_ref, qn_ref, qr_ref, sq_ref, kn_ref, v_ref, kr4_ref, sk_ref, sv_ref,
                           knc_ref, vc_ref, kr4c_ref, skc_ref, svc_ref, o_ref):
    i = pl.program_id(1)
    _mla_pairs(qn_ref, qr_ref,
               [(knc_ref, kr4c_ref[...].astype(BF16), vc_ref), (kn_ref, kr4_ref[...].astype(BF16), v_ref)],
               o_ref)
    qpos = i * TQ_S + lax.broadcasted_iota(jnp.int32, (TQ_S, DEC_SEQ), 0)
    kpos = lax.broadcasted_iota(jnp.int32, (TQ_S, DEC_SEQ), 1)
    band = jnp.abs(qpos - kpos) <= SWA_WINDOW
    _swa_pairs(sink_ref, sq_ref, [(skc_ref[...], svc_ref[...]), (sk_ref[...], sv_ref[...])],
               [None, band], o_ref)


def _l0_attn_sample(sink, qn, qr, kn, vm, kr4, sq, sk, sv, knc, vc, kr4c, skc, svc):
    nq = DEC_SEQ // TQ_S
    qspec = lambda w: pl.BlockSpec((TQ_S, w), lambda b, i, *_: (T_P // TQ_S + b * nq + i, 0))
    kspec = lambda w: pl.BlockSpec((DEC_SEQ, w), lambda b, i, *_: (T_P // DEC_SEQ + b, 0))
    cspec = lambda w: pl.BlockSpec((PAST, w), lambda b, i, *_: (b, 0))
    return pl.pallas_call(
        _l0_attn_sample_kernel,
        out_shape=jax.ShapeDtypeStruct((T_S, D), BF16),
        grid_spec=pltpu.PrefetchScalarGridSpec(
            num_scalar_prefetch=1, grid=(DEC_BATCH, nq),
            in_specs=[qspec(512), qspec(256), qspec(512),
                      kspec(512), kspec(512), kspec(128), kspec(128), kspec(128),
                      cspec(512), cspec(512), cspec(128), cspec(128), cspec(128)],
            out_specs=pl.BlockSpec((TQ_S, D), lambda b, i, *_: (b * nq + i, 0))),
        compiler_params=_params("arbitrary", "arbitrary"),
        name="l0_attn_sample",
    )(sink, qn, qr, sq, kn, vm, kr4, sk, sv, knc, vc, kr4c, skc, svc)


def _out_ln_kernel(op_ref, os_ref, w_ref, y_ref, mod_ref, g_ref, b_ref, out_ref):
    i = pl.program_id(0)
    m = mod_ref[0]

    def finish(o):
        r = ALPHA * y_ref[...] + m[2:3] * _dot(o, w_ref[...])
        out_ref[...] = _layer_norm(r, g_ref[...], b_ref[...])

    @pl.when(i < NP_TILES)
    def _():
        finish(op_ref[...])

    @pl.when(i >= NP_TILES)
    def _():
        finish(os_ref[...])


def _out_ln(o_prompt, o_sample, w_out, y, mods, ln_g, ln_b):
    return pl.pallas_call(
        _out_ln_kernel,
        out_shape=jax.ShapeDtypeStruct((T, D), F32),
        grid=(NT,),
        in_specs=[_prompt_tile_spec(D), _sample_tile_spec(D),
                  _const_spec((D, D)), _row_spec(D), _mod_spec(),
                  _const_spec((1, D)), _const_spec((1, D))],
        out_specs=_row_spec(D),
        compiler_params=_params("arbitrary"),
        name="out_proj_ln",
    )(o_prompt, o_sample, w_out, y, mods, ln_g.reshape(1, D), ln_b.reshape(1, D))


FFN_CHUNK = D_FF // 2


def _ffn_ln_kernel(y_ref, mod_ref, w1_ref, w3_ref, w2_ref, g_ref, b_ref, out_ref):
    m = mod_ref[0]
    y = y_ref[...]
    h = (y * (1.0 + m[4:5]) + m[3:4]).astype(BF16)
    acc = None
    for c in range(D_FF // FFN_CHUNK):
        cols = slice(c * FFN_CHUNK, (c + 1) * FFN_CHUNK)
        a = _dot(h, w1_ref[:, cols])
        g = _dot(h, w3_ref[:, cols])
        part = _dot((_silu(a) * g).astype(BF16), w2_ref[cols, :])
        acc = part if acc is None else acc + part
    out_ref[...] = _layer_norm(ALPHA * y + m[5:6] * acc, g_ref[...], b_ref[...])


def _ffn_ln(y, mods, w1, w3, w2, ln_g, ln_b):
    return pl.pallas_call(
        _ffn_ln_kernel,
        out_shape=jax.ShapeDtypeStruct((T, D), F32),
        grid=(NT,),
        in_specs=[_row_spec(D), _mod_spec(), _const_spec((D, D_FF), True), _const_spec((D, D_FF), True),
                  _const_spec((D_FF, D), True), _const_spec((1, D)), _const_spec((1, D))],
        out_specs=_row_spec(D),
        compiler_params=_params("arbitrary"),
        name="ffn_ln",
    )(y, mods, w1, w3, w2, ln_g.reshape(1, D), ln_b.reshape(1, D))


def _l1_in_kernel(y_ref, mod_ref, w_ref, q_o, kp_o, vp_o, ks_o, vs_o):
    i = pl.program_id(0)
    m = mod_ref[0]
    h = (y_ref[...] * (1.0 + m[1:2]) + m[0:1]).astype(BF16)
    z = _dot(h, w_ref[...])
    q_o[...] = z[:, 0:D].astype(BF16)

    @pl.when(i < NP_TILES)
    def _():
        kp_o[...] = z[:, D:2 * D]
        vp_o[...] = z[:, 2 * D:3 * D]

    @pl.when(i >= NP_TILES)
    def _():
        ks_o[...] = z[:, D:2 * D].astype(BF16)
        vs_o[...] = z[:, 2 * D:3 * D].astype(BF16)


def _l1_in(y, mods, w_in):
    return pl.pallas_call(
        _l1_in_kernel,
        out_shape=[jax.ShapeDtypeStruct((T, D), BF16),
                   jax.ShapeDtypeStruct((T_P, D), F32), jax.ShapeDtypeStruct((T_P, D), F32),
                   jax.ShapeDtypeStruct((T_S, D), BF16), jax.ShapeDtypeStruct((T_S, D), BF16)],
        grid=(NT,),
        in_specs=[_row_spec(D), _mod_spec(), _const_spec((D, 3 * D))],
        out_specs=[_row_spec(D), _prompt_tile_spec(D), _prompt_tile_spec(D),
                   _sample_tile_spec(D), _sample_tile_spec(D)],
        compiler_params=_params("arbitrary"),
        name="l1_in_proj",
    )(y, mods, w_in)


def _l1_attn_prompt_kernel(q_ref, k_ref, v_ref, o_ref):
    for j in range(NA_HEADS // 2):
        cols = slice(128 * j, 128 * (j + 1))
        o = _head_pair(q_ref[:, cols], [k_ref[:, cols].astype(BF16)], [v_ref[:, cols].astype(BF16)], HD_SCALE)
        o_ref[:, cols] = o.astype(BF16)


def _l1_attn_prompt(q, k, v):
    spec = pl.BlockSpec((SEQ, D), lambda b: (b, 0))
    return pl.pallas_call(
        _l1_attn_prompt_kernel,
        out_shape=jax.ShapeDtypeStruct((T_P, D), BF16),
        grid=(BATCH,),
        in_specs=[spec, spec, spec],
        out_specs=spec,
        compiler_params=_params("arbitrary"),
        name="l1_attn_prompt",
    )(q, k, v)


NA_ROWS = DEC_SEQ // GRID_W
NA_TILE_ROWS = 4
NA_TQ = NA_TILE_ROWS * GRID_W
NA_SPAN = NA_WIN_ROWS + NA_TILE_ROWS
NA_DR = 2 * NA_WIN_ROWS - 1
COL_SPAN = 2 * GRID_W - 1


def _na_span_start(t):
    first = max(0, min(t * NA_TILE_ROWS - NA_WIN_ROWS // 2, NA_ROWS - NA_WIN_ROWS))
    return min(first, NA_ROWS - NA_SPAN)


def _l1_attn_sample_kernel(q_ref, k_ref, v_ref, kc_ref, vc_ref, ext_ref, o_ref):
    kc = kc_ref[0].astype(BF16)
    vc = vc_ref[0].astype(BF16)
    lo = _lane() < GRID_W
    qcol = lax.broadcasted_iota(jnp.int32, (GRID_W, LANES), 0)
    kcol = lax.broadcasted_iota(jnp.int32, (GRID_W, LANES), 1) % GRID_W
    cs = jnp.clip(qcol - NA_WIN_COLS // 2, 0, GRID_W - NA_WIN_COLS)
    col_ok = jnp.logical_and(kcol >= cs, kcol < cs + NA_WIN_COLS)
    neg = jnp.full((GRID_W, LANES), NEG, F32)

    def bias_tile(hh, d, half):
        x = jnp.broadcast_to(ext_ref[hh, d:d + 1, :], (GRID_W, LANES))
        shift = (LANES - COL_SPAN // 2 + GRID_W * half) % LANES
        return jnp.where(col_ok, pltpu.roll(x, shift, 1, stride=1, stride_axis=0), neg)

    tiles = [[[bias_tile(hh, d, half) for half in range(2)] for d in range(NA_DR)] for hh in range(2)]

    for t in range(NA_ROWS // NA_TILE_ROWS):
        ws = _na_span_start(t)
        keys = slice(ws * GRID_W, (ws + NA_SPAN) * GRID_W)
        kw = k_ref[keys, :].astype(BF16)
        vw = v_ref[keys, :].astype(BF16)
        bias = []
        for hh in range(2):
            rows = []
            for rr in range(NA_TILE_ROWS):
                r = t * NA_TILE_ROWS + rr
                rs = max(0, min(r - NA_WIN_ROWS // 2, NA_ROWS - NA_WIN_ROWS))
                blocks = []
                for u in range(NA_SPAN // 2):
                    halves = []
                    for half in range(2):
                        kr = ws + 2 * u + half
                        ok = rs <= kr < rs + NA_WIN_ROWS
                        halves.append(tiles[hh][kr - r + NA_WIN_ROWS - 1][half] if ok else neg)
                    blocks.append(jnp.where(lo, halves[0], halves[1]))
                rows.append(jnp.concatenate(blocks, axis=1))
            bias.append(jnp.concatenate(rows, axis=0))
        qrows = slice(t * NA_TQ, (t + 1) * NA_TQ)
        o = _head_pair(q_ref[qrows, :], [kw, kc], [vw, vc], HD_SCALE, biases=[bias, None])
        o_ref[qrows, :] = o.astype(BF16)


def _l1_attn_sample(q, k_s, v_s, kc, vc, ext):
    lat = pl.BlockSpec((DEC_SEQ, LANES), lambda j, b: (b, j))
    ctx = pl.BlockSpec((1, PAST, LANES), lambda j, b: (b, 0, j))
    return pl.pallas_call(
        _l1_attn_sample_kernel,
        out_shape=jax.ShapeDtypeStruct((T_S, D), BF16),
        grid=(NA_HEADS // 2, DEC_BATCH),
        in_specs=[pl.BlockSpec((DEC_SEQ, LANES), lambda j, b: (T_P // DEC_SEQ + b, j)), lat, lat, ctx, ctx,
                  pl.BlockSpec((2, NA_DR, LANES), lambda j, b: (j, 0, 0))],
        out_specs=lat,
        compiler_params=_params("arbitrary", "arbitrary"),
        name="l1_attn_sample",
    )(q, k_s, v_s, kc, vc, ext)


def _na_bias_rows(rel_bias):
    rb = rel_bias.astype(F32)
    n_lo = GRID_W - 1 - (NA_WIN_COLS - 1)
    n_hi = LANES - n_lo - rb.shape[-1]
    return jnp.concatenate([jnp.repeat(rb[..., :1], n_lo, axis=-1), rb,
                            jnp.repeat(rb[..., -1:], n_hi, axis=-1)], axis=-1)


def _router_kernel(y_ref, mod_ref, rw_ref, rb_ref, h_o, meta_o, cnt_o, carry_ref):
    i = pl.program_id(0)

    @pl.when(i == 0)
    def _():
        carry_ref[...] = jnp.zeros_like(carry_ref)

    m = mod_ref[0]
    h = y_ref[...] * (1.0 + m[4:5]) + m[3:4]
    _store_rows_tiled(h_o, h)
    logits = jnp.dot(h, rw_ref[...], preferred_element_type=F32, precision=lax.Precision.HIGHEST) + rb_ref[...]
    lane = lax.broadcasted_iota(jnp.int32, (TM, LANES), 1).astype(F32)
    m1 = jnp.max(logits, axis=1, keepdims=True)
    i1 = jnp.min(jnp.where(logits == m1, lane, float(LANES)), axis=1, keepdims=True)
    sel1 = lane == i1
    rest = jnp.where(sel1, -jnp.inf, logits)
    m2 = jnp.max(rest, axis=1, keepdims=True)
    i2 = jnp.min(jnp.where(rest == m2, lane, float(LANES)), axis=1, keepdims=True)
    sel2 = lane == i2
    e2 = jnp.exp(m2 - m1)
    w1 = 1.0 / (1.0 + e2)
    w2 = e2 / (1.0 + e2)
    sel = jnp.logical_or(sel1, sel2)
    rr = lax.broadcasted_iota(jnp.int32, (TM, TM), 0)
    cc = lax.broadcasted_iota(jnp.int32, (TM, TM), 1)
    tri = jnp.where(cc < rr, 1.0, 0.0).astype(BF16)
    ahead = _dot(tri, jnp.where(sel, 1.0, 0.0).astype(BF16)) + carry_ref[...]
    r1 = jnp.sum(jnp.where(sel1, ahead, 0.0), axis=1, keepdims=True)
    r2 = jnp.sum(jnp.where(sel2, ahead, 0.0), axis=1, keepdims=True)
    meta = jnp.where(lane == 0, i1, 0.0)
    meta = jnp.where(lane == 1, i2, meta)
    meta = jnp.where(lane == 2, r1, meta)
    meta = jnp.where(lane == 3, r2, meta)
    meta = jnp.where(lane == 4, w1, meta)
    meta = jnp.where(lane == 5, w2, meta)
    meta_o[...] = meta
    carry_ref[...] = carry_ref[...] + jnp.sum(jnp.where(sel, 1.0, 0.0), axis=0, keepdims=True)
    cnt_o[...] = carry_ref[...]


def _router(y, mods, router_w, router_b):
    rw = jnp.zeros((D, LANES), F32).at[:, :N_EXPERTS].set(router_w)
    rb = jnp.full((1, LANES), NEG, F32).at[0, :N_EXPERTS].set(router_b)
    return pl.pallas_call(
        _router_kernel,
        out_shape=[jax.ShapeDtypeStruct((T * ROW_TILE, LANES), F32),
                   jax.ShapeDtypeStruct((T, LANES), F32),
                   jax.ShapeDtypeStruct((1, LANES), F32)],
        grid=(NT,),
        in_specs=[_row_spec(D), _mod_spec(), _const_spec((D, LANES)), _const_spec((1, LANES))],
        out_specs=[_row_spec(LANES, TM * ROW_TILE), _row_spec(LANES), _const_spec((1, LANES))],
        scratch_shapes=[pltpu.VMEM((1, LANES), F32)],
        compiler_params=_params("arbitrary"),
        name="moe_router",
    )(y, mods, rw, rb)


def _row_copy(src_ref, src_row, dst_ref, dst_row, sem):
    return pltpu.make_async_copy(src_ref.at[pl.ds(pl.multiple_of(src_row * ROW_TILE, ROW_TILE), ROW_TILE)],
                                 dst_ref.at[pl.ds(pl.multiple_of(dst_row * ROW_TILE, ROW_TILE), ROW_TILE)], sem)


def _dispatch_kernel(pos_ref, h_ref, xg_in_ref, xg_ref, sem):
    del xg_in_ref
    i = pl.program_id(0)

    def issue(r, carry):
        t = i * TM + r
        _row_copy(h_ref, r, xg_ref, pos_ref[2 * t], sem).start()
        _row_copy(h_ref, r, xg_ref, pos_ref[2 * t + 1], sem).start()
        return carry

    lax.fori_loop(0, TM, issue, 0)

    def drain(r, carry):
        _row_copy(h_ref, 0, xg_ref, 0, sem).wait()
        _row_copy(h_ref, 0, xg_ref, 0, sem).wait()
        return carry

    lax.fori_loop(0, TM, drain, 0)


def _dispatch(pos, h_tiled, xg0):
    return pl.pallas_call(
        _dispatch_kernel,
        out_shape=jax.ShapeDtypeStruct((P_ROWS * ROW_TILE, LANES), F32),
        grid_spec=pltpu.PrefetchScalarGridSpec(
            num_scalar_prefetch=1, grid=(NT,),
            in_specs=[pl.BlockSpec((TM * ROW_TILE, LANES), lambda i, *_: (i, 0)),
                      pl.BlockSpec(memory_space=pl.ANY)],
            out_specs=pl.BlockSpec(memory_space=pl.ANY),
            scratch_shapes=[pltpu.SemaphoreType.DMA(())]),
        input_output_aliases={2: 0},
        compiler_params=_params("arbitrary"),
        name="moe_dispatch",
    )(pos, h_tiled, xg0)


def _expert_ffn_kernel(te_ref, na_ref, x_ref, w1_ref, w3_ref, w2_ref, o_ref):
    g = pl.program_id(0)

    @pl.when(g < na_ref[0])
    def _():
        x = _load_rows_tiled(x_ref, TG).astype(BF16)
        acc = None
        for c in range(E_FF // F_CHUNK):
            cols = slice(c * F_CHUNK, (c + 1) * F_CHUNK)
            a = _dot(x, w1_ref[0, :, cols])
            b = _dot(x, w3_ref[0, :, cols])
            part = _dot((_silu(a) * b).astype(BF16), w2_ref[0, cols, :])
            acc = part if acc is None else acc + part
        _store_rows_tiled(o_ref, acc)

    @pl.when(g >= na_ref[0])
    def _():
        o_ref[...] = jnp.zeros_like(o_ref)


def _expert_ffn(tile_expert, n_active, xg, w1, w3, w2):
    rows = pl.BlockSpec((TG * ROW_TILE, LANES), lambda g, te, na: (g, 0))
    w_up = pl.BlockSpec((1, D, E_FF), lambda g, te, na: (te[g], 0, 0), pipeline_mode=pl.Buffered(1))
    w_dn = pl.BlockSpec((1, E_FF, D), lambda g, te, na: (te[g], 0, 0), pipeline_mode=pl.Buffered(1))
    return pl.pallas_call(
        _expert_ffn_kernel,
        out_shape=jax.ShapeDtypeStruct((P_ROWS * ROW_TILE, LANES), F32),
        grid_spec=pltpu.PrefetchScalarGridSpec(
            num_scalar_prefetch=2, grid=(NG,),
            in_specs=[rows, w_up, w_up, w_dn],
            out_specs=rows),
        compiler_params=_params("arbitrary"),
        name="moe_expert_ffn",
    )(tile_expert, n_active, xg, w1, w3, w2)


def _combine_ln_kernel(pos_ref, y_ref, mod_ref, meta_ref, g_ref, b_ref, eo_ref, outp_ref, outs_ref,
                       buf1, buf2, sem):
    i = pl.program_id(0)

    def issue(r, carry):
        t = i * TM + r
        _row_copy(eo_ref, pos_ref[2 * t], buf1, r, sem).start()
        _row_copy(eo_ref, pos_ref[2 * t + 1], buf2, r, sem).start()
        return carry

    lax.fori_loop(0, TM, issue, 0)

    def drain(r, carry):
        _row_copy(eo_ref, 0, buf1, 0, sem).wait()
        _row_copy(eo_ref, 0, buf2, 0, sem).wait()
        return carry

    lax.fori_loop(0, TM, drain, 0)
    m = mod_ref[0]
    meta = meta_ref[...]
    f = meta[:, 4:5] * _load_rows_tiled(buf1, TM) + meta[:, 5:6] * _load_rows_tiled(buf2, TM)
    out = _layer_norm(ALPHA * y_ref[...] + m[5:6] * f, g_ref[...], b_ref[...])

    @pl.when(i < NP_TILES)
    def _():
        outp_ref[...] = out

    @pl.when(i >= NP_TILES)
    def _():
        outs_ref[...] = out


def _combine_ln(pos, y, mods, meta, ln_g, ln_b, eo):
    return pl.pallas_call(
        _combine_ln_kernel,
        out_shape=[jax.ShapeDtypeStruct((T_P, D), F32), jax.ShapeDtypeStruct((T_S, D), F32)],
        grid_spec=pltpu.PrefetchScalarGridSpec(
            num_scalar_prefetch=1, grid=(NT,),
            in_specs=[pl.BlockSpec((TM, D), lambda i, *_: (i, 0)),
                      pl.BlockSpec((1, 6, D), lambda i, *_: (_cond_of_tile(i), 0, 0)),
                      pl.BlockSpec((TM, LANES), lambda i, *_: (i, 0)),
                      pl.BlockSpec((1, D), lambda i, *_: (0, 0)),
                      pl.BlockSpec((1, D), lambda i, *_: (0, 0)),
                      pl.BlockSpec(memory_space=pl.ANY)],
            out_specs=[_prompt_tile_spec(D), _sample_tile_spec(D)],
            scratch_shapes=[pltpu.VMEM((TM * ROW_TILE, LANES), F32),
                            pltpu.VMEM((TM * ROW_TILE, LANES), F32),
                            pltpu.SemaphoreType.DMA(())]),
        compiler_params=_params("arbitrary"),
        name="moe_combine_ln",
    )(pos, y, mods, meta, ln_g.reshape(1, D), ln_b.reshape(1, D), eo)


def _moe_ln(y, mods, router_w, router_b, w1, w3, w2, ln_g, ln_b):
    h_tiled, meta, counts = _router(y, mods, router_w, router_b)
    cnt = counts[0, :N_EXPERTS].astype(jnp.int32)
    tiles = (cnt + TG - 1) // TG
    tile_end = jnp.cumsum(tiles)
    offs = (tile_end - tiles) * TG
    e1 = meta[:, 0].astype(jnp.int32)
    e2 = meta[:, 1].astype(jnp.int32)
    pos = jnp.stack([offs[e1] + meta[:, 2].astype(jnp.int32), offs[e2] + meta[:, 3].astype(jnp.int32)], axis=1)
    pos = pos.reshape(2 * T)
    tile_expert = jnp.sum((jnp.arange(NG)[:, None] >= tile_end[None, :]).astype(jnp.int32), axis=1)
    tile_expert = jnp.minimum(tile_expert, N_EXPERTS - 1)
    n_active = tile_end[-1:].astype(jnp.int32)
    xg = _dispatch(pos, h_tiled, jnp.zeros((P_ROWS * ROW_TILE, LANES), F32))
    eo = _expert_ffn(tile_expert, n_active, xg, w1.astype(BF16), w3.astype(BF16), w2.astype(BF16))
    return _combine_ln(pos, y, mods, meta, ln_g, ln_b, eo)


def _l0_weight_layouts(w_in, w_q_up, w_kv_up):
    a, b, c = MLA_Q_LORA, MLA_Q_LORA + MLA_KV_LORA, MLA_Q_LORA + MLA_KV_LORA + MLA_ROPE
    k_rope = w_in[:, b:c]
    w_in_r = jnp.concatenate([w_in[:, :b], w_in[:, c:], k_rope, k_rope, k_rope, k_rope], axis=1)
    wq = w_q_up.reshape(MLA_Q_LORA, MLA_HEADS, MLA_NOPE + MLA_ROPE)
    w_q_up_r = jnp.concatenate([wq[:, :, :MLA_NOPE].reshape(MLA_Q_LORA, -1),
                                wq[:, :, MLA_NOPE:].reshape(MLA_Q_LORA, -1)], axis=1)
    wkv = w_kv_up.reshape(MLA_KV_LORA, MLA_HEADS, MLA_NOPE + MLA_V)
    w_kv_up_r = jnp.concatenate([wkv[:, :, :MLA_NOPE].reshape(MLA_KV_LORA, -1),
                                 wkv[:, :, MLA_NOPE:].reshape(MLA_KV_LORA, -1)], axis=1)
    return w_in_r.astype(BF16), w_q_up_r.astype(BF16), w_kv_up_r.astype(BF16)


def kernel(x_prompt, x_sample, cache_l0_mla_ckv, cache_l0_mla_krope, cache_l0_swa_k, cache_l0_swa_v,
           cache_l1_na_k, cache_l1_na_v, c, c_ctx,
           l0_ada_w, l0_ada_b, l0_w_in, l0_mla_q_norm, l0_mla_w_q_up, l0_mla_kv_norm, l0_mla_w_kv_up,
           l0_swa_sink, l0_w_out, l0_ln1_g, l0_ln1_b, l0_ffn_w1, l0_ffn_w3, l0_ffn_w2, l0_ln2_g, l0_ln2_b,
           l1_ada_w, l1_ada_b, l1_w_in, l1_na_rel_bias, l1_w_out, l1_ln1_g, l1_ln1_b,
           l1_moe_router_w, l1_moe_router_b, l1_moe_w1, l1_moe_w3, l1_moe_w2, l1_ln2_g, l1_ln2_b):
    cond = jnp.concatenate([c_ctx[None, :], c, jnp.zeros((8 - N_COND, D), F32)], axis=0)

    mods = _adaln(cond, l0_ada_w, l0_ada_b)
    w_in_r, w_q_up_r, w_kv_up_r = _l0_weight_layouts(l0_w_in, l0_mla_w_q_up, l0_mla_w_kv_up)
    y, qn, qr, ckv, kn, vm, kr4, sq, sk, sv = _l0_in(
        x_prompt.reshape(T_P, D), x_sample.reshape(T_S, D), mods, w_in_r, l0_mla_q_norm, w_q_up_r,
        l0_mla_kv_norm, w_kv_up_r, _rope_tables(MLA_ROPE, 256), _rope_tables(HEAD_DIM, 512))
    knc, vc = _kv_up(cache_l0_mla_ckv.reshape(DEC_BATCH * PAST, MLA_KV_LORA), w_kv_up_r)
    kr4c = jnp.tile(cache_l0_mla_krope.reshape(DEC_BATCH * PAST, MLA_ROPE), (1, 4))
    skc = cache_l0_swa_k.reshape(DEC_BATCH * PAST, SWA_KV_HEADS * HEAD_DIM)
    svc = cache_l0_swa_v.reshape(DEC_BATCH * PAST, SWA_KV_HEADS * HEAD_DIM)
    o_p = _l0_attn_prompt(l0_swa_sink, qn, qr, kn, vm, kr4, sq, sk, sv)
    o_s = _l0_attn_sample(l0_swa_sink, qn, qr, kn, vm, kr4, sq, sk, sv, knc, vc, kr4c, skc, svc)
    y = _out_ln(o_p, o_s, l0_w_out.astype(BF16), y, mods, l0_ln1_g, l0_ln1_b)
    y = _ffn_ln(y, mods, l0_ffn_w1.astype(BF16), l0_ffn_w3.astype(BF16), l0_ffn_w2.astype(BF16),
                l0_ln2_g, l0_ln2_b)
    new_ckv = ckv[:T_P].reshape(BATCH, SEQ, MLA_KV_LORA)
    new_krope = kr4[:T_P, :MLA_ROPE].reshape(BATCH, SEQ, MLA_ROPE)
    new_sk = sk[:T_P].reshape(BATCH, SEQ, SWA_KV_HEADS, HEAD_DIM)
    new_sv = sv[:T_P].reshape(BATCH, SEQ, SWA_KV_HEADS, HEAD_DIM)

    mods = _adaln(cond, l1_ada_w, l1_ada_b)
    q, k_p, v_p, k_s, v_s = _l1_in(y, mods, l1_w_in.astype(BF16))
    o_p = _l1_attn_prompt(q, k_p, v_p)
    o_s = _l1_attn_sample(q, k_s, v_s, cache_l1_na_k.reshape(DEC_BATCH, PAST, D),
                          cache_l1_na_v.reshape(DEC_BATCH, PAST, D), _na_bias_rows(l1_na_rel_bias))
    y = _out_ln(o_p, o_s, l1_w_out.astype(BF16), y, mods, l1_ln1_g, l1_ln1_b)
    y_p, y_s = _moe_ln(y, mods, l1_moe_router_w, l1_moe_router_b, l1_moe_w1, l1_moe_w3, l1_moe_w2,
                       l1_ln2_g, l1_ln2_b)
    new_k = k_p.reshape(BATCH, SEQ, NA_HEADS, HEAD_DIM)
    new_v = v_p.reshape(BATCH, SEQ, NA_HEADS, HEAD_DIM)

    return (y_p.reshape(BATCH, SEQ, D), y_s.reshape(DEC_BATCH, DEC_SEQ, D),
            new_ckv, new_krope, new_sk, new_sv, new_k, new_v)
```

```python
import functools

import jax
import jax.numpy as jnp
from jax import lax
from jax.experimental import pallas as pl
from jax.experimental.pallas import tpu as pltpu

F32 = jnp.float32
BF16 = jnp.bfloat16

D = 1024
BATCH, SEQ = 32, 256
DEC_BATCH, DEC_SEQ = 2, 1024
PAST = 256
GRID_W = 64
T_P = BATCH * SEQ
T_S = DEC_BATCH * DEC_SEQ
T = T_P + T_S
N_COND = 1 + DEC_BATCH

MLA_HEADS, MLA_Q_LORA, MLA_KV_LORA, MLA_NOPE, MLA_ROPE, MLA_V = 8, 384, 256, 64, 32, 64
SWA_HEADS, SWA_KV_HEADS, SWA_WINDOW, HEAD_DIM = 8, 2, 128, 64
NA_HEADS, NA_WIN_ROWS, NA_WIN_COLS = 16, 8, 16
D_FF, N_EXPERTS, E_FF = 2816, 8, 3584
ROPE_THETA = 10000.0
LN_EPS, RMS_EPS = 1e-5, 1e-6
NEG = -1e30
ALPHA = 4.0 ** 0.25

LANES = 128
TM = 512
NT = T // TM
NP_TILES = T_P // TM
TILES_PER_SAMPLE = DEC_SEQ // TM
TG = 512
P_ROWS = 2 * T + N_EXPERTS * TG
NG = P_ROWS // TG
F_CHUNK = 896
VMEM_LIMIT = 56 * 1024 * 1024


def _params(*sem):
    return pltpu.CompilerParams(dimension_semantics=sem, vmem_limit_bytes=VMEM_LIMIT)


def _const_spec(shape, single_buffer=False):
    if single_buffer:
        return pl.BlockSpec(shape, lambda *_: (0,) * len(shape), pipeline_mode=pl.Buffered(1))
    return pl.BlockSpec(shape, lambda *_: (0,) * len(shape))


def _cast_specs(shape, steps):
    rows, width = shape
    spec = pl.BlockSpec((rows // steps, width), lambda i, *_: (jnp.minimum(i, steps - 1), 0))
    return spec, jax.ShapeDtypeStruct(shape, BF16)


def _cast_block(i, steps, src_ref, dst_ref):
    @pl.when(i < steps)
    def _():
        dst_ref[...] = src_ref[...].astype(BF16)


ROW_TILE = D // LANES


def _store_rows_tiled(ref, x):
    n = x.shape[0]
    for k in range(ROW_TILE):
        ref[pl.ds(k, n, stride=ROW_TILE), :] = x[:, k * LANES:(k + 1) * LANES]


def _load_rows_tiled(ref, n):
    return jnp.concatenate([ref[pl.ds(k, n, stride=ROW_TILE), :] for k in range(ROW_TILE)], axis=1)


def _cond_of_tile(i):
    return jnp.maximum((i - (NP_TILES - TILES_PER_SAMPLE)) // TILES_PER_SAMPLE, 0)


def _mod_spec():
    return pl.BlockSpec((1, 6, D), lambda i: (_cond_of_tile(i), 0, 0))


def _row_spec(width, rows=TM):
    return pl.BlockSpec((rows, width), lambda i: (i, 0))


def _dot(a, b):
    return jnp.dot(a, b, preferred_element_type=F32)


def _dot_nt(a, b):
    return lax.dot_general(a, b, (((1,), (1,)), ((), ())), preferred_element_type=F32)


def _layer_norm(r, g, b):
    mu = jnp.mean(r, axis=-1, keepdims=True)
    d = r - mu
    var = jnp.mean(d * d, axis=-1, keepdims=True)
    return d * lax.rsqrt(var + LN_EPS) * g + b


def _rms_norm(x, g):
    return x * lax.rsqrt(jnp.mean(x * x, axis=-1, keepdims=True) + RMS_EPS) * g


def _silu(x):
    return x * jax.nn.sigmoid(x)


def _ada_kernel(c_ref, w_ref, b_ref, o_ref):
    s = _silu(c_ref[...]).astype(BF16)
    o_ref[...] = _dot(s, w_ref[...].astype(BF16)) + b_ref[...]


def _adaln(cond, ada_w, ada_b):
    nb = 1536
    out = pl.pallas_call(
        _ada_kernel,
        out_shape=jax.ShapeDtypeStruct((8, 6 * D), F32),
        grid=(6 * D // nb,),
        in_specs=[_const_spec((8, D)), pl.BlockSpec((D, nb), lambda j: (0, j)),
                  pl.BlockSpec((1, nb), lambda j: (0, j))],
        out_specs=pl.BlockSpec((8, nb), lambda j: (0, j)),
        compiler_params=_params("arbitrary"),
        name="adaln",
    )(cond, ada_w, ada_b.reshape(1, 6 * D))
    return out[:N_COND].reshape(N_COND, 6, D)


def _rope_tables(head_dim, width):
    half = head_dim // 2
    nf = half // 2
    lane = jnp.arange(width)
    d = lane % head_dim
    dd = d % half
    f = dd % nf
    inv = ROPE_THETA ** (-f.astype(F32) / nf)
    t = jnp.arange(DEC_SEQ)
    pos = jnp.where((d // half)[None, :] == 0, (t // GRID_W)[:, None], (t % GRID_W)[:, None])
    ang = pos.astype(F32) * inv[None, :]
    cos, sin = jnp.cos(ang), jnp.sin(ang)
    first = (dd < nf)[None, :]
    return cos, jnp.where(first, -sin, 0.0), jnp.where(first, 0.0, sin)


def _rope(x, cos, sin_up, sin_dn, nf):
    w = x.shape[-1]
    return x * cos + pltpu.roll(x, w - nf, 1) * sin_up + pltpu.roll(x, nf, 1) * sin_dn


L0_COLS = MLA_Q_LORA + MLA_KV_LORA + 512 + 128 + 128 + 128


def _l0_in_kernel(xp_ref, xs_ref, mod_ref, win_ref, qn_ref, wq_ref, kvn_ref, wkv_ref,
                  c8_ref, su8_ref, sd8_ref, c16_ref, su16_ref, sd16_ref,
                  y_o, qnope_o, qrope_o, ckv_o, knope_o, vmla_o, kr4_o, sq_o, sk_o, sv_o):
    i = pl.program_id(0)
    m = mod_ref[0]
    x = jnp.where(i < NP_TILES, xp_ref[...], xs_ref[...])
    y_o[...] = x
    h = (x * (1.0 + m[1:2]) + m[0:1]).astype(BF16)
    z = _dot(h, win_ref[...])
    q_lat = z[:, 0:384]
    kv_lat = z[:, 384:640]
    sq = z[:, 640:1152]
    sk = z[:, 1152:1280]
    sv = z[:, 1280:1408]
    kr4 = z[:, 1408:1536]
    q = _dot(_rms_norm(q_lat, qn_ref[...]).astype(BF16), wq_ref[...])
    c_kv = _rms_norm(kv_lat, kvn_ref[...])
    kv = _dot(c_kv.astype(BF16), wkv_ref[...])
    qnope_o[...] = q[:, 0:512].astype(BF16)
    ckv_o[...] = c_kv
    knope_o[...] = kv[:, 0:512].astype(BF16)
    vmla_o[...] = kv[:, 512:1024].astype(BF16)
    sv_o[...] = sv
    q_rope = q[:, 512:768]

    @pl.when(i < NP_TILES)
    def _():
        qrope_o[...] = q_rope.astype(BF16)
        kr4_o[...] = kr4
        sq_o[...] = sq.astype(BF16)
        sk_o[...] = sk

    @pl.when(i >= NP_TILES)
    def _():
        c8, su8, sd8 = c8_ref[...], su8_ref[...], sd8_ref[...]
        c16, su16, sd16 = c16_ref[...], su16_ref[...], sd16_ref[...]
        qrope_o[...] = _rope(q_rope, c8, su8, sd8, 8).astype(BF16)
        kr4_o[...] = _rope(kr4, c8[:, :128], su8[:, :128], sd8[:, :128], 8)
        sq_o[...] = _rope(sq, c16, su16, sd16, 16).astype(BF16)
        sk_o[...] = _rope(sk, c16[:, :128], su16[:, :128], sd16[:, :128], 16)


def _prompt_tile_spec(width):
    return pl.BlockSpec((TM, width), lambda i, *_: (jnp.minimum(i, NP_TILES - 1), 0))


def _sample_tile_spec(width):
    return pl.BlockSpec((TM, width), lambda i, *_: (jnp.maximum(i - NP_TILES, 0), 0))


def _l0_in(x_prompt, x_sample, mods, w_in_r, q_norm, w_q_up_r, kv_norm, w_kv_up_r, tabs8, tabs16):
    def tab_spec(width):
        return pl.BlockSpec((TM, width), lambda i: (jnp.maximum(i - NP_TILES, 0) % TILES_PER_SAMPLE, 0))

    out_shape = [
        jax.ShapeDtypeStruct((T, D), F32),
        jax.ShapeDtypeStruct((T, 512), BF16),
        jax.ShapeDtypeStruct((T, 256), BF16),
        jax.ShapeDtypeStruct((T, 256), F32),
        jax.ShapeDtypeStruct((T, 512), BF16),
        jax.ShapeDtypeStruct((T, 512), BF16),
        jax.ShapeDtypeStruct((T, 128), F32),
        jax.ShapeDtypeStruct((T, 512), BF16),
        jax.ShapeDtypeStruct((T, 128), F32),
        jax.ShapeDtypeStruct((T, 128), F32),
    ]
    return pl.pallas_call(
        _l0_in_kernel,
        out_shape=out_shape,
        grid=(NT,),
        in_specs=[_prompt_tile_spec(D), _sample_tile_spec(D), _mod_spec(), _const_spec((D, L0_COLS)),
                  _const_spec((1, MLA_Q_LORA)), _const_spec((MLA_Q_LORA, 768)),
                  _const_spec((1, MLA_KV_LORA)), _const_spec((MLA_KV_LORA, 1024)),
                  tab_spec(256), tab_spec(256), tab_spec(256),
                  tab_spec(512), tab_spec(512), tab_spec(512)],
        out_specs=[_row_spec(s.shape[1]) for s in out_shape],
        compiler_params=_params("arbitrary"),
        name="l0_in_proj",
    )(x_prompt, x_sample, mods, w_in_r, q_norm.reshape(1, -1), w_q_up_r, kv_norm.reshape(1, -1), w_kv_up_r,
      *tabs8, *tabs16)


def _kv_up_kernel(c_ref, w_ref, k_o, v_o):
    kv = _dot(c_ref[...].astype(BF16), w_ref[...])
    k_o[...] = kv[:, 0:512].astype(BF16)
    v_o[...] = kv[:, 512:1024].astype(BF16)


def _kv_up(ckv, w_kv_up_r):
    n = ckv.shape[0]
    return pl.pallas_call(
        _kv_up_kernel,
        out_shape=[jax.ShapeDtypeStruct((n, 512), BF16)] * 2,
        grid=(1,),
        in_specs=[_const_spec((n, MLA_KV_LORA)), _const_spec((MLA_KV_LORA, 1024))],
        out_specs=[_const_spec((n, 512))] * 2,
        compiler_params=_params("arbitrary"),
        name="l0_ctx_kv_up",
    )(ckv, w_kv_up_r)


def _lane():
    return lax.broadcasted_iota(jnp.int32, (1, LANES), 1)


def _attend(scores, values, extra_logit=None):
    m = jnp.max(scores[0], axis=1, keepdims=True)
    for s in scores[1:]:
        m = jnp.maximum(m, jnp.max(s, axis=1, keepdims=True))
    if extra_logit is not None:
        m = jnp.maximum(m, extra_logit)
    den = None
    acc = None
    for s, v in zip(scores, values):
        e = jnp.exp(s - m)
        d = jnp.sum(e, axis=1, keepdims=True)
        a = _dot(e.astype(BF16), v)
        den = d if den is None else den + d
        acc = a if acc is None else acc + a
    if extra_logit is not None:
        den = den + jnp.exp(extra_logit - m)
    return acc / den


def _head_pair(q2, keys, values, scale, masks=None, biases=None, sinks=None, q_extra=None):
    lo = _lane() < 64
    zero = jnp.zeros_like(q2)
    outs = []
    for hh in range(2):
        qm = jnp.where(lo if hh == 0 else jnp.logical_not(lo), q2, zero)
        if q_extra is not None:
            qm = jnp.concatenate([qm, q_extra[hh]], axis=1)
        scores = []
        for n, k in enumerate(keys):
            s = _dot_nt(qm, k) * scale
            if biases is not None and biases[n] is not None:
                s = s + biases[n][hh]
            if masks is not None and masks[n] is not None:
                s = jnp.where(masks[n], s, NEG)
            scores.append(s)
        outs.append(_attend(scores, values, None if sinks is None else sinks[hh]))
    return jnp.where(lo, outs[0], outs[1])


def _dup_halves(x):
    lo = _lane() < 64
    sw = pltpu.roll(x, 64, 1)
    return jnp.where(lo, x, sw), jnp.where(lo, sw, x)


MLA_SCALE = (MLA_NOPE + MLA_ROPE) ** -0.5
HD_SCALE = HEAD_DIM ** -0.5


def _mla_pairs(qn_ref, qr_ref, key_sets, o_ref):
    lane = _lane()
    for j in range(MLA_HEADS // 2):
        cols = slice(128 * j, 128 * (j + 1))
        qr = qr_ref[:, 128 * (j // 2):128 * (j // 2 + 1)]
        zero = jnp.zeros_like(qr)
        q_extra = [jnp.where((lane // MLA_ROPE) == ((2 * j + hh) % 4), qr, zero) for hh in range(2)]
        keys = [jnp.concatenate([kn[:, cols], kr4], axis=1) for kn, kr4, _ in key_sets]
        values = [v[:, cols] for _, _, v in key_sets]
        o = _head_pair(qn_ref[:, cols], keys, values, MLA_SCALE, q_extra=q_extra)
        o_ref[:, cols] = o.astype(BF16)


def _swa_pairs(sink_ref, sq_ref, key_sets, masks, o_ref):
    kd = [[a.astype(BF16) for a in _dup_halves(k)] for k, _ in key_sets]
    vd = [[a.astype(BF16) for a in _dup_halves(v)] for _, v in key_sets]
    for g in range(SWA_KV_HEADS):
        for u in range(2):
            c = 2 * g + u
            cols = slice(128 * c, 128 * (c + 1))
            sinks = [sink_ref[2 * c + hh] for hh in range(2)]
            o = _head_pair(sq_ref[:, cols], [k[g] for k in kd], [v[g] for v in vd], HD_SCALE,
                           masks=masks, sinks=sinks)
            o_ref[:, 512 + 128 * c:512 + 128 * (c + 1)] = o.astype(BF16)


def _l0_attn_prompt_kernel(sink_ref, qn_ref, qr_ref, kn_ref, v_ref, kr4_ref, sq_ref, sk_ref, sv_ref, wf_ref,
                           o_ref, wb_ref):
    _mla_pairs(qn_ref, qr_ref, [(kn_ref, kr4_ref[...].astype(BF16), v_ref)], o_ref)
    _swa_pairs(sink_ref, sq_ref, [(sk_ref[...], sv_ref[...])], None, o_ref)
    _cast_block(pl.program_id(0), BATCH, wf_ref, wb_ref)


def _l0_attn_prompt(sink, qn, qr, kn, vm, kr4, sq, sk, sv, w_f32):
    spec = lambda w: pl.BlockSpec((SEQ, w), lambda b, *_: (b, 0))
    w_spec, wb_shape = _cast_specs(w_f32.shape, BATCH)
    return pl.pallas_call(
        _l0_attn_prompt_kernel,
        out_shape=[jax.ShapeDtypeStruct((T_P, D), BF16), wb_shape],
        grid_spec=pltpu.PrefetchScalarGridSpec(
            num_scalar_prefetch=1, grid=(BATCH,),
            in_specs=[spec(512), spec(256), spec(512), spec(512), spec(128), spec(512), spec(128), spec(128),
                      w_spec],
            out_specs=[spec(D), w_spec]),
        compiler_params=_params("arbitrary"),
        name="l0_attn_prompt",
    )(sink, qn, qr, kn, vm, kr4, sq, sk, sv, w_f32)


TQ_S = 256


def _l0_attn_sample_kernel(sink_ref, qn_ref, qr_ref, sq_ref, kn_ref, v_ref, kr4_ref, sk_ref, sv_ref,
                           knc_ref, vc_ref, kr4c_ref, skc_ref, svc_ref, o_ref):
    i = pl.program_id(1)
    _mla_pairs(qn_ref, qr_ref,
               [(knc_ref, kr4c_ref[...].astype(BF16), vc_ref), (kn_ref, kr4_ref[...].astype(BF16), v_ref)],
               o_ref)
    span = TQ_S + 2 * SWA_WINDOW
    start = pl.multiple_of(jnp.clip(i * TQ_S - SWA_WINDOW, 0, DEC_SEQ - span), SWA_WINDOW)
    qpos = i * TQ_S + lax.broadcasted_iota(jnp.int32, (TQ_S, span), 0)
    kpos = start + lax.broadcasted_iota(jnp.int32, (TQ_S, span), 1)
    band = jnp.abs(qpos - kpos) <= SWA_WINDOW
    keys = pl.ds(start, span)
    _swa_pairs(sink_ref, sq_ref, [(skc_ref[...], svc_ref[...]), (sk_ref[keys, :], sv_ref[keys, :])],
               [None, band], o_ref)


def _l0_attn_sample(sink, qn, qr, kn, vm, kr4, sq, sk, sv, knc, vc, kr4c, skc, svc):
    nq = DEC_SEQ // TQ_S
    qspec = lambda w: pl.BlockSpec((TQ_S, w), lambda b, i, *_: (T_P // TQ_S + b * nq + i, 0))
    kspec = lambda w: pl.BlockSpec((DEC_SEQ, w), lambda b, i, *_: (T_P // DEC_SEQ + b, 0))
    cspec = lambda w: pl.BlockSpec((PAST, w), lambda b, i, *_: (b, 0))
    return pl.pallas_call(
        _l0_attn_sample_kernel,
        out_shape=jax.ShapeDtypeStruct((T_S, D), BF16),
        grid_spec=pltpu.PrefetchScalarGridSpec(
            num_scalar_prefetch=1, grid=(DEC_BATCH, nq),
            in_specs=[qspec(512), qspec(256), qspec(512),
                      kspec(512), kspec(512), kspec(128), kspec(128), kspec(128),
                      cspec(512), cspec(512), cspec(128), cspec(128), cspec(128)],
            out_specs=pl.BlockSpec((TQ_S, D), lambda b, i, *_: (b * nq + i, 0))),
        compiler_params=_params("arbitrary", "arbitrary"),
        name="l0_attn_sample",
    )(sink, qn, qr, sq, kn, vm, kr4, sk, sv, knc, vc, kr4c, skc, svc)


def _out_ln_kernel(op_ref, os_ref, w_ref, y_ref, mod_ref, g_ref, b_ref, out_ref):
    i = pl.program_id(0)
    m = mod_ref[0]

    def finish(o):
        r = ALPHA * y_ref[...] + m[2:3] * _dot(o, w_ref[...])
        out_ref[...] = _layer_norm(r, g_ref[...], b_ref[...])

    @pl.when(i < NP_TILES)
    def _():
        finish(op_ref[...])

    @pl.when(i >= NP_TILES)
    def _():
        finish(os_ref[...])


def _out_ln(o_prompt, o_sample, w_out, y, mods, ln_g, ln_b):
    return pl.pallas_call(
        _out_ln_kernel,
        out_shape=jax.ShapeDtypeStruct((T, D), F32),
        grid=(NT,),
        in_specs=[_prompt_tile_spec(D), _sample_tile_spec(D),
                  _const_spec((D, D)), _row_spec(D), _mod_spec(),
                  _const_spec((1, D)), _const_spec((1, D))],
        out_specs=_row_spec(D),
        compiler_params=_params("arbitrary"),
        name="out_proj_ln",
    )(o_prompt, o_sample, w_out, y, mods, ln_g.reshape(1, D), ln_b.reshape(1, D))


FFN_CHUNK = D_FF // 2


def _ffn_ln_kernel(y_ref, mod_ref, w1_ref, w3_ref, w2_ref, g_ref, b_ref, out_ref):
    m = mod_ref[0]
    y = y_ref[...]
    h = (y * (1.0 + m[4:5]) + m[3:4]).astype(BF16)
    acc = None
    for c in range(D_FF // FFN_CHUNK):
        cols = slice(c * FFN_CHUNK, (c + 1) * FFN_CHUNK)
        a = _dot(h, w1_ref[:, cols])
        g = _dot(h, w3_ref[:, cols])
        part = _dot((_silu(a) * g).astype(BF16), w2_ref[cols, :])
        acc = part if acc is None else acc + part
    out_ref[...] = _layer_norm(ALPHA * y + m[5:6] * acc, g_ref[...], b_ref[...])


def _ffn_ln(y, mods, w1, w3, w2, ln_g, ln_b):
    return pl.pallas_call(
        _ffn_ln_kernel,
        out_shape=jax.ShapeDtypeStruct((T, D), F32),
        grid=(NT,),
        in_specs=[_row_spec(D), _mod_spec(), _const_spec((D, D_FF), True), _const_spec((D, D_FF), True),
                  _const_spec((D_FF, D), True), _const_spec((1, D)), _const_spec((1, D))],
        out_specs=_row_spec(D),
        compiler_params=_params("arbitrary"),
        name="ffn_ln",
    )(y, mods, w1, w3, w2, ln_g.reshape(1, D), ln_b.reshape(1, D))


def _l1_in_kernel(y_ref, mod_ref, w_ref, wf_ref, q_o, kp_o, vp_o, ks_o, vs_o, wb_ref):
    i = pl.program_id(0)
    _cast_block(i, NP_TILES, wf_ref, wb_ref)
    m = mod_ref[0]
    h = (y_ref[...] * (1.0 + m[1:2]) + m[0:1]).astype(BF16)
    z = _dot(h, w_ref[...])
    q_o[...] = z[:, 0:D].astype(BF16)

    @pl.when(i < NP_TILES)
    def _():
        kp_o[...] = z[:, D:2 * D]
        vp_o[...] = z[:, 2 * D:3 * D]

    @pl.when(i >= NP_TILES)
    def _():
        ks_o[...] = z[:, D:2 * D].astype(BF16)
        vs_o[...] = z[:, 2 * D:3 * D].astype(BF16)


def _l1_in(y, mods, w_in, w_f32):
    w_spec, wb_shape = _cast_specs(w_f32.shape, NP_TILES)
    return pl.pallas_call(
        _l1_in_kernel,
        out_shape=[jax.ShapeDtypeStruct((T, D), BF16),
                   jax.ShapeDtypeStruct((T_P, D), F32), jax.ShapeDtypeStruct((T_P, D), F32),
                   jax.ShapeDtypeStruct((T_S, D), BF16), jax.ShapeDtypeStruct((T_S, D), BF16), wb_shape],
        grid=(NT,),
        in_specs=[_row_spec(D), _mod_spec(), _const_spec((D, 3 * D)), w_spec],
        out_specs=[_row_spec(D), _prompt_tile_spec(D), _prompt_tile_spec(D),
                   _sample_tile_spec(D), _sample_tile_spec(D), w_spec],
        compiler_params=_params("arbitrary"),
        name="l1_in_proj",
    )(y, mods, w_in, w_f32)


def _l1_attn_prompt_kernel(q_ref, k_ref, v_ref, wf_ref, o_ref, wb_ref):
    for j in range(NA_HEADS // 2):
        cols = slice(128 * j, 128 * (j + 1))
        o = _head_pair(q_ref[:, cols], [k_ref[:, cols].astype(BF16)], [v_ref[:, cols].astype(BF16)], HD_SCALE)
        o_ref[:, cols] = o.astype(BF16)
    _cast_block(pl.program_id(0), BATCH, wf_ref, wb_ref)


def _l1_attn_prompt(q, k, v, w_f32):
    spec = pl.BlockSpec((SEQ, D), lambda b: (b, 0))
    w_spec, wb_shape = _cast_specs(w_f32.shape, BATCH)
    return pl.pallas_call(
        _l1_attn_prompt_kernel,
        out_shape=[jax.ShapeDtypeStruct((T_P, D), BF16), wb_shape],
        grid=(BATCH,),
        in_specs=[spec, spec, spec, w_spec],
        out_specs=[spec, w_spec],
        compiler_params=_params("arbitrary"),
        name="l1_attn_prompt",
    )(q, k, v, w_f32)


NA_ROWS = DEC_SEQ // GRID_W
NA_TILE_ROWS = 4
NA_TQ = NA_TILE_ROWS * GRID_W
NA_SPAN = NA_WIN_ROWS + NA_TILE_ROWS
NA_DR = 2 * NA_WIN_ROWS - 1
COL_SPAN = 2 * GRID_W - 1


def _na_span_start(t):
    first = max(0, min(t * NA_TILE_ROWS - NA_WIN_ROWS // 2, NA_ROWS - NA_WIN_ROWS))
    return min(first, NA_ROWS - NA_SPAN)


def _l1_attn_sample_kernel(q_ref, k_ref, v_ref, kc_ref, vc_ref, ext_ref, o_ref):
    kc = kc_ref[0].astype(BF16)
    vc = vc_ref[0].astype(BF16)
    lo = _lane() < GRID_W
    qcol = lax.broadcasted_iota(jnp.int32, (GRID_W, LANES), 0)
    kcol = lax.broadcasted_iota(jnp.int32, (GRID_W, LANES), 1) % GRID_W
    cs = jnp.clip(qcol - NA_WIN_COLS // 2, 0, GRID_W - NA_WIN_COLS)
    col_ok = jnp.logical_and(kcol >= cs, kcol < cs + NA_WIN_COLS)
    neg = jnp.full((GRID_W, LANES), NEG, F32)

    def bias_tile(hh, d, half):
        x = jnp.broadcast_to(ext_ref[hh, d:d + 1, :], (GRID_W, LANES))
        shift = (LANES - COL_SPAN // 2 + GRID_W * half) % LANES
        return jnp.where(col_ok, pltpu.roll(x, shift, 1, stride=1, stride_axis=0), neg)

    tiles = [[[bias_tile(hh, d, half) for half in range(2)] for d in range(NA_DR)] for hh in range(2)]

    for t in range(NA_ROWS // NA_TILE_ROWS):
        ws = _na_span_start(t)
        keys = slice(ws * GRID_W, (ws + NA_SPAN) * GRID_W)
        kw = k_ref[keys, :].astype(BF16)
        vw = v_ref[keys, :].astype(BF16)
        bias = []
        for hh in range(2):
            rows = []
            for rr in range(NA_TILE_ROWS):
                r = t * NA_TILE_ROWS + rr
                rs = max(0, min(r - NA_WIN_ROWS // 2, NA_ROWS - NA_WIN_ROWS))
                blocks = []
                for u in range(NA_SPAN // 2):
                    halves = []
                    for half in range(2):
                        kr = ws + 2 * u + half
                        ok = rs <= kr < rs + NA_WIN_ROWS
                        halves.append(tiles[hh][kr - r + NA_WIN_ROWS - 1][half] if ok else neg)
                    blocks.append(jnp.where(lo, halves[0], halves[1]))
                rows.append(jnp.concatenate(blocks, axis=1))
            bias.append(jnp.concatenate(rows, axis=0))
        qrows = slice(t * NA_TQ, (t + 1) * NA_TQ)
        o = _head_pair(q_ref[qrows, :], [kw, kc], [vw, vc], HD_SCALE, biases=[bias, None])
        o_ref[qrows, :] = o.astype(BF16)


def _l1_attn_sample(q, k_s, v_s, kc, vc, ext):
    lat = pl.BlockSpec((DEC_SEQ, LANES), lambda j, b: (b, j))
    ctx = pl.BlockSpec((1, PAST, LANES), lambda j, b: (b, 0, j))
    return pl.pallas_call(
        _l1_attn_sample_kernel,
        out_shape=jax.ShapeDtypeStruct((T_S, D), BF16),
        grid=(NA_HEADS // 2, DEC_BATCH),
        in_specs=[pl.BlockSpec((DEC_SEQ, LANES), lambda j, b: (T_P // DEC_SEQ + b, j)), lat, lat, ctx, ctx,
                  pl.BlockSpec((2, NA_DR, LANES), lambda j, b: (j, 0, 0))],
        out_specs=lat,
        compiler_params=_params("arbitrary", "arbitrary"),
        name="l1_attn_sample",
    )(q, k_s, v_s, kc, vc, ext)


def _na_bias_rows(rel_bias):
    rb = rel_bias.astype(F32)
    n_lo = GRID_W - 1 - (NA_WIN_COLS - 1)
    n_hi = LANES - n_lo - rb.shape[-1]
    return jnp.concatenate([jnp.repeat(rb[..., :1], n_lo, axis=-1), rb,
                            jnp.repeat(rb[..., -1:], n_hi, axis=-1)], axis=-1)


def _router_kernel(y_ref, mod_ref, rw_ref, rb_ref, h_o, meta_o, cnt_o, carry_ref):
    i = pl.program_id(0)

    @pl.when(i == 0)
    def _():
        carry_ref[...] = jnp.zeros_like(carry_ref)

    m = mod_ref[0]
    h = y_ref[...] * (1.0 + m[4:5]) + m[3:4]
    _store_rows_tiled(h_o, h)
    logits = jnp.dot(h, rw_ref[...], preferred_element_type=F32, precision=lax.Precision.HIGHEST) + rb_ref[...]
    lane = lax.broadcasted_iota(jnp.int32, (TM, LANES), 1).astype(F32)
    m1 = jnp.max(logits, axis=1, keepdims=True)
    i1 = jnp.min(jnp.where(logits == m1, lane, float(LANES)), axis=1, keepdims=True)
    sel1 = lane == i1
    rest = jnp.where(sel1, -jnp.inf, logits)
    m2 = jnp.max(rest, axis=1, keepdims=True)
    i2 = jnp.min(jnp.where(rest == m2, lane, float(LANES)), axis=1, keepdims=True)
    sel2 = lane == i2
    e2 = jnp.exp(m2 - m1)
    w1 = 1.0 / (1.0 + e2)
    w2 = e2 / (1.0 + e2)
    sel = jnp.logical_or(sel1, sel2)
    rr = lax.broadcasted_iota(jnp.int32, (TM, TM), 0)
    cc = lax.broadcasted_iota(jnp.int32, (TM, TM), 1)
    tri = jnp.where(cc < rr, 1.0, 0.0).astype(BF16)
    ahead = _dot(tri, jnp.where(sel, 1.0, 0.0).astype(BF16)) + carry_ref[...]
    r1 = jnp.sum(jnp.where(sel1, ahead, 0.0), axis=1, keepdims=True)
    r2 = jnp.sum(jnp.where(sel2, ahead, 0.0), axis=1, keepdims=True)
    meta = jnp.where(lane == 0, i1, 0.0)
    meta = jnp.where(lane == 1, i2, meta)
    meta = jnp.where(lane == 2, r1, meta)
    meta = jnp.where(lane == 3, r2, meta)
    meta = jnp.where(lane == 4, w1, meta)
    meta = jnp.where(lane == 5, w2, meta)
    meta_o[...] = meta
    carry_ref[...] = carry_ref[...] + jnp.sum(jnp.where(sel, 1.0, 0.0), axis=0, keepdims=True)
    cnt_o[...] = carry_ref[...]


def _router(y, mods, router_w, router_b):
    rw = jnp.zeros((D, LANES), F32).at[:, :N_EXPERTS].set(router_w)
    rb = jnp.full((1, LANES), NEG, F32).at[0, :N_EXPERTS].set(router_b)
    return pl.pallas_call(
        _router_kernel,
        out_shape=[jax.ShapeDtypeStruct((T * ROW_TILE, LANES), F32),
                   jax.ShapeDtypeStruct((T, LANES), F32),
                   jax.ShapeDtypeStruct((1, LANES), F32)],
        grid=(NT,),
        in_specs=[_row_spec(D), _mod_spec(), _const_spec((D, LANES)), _const_spec((1, LANES))],
        out_specs=[_row_spec(LANES, TM * ROW_TILE), _row_spec(LANES), _const_spec((1, LANES))],
        scratch_shapes=[pltpu.VMEM((1, LANES), F32)],
        compiler_params=_params("arbitrary"),
        name="moe_router",
    )(y, mods, rw, rb)


DMA_UNROLL = 8


def _row_copy(src_ref, src_row, dst_ref, dst_row, sem):
    return pltpu.make_async_copy(src_ref.at[pl.ds(pl.multiple_of(src_row * ROW_TILE, ROW_TILE), ROW_TILE)],
                                 dst_ref.at[pl.ds(pl.multiple_of(dst_row * ROW_TILE, ROW_TILE), ROW_TILE)], sem)


def _dispatch_kernel(pos_ref, h_ref, xg_in_ref, xg_ref, sem):
    del xg_in_ref
    i = pl.program_id(0)

    def issue(r, carry):
        t = i * TM + r
        _row_copy(h_ref, r, xg_ref, pos_ref[2 * t], sem).start()
        _row_copy(h_ref, r, xg_ref, pos_ref[2 * t + 1], sem).start()
        return carry

    lax.fori_loop(0, TM, issue, 0, unroll=DMA_UNROLL)
    for _ in range(2):
        pltpu.make_async_copy(h_ref, xg_ref.at[pl.ds(0, TM * ROW_TILE)], sem).wait()


def _dispatch(pos, h_tiled, xg0):
    return pl.pallas_call(
        _dispatch_kernel,
        out_shape=jax.ShapeDtypeStruct((P_ROWS * ROW_TILE, LANES), F32),
        grid_spec=pltpu.PrefetchScalarGridSpec(
            num_scalar_prefetch=1, grid=(NT,),
            in_specs=[pl.BlockSpec((TM * ROW_TILE, LANES), lambda i, *_: (i, 0)),
                      pl.BlockSpec(memory_space=pl.ANY)],
            out_specs=pl.BlockSpec(memory_space=pl.ANY),
            scratch_shapes=[pltpu.SemaphoreType.DMA(())]),
        input_output_aliases={2: 0},
        compiler_params=_params("arbitrary"),
        name="moe_dispatch",
    )(pos, h_tiled, xg0)


def _expert_ffn_kernel(te_ref, na_ref, x_ref, w1_ref, w3_ref, w2_ref, o_ref):
    g = pl.program_id(0)

    @pl.when(g < na_ref[0])
    def _():
        x = _load_rows_tiled(x_ref, TG).astype(BF16)
        acc = None
        for c in range(E_FF // F_CHUNK):
            cols = slice(c * F_CHUNK, (c + 1) * F_CHUNK)
            a = _dot(x, w1_ref[0, :, cols])
            b = _dot(x, w3_ref[0, :, cols])
            part = _dot((_silu(a) * b).astype(BF16), w2_ref[0, cols, :])
            acc = part if acc is None else acc + part
        _store_rows_tiled(o_ref, acc)

    @pl.when(g >= na_ref[0])
    def _():
        o_ref[...] = jnp.zeros_like(o_ref)


def _expert_ffn(tile_expert, n_active, xg, w1, w3, w2):
    rows = pl.BlockSpec((TG * ROW_TILE, LANES), lambda g, te, na: (g, 0))
    w_up = pl.BlockSpec((1, D, E_FF), lambda g, te, na: (te[g], 0, 0), pipeline_mode=pl.Buffered(1))
    w_dn = pl.BlockSpec((1, E_FF, D), lambda g, te, na: (te[g], 0, 0), pipeline_mode=pl.Buffered(1))
    return pl.pallas_call(
        _expert_ffn_kernel,
        out_shape=jax.ShapeDtypeStruct((P_ROWS * ROW_TILE, LANES), F32),
        grid_spec=pltpu.PrefetchScalarGridSpec(
            num_scalar_prefetch=2, grid=(NG,),
            in_specs=[rows, w_up, w_up, w_dn],
            out_specs=rows),
        compiler_params=_params("arbitrary"),
        name="moe_expert_ffn",
    )(tile_expert, n_active, xg, w1, w3, w2)


def _combine_ln_kernel(pos_ref, y_ref, mod_ref, meta_ref, g_ref, b_ref, eo_ref, outp_ref, outs_ref,
                       buf1, buf2, sem):
    i = pl.program_id(0)

    def issue(r, carry):
        t = i * TM + r
        _row_copy(eo_ref, pos_ref[2 * t], buf1, r, sem).start()
        _row_copy(eo_ref, pos_ref[2 * t + 1], buf2, r, sem).start()
        return carry

    lax.fori_loop(0, TM, issue, 0, unroll=DMA_UNROLL)
    for buf in (buf1, buf2):
        pltpu.make_async_copy(eo_ref.at[pl.ds(0, TM * ROW_TILE)], buf, sem).wait()
    m = mod_ref[0]
    meta = meta_ref[...]
    f = meta[:, 4:5] * _load_rows_tiled(buf1, TM) + meta[:, 5:6] * _load_rows_tiled(buf2, TM)
    out = _layer_norm(ALPHA * y_ref[...] + m[5:6] * f, g_ref[...], b_ref[...])

    @pl.when(i < NP_TILES)
    def _():
        outp_ref[...] = out

    @pl.when(i >= NP_TILES)
    def _():
        outs_ref[...] = out


def _combine_ln(pos, y, mods, meta, ln_g, ln_b, eo):
    return pl.pallas_call(
        _combine_ln_kernel,
        out_shape=[jax.ShapeDtypeStruct((T_P, D), F32), jax.ShapeDtypeStruct((T_S, D), F32)],
        grid_spec=pltpu.PrefetchScalarGridSpec(
            num_scalar_prefetch=1, grid=(NT,),
            in_specs=[pl.BlockSpec((TM, D), lambda i, *_: (i, 0)),
                      pl.BlockSpec((1, 6, D), lambda i, *_: (_cond_of_tile(i), 0, 0)),
                      pl.BlockSpec((TM, LANES), lambda i, *_: (i, 0)),
                      pl.BlockSpec((1, D), lambda i, *_: (0, 0)),
                      pl.BlockSpec((1, D), lambda i, *_: (0, 0)),
                      pl.BlockSpec(memory_space=pl.ANY)],
            out_specs=[_prompt_tile_spec(D), _sample_tile_spec(D)],
            scratch_shapes=[pltpu.VMEM((TM * ROW_TILE, LANES), F32),
                            pltpu.VMEM((TM * ROW_TILE, LANES), F32),
                            pltpu.SemaphoreType.DMA(())]),
        compiler_params=_params("arbitrary"),
        name="moe_combine_ln",
    )(pos, y, mods, meta, ln_g.reshape(1, D), ln_b.reshape(1, D), eo)


def _moe_ln(y, mods, router_w, router_b, w1, w3, w2, ln_g, ln_b):
    h_tiled, meta, counts = _router(y, mods, router_w, router_b)
    cnt = counts[0, :N_EXPERTS].astype(jnp.int32)
    tiles = (cnt + TG - 1) // TG
    tile_end = jnp.cumsum(tiles)
    offs = (tile_end - tiles) * TG
    e1 = meta[:, 0].astype(jnp.int32)
    e2 = meta[:, 1].astype(jnp.int32)
    pos = jnp.stack([offs[e1] + meta[:, 2].astype(jnp.int32), offs[e2] + meta[:, 3].astype(jnp.int32)], axis=1)
    pos = pos.reshape(2 * T)
    tile_expert = jnp.sum((jnp.arange(NG)[:, None] >= tile_end[None, :]).astype(jnp.int32), axis=1)
    tile_expert = jnp.minimum(tile_expert, N_EXPERTS - 1)
    n_active = tile_end[-1:].astype(jnp.int32)
    xg = _dispatch(pos, h_tiled, jnp.zeros((P_ROWS * ROW_TILE, LANES), F32))
    eo = _expert_ffn(tile_expert, n_active, xg, w1, w3, w2)
    return _combine_ln(pos, y, mods, meta, ln_g, ln_b, eo)


def _l0_weight_layouts(w_in, w_q_up, w_kv_up):
    a, b, c = MLA_Q_LORA, MLA_Q_LORA + MLA_KV_LORA, MLA_Q_LORA + MLA_KV_LORA + MLA_ROPE
    k_rope = w_in[:, b:c]
    w_in_r = jnp.concatenate([w_in[:, :b], w_in[:, c:], k_rope, k_rope, k_rope, k_rope], axis=1)
    wq = w_q_up.reshape(MLA_Q_LORA, MLA_HEADS, MLA_NOPE + MLA_ROPE)
    w_q_up_r = jnp.concatenate([wq[:, :, :MLA_NOPE].reshape(MLA_Q_LORA, -1),
                                wq[:, :, MLA_NOPE:].reshape(MLA_Q_LORA, -1)], axis=1)
    wkv = w_kv_up.reshape(MLA_KV_LORA, MLA_HEADS, MLA_NOPE + MLA_V)
    w_kv_up_r = jnp.concatenate([wkv[:, :, :MLA_NOPE].reshape(MLA_KV_LORA, -1),
                                 wkv[:, :, MLA_NOPE:].reshape(MLA_KV_LORA, -1)], axis=1)
    return w_in_r.astype(BF16), w_q_up_r.astype(BF16), w_kv_up_r.astype(BF16)


def kernel(x_prompt, x_sample, cache_l0_mla_ckv, cache_l0_mla_krope, cache_l0_swa_k, cache_l0_swa_v,
           cache_l1_na_k, cache_l1_na_v, c, c_ctx,
           l0_ada_w, l0_ada_b, l0_w_in, l0_mla_q_norm, l0_mla_w_q_up, l0_mla_kv_norm, l0_mla_w_kv_up,
           l0_swa_sink, l0_w_out, l0_ln1_g, l0_ln1_b, l0_ffn_w1, l0_ffn_w3, l0_ffn_w2, l0_ln2_g, l0_ln2_b,
           l1_ada_w, l1_ada_b, l1_w_in, l1_na_rel_bias, l1_w_out, l1_ln1_g, l1_ln1_b,
           l1_moe_router_w, l1_moe_router_b, l1_moe_w1, l1_moe_w3, l1_moe_w2, l1_ln2_g, l1_ln2_b):
    cond = jnp.concatenate([c_ctx[None, :], c, jnp.zeros((8 - N_COND, D), F32)], axis=0)

    mods = _adaln(cond, l0_ada_w, l0_ada_b)
    w_in_r, w_q_up_r, w_kv_up_r = _l0_weight_layouts(l0_w_in, l0_mla_w_q_up, l0_mla_w_kv_up)
    y, qn, qr, ckv, kn, vm, kr4, sq, sk, sv = _l0_in(
        x_prompt.reshape(T_P, D), x_sample.reshape(T_S, D), mods, w_in_r, l0_mla_q_norm, w_q_up_r,
        l0_mla_kv_norm, w_kv_up_r, _rope_tables(MLA_ROPE, 256), _rope_tables(HEAD_DIM, 512))
    knc, vc = _kv_up(cache_l0_mla_ckv.reshape(DEC_BATCH * PAST, MLA_KV_LORA), w_kv_up_r)
    kr4c = jnp.tile(cache_l0_mla_krope.reshape(DEC_BATCH * PAST, MLA_ROPE), (1, 4))
    skc = cache_l0_swa_k.reshape(DEC_BATCH * PAST, SWA_KV_HEADS * HEAD_DIM)
    svc = cache_l0_swa_v.reshape(DEC_BATCH * PAST, SWA_KV_HEADS * HEAD_DIM)
    o_p, moe_w1 = _l0_attn_prompt(l0_swa_sink, qn, qr, kn, vm, kr4, sq, sk, sv,
                                  l1_moe_w1.reshape(N_EXPERTS * D, E_FF))
    o_s = _l0_attn_sample(l0_swa_sink, qn, qr, kn, vm, kr4, sq, sk, sv, knc, vc, kr4c, skc, svc)
    y = _out_ln(o_p, o_s, l0_w_out.astype(BF16), y, mods, l0_ln1_g, l0_ln1_b)
    y = _ffn_ln(y, mods, l0_ffn_w1.astype(BF16), l0_ffn_w3.astype(BF16), l0_ffn_w2.astype(BF16),
                l0_ln2_g, l0_ln2_b)
    new_ckv = ckv[:T_P].reshape(BATCH, SEQ, MLA_KV_LORA)
    new_krope = kr4[:T_P, :MLA_ROPE].reshape(BATCH, SEQ, MLA_ROPE)
    new_sk = sk[:T_P].reshape(BATCH, SEQ, SWA_KV_HEADS, HEAD_DIM)
    new_sv = sv[:T_P].reshape(BATCH, SEQ, SWA_KV_HEADS, HEAD_DIM)

    mods = _adaln(cond, l1_ada_w, l1_ada_b)
    q, k_p, v_p, k_s, v_s, moe_w2 = _l1_in(y, mods, l1_w_in.astype(BF16),
                                           l1_moe_w2.reshape(N_EXPERTS * E_FF, D))
    o_p, moe_w3 = _l1_attn_prompt(q, k_p, v_p, l1_moe_w3.reshape(N_EXPERTS * D, E_FF))
    o_s = _l1_attn_sample(q, k_s, v_s, cache_l1_na_k.reshape(DEC_BATCH, PAST, D),
                          cache_l1_na_v.reshape(DEC_BATCH, PAST, D), _na_bias_rows(l1_na_rel_bias))
    y = _out_ln(o_p, o_s, l1_w_out.astype(BF16), y, mods, l1_ln1_g, l1_ln1_b)
    y_p, y_s = _moe_ln(y, mods, l1_moe_router_w, l1_moe_router_b, moe_w1.reshape(N_EXPERTS, D, E_FF),
                       moe_w3.reshape(N_EXPERTS, D, E_FF), moe_w2.reshape(N_EXPERTS, E_FF, D), l1_ln2_g, l1_ln2_b)
    new_k = k_p.reshape(BATCH, SEQ, NA_HEADS, HEAD_DIM)
    new_v = v_p.reshape(BATCH, SEQ, NA_HEADS, HEAD_DIM)

    return (y_p.reshape(BATCH, SEQ, D), y_s.reshape(DEC_BATCH, DEC_SEQ, D),
            new_ckv, new_krope, new_sk, new_sv, new_k, new_v)
```

```python
import functools

import jax
import jax.numpy as jnp
from jax import lax
from jax.experimental import pallas as pl
from jax.experimental.pallas import tpu as pltpu

F32 = jnp.float32
BF16 = jnp.bfloat16

D = 1024
BATCH, SEQ = 32, 256
DEC_BATCH, DEC_SEQ = 2, 1024
PAST = 256
GRID_W = 64
T_P = BATCH * SEQ
T_S = DEC_BATCH * DEC_SEQ
T = T_P + T_S
N_COND = 1 + DEC_BATCH

MLA_HEADS, MLA_Q_LORA, MLA_KV_LORA, MLA_NOPE, MLA_ROPE, MLA_V = 8, 384, 256, 64, 32, 64
SWA_HEADS, SWA_KV_HEADS, SWA_WINDOW, HEAD_DIM = 8, 2, 128, 64
NA_HEADS, NA_WIN_ROWS, NA_WIN_COLS = 16, 8, 16
D_FF, N_EXPERTS, E_FF = 2816, 8, 3584
ROPE_THETA = 10000.0
LN_EPS, RMS_EPS = 1e-5, 1e-6
NEG = -1e30
ALPHA = 4.0 ** 0.25

LANES = 128
TM = 512
NT = T // TM
NP_TILES = T_P // TM
TILES_PER_SAMPLE = DEC_SEQ // TM
TG = 256
P_ROWS = 2 * T + N_EXPERTS * TG
NG = P_ROWS // TG
F_CHUNK = 896
VMEM_LIMIT = 56 * 1024 * 1024


def _params(*sem):
    return pltpu.CompilerParams(dimension_semantics=sem, vmem_limit_bytes=VMEM_LIMIT)


def _const_spec(shape, single_buffer=False):
    if single_buffer:
        return pl.BlockSpec(shape, lambda *_: (0,) * len(shape), pipeline_mode=pl.Buffered(1))
    return pl.BlockSpec(shape, lambda *_: (0,) * len(shape))


def _cast_specs(shape, steps):
    rows, width = shape
    spec = pl.BlockSpec((rows // steps, width), lambda i, *_: (jnp.minimum(i, steps - 1), 0))
    return spec, jax.ShapeDtypeStruct(shape, BF16)


def _cast_block(i, steps, src_ref, dst_ref):
    @pl.when(i < steps)
    def _():
        dst_ref[...] = src_ref[...].astype(BF16)


ROW_TILE = D // LANES


def _store_rows_tiled(ref, x):
    n = x.shape[0]
    for k in range(ROW_TILE):
        ref[pl.ds(k, n, stride=ROW_TILE), :] = x[:, k * LANES:(k + 1) * LANES]


def _load_rows_tiled(ref, n):
    return jnp.concatenate([ref[pl.ds(k, n, stride=ROW_TILE), :] for k in range(ROW_TILE)], axis=1)


def _cond_of_tile(i, rows=TM):
    per_sample = DEC_SEQ // rows
    return jnp.maximum((i - (T_P // rows - per_sample)) // per_sample, 0)


def _mod_spec(rows=TM):
    return pl.BlockSpec((1, 6, D), lambda i: (_cond_of_tile(i, rows), 0, 0))


def _row_spec(width, rows=TM):
    return pl.BlockSpec((rows, width), lambda i: (i, 0))


def _dot(a, b):
    return jnp.dot(a, b, preferred_element_type=F32)


def _dot_nt(a, b):
    return lax.dot_general(a, b, (((1,), (1,)), ((), ())), preferred_element_type=F32)


def _layer_norm(r, g, b):
    mu = jnp.mean(r, axis=-1, keepdims=True)
    d = r - mu
    var = jnp.mean(d * d, axis=-1, keepdims=True)
    return d * lax.rsqrt(var + LN_EPS) * g + b


def _rms_norm(x, g):
    return x * lax.rsqrt(jnp.mean(x * x, axis=-1, keepdims=True) + RMS_EPS) * g


def _silu(x):
    return x * jax.nn.sigmoid(x)


def _ada_kernel(c_ref, w_ref, b_ref, o_ref):
    s = _silu(c_ref[...]).astype(BF16)
    o_ref[...] = _dot(s, w_ref[...].astype(BF16)) + b_ref[...]


def _adaln(cond, ada_w, ada_b):
    nb = 1536
    out = pl.pallas_call(
        _ada_kernel,
        out_shape=jax.ShapeDtypeStruct((8, 6 * D), F32),
        grid=(6 * D // nb,),
        in_specs=[_const_spec((8, D)), pl.BlockSpec((D, nb), lambda j: (0, j)),
                  pl.BlockSpec((1, nb), lambda j: (0, j))],
        out_specs=pl.BlockSpec((8, nb), lambda j: (0, j)),
        compiler_params=_params("arbitrary"),
        name="adaln",
    )(cond, ada_w, ada_b.reshape(1, 6 * D))
    return out[:N_COND].reshape(N_COND, 6, D)


def _rope_tables(head_dim, width):
    half = head_dim // 2
    nf = half // 2
    lane = jnp.arange(width)
    d = lane % head_dim
    dd = d % half
    f = dd % nf
    inv = ROPE_THETA ** (-f.astype(F32) / nf)
    t = jnp.arange(DEC_SEQ)
    pos = jnp.where((d // half)[None, :] == 0, (t // GRID_W)[:, None], (t % GRID_W)[:, None])
    ang = pos.astype(F32) * inv[None, :]
    cos, sin = jnp.cos(ang), jnp.sin(ang)
    first = (dd < nf)[None, :]
    return cos, jnp.where(first, -sin, 0.0), jnp.where(first, 0.0, sin)


def _rope(x, cos, sin_up, sin_dn, nf):
    w = x.shape[-1]
    return x * cos + pltpu.roll(x, w - nf, 1) * sin_up + pltpu.roll(x, nf, 1) * sin_dn


L0_COLS = MLA_Q_LORA + MLA_KV_LORA + 512 + 128 + 128 + 128


def _l0_in_kernel(xp_ref, xs_ref, mod_ref, win_ref, qn_ref, wq_ref, kvn_ref, wkv_ref,
                  c8_ref, su8_ref, sd8_ref, c16_ref, su16_ref, sd16_ref,
                  y_o, qnope_o, qrope_o, ckv_o, knope_o, vmla_o, kr4_o, sq_o, sk_o, sv_o):
    i = pl.program_id(0)
    m = mod_ref[0]
    x = jnp.where(i < NP_TILES, xp_ref[...], xs_ref[...])
    y_o[...] = x
    h = (x * (1.0 + m[1:2]) + m[0:1]).astype(BF16)
    z = _dot(h, win_ref[...])
    q_lat = z[:, 0:384]
    kv_lat = z[:, 384:640]
    sq = z[:, 640:1152]
    sk = z[:, 1152:1280]
    sv = z[:, 1280:1408]
    kr4 = z[:, 1408:1536]
    q = _dot(_rms_norm(q_lat, qn_ref[...]).astype(BF16), wq_ref[...])
    c_kv = _rms_norm(kv_lat, kvn_ref[...])
    kv = _dot(c_kv.astype(BF16), wkv_ref[...])
    qnope_o[...] = q[:, 0:512].astype(BF16)
    ckv_o[...] = c_kv
    knope_o[...] = kv[:, 0:512].astype(BF16)
    vmla_o[...] = kv[:, 512:1024].astype(BF16)
    sv_o[...] = sv
    q_rope = q[:, 512:768]

    @pl.when(i < NP_TILES)
    def _():
        qrope_o[...] = q_rope.astype(BF16)
        kr4_o[...] = kr4
        sq_o[...] = sq.astype(BF16)
        sk_o[...] = sk

    @pl.when(i >= NP_TILES)
    def _():
        c8, su8, sd8 = c8_ref[...], su8_ref[...], sd8_ref[...]
        c16, su16, sd16 = c16_ref[...], su16_ref[...], sd16_ref[...]
        qrope_o[...] = _rope(q_rope, c8, su8, sd8, 8).astype(BF16)
        kr4_o[...] = _rope(kr4, c8[:, :128], su8[:, :128], sd8[:, :128], 8)
        sq_o[...] = _rope(sq, c16, su16, sd16, 16).astype(BF16)
        sk_o[...] = _rope(sk, c16[:, :128], su16[:, :128], sd16[:, :128], 16)


def _prompt_tile_spec(width):
    return pl.BlockSpec((TM, width), lambda i, *_: (jnp.minimum(i, NP_TILES - 1), 0))


def _sample_tile_spec(width):
    return pl.BlockSpec((TM, width), lambda i, *_: (jnp.maximum(i - NP_TILES, 0), 0))


def _l0_in(x_prompt, x_sample, mods, w_in_r, q_norm, w_q_up_r, kv_norm, w_kv_up_r, tabs8, tabs16):
    def tab_spec(width):
        return pl.BlockSpec((TM, width), lambda i: (jnp.maximum(i - NP_TILES, 0) % TILES_PER_SAMPLE, 0))

    out_shape = [
        jax.ShapeDtypeStruct((T, D), F32),
        jax.ShapeDtypeStruct((T, 512), BF16),
        jax.ShapeDtypeStruct((T, 256), BF16),
        jax.ShapeDtypeStruct((T, 256), F32),
        jax.ShapeDtypeStruct((T, 512), BF16),
        jax.ShapeDtypeStruct((T, 512), BF16),
        jax.ShapeDtypeStruct((T, 128), F32),
        jax.ShapeDtypeStruct((T, 512), BF16),
        jax.ShapeDtypeStruct((T, 128), F32),
        jax.ShapeDtypeStruct((T, 128), F32),
    ]
    return pl.pallas_call(
        _l0_in_kernel,
        out_shape=out_shape,
        grid=(NT,),
        in_specs=[_prompt_tile_spec(D), _sample_tile_spec(D), _mod_spec(), _const_spec((D, L0_COLS)),
                  _const_spec((1, MLA_Q_LORA)), _const_spec((MLA_Q_LORA, 768)),
                  _const_spec((1, MLA_KV_LORA)), _const_spec((MLA_KV_LORA, 1024)),
                  tab_spec(256), tab_spec(256), tab_spec(256),
                  tab_spec(512), tab_spec(512), tab_spec(512)],
        out_specs=[_row_spec(s.shape[1]) for s in out_shape],
        compiler_params=_params("arbitrary"),
        name="l0_in_proj",
    )(x_prompt, x_sample, mods, w_in_r, q_norm.reshape(1, -1), w_q_up_r, kv_norm.reshape(1, -1), w_kv_up_r,
      *tabs8, *tabs16)


def _kv_up_kernel(c_ref, w_ref, k_o, v_o):
    kv = _dot(c_ref[...].astype(BF16), w_ref[...])
    k_o[...] = kv[:, 0:512].astype(BF16)
    v_o[...] = kv[:, 512:1024].astype(BF16)


def _kv_up(ckv, w_kv_up_r):
    n = ckv.shape[0]
    return pl.pallas_call(
        _kv_up_kernel,
        out_shape=[jax.ShapeDtypeStruct((n, 512), BF16)] * 2,
        grid=(1,),
        in_specs=[_const_spec((n, MLA_KV_LORA)), _const_spec((MLA_KV_LORA, 1024))],
        out_specs=[_const_spec((n, 512))] * 2,
        compiler_params=_params("arbitrary"),
        name="l0_ctx_kv_up",
    )(ckv, w_kv_up_r)


def _lane():
    return lax.broadcasted_iota(jnp.int32, (1, LANES), 1)


def _attend(scores, values, extra_logit=None):
    m = jnp.max(scores[0], axis=1, keepdims=True)
    for s in scores[1:]:
        m = jnp.maximum(m, jnp.max(s, axis=1, keepdims=True))
    if extra_logit is not None:
        m = jnp.maximum(m, extra_logit)
    den = None
    acc = None
    for s, v in zip(scores, values):
        e = jnp.exp(s - m)
        d = jnp.sum(e, axis=1, keepdims=True)
        a = _dot(e.astype(BF16), v)
        den = d if den is None else den + d
        acc = a if acc is None else acc + a
    if extra_logit is not None:
        den = den + jnp.exp(extra_logit - m)
    return acc / den


def _head_pair(q2, keys, values, scale, masks=None, biases=None, sinks=None, q_extra=None):
    lo = _lane() < 64
    zero = jnp.zeros_like(q2)
    outs = []
    for hh in range(2):
        qm = jnp.where(lo if hh == 0 else jnp.logical_not(lo), q2, zero)
        if q_extra is not None:
            qm = jnp.concatenate([qm, q_extra[hh]], axis=1)
        scores = []
        for n, k in enumerate(keys):
            s = _dot_nt(qm, k) * scale
            if biases is not None and biases[n] is not None:
                s = s + biases[n][hh]
            if masks is not None and masks[n] is not None:
                s = jnp.where(masks[n], s, NEG)
            scores.append(s)
        outs.append(_attend(scores, values, None if sinks is None else sinks[hh]))
    return jnp.where(lo, outs[0], outs[1])


def _dup_halves(x):
    lo = _lane() < 64
    sw = pltpu.roll(x, 64, 1)
    return jnp.where(lo, x, sw), jnp.where(lo, sw, x)


MLA_SCALE = (MLA_NOPE + MLA_ROPE) ** -0.5
HD_SCALE = HEAD_DIM ** -0.5


def _mla_pairs(qn_ref, qr_ref, key_sets, o_ref):
    lane = _lane()
    for j in range(MLA_HEADS // 2):
        cols = slice(128 * j, 128 * (j + 1))
        qr = qr_ref[:, 128 * (j // 2):128 * (j // 2 + 1)]
        zero = jnp.zeros_like(qr)
        q_extra = [jnp.where((lane // MLA_ROPE) == ((2 * j + hh) % 4), qr, zero) for hh in range(2)]
        keys = [jnp.concatenate([kn[:, cols], kr4], axis=1) for kn, kr4, _ in key_sets]
        values = [v[:, cols] for _, _, v in key_sets]
        o = _head_pair(qn_ref[:, cols], keys, values, MLA_SCALE, q_extra=q_extra)
        o_ref[:, cols] = o.astype(BF16)


def _swa_pairs(sink_ref, sq_ref, key_sets, masks, o_ref):
    kd = [[a.astype(BF16) for a in _dup_halves(k)] for k, _ in key_sets]
    vd = [[a.astype(BF16) for a in _dup_halves(v)] for _, v in key_sets]
    for g in range(SWA_KV_HEADS):
        for u in range(2):
            c = 2 * g + u
            cols = slice(128 * c, 128 * (c + 1))
            sinks = [sink_ref[2 * c + hh] for hh in range(2)]
            o = _head_pair(sq_ref[:, cols], [k[g] for k in kd], [v[g] for v in vd], HD_SCALE,
                           masks=masks, sinks=sinks)
            o_ref[:, 512 + 128 * c:512 + 128 * (c + 1)] = o.astype(BF16)


def _cast_steps(rows, max_steps):
    steps = max_steps
    while rows % (16 * steps):
        steps //= 2
    return steps


def _l0_attn_prompt_kernel(cast_steps, sink_ref, qn_ref, qr_ref, kn_ref, v_ref, kr4_ref, sq_ref, sk_ref, sv_ref,
                           wf1_ref, wf2_ref, wf3_ref, o_ref, wb1_ref, wb2_ref, wb3_ref):
    _mla_pairs(qn_ref, qr_ref, [(kn_ref, kr4_ref[...].astype(BF16), v_ref)], o_ref)
    _swa_pairs(sink_ref, sq_ref, [(sk_ref[...], sv_ref[...])], None, o_ref)
    for steps, wf_ref, wb_ref in zip(cast_steps, (wf1_ref, wf2_ref, wf3_ref), (wb1_ref, wb2_ref, wb3_ref)):
        _cast_block(pl.program_id(0), steps, wf_ref, wb_ref)


def _l0_attn_prompt(sink, qn, qr, kn, vm, kr4, sq, sk, sv, weights_f32):
    spec = lambda w: pl.BlockSpec((SEQ, w), lambda b, *_: (b, 0))
    cast_steps = tuple(_cast_steps(w.shape[0], BATCH) for w in weights_f32)
    casts = [_cast_specs(w.shape, s) for w, s in zip(weights_f32, cast_steps)]
    return pl.pallas_call(
        functools.partial(_l0_attn_prompt_kernel, cast_steps),
        out_shape=[jax.ShapeDtypeStruct((T_P, D), BF16)] + [c[1] for c in casts],
        grid_spec=pltpu.PrefetchScalarGridSpec(
            num_scalar_prefetch=1, grid=(BATCH,),
            in_specs=[spec(512), spec(256), spec(512), spec(512), spec(128), spec(512), spec(128), spec(128)]
            + [c[0] for c in casts],
            out_specs=[spec(D)] + [c[0] for c in casts]),
        compiler_params=_params("arbitrary"),
        name="l0_attn_prompt",
    )(sink, qn, qr, kn, vm, kr4, sq, sk, sv, *weights_f32)


TQ_S = 256


def _l0_attn_sample_kernel(sink_ref, qn_ref, qr_ref, sq_ref, kn_ref, v_ref, kr4_ref, sk_ref, sv_ref,
                           knc_ref, vc_ref, kr4c_ref, skc_ref, svc_ref, o_ref):
    i = pl.program_id(1)
    _mla_pairs(qn_ref, qr_ref,
               [(knc_ref, kr4c_ref[...].astype(BF16), vc_ref), (kn_ref, kr4_ref[...].astype(BF16), v_ref)],
               o_ref)
    span = TQ_S + 2 * SWA_WINDOW
    start = pl.multiple_of(jnp.clip(i * TQ_S - SWA_WINDOW, 0, DEC_SEQ - span), SWA_WINDOW)
    qpos = i * TQ_S + lax.broadcasted_iota(jnp.int32, (TQ_S, span), 0)
    kpos = start + lax.broadcasted_iota(jnp.int32, (TQ_S, span), 1)
    band = jnp.abs(qpos - kpos) <= SWA_WINDOW
    keys = pl.ds(start, span)
    _swa_pairs(sink_ref, sq_ref, [(skc_ref[...], svc_ref[...]), (sk_ref[keys, :], sv_ref[keys, :])],
               [None, band], o_ref)


def _l0_attn_sample(sink, qn, qr, kn, vm, kr4, sq, sk, sv, knc, vc, kr4c, skc, svc):
    nq = DEC_SEQ // TQ_S
    qspec = lambda w: pl.BlockSpec((TQ_S, w), lambda b, i, *_: (T_P // TQ_S + b * nq + i, 0))
    kspec = lambda w: pl.BlockSpec((DEC_SEQ, w), lambda b, i, *_: (T_P // DEC_SEQ + b, 0))
    cspec = lambda w: pl.BlockSpec((PAST, w), lambda b, i, *_: (b, 0))
    return pl.pallas_call(
        _l0_attn_sample_kernel,
        out_shape=jax.ShapeDtypeStruct((T_S, D), BF16),
        grid_spec=pltpu.PrefetchScalarGridSpec(
            num_scalar_prefetch=1, grid=(DEC_BATCH, nq),
            in_specs=[qspec(512), qspec(256), qspec(512),
                      kspec(512), kspec(512), kspec(128), kspec(128), kspec(128),
                      cspec(512), cspec(512), cspec(128), cspec(128), cspec(128)],
            out_specs=pl.BlockSpec((TQ_S, D), lambda b, i, *_: (b * nq + i, 0))),
        compiler_params=_params("arbitrary", "arbitrary"),
        name="l0_attn_sample",
    )(sink, qn, qr, sq, kn, vm, kr4, sk, sv, knc, vc, kr4c, skc, svc)


def _out_ln_kernel(op_ref, os_ref, w_ref, y_ref, mod_ref, g_ref, b_ref, out_ref):
    i = pl.program_id(0)
    m = mod_ref[0]

    def finish(o):
        r = ALPHA * y_ref[...] + m[2:3] * _dot(o, w_ref[...])
        out_ref[...] = _layer_norm(r, g_ref[...], b_ref[...])

    @pl.when(i < NP_TILES)
    def _():
        finish(op_ref[...])

    @pl.when(i >= NP_TILES)
    def _():
        finish(os_ref[...])


def _out_ln(o_prompt, o_sample, w_out, y, mods, ln_g, ln_b):
    return pl.pallas_call(
        _out_ln_kernel,
        out_shape=jax.ShapeDtypeStruct((T, D), F32),
        grid=(NT,),
        in_specs=[_prompt_tile_spec(D), _sample_tile_spec(D),
                  _const_spec((D, D)), _row_spec(D), _mod_spec(),
                  _const_spec((1, D)), _const_spec((1, D))],
        out_specs=_row_spec(D),
        compiler_params=_params("arbitrary"),
        name="out_proj_ln",
    )(o_prompt, o_sample, w_out, y, mods, ln_g.reshape(1, D), ln_b.reshape(1, D))


FFN_CHUNK = D_FF // 2
FFN_TM = 256
FFN_CAST_STEPS = 32


def _ffn_ln_kernel(y_ref, mod_ref, w1_ref, w3_ref, w2_ref, g_ref, b_ref, wfa_ref, wfb_ref,
                   out_ref, wba_ref, wbb_ref):
    i = pl.program_id(0)
    _cast_block(i, FFN_CAST_STEPS, wfa_ref, wba_ref)
    _cast_block(i, FFN_CAST_STEPS, wfb_ref, wbb_ref)
    m = mod_ref[0]
    y = y_ref[...]
    h = (y * (1.0 + m[4:5]) + m[3:4]).astype(BF16)
    acc = None
    for c in range(D_FF // FFN_CHUNK):
        cols = slice(c * FFN_CHUNK, (c + 1) * FFN_CHUNK)
        a = _dot(h, w1_ref[:, cols])
        g = _dot(h, w3_ref[:, cols])
        part = _dot((_silu(a) * g).astype(BF16), w2_ref[cols, :])
        acc = part if acc is None else acc + part
    out_ref[...] = _layer_norm(ALPHA * y + m[5:6] * acc, g_ref[...], b_ref[...])


def _ffn_ln(y, mods, w1, w3, w2, ln_g, ln_b, wa_f32, wb_f32):
    wa_spec, wa_shape = _cast_specs(wa_f32.shape, FFN_CAST_STEPS)
    wb_spec, wb_shape = _cast_specs(wb_f32.shape, FFN_CAST_STEPS)
    rows = _row_spec(D, FFN_TM)
    return pl.pallas_call(
        _ffn_ln_kernel,
        out_shape=[jax.ShapeDtypeStruct((T, D), F32), wa_shape, wb_shape],
        grid=(T // FFN_TM,),
        in_specs=[rows, _mod_spec(FFN_TM), _const_spec((D, D_FF), True), _const_spec((D, D_FF), True),
                  _const_spec((D_FF, D), True), _const_spec((1, D)), _const_spec((1, D)), wa_spec, wb_spec],
        out_specs=[rows, wa_spec, wb_spec],
        compiler_params=_params("arbitrary"),
        name="ffn_ln",
    )(y, mods, w1, w3, w2, ln_g.reshape(1, D), ln_b.reshape(1, D), wa_f32, wb_f32)


def _l1_in_kernel(y_ref, mod_ref, w_ref, q_o, kp_o, vp_o, ks_o, vs_o):
    i = pl.program_id(0)
    m = mod_ref[0]
    h = (y_ref[...] * (1.0 + m[1:2]) + m[0:1]).astype(BF16)
    z = _dot(h, w_ref[...])
    q_o[...] = z[:, 0:D].astype(BF16)

    @pl.when(i < NP_TILES)
    def _():
        kp_o[...] = z[:, D:2 * D]
        vp_o[...] = z[:, 2 * D:3 * D]

    @pl.when(i >= NP_TILES)
    def _():
        ks_o[...] = z[:, D:2 * D].astype(BF16)
        vs_o[...] = z[:, 2 * D:3 * D].astype(BF16)


def _l1_in(y, mods, w_in):
    return pl.pallas_call(
        _l1_in_kernel,
        out_shape=[jax.ShapeDtypeStruct((T, D), BF16),
                   jax.ShapeDtypeStruct((T_P, D), F32), jax.ShapeDtypeStruct((T_P, D), F32),
                   jax.ShapeDtypeStruct((T_S, D), BF16), jax.ShapeDtypeStruct((T_S, D), BF16)],
        grid=(NT,),
        in_specs=[_row_spec(D), _mod_spec(), _const_spec((D, 3 * D))],
        out_specs=[_row_spec(D), _prompt_tile_spec(D), _prompt_tile_spec(D),
                   _sample_tile_spec(D), _sample_tile_spec(D)],
        compiler_params=_params("arbitrary"),
        name="l1_in_proj",
    )(y, mods, w_in)


def _l1_attn_prompt_kernel(q_ref, k_ref, v_ref, o_ref):
    for j in range(NA_HEADS // 2):
        cols = slice(128 * j, 128 * (j + 1))
        o = _head_pair(q_ref[:, cols], [k_ref[:, cols].astype(BF16)], [v_ref[:, cols].astype(BF16)], HD_SCALE)
        o_ref[:, cols] = o.astype(BF16)


def _l1_attn_prompt(q, k, v):
    spec = pl.BlockSpec((SEQ, D), lambda b: (b, 0))
    return pl.pallas_call(
        _l1_attn_prompt_kernel,
        out_shape=jax.ShapeDtypeStruct((T_P, D), BF16),
        grid=(BATCH,),
        in_specs=[spec, spec, spec],
        out_specs=spec,
        compiler_params=_params("arbitrary"),
        name="l1_attn_prompt",
    )(q, k, v)


NA_ROWS = DEC_SEQ // GRID_W
NA_TILE_ROWS = 4
NA_TQ = NA_TILE_ROWS * GRID_W
NA_SPAN = NA_WIN_ROWS + NA_TILE_ROWS
NA_DR = 2 * NA_WIN_ROWS - 1
COL_SPAN = 2 * GRID_W - 1


def _na_span_start(t):
    first = max(0, min(t * NA_TILE_ROWS - NA_WIN_ROWS // 2, NA_ROWS - NA_WIN_ROWS))
    return min(first, NA_ROWS - NA_SPAN)


def _l1_attn_sample_kernel(q_ref, k_ref, v_ref, kc_ref, vc_ref, ext_ref, o_ref):
    kc = kc_ref[0].astype(BF16)
    vc = vc_ref[0].astype(BF16)
    lo = _lane() < GRID_W
    qcol = lax.broadcasted_iota(jnp.int32, (GRID_W, LANES), 0)
    kcol = lax.broadcasted_iota(jnp.int32, (GRID_W, LANES), 1) % GRID_W
    cs = jnp.clip(qcol - NA_WIN_COLS // 2, 0, GRID_W - NA_WIN_COLS)
    col_ok = jnp.logical_and(kcol >= cs, kcol < cs + NA_WIN_COLS)
    neg = jnp.full((GRID_W, LANES), NEG, F32)

    def bias_tile(hh, d, half):
        x = jnp.broadcast_to(ext_ref[hh, d:d + 1, :], (GRID_W, LANES))
        shift = (LANES - COL_SPAN // 2 + GRID_W * half) % LANES
        return jnp.where(col_ok, pltpu.roll(x, shift, 1, stride=1, stride_axis=0), neg)

    tiles = [[[bias_tile(hh, d, half) for half in range(2)] for d in range(NA_DR)] for hh in range(2)]

    for t in range(NA_ROWS // NA_TILE_ROWS):
        ws = _na_span_start(t)
        keys = slice(ws * GRID_W, (ws + NA_SPAN) * GRID_W)
        kw = k_ref[keys, :].astype(BF16)
        vw = v_ref[keys, :].astype(BF16)
        bias = []
        for hh in range(2):
            rows = []
            for rr in range(NA_TILE_ROWS):
                r = t * NA_TILE_ROWS + rr
                rs = max(0, min(r - NA_WIN_ROWS // 2, NA_ROWS - NA_WIN_ROWS))
                blocks = []
                for u in range(NA_SPAN // 2):
                    halves = []
                    for half in range(2):
                        kr = ws + 2 * u + half
                        ok = rs <= kr < rs + NA_WIN_ROWS
                        halves.append(tiles[hh][kr - r + NA_WIN_ROWS - 1][half] if ok else neg)
                    blocks.append(jnp.where(lo, halves[0], halves[1]))
                rows.append(jnp.concatenate(blocks, axis=1))
            bias.append(jnp.concatenate(rows, axis=0))
        qrows = slice(t * NA_TQ, (t + 1) * NA_TQ)
        o = _head_pair(q_ref[qrows, :], [kw, kc], [vw, vc], HD_SCALE, biases=[bias, None])
        o_ref[qrows, :] = o.astype(BF16)


def _l1_attn_sample(q, k_s, v_s, kc, vc, ext):
    lat = pl.BlockSpec((DEC_SEQ, LANES), lambda j, b: (b, j))
    ctx = pl.BlockSpec((1, PAST, LANES), lambda j, b: (b, 0, j))
    return pl.pallas_call(
        _l1_attn_sample_kernel,
        out_shape=jax.ShapeDtypeStruct((T_S, D), BF16),
        grid=(NA_HEADS // 2, DEC_BATCH),
        in_specs=[pl.BlockSpec((DEC_SEQ, LANES), lambda j, b: (T_P // DEC_SEQ + b, j)), lat, lat, ctx, ctx,
                  pl.BlockSpec((2, NA_DR, LANES), lambda j, b: (j, 0, 0))],
        out_specs=lat,
        compiler_params=_params("arbitrary", "arbitrary"),
        name="l1_attn_sample",
    )(q, k_s, v_s, kc, vc, ext)


def _na_bias_rows(rel_bias):
    rb = rel_bias.astype(F32)
    n_lo = GRID_W - 1 - (NA_WIN_COLS - 1)
    n_hi = LANES - n_lo - rb.shape[-1]
    return jnp.concatenate([jnp.repeat(rb[..., :1], n_lo, axis=-1), rb,
                            jnp.repeat(rb[..., -1:], n_hi, axis=-1)], axis=-1)


def _router_kernel(y_ref, mod_ref, rw_ref, rb_ref, h_o, meta_o, cnt_o, carry_ref):
    i = pl.program_id(0)

    @pl.when(i == 0)
    def _():
        carry_ref[...] = jnp.zeros_like(carry_ref)

    m = mod_ref[0]
    h = y_ref[...] * (1.0 + m[4:5]) + m[3:4]
    _store_rows_tiled(h_o, h)
    logits = jnp.dot(h, rw_ref[...], preferred_element_type=F32, precision=lax.Precision.HIGHEST) + rb_ref[...]
    lane = lax.broadcasted_iota(jnp.int32, (TM, LANES), 1).astype(F32)
    m1 = jnp.max(logits, axis=1, keepdims=True)
    i1 = jnp.min(jnp.where(logits == m1, lane, float(LANES)), axis=1, keepdims=True)
    sel1 = lane == i1
    rest = jnp.where(sel1, -jnp.inf, logits)
    m2 = jnp.max(rest, axis=1, keepdims=True)
    i2 = jnp.min(jnp.where(rest == m2, lane, float(LANES)), axis=1, keepdims=True)
    sel2 = lane == i2
    e2 = jnp.exp(m2 - m1)
    w1 = 1.0 / (1.0 + e2)
    w2 = e2 / (1.0 + e2)
    sel = jnp.logical_or(sel1, sel2)
    rr = lax.broadcasted_iota(jnp.int32, (TM, TM), 0)
    cc = lax.broadcasted_iota(jnp.int32, (TM, TM), 1)
    tri = jnp.where(cc < rr, 1.0, 0.0).astype(BF16)
    ahead = _dot(tri, jnp.where(sel, 1.0, 0.0).astype(BF16)) + carry_ref[...]
    r1 = jnp.sum(jnp.where(sel1, ahead, 0.0), axis=1, keepdims=True)
    r2 = jnp.sum(jnp.where(sel2, ahead, 0.0), axis=1, keepdims=True)
    meta = jnp.where(lane == 0, i1, 0.0)
    meta = jnp.where(lane == 1, i2, meta)
    meta = jnp.where(lane == 2, r1, meta)
    meta = jnp.where(lane == 3, r2, meta)
    meta = jnp.where(lane == 4, w1, meta)
    meta = jnp.where(lane == 5, w2, meta)
    meta_o[...] = meta
    carry_ref[...] = carry_ref[...] + jnp.sum(jnp.where(sel, 1.0, 0.0), axis=0, keepdims=True)
    cnt_o[...] = carry_ref[...]


def _router(y, mods, router_w, router_b):
    rw = jnp.zeros((D, LANES), F32).at[:, :N_EXPERTS].set(router_w)
    rb = jnp.full((1, LANES), NEG, F32).at[0, :N_EXPERTS].set(router_b)
    return pl.pallas_call(
        _router_kernel,
        out_shape=[jax.ShapeDtypeStruct((T * ROW_TILE, LANES), F32),
                   jax.ShapeDtypeStruct((T, LANES), F32),
                   jax.ShapeDtypeStruct((1, LANES), F32)],
        grid=(NT,),
        in_specs=[_row_spec(D), _mod_spec(), _const_spec((D, LANES)), _const_spec((1, LANES))],
        out_specs=[_row_spec(LANES, TM * ROW_TILE), _row_spec(LANES), _const_spec((1, LANES))],
        scratch_shapes=[pltpu.VMEM((1, LANES), F32)],
        compiler_params=_params("arbitrary"),
        name="moe_router",
    )(y, mods, rw, rb)


DMA_UNROLL = 8


def _row_copy(src_ref, src_row, dst_ref, dst_row, sem):
    return pltpu.make_async_copy(src_ref.at[pl.ds(pl.multiple_of(src_row * ROW_TILE, ROW_TILE), ROW_TILE)],
                                 dst_ref.at[pl.ds(pl.multiple_of(dst_row * ROW_TILE, ROW_TILE), ROW_TILE)], sem)


def _dispatch_kernel(pos_ref, h_ref, xg_in_ref, wf_ref, xg_ref, wb_ref, sem):
    del xg_in_ref
    i = pl.program_id(0)
    _cast_block(i, NP_TILES, wf_ref, wb_ref)

    def issue(r, carry):
        t = i * TM + r
        _row_copy(h_ref, r, xg_ref, pos_ref[2 * t], sem).start()
        _row_copy(h_ref, r, xg_ref, pos_ref[2 * t + 1], sem).start()
        return carry

    lax.fori_loop(0, TM, issue, 0, unroll=DMA_UNROLL)
    for _ in range(2):
        pltpu.make_async_copy(h_ref, xg_ref.at[pl.ds(0, TM * ROW_TILE)], sem).wait()


def _dispatch(pos, h_tiled, xg0, w_f32):
    w_spec, wb_shape = _cast_specs(w_f32.shape, NP_TILES)
    return pl.pallas_call(
        _dispatch_kernel,
        out_shape=[jax.ShapeDtypeStruct((P_ROWS * ROW_TILE, LANES), F32), wb_shape],
        grid_spec=pltpu.PrefetchScalarGridSpec(
            num_scalar_prefetch=1, grid=(NT,),
            in_specs=[pl.BlockSpec((TM * ROW_TILE, LANES), lambda i, *_: (i, 0)),
                      pl.BlockSpec(memory_space=pl.ANY), w_spec],
            out_specs=[pl.BlockSpec(memory_space=pl.ANY), w_spec],
            scratch_shapes=[pltpu.SemaphoreType.DMA(())]),
        input_output_aliases={2: 0},
        compiler_params=_params("arbitrary"),
        name="moe_dispatch",
    )(pos, h_tiled, xg0, w_f32)


def _expert_ffn_kernel(te_ref, na_ref, x_ref, w1_ref, w3_ref, w2_ref, o_ref):
    g = pl.program_id(0)

    @pl.when(g < na_ref[0])
    def _():
        x = _load_rows_tiled(x_ref, TG).astype(BF16)
        acc = None
        for c in range(E_FF // F_CHUNK):
            cols = slice(c * F_CHUNK, (c + 1) * F_CHUNK)
            a = _dot(x, w1_ref[0, :, cols])
            b = _dot(x, w3_ref[0, :, cols])
            part = _dot((_silu(a) * b).astype(BF16), w2_ref[0, cols, :])
            acc = part if acc is None else acc + part
        _store_rows_tiled(o_ref, acc)

    @pl.when(g >= na_ref[0])
    def _():
        o_ref[...] = jnp.zeros_like(o_ref)


def _expert_ffn(tile_expert, n_active, xg, w1, w3, w2):
    rows = pl.BlockSpec((TG * ROW_TILE, LANES), lambda g, te, na: (g, 0))
    w_up = pl.BlockSpec((1, D, E_FF), lambda g, te, na: (te[g], 0, 0))
    w_dn = pl.BlockSpec((1, E_FF, D), lambda g, te, na: (te[g], 0, 0))
    return pl.pallas_call(
        _expert_ffn_kernel,
        out_shape=jax.ShapeDtypeStruct((P_ROWS * ROW_TILE, LANES), F32),
        grid_spec=pltpu.PrefetchScalarGridSpec(
            num_scalar_prefetch=2, grid=(NG,),
            in_specs=[rows, w_up, w_up, w_dn],
            out_specs=rows),
        compiler_params=_params("arbitrary"),
        name="moe_expert_ffn",
    )(tile_expert, n_active, xg, w1, w3, w2)


def _combine_ln_kernel(pos_ref, y_ref, mod_ref, meta_ref, g_ref, b_ref, eo_ref, outp_ref, outs_ref,
                       buf1, buf2, sem):
    i = pl.program_id(0)

    def issue(r, carry):
        t = i * TM + r
        _row_copy(eo_ref, pos_ref[2 * t], buf1, r, sem).start()
        _row_copy(eo_ref, pos_ref[2 * t + 1], buf2, r, sem).start()
        return carry

    lax.fori_loop(0, TM, issue, 0, unroll=DMA_UNROLL)
    for buf in (buf1, buf2):
        pltpu.make_async_copy(eo_ref.at[pl.ds(0, TM * ROW_TILE)], buf, sem).wait()
    m = mod_ref[0]
    meta = meta_ref[...]
    f = meta[:, 4:5] * _load_rows_tiled(buf1, TM) + meta[:, 5:6] * _load_rows_tiled(buf2, TM)
    out = _layer_norm(ALPHA * y_ref[...] + m[5:6] * f, g_ref[...], b_ref[...])

    @pl.when(i < NP_TILES)
    def _():
        outp_ref[...] = out

    @pl.when(i >= NP_TILES)
    def _():
        outs_ref[...] = out


def _combine_ln(pos, y, mods, meta, ln_g, ln_b, eo):
    return pl.pallas_call(
        _combine_ln_kernel,
        out_shape=[jax.ShapeDtypeStruct((T_P, D), F32), jax.ShapeDtypeStruct((T_S, D), F32)],
        grid_spec=pltpu.PrefetchScalarGridSpec(
            num_scalar_prefetch=1, grid=(NT,),
            in_specs=[pl.BlockSpec((TM, D), lambda i, *_: (i, 0)),
                      pl.BlockSpec((1, 6, D), lambda i, *_: (_cond_of_tile(i), 0, 0)),
                      pl.BlockSpec((TM, LANES), lambda i, *_: (i, 0)),
                      pl.BlockSpec((1, D), lambda i, *_: (0, 0)),
                      pl.BlockSpec((1, D), lambda i, *_: (0, 0)),
                      pl.BlockSpec(memory_space=pl.ANY)],
            out_specs=[_prompt_tile_spec(D), _sample_tile_spec(D)],
            scratch_shapes=[pltpu.VMEM((TM * ROW_TILE, LANES), F32),
                            pltpu.VMEM((TM * ROW_TILE, LANES), F32),
                            pltpu.SemaphoreType.DMA(())]),
        compiler_params=_params("arbitrary"),
        name="moe_combine_ln",
    )(pos, y, mods, meta, ln_g.reshape(1, D), ln_b.reshape(1, D), eo)


def _moe_ln(y, mods, router_w, router_b, w1, w3, w2_f32, ln_g, ln_b):
    h_tiled, meta, counts = _router(y, mods, router_w, router_b)
    cnt = counts[0, :N_EXPERTS].astype(jnp.int32)
    tiles = (cnt + TG - 1) // TG
    tile_end = jnp.cumsum(tiles)
    offs = (tile_end - tiles) * TG
    e1 = meta[:, 0].astype(jnp.int32)
    e2 = meta[:, 1].astype(jnp.int32)
    pos = jnp.stack([offs[e1] + meta[:, 2].astype(jnp.int32), offs[e2] + meta[:, 3].astype(jnp.int32)], axis=1)
    pos = pos.reshape(2 * T)
    tile_expert = jnp.sum((jnp.arange(NG)[:, None] >= tile_end[None, :]).astype(jnp.int32), axis=1)
    tile_expert = jnp.minimum(tile_expert, N_EXPERTS - 1)
    n_active = tile_end[-1:].astype(jnp.int32)
    xg, w2 = _dispatch(pos, h_tiled, jnp.zeros((P_ROWS * ROW_TILE, LANES), F32),
                       w2_f32.reshape(N_EXPERTS * E_FF, D))
    eo = _expert_ffn(tile_expert, n_active, xg, w1, w3, w2.reshape(N_EXPERTS, E_FF, D))
    return _combine_ln(pos, y, mods, meta, ln_g, ln_b, eo)


def _l0_weight_layouts(w_in, w_q_up, w_kv_up):
    a, b, c = MLA_Q_LORA, MLA_Q_LORA + MLA_KV_LORA, MLA_Q_LORA + MLA_KV_LORA + MLA_ROPE
    k_rope = w_in[:, b:c]
    w_in_r = jnp.concatenate([w_in[:, :b], w_in[:, c:], k_rope, k_rope, k_rope, k_rope], axis=1)
    wq = w_q_up.reshape(MLA_Q_LORA, MLA_HEADS, MLA_NOPE + MLA_ROPE)
    w_q_up_r = jnp.concatenate([wq[:, :, :MLA_NOPE].reshape(MLA_Q_LORA, -1),
                                wq[:, :, MLA_NOPE:].reshape(MLA_Q_LORA, -1)], axis=1)
    wkv = w_kv_up.reshape(MLA_KV_LORA, MLA_HEADS, MLA_NOPE + MLA_V)
    w_kv_up_r = jnp.concatenate([wkv[:, :, :MLA_NOPE].reshape(MLA_KV_LORA, -1),
                                 wkv[:, :, MLA_NOPE:].reshape(MLA_KV_LORA, -1)], axis=1)
    return w_in_r.astype(BF16), w_q_up_r.astype(BF16), w_kv_up_r.astype(BF16)


def kernel(x_prompt, x_sample, cache_l0_mla_ckv, cache_l0_mla_krope, cache_l0_swa_k, cache_l0_swa_v,
           cache_l1_na_k, cache_l1_na_v, c, c_ctx,
           l0_ada_w, l0_ada_b, l0_w_in, l0_mla_q_norm, l0_mla_w_q_up, l0_mla_kv_norm, l0_mla_w_kv_up,
           l0_swa_sink, l0_w_out, l0_ln1_g, l0_ln1_b, l0_ffn_w1, l0_ffn_w3, l0_ffn_w2, l0_ln2_g, l0_ln2_b,
           l1_ada_w, l1_ada_b, l1_w_in, l1_na_rel_bias, l1_w_out, l1_ln1_g, l1_ln1_b,
           l1_moe_router_w, l1_moe_router_b, l1_moe_w1, l1_moe_w3, l1_moe_w2, l1_ln2_g, l1_ln2_b):
    cond = jnp.concatenate([c_ctx[None, :], c, jnp.zeros((8 - N_COND, D), F32)], axis=0)

    mods = _adaln(cond, l0_ada_w, l0_ada_b)
    w_in_r, w_q_up_r, w_kv_up_r = _l0_weight_layouts(l0_w_in, l0_mla_w_q_up, l0_mla_w_kv_up)
    y, qn, qr, ckv, kn, vm, kr4, sq, sk, sv = _l0_in(
        x_prompt.reshape(T_P, D), x_sample.reshape(T_S, D), mods, w_in_r, l0_mla_q_norm, w_q_up_r,
        l0_mla_kv_norm, w_kv_up_r, _rope_tables(MLA_ROPE, 256), _rope_tables(HEAD_DIM, 512))
    knc, vc = _kv_up(cache_l0_mla_ckv.reshape(DEC_BATCH * PAST, MLA_KV_LORA), w_kv_up_r)
    kr4c = jnp.tile(cache_l0_mla_krope.reshape(DEC_BATCH * PAST, MLA_ROPE), (1, 4))
    skc = cache_l0_swa_k.reshape(DEC_BATCH * PAST, SWA_KV_HEADS * HEAD_DIM)
    svc = cache_l0_swa_v.reshape(DEC_BATCH * PAST, SWA_KV_HEADS * HEAD_DIM)
    o_p, ffn_w1, ffn_w3, ffn_w2 = _l0_attn_prompt(l0_swa_sink, qn, qr, kn, vm, kr4, sq, sk, sv,
                                                  [l0_ffn_w1, l0_ffn_w3, l0_ffn_w2])
    o_s = _l0_attn_sample(l0_swa_sink, qn, qr, kn, vm, kr4, sq, sk, sv, knc, vc, kr4c, skc, svc)
    y = _out_ln(o_p, o_s, l0_w_out.astype(BF16), y, mods, l0_ln1_g, l0_ln1_b)
    y, moe_w1, moe_w3 = _ffn_ln(y, mods, ffn_w1, ffn_w3, ffn_w2, l0_ln2_g, l0_ln2_b,
                                l1_moe_w1.reshape(N_EXPERTS * D, E_FF), l1_moe_w3.reshape(N_EXPERTS * D, E_FF))
    new_ckv = ckv[:T_P].reshape(BATCH, SEQ, MLA_KV_LORA)
    new_krope = kr4[:T_P, :MLA_ROPE].reshape(BATCH, SEQ, MLA_ROPE)
    new_sk = sk[:T_P].reshape(BATCH, SEQ, SWA_KV_HEADS, HEAD_DIM)
    new_sv = sv[:T_P].reshape(BATCH, SEQ, SWA_KV_HEADS, HEAD_DIM)

    mods = _adaln(cond, l1_ada_w, l1_ada_b)
    q, k_p, v_p, k_s, v_s = _l1_in(y, mods, l1_w_in.astype(BF16))
    o_p = _l1_attn_prompt(q, k_p, v_p)
    o_s = _l1_attn_sample(q, k_s, v_s, cache_l1_na_k.reshape(DEC_BATCH, PAST, D),
                          cache_l1_na_v.reshape(DEC_BATCH, PAST, D), _na_bias_rows(l1_na_rel_bias))
    y = _out_ln(o_p, o_s, l1_w_out.astype(BF16), y, mods, l1_ln1_g, l1_ln1_b)
    y_p, y_s = _moe_ln(y, mods, l1_moe_router_w, l1_moe_router_b, moe_w1.reshape(N_EXPERTS, D, E_FF),
                       moe_w3.reshape(N_EXPERTS, D, E_FF), l1_moe_w2, l1_ln2_g, l1_ln2_b)
    new_k = k_p.reshape(BATCH, SEQ, NA_HEADS, HEAD_DIM)
    new_v = v_p.reshape(BATCH, SEQ, NA_HEADS, HEAD_DIM)

    return (y_p.reshape(BATCH, SEQ, D), y_s.reshape(DEC_BATCH, DEC_SEQ, D),
            new_ckv, new_krope, new_sk, new_sv, new_k, new_v)
```

```python
import functools

import jax
import jax.numpy as jnp
from jax import lax
from jax.experimental import pallas as pl
from jax.experimental.pallas import tpu as pltpu

F32 = jnp.float32
BF16 = jnp.bfloat16

D = 1024
BATCH, SEQ = 32, 256
DEC_BATCH, DEC_SEQ = 2, 1024
PAST = 256
GRID_W = 64
T_P = BATCH * SEQ
T_S = DEC_BATCH * DEC_SEQ
T = T_P + T_S
N_COND = 1 + DEC_BATCH

MLA_HEADS, MLA_Q_LORA, MLA_KV_LORA, MLA_NOPE, MLA_ROPE, MLA_V = 8, 384, 256, 64, 32, 64
SWA_HEADS, SWA_KV_HEADS, SWA_WINDOW, HEAD_DIM = 8, 2, 128, 64
NA_HEADS, NA_WIN_ROWS, NA_WIN_COLS = 16, 8, 16
D_FF, N_EXPERTS, E_FF = 2816, 8, 3584
ROPE_THETA = 10000.0
LN_EPS, RMS_EPS = 1e-5, 1e-6
NEG = -1e30
ALPHA = 4.0 ** 0.25

LANES = 128
TM = 512
NT = T // TM
NP_TILES = T_P // TM
TILES_PER_SAMPLE = DEC_SEQ // TM
TG = 256
P_ROWS = 2 * T + N_EXPERTS * TG
NG = P_ROWS // TG
F_CHUNK = 896
VMEM_LIMIT = 56 * 1024 * 1024


def _params(*sem):
    return pltpu.CompilerParams(dimension_semantics=sem, vmem_limit_bytes=VMEM_LIMIT)


def _const_spec(shape, single_buffer=False):
    if single_buffer:
        return pl.BlockSpec(shape, lambda *_: (0,) * len(shape), pipeline_mode=pl.Buffered(1))
    return pl.BlockSpec(shape, lambda *_: (0,) * len(shape))


def _cast_specs(shape, steps):
    rows, width = shape
    spec = pl.BlockSpec((rows // steps, width), lambda i, *_: (jnp.minimum(i, steps - 1), 0))
    return spec, jax.ShapeDtypeStruct(shape, BF16)


def _cast_block(i, steps, src_ref, dst_ref):
    @pl.when(i < steps)
    def _():
        dst_ref[...] = src_ref[...].astype(BF16)


ROW_TILE = D // LANES


def _store_rows_tiled(ref, x):
    n = x.shape[0]
    for k in range(ROW_TILE):
        ref[pl.ds(k, n, stride=ROW_TILE), :] = x[:, k * LANES:(k + 1) * LANES]


def _load_rows_tiled(ref, n):
    return jnp.concatenate([ref[pl.ds(k, n, stride=ROW_TILE), :] for k in range(ROW_TILE)], axis=1)


def _cond_of_tile(i, rows=TM):
    per_sample = DEC_SEQ // rows
    return jnp.maximum((i - (T_P // rows - per_sample)) // per_sample, 0)


def _mod_spec(rows=TM):
    return pl.BlockSpec((1, 6, D), lambda i: (_cond_of_tile(i, rows), 0, 0))


def _row_spec(width, rows=TM):
    return pl.BlockSpec((rows, width), lambda i: (i, 0))


def _dot(a, b):
    return jnp.dot(a, b, preferred_element_type=F32)


def _dot_nt(a, b):
    return lax.dot_general(a, b, (((1,), (1,)), ((), ())), preferred_element_type=F32)


def _layer_norm(r, g, b):
    mu = jnp.mean(r, axis=-1, keepdims=True)
    d = r - mu
    var = jnp.mean(d * d, axis=-1, keepdims=True)
    return d * lax.rsqrt(var + LN_EPS) * g + b


def _rms_norm(x, g):
    return x * lax.rsqrt(jnp.mean(x * x, axis=-1, keepdims=True) + RMS_EPS) * g


def _silu(x):
    return x * jax.nn.sigmoid(x)


def _ada_kernel(c_ref, w_ref, b_ref, o_ref):
    s = _silu(c_ref[...]).astype(BF16)
    o_ref[...] = _dot(s, w_ref[...].astype(BF16)) + b_ref[...]


def _adaln(cond, ada_w, ada_b):
    nb = 1536
    out = pl.pallas_call(
        _ada_kernel,
        out_shape=jax.ShapeDtypeStruct((8, 6 * D), F32),
        grid=(6 * D // nb,),
        in_specs=[_const_spec((8, D)), pl.BlockSpec((D, nb), lambda j: (0, j)),
                  pl.BlockSpec((1, nb), lambda j: (0, j))],
        out_specs=pl.BlockSpec((8, nb), lambda j: (0, j)),
        compiler_params=_params("arbitrary"),
        name="adaln",
    )(cond, ada_w, ada_b.reshape(1, 6 * D))
    return out[:N_COND].reshape(N_COND, 6, D)


def _rope_tables(head_dim, width):
    half = head_dim // 2
    nf = half // 2
    lane = jnp.arange(width)
    d = lane % head_dim
    dd = d % half
    f = dd % nf
    inv = ROPE_THETA ** (-f.astype(F32) / nf)
    t = jnp.arange(DEC_SEQ)
    pos = jnp.where((d // half)[None, :] == 0, (t // GRID_W)[:, None], (t % GRID_W)[:, None])
    ang = pos.astype(F32) * inv[None, :]
    cos, sin = jnp.cos(ang), jnp.sin(ang)
    first = (dd < nf)[None, :]
    return cos, jnp.where(first, -sin, 0.0), jnp.where(first, 0.0, sin)


def _rope(x, cos, sin_up, sin_dn, nf):
    w = x.shape[-1]
    return x * cos + pltpu.roll(x, w - nf, 1) * sin_up + pltpu.roll(x, nf, 1) * sin_dn


L0_COLS = MLA_Q_LORA + MLA_KV_LORA + 512 + 128 + 128 + 128


def _l0_in_kernel(xp_ref, xs_ref, mod_ref, win_ref, qn_ref, wq_ref, kvn_ref, wkv_ref,
                  c8_ref, su8_ref, sd8_ref, c16_ref, su16_ref, sd16_ref,
                  y_o, qnope_o, qrope_o, ckv_o, knope_o, vmla_o, kr4_o, sq_o, sk_o, sv_o):
    i = pl.program_id(0)
    m = mod_ref[0]
    x = jnp.where(i < NP_TILES, xp_ref[...], xs_ref[...])
    y_o[...] = x
    h = (x * (1.0 + m[1:2]) + m[0:1]).astype(BF16)
    z = _dot(h, win_ref[...])
    q_lat = z[:, 0:384]
    kv_lat = z[:, 384:640]
    sq = z[:, 640:1152]
    sk = z[:, 1152:1280]
    sv = z[:, 1280:1408]
    kr4 = z[:, 1408:1536]
    q = _dot(_rms_norm(q_lat, qn_ref[...]).astype(BF16), wq_ref[...])
    c_kv = _rms_norm(kv_lat, kvn_ref[...])
    kv = _dot(c_kv.astype(BF16), wkv_ref[...])
    qnope_o[...] = q[:, 0:512].astype(BF16)
    ckv_o[...] = c_kv
    knope_o[...] = kv[:, 0:512].astype(BF16)
    vmla_o[...] = kv[:, 512:1024].astype(BF16)
    sv_o[...] = sv
    q_rope = q[:, 512:768]

    @pl.when(i < NP_TILES)
    def _():
        qrope_o[...] = q_rope.astype(BF16)
        kr4_o[...] = kr4
        sq_o[...] = sq.astype(BF16)
        sk_o[...] = sk

    @pl.when(i >= NP_TILES)
    def _():
        c8, su8, sd8 = c8_ref[...], su8_ref[...], sd8_ref[...]
        c16, su16, sd16 = c16_ref[...], su16_ref[...], sd16_ref[...]
        qrope_o[...] = _rope(q_rope, c8, su8, sd8, 8).astype(BF16)
        kr4_o[...] = _rope(kr4, c8[:, :128], su8[:, :128], sd8[:, :128], 8)
        sq_o[...] = _rope(sq, c16, su16, sd16, 16).astype(BF16)
        sk_o[...] = _rope(sk, c16[:, :128], su16[:, :128], sd16[:, :128], 16)


def _prompt_tile_spec(width):
    return pl.BlockSpec((TM, width), lambda i, *_: (jnp.minimum(i, NP_TILES - 1), 0))


def _sample_tile_spec(width):
    return pl.BlockSpec((TM, width), lambda i, *_: (jnp.maximum(i - NP_TILES, 0), 0))


def _l0_in(x_prompt, x_sample, mods, w_in_r, q_norm, w_q_up_r, kv_norm, w_kv_up_r, tabs8, tabs16):
    def tab_spec(width):
        return pl.BlockSpec((TM, width), lambda i: (jnp.maximum(i - NP_TILES, 0) % TILES_PER_SAMPLE, 0))

    out_shape = [
        jax.ShapeDtypeStruct((T, D), F32),
        jax.ShapeDtypeStruct((T, 512), BF16),
        jax.ShapeDtypeStruct((T, 256), BF16),
        jax.ShapeDtypeStruct((T, 256), F32),
        jax.ShapeDtypeStruct((T, 512), BF16),
        jax.ShapeDtypeStruct((T, 512), BF16),
        jax.ShapeDtypeStruct((T, 128), F32),
        jax.ShapeDtypeStruct((T, 512), BF16),
        jax.ShapeDtypeStruct((T, 128), F32),
        jax.ShapeDtypeStruct((T, 128), F32),
    ]
    return pl.pallas_call(
        _l0_in_kernel,
        out_shape=out_shape,
        grid=(NT,),
        in_specs=[_prompt_tile_spec(D), _sample_tile_spec(D), _mod_spec(), _const_spec((D, L0_COLS)),
                  _const_spec((1, MLA_Q_LORA)), _const_spec((MLA_Q_LORA, 768)),
                  _const_spec((1, MLA_KV_LORA)), _const_spec((MLA_KV_LORA, 1024)),
                  tab_spec(256), tab_spec(256), tab_spec(256),
                  tab_spec(512), tab_spec(512), tab_spec(512)],
        out_specs=[_row_spec(s.shape[1]) for s in out_shape],
        compiler_params=_params("arbitrary"),
        name="l0_in_proj",
    )(x_prompt, x_sample, mods, w_in_r, q_norm.reshape(1, -1), w_q_up_r, kv_norm.reshape(1, -1), w_kv_up_r,
      *tabs8, *tabs16)


def _kv_up_kernel(c_ref, w_ref, k_o, v_o):
    kv = _dot(c_ref[...].astype(BF16), w_ref[...])
    k_o[...] = kv[:, 0:512].astype(BF16)
    v_o[...] = kv[:, 512:1024].astype(BF16)


def _kv_up(ckv, w_kv_up_r):
    n = ckv.shape[0]
    return pl.pallas_call(
        _kv_up_kernel,
        out_shape=[jax.ShapeDtypeStruct((n, 512), BF16)] * 2,
        grid=(1,),
        in_specs=[_const_spec((n, MLA_KV_LORA)), _const_spec((MLA_KV_LORA, 1024))],
        out_specs=[_const_spec((n, 512))] * 2,
        compiler_params=_params("arbitrary"),
        name="l0_ctx_kv_up",
    )(ckv, w_kv_up_r)


def _lane():
    return lax.broadcasted_iota(jnp.int32, (1, LANES), 1)


def _attend(scores, values, extra_logit=None):
    m = jnp.max(scores[0], axis=1, keepdims=True)
    for s in scores[1:]:
        m = jnp.maximum(m, jnp.max(s, axis=1, keepdims=True))
    if extra_logit is not None:
        m = jnp.maximum(m, extra_logit)
    den = None
    acc = None
    for s, v in zip(scores, values):
        e = jnp.exp(s - m)
        d = jnp.sum(e, axis=1, keepdims=True)
        a = _dot(e.astype(BF16), v)
        den = d if den is None else den + d
        acc = a if acc is None else acc + a
    if extra_logit is not None:
        den = den + jnp.exp(extra_logit - m)
    return acc / den


def _head_pair(q2, keys, values, scale, masks=None, biases=None, sinks=None, q_extra=None):
    lo = _lane() < 64
    zero = jnp.zeros_like(q2)
    outs = []
    for hh in range(2):
        qm = jnp.where(lo if hh == 0 else jnp.logical_not(lo), q2, zero)
        if q_extra is not None:
            qm = jnp.concatenate([qm, q_extra[hh]], axis=1)
        scores = []
        for n, k in enumerate(keys):
            s = _dot_nt(qm, k) * scale
            if biases is not None and biases[n] is not None:
                s = s + biases[n][hh]
            if masks is not None and masks[n] is not None:
                s = jnp.where(masks[n], s, NEG)
            scores.append(s)
        outs.append(_attend(scores, values, None if sinks is None else sinks[hh]))
    return jnp.where(lo, outs[0], outs[1])


def _dup_halves(x):
    lo = _lane() < 64
    sw = pltpu.roll(x, 64, 1)
    return jnp.where(lo, x, sw), jnp.where(lo, sw, x)


MLA_SCALE = (MLA_NOPE + MLA_ROPE) ** -0.5
HD_SCALE = HEAD_DIM ** -0.5


def _mla_pairs(qn_ref, qr_ref, key_sets, o_ref):
    lane = _lane()
    for j in range(MLA_HEADS // 2):
        cols = slice(128 * j, 128 * (j + 1))
        qr = qr_ref[:, 128 * (j // 2):128 * (j // 2 + 1)]
        zero = jnp.zeros_like(qr)
        q_extra = [jnp.where((lane // MLA_ROPE) == ((2 * j + hh) % 4), qr, zero) for hh in range(2)]
        keys = [jnp.concatenate([kn[:, cols], kr4], axis=1) for kn, kr4, _ in key_sets]
        values = [v[:, cols] for _, _, v in key_sets]
        o = _head_pair(qn_ref[:, cols], keys, values, MLA_SCALE, q_extra=q_extra)
        o_ref[:, cols] = o.astype(BF16)


def _swa_pairs(sink_ref, sq_ref, key_sets, masks, o_ref):
    kd = [[a.astype(BF16) for a in _dup_halves(k)] for k, _ in key_sets]
    vd = [[a.astype(BF16) for a in _dup_halves(v)] for _, v in key_sets]
    for g in range(SWA_KV_HEADS):
        for u in range(2):
            c = 2 * g + u
            cols = slice(128 * c, 128 * (c + 1))
            sinks = [sink_ref[2 * c + hh] for hh in range(2)]
            o = _head_pair(sq_ref[:, cols], [k[g] for k in kd], [v[g] for v in vd], HD_SCALE,
                           masks=masks, sinks=sinks)
            o_ref[:, 512 + 128 * c:512 + 128 * (c + 1)] = o.astype(BF16)


def _cast_steps(rows, max_steps):
    steps = max_steps
    while rows % (16 * steps):
        steps //= 2
    return steps


def _l0_attn_prompt_kernel(cast_steps, sink_ref, qn_ref, qr_ref, kn_ref, v_ref, kr4_ref, sq_ref, sk_ref, sv_ref,
                           wf1_ref, wf2_ref, wf3_ref, o_ref, wb1_ref, wb2_ref, wb3_ref):
    _mla_pairs(qn_ref, qr_ref, [(kn_ref, kr4_ref[...].astype(BF16), v_ref)], o_ref)
    _swa_pairs(sink_ref, sq_ref, [(sk_ref[...], sv_ref[...])], None, o_ref)
    for steps, wf_ref, wb_ref in zip(cast_steps, (wf1_ref, wf2_ref, wf3_ref), (wb1_ref, wb2_ref, wb3_ref)):
        _cast_block(pl.program_id(0), steps, wf_ref, wb_ref)


def _l0_attn_prompt(sink, qn, qr, kn, vm, kr4, sq, sk, sv, weights_f32):
    spec = lambda w: pl.BlockSpec((SEQ, w), lambda b, *_: (b, 0))
    cast_steps = tuple(_cast_steps(w.shape[0], BATCH) for w in weights_f32)
    casts = [_cast_specs(w.shape, s) for w, s in zip(weights_f32, cast_steps)]
    return pl.pallas_call(
        functools.partial(_l0_attn_prompt_kernel, cast_steps),
        out_shape=[jax.ShapeDtypeStruct((T_P, D), BF16)] + [c[1] for c in casts],
        grid_spec=pltpu.PrefetchScalarGridSpec(
            num_scalar_prefetch=1, grid=(BATCH,),
            in_specs=[spec(512), spec(256), spec(512), spec(512), spec(128), spec(512), spec(128), spec(128)]
            + [c[0] for c in casts],
            out_specs=[spec(D)] + [c[0] for c in casts]),
        compiler_params=_params("arbitrary"),
        name="l0_attn_prompt",
    )(sink, qn, qr, kn, vm, kr4, sq, sk, sv, *weights_f32)


TQ_S = 256


def _l0_attn_sample_kernel(sink_ref, qn_ref, qr_ref, sq_ref, kn_ref, v_ref, kr4_ref, sk_ref, sv_ref,
                           knc_ref, vc_ref, kr4c_ref, skc_ref, svc_ref, o_ref):
    i = pl.program_id(1)
    _mla_pairs(qn_ref, qr_ref,
               [(knc_ref, kr4c_ref[...].astype(BF16), vc_ref), (kn_ref, kr4_ref[...].astype(BF16), v_ref)],
               o_ref)
    span = TQ_S + 2 * SWA_WINDOW
    start = pl.multiple_of(jnp.clip(i * TQ_S - SWA_WINDOW, 0, DEC_SEQ - span), SWA_WINDOW)
    qpos = i * TQ_S + lax.broadcasted_iota(jnp.int32, (TQ_S, span), 0)
    kpos = start + lax.broadcasted_iota(jnp.int32, (TQ_S, span), 1)
    band = jnp.abs(qpos - kpos) <= SWA_WINDOW
    keys = pl.ds(start, span)
    _swa_pairs(sink_ref, sq_ref, [(skc_ref[...], svc_ref[...]), (sk_ref[keys, :], sv_ref[keys, :])],
               [None, band], o_ref)


def _l0_attn_sample(sink, qn, qr, kn, vm, kr4, sq, sk, sv, knc, vc, kr4c, skc, svc):
    nq = DEC_SEQ // TQ_S
    qspec = lambda w: pl.BlockSpec((TQ_S, w), lambda b, i, *_: (T_P // TQ_S + b * nq + i, 0))
    kspec = lambda w: pl.BlockSpec((DEC_SEQ, w), lambda b, i, *_: (T_P // DEC_SEQ + b, 0))
    cspec = lambda w: pl.BlockSpec((PAST, w), lambda b, i, *_: (b, 0))
    return pl.pallas_call(
        _l0_attn_sample_kernel,
        out_shape=jax.ShapeDtypeStruct((T_S, D), BF16),
        grid_spec=pltpu.PrefetchScalarGridSpec(
            num_scalar_prefetch=1, grid=(DEC_BATCH, nq),
            in_specs=[qspec(512), qspec(256), qspec(512),
                      kspec(512), kspec(512), kspec(128), kspec(128), kspec(128),
                      cspec(512), cspec(512), cspec(128), cspec(128), cspec(128)],
            out_specs=pl.BlockSpec((TQ_S, D), lambda b, i, *_: (b * nq + i, 0))),
        compiler_params=_params("arbitrary", "arbitrary"),
        name="l0_attn_sample",
    )(sink, qn, qr, sq, kn, vm, kr4, sk, sv, knc, vc, kr4c, skc, svc)


def _out_ln_kernel(op_ref, os_ref, w_ref, y_ref, mod_ref, g_ref, b_ref, out_ref):
    i = pl.program_id(0)
    m = mod_ref[0]

    def finish(o):
        r = ALPHA * y_ref[...] + m[2:3] * _dot(o, w_ref[...])
        out_ref[...] = _layer_norm(r, g_ref[...], b_ref[...])

    @pl.when(i < NP_TILES)
    def _():
        finish(op_ref[...])

    @pl.when(i >= NP_TILES)
    def _():
        finish(os_ref[...])


def _out_ln(o_prompt, o_sample, w_out, y, mods, ln_g, ln_b):
    return pl.pallas_call(
        _out_ln_kernel,
        out_shape=jax.ShapeDtypeStruct((T, D), F32),
        grid=(NT,),
        in_specs=[_prompt_tile_spec(D), _sample_tile_spec(D),
                  _const_spec((D, D)), _row_spec(D), _mod_spec(),
                  _const_spec((1, D)), _const_spec((1, D))],
        out_specs=_row_spec(D),
        compiler_params=_params("arbitrary"),
        name="out_proj_ln",
    )(o_prompt, o_sample, w_out, y, mods, ln_g.reshape(1, D), ln_b.reshape(1, D))


FFN_CHUNK = D_FF // 2
FFN_TM = 256
FFN_CAST_STEPS = 32


def _ffn_ln_kernel(y_ref, mod_ref, w1_ref, w3_ref, w2_ref, g_ref, b_ref, wfa_ref, wfb_ref,
                   out_ref, wba_ref, wbb_ref):
    i = pl.program_id(0)
    _cast_block(i, FFN_CAST_STEPS, wfa_ref, wba_ref)
    _cast_block(i, FFN_CAST_STEPS, wfb_ref, wbb_ref)
    m = mod_ref[0]
    y = y_ref[...]
    h = (y * (1.0 + m[4:5]) + m[3:4]).astype(BF16)
    acc = None
    for c in range(D_FF // FFN_CHUNK):
        cols = slice(c * FFN_CHUNK, (c + 1) * FFN_CHUNK)
        a = _dot(h, w1_ref[:, cols])
        g = _dot(h, w3_ref[:, cols])
        part = _dot((_silu(a) * g).astype(BF16), w2_ref[cols, :])
        acc = part if acc is None else acc + part
    out_ref[...] = _layer_norm(ALPHA * y + m[5:6] * acc, g_ref[...], b_ref[...])


def _ffn_ln(y, mods, w1, w3, w2, ln_g, ln_b, wa_f32, wb_f32):
    wa_spec, wa_shape = _cast_specs(wa_f32.shape, FFN_CAST_STEPS)
    wb_spec, wb_shape = _cast_specs(wb_f32.shape, FFN_CAST_STEPS)
    rows = _row_spec(D, FFN_TM)
    return pl.pallas_call(
        _ffn_ln_kernel,
        out_shape=[jax.ShapeDtypeStruct((T, D), F32), wa_shape, wb_shape],
        grid=(T // FFN_TM,),
        in_specs=[rows, _mod_spec(FFN_TM), _const_spec((D, D_FF), True), _const_spec((D, D_FF), True),
                  _const_spec((D_FF, D), True), _const_spec((1, D)), _const_spec((1, D)), wa_spec, wb_spec],
        out_specs=[rows, wa_spec, wb_spec],
        compiler_params=_params("arbitrary"),
        name="ffn_ln",
    )(y, mods, w1, w3, w2, ln_g.reshape(1, D), ln_b.reshape(1, D), wa_f32, wb_f32)


def _l1_in_kernel(y_ref, mod_ref, w_ref, q_o, k_o, v_o, kh_o, vh_o):
    i = pl.program_id(0)
    m = mod_ref[0]
    h = (y_ref[...] * (1.0 + m[1:2]) + m[0:1]).astype(BF16)
    z = _dot(h, w_ref[...])
    k = z[:, D:2 * D]
    v = z[:, 2 * D:3 * D]
    q_o[...] = z[:, 0:D].astype(BF16)
    k_o[...] = k.astype(BF16)
    v_o[...] = v.astype(BF16)

    @pl.when(i < NP_TILES)
    def _():
        for hd in range(NA_HEADS):
            cols = slice(hd * HEAD_DIM, (hd + 1) * HEAD_DIM)
            kh_o[pl.ds(hd, TM, stride=NA_HEADS), :] = k[:, cols]
            vh_o[pl.ds(hd, TM, stride=NA_HEADS), :] = v[:, cols]


def _l1_in(y, mods, w_in):
    heads = pl.BlockSpec((TM * NA_HEADS, HEAD_DIM), lambda i: (jnp.minimum(i, NP_TILES - 1), 0))
    return pl.pallas_call(
        _l1_in_kernel,
        out_shape=[jax.ShapeDtypeStruct((T, D), BF16)] * 3
        + [jax.ShapeDtypeStruct((T_P * NA_HEADS, HEAD_DIM), F32)] * 2,
        grid=(NT,),
        in_specs=[_row_spec(D), _mod_spec(), _const_spec((D, 3 * D))],
        out_specs=[_row_spec(D)] * 3 + [heads, heads],
        compiler_params=_params("arbitrary"),
        name="l1_in_proj",
    )(y, mods, w_in)


def _l1_attn_prompt_kernel(q_ref, k_ref, v_ref, o_ref):
    for j in range(NA_HEADS // 2):
        cols = slice(128 * j, 128 * (j + 1))
        o = _head_pair(q_ref[:, cols], [k_ref[:, cols].astype(BF16)], [v_ref[:, cols].astype(BF16)], HD_SCALE)
        o_ref[:, cols] = o.astype(BF16)


def _l1_attn_prompt(q, k, v):
    spec = pl.BlockSpec((SEQ, D), lambda b: (b, 0))
    return pl.pallas_call(
        _l1_attn_prompt_kernel,
        out_shape=jax.ShapeDtypeStruct((T_P, D), BF16),
        grid=(BATCH,),
        in_specs=[spec, spec, spec],
        out_specs=spec,
        compiler_params=_params("arbitrary"),
        name="l1_attn_prompt",
    )(q, k, v)


NA_ROWS = DEC_SEQ // GRID_W
NA_TILE_ROWS = 4
NA_TQ = NA_TILE_ROWS * GRID_W
NA_SPAN = NA_WIN_ROWS + NA_TILE_ROWS
NA_DR = 2 * NA_WIN_ROWS - 1
COL_SPAN = 2 * GRID_W - 1


def _na_span_start(t):
    first = max(0, min(t * NA_TILE_ROWS - NA_WIN_ROWS // 2, NA_ROWS - NA_WIN_ROWS))
    return min(first, NA_ROWS - NA_SPAN)


def _l1_attn_sample_kernel(q_ref, k_ref, v_ref, kc_ref, vc_ref, ext_ref, o_ref):
    kc = kc_ref[0].astype(BF16)
    vc = vc_ref[0].astype(BF16)
    lo = _lane() < GRID_W
    qcol = lax.broadcasted_iota(jnp.int32, (GRID_W, LANES), 0)
    kcol = lax.broadcasted_iota(jnp.int32, (GRID_W, LANES), 1) % GRID_W
    cs = jnp.clip(qcol - NA_WIN_COLS // 2, 0, GRID_W - NA_WIN_COLS)
    col_ok = jnp.logical_and(kcol >= cs, kcol < cs + NA_WIN_COLS)
    neg = jnp.full((GRID_W, LANES), NEG, F32)

    def bias_tile(hh, d, half):
        x = jnp.broadcast_to(ext_ref[hh, d:d + 1, :], (GRID_W, LANES))
        shift = (LANES - COL_SPAN // 2 + GRID_W * half) % LANES
        return jnp.where(col_ok, pltpu.roll(x, shift, 1, stride=1, stride_axis=0), neg)

    tiles = [[[bias_tile(hh, d, half) for half in range(2)] for d in range(NA_DR)] for hh in range(2)]

    for t in range(NA_ROWS // NA_TILE_ROWS):
        ws = _na_span_start(t)
        keys = slice(ws * GRID_W, (ws + NA_SPAN) * GRID_W)
        kw = k_ref[keys, :].astype(BF16)
        vw = v_ref[keys, :].astype(BF16)
        bias = []
        for hh in range(2):
            rows = []
            for rr in range(NA_TILE_ROWS):
                r = t * NA_TILE_ROWS + rr
                rs = max(0, min(r - NA_WIN_ROWS // 2, NA_ROWS - NA_WIN_ROWS))
                blocks = []
                for u in range(NA_SPAN // 2):
                    halves = []
                    for half in range(2):
                        kr = ws + 2 * u + half
                        ok = rs <= kr < rs + NA_WIN_ROWS
                        halves.append(tiles[hh][kr - r + NA_WIN_ROWS - 1][half] if ok else neg)
                    blocks.append(jnp.where(lo, halves[0], halves[1]))
                rows.append(jnp.concatenate(blocks, axis=1))
            bias.append(jnp.concatenate(rows, axis=0))
        qrows = slice(t * NA_TQ, (t + 1) * NA_TQ)
        o = _head_pair(q_ref[qrows, :], [kw, kc], [vw, vc], HD_SCALE, biases=[bias, None])
        o_ref[qrows, :] = o.astype(BF16)


def _l1_attn_sample(q, k, v, kc, vc, ext):
    lat = pl.BlockSpec((DEC_SEQ, LANES), lambda j, b: (T_P // DEC_SEQ + b, j))
    ctx = pl.BlockSpec((1, PAST, LANES), lambda j, b: (b, 0, j))
    return pl.pallas_call(
        _l1_attn_sample_kernel,
        out_shape=jax.ShapeDtypeStruct((T_S, D), BF16),
        grid=(NA_HEADS // 2, DEC_BATCH),
        in_specs=[lat, lat, lat, ctx, ctx, pl.BlockSpec((2, NA_DR, LANES), lambda j, b: (j, 0, 0))],
        out_specs=pl.BlockSpec((DEC_SEQ, LANES), lambda j, b: (b, j)),
        compiler_params=_params("arbitrary", "arbitrary"),
        name="l1_attn_sample",
    )(q, k, v, kc, vc, ext)


def _na_bias_rows(rel_bias):
    rb = rel_bias.astype(F32)
    n_lo = GRID_W - 1 - (NA_WIN_COLS - 1)
    n_hi = LANES - n_lo - rb.shape[-1]
    return jnp.concatenate([jnp.repeat(rb[..., :1], n_lo, axis=-1), rb,
                            jnp.repeat(rb[..., -1:], n_hi, axis=-1)], axis=-1)


def _router_kernel(y_ref, mod_ref, rw_ref, rb_ref, h_o, meta_o, cnt_o, carry_ref):
    i = pl.program_id(0)

    @pl.when(i == 0)
    def _():
        carry_ref[...] = jnp.zeros_like(carry_ref)

    m = mod_ref[0]
    h = y_ref[...] * (1.0 + m[4:5]) + m[3:4]
    _store_rows_tiled(h_o, h)
    logits = jnp.dot(h, rw_ref[...], preferred_element_type=F32, precision=lax.Precision.HIGHEST) + rb_ref[...]
    lane = lax.broadcasted_iota(jnp.int32, (TM, LANES), 1).astype(F32)
    m1 = jnp.max(logits, axis=1, keepdims=True)
    i1 = jnp.min(jnp.where(logits == m1, lane, float(LANES)), axis=1, keepdims=True)
    sel1 = lane == i1
    rest = jnp.where(sel1, -jnp.inf, logits)
    m2 = jnp.max(rest, axis=1, keepdims=True)
    i2 = jnp.min(jnp.where(rest == m2, lane, float(LANES)), axis=1, keepdims=True)
    sel2 = lane == i2
    e2 = jnp.exp(m2 - m1)
    w1 = 1.0 / (1.0 + e2)
    w2 = e2 / (1.0 + e2)
    sel = jnp.logical_or(sel1, sel2)
    rr = lax.broadcasted_iota(jnp.int32, (TM, TM), 0)
    cc = lax.broadcasted_iota(jnp.int32, (TM, TM), 1)
    tri = jnp.where(cc < rr, 1.0, 0.0).astype(BF16)
    ahead = _dot(tri, jnp.where(sel, 1.0, 0.0).astype(BF16)) + carry_ref[...]
    r1 = jnp.sum(jnp.where(sel1, ahead, 0.0), axis=1, keepdims=True)
    r2 = jnp.sum(jnp.where(sel2, ahead, 0.0), axis=1, keepdims=True)
    meta = jnp.where(lane == 0, i1, 0.0)
    meta = jnp.where(lane == 1, i2, meta)
    meta = jnp.where(lane == 2, r1, meta)
    meta = jnp.where(lane == 3, r2, meta)
    meta = jnp.where(lane == 4, w1, meta)
    meta = jnp.where(lane == 5, w2, meta)
    meta_o[...] = meta
    carry_ref[...] = carry_ref[...] + jnp.sum(jnp.where(sel, 1.0, 0.0), axis=0, keepdims=True)
    cnt_o[...] = carry_ref[...]


def _router(y, mods, router_w, router_b):
    rw = jnp.zeros((D, LANES), F32).at[:, :N_EXPERTS].set(router_w)
    rb = jnp.full((1, LANES), NEG, F32).at[0, :N_EXPERTS].set(router_b)
    return pl.pallas_call(
        _router_kernel,
        out_shape=[jax.ShapeDtypeStruct((T * ROW_TILE, LANES), F32),
                   jax.ShapeDtypeStruct((T, LANES), F32),
                   jax.ShapeDtypeStruct((1, LANES), F32)],
        grid=(NT,),
        in_specs=[_row_spec(D), _mod_spec(), _const_spec((D, LANES)), _const_spec((1, LANES))],
        out_specs=[_row_spec(LANES, TM * ROW_TILE), _row_spec(LANES), _const_spec((1, LANES))],
        scratch_shapes=[pltpu.VMEM((1, LANES), F32)],
        compiler_params=_params("arbitrary"),
        name="moe_router",
    )(y, mods, rw, rb)


DMA_UNROLL = 8


def _row_copy(src_ref, src_row, dst_ref, dst_row, sem):
    return pltpu.make_async_copy(src_ref.at[pl.ds(pl.multiple_of(src_row * ROW_TILE, ROW_TILE), ROW_TILE)],
                                 dst_ref.at[pl.ds(pl.multiple_of(dst_row * ROW_TILE, ROW_TILE), ROW_TILE)], sem)


def _dispatch_kernel(pos_ref, h_ref, xg_in_ref, wf_ref, xg_ref, wb_ref, sem):
    del xg_in_ref
    i = pl.program_id(0)
    _cast_block(i, NP_TILES, wf_ref, wb_ref)

    def issue(r, carry):
        t = i * TM + r
        _row_copy(h_ref, r, xg_ref, pos_ref[2 * t], sem).start()
        _row_copy(h_ref, r, xg_ref, pos_ref[2 * t + 1], sem).start()
        return carry

    lax.fori_loop(0, TM, issue, 0, unroll=DMA_UNROLL)
    for _ in range(2):
        pltpu.make_async_copy(h_ref, xg_ref.at[pl.ds(0, TM * ROW_TILE)], sem).wait()


def _dispatch(pos, h_tiled, xg0, w_f32):
    w_spec, wb_shape = _cast_specs(w_f32.shape, NP_TILES)
    return pl.pallas_call(
        _dispatch_kernel,
        out_shape=[jax.ShapeDtypeStruct((P_ROWS * ROW_TILE, LANES), F32), wb_shape],
        grid_spec=pltpu.PrefetchScalarGridSpec(
            num_scalar_prefetch=1, grid=(NT,),
            in_specs=[pl.BlockSpec((TM * ROW_TILE, LANES), lambda i, *_: (i, 0)),
                      pl.BlockSpec(memory_space=pl.ANY), w_spec],
            out_specs=[pl.BlockSpec(memory_space=pl.ANY), w_spec],
            scratch_shapes=[pltpu.SemaphoreType.DMA(())]),
        input_output_aliases={2: 0},
        compiler_params=_params("arbitrary"),
        name="moe_dispatch",
    )(pos, h_tiled, xg0, w_f32)


def _expert_ffn_kernel(te_ref, na_ref, x_ref, w1_ref, w3_ref, w2_ref, o_ref):
    g = pl.program_id(0)

    @pl.when(g < na_ref[0])
    def _():
        x = _load_rows_tiled(x_ref, TG).astype(BF16)
        acc = None
        for c in range(E_FF // F_CHUNK):
            cols = slice(c * F_CHUNK, (c + 1) * F_CHUNK)
            a = _dot(x, w1_ref[0, :, cols])
            b = _dot(x, w3_ref[0, :, cols])
            part = _dot((_silu(a) * b).astype(BF16), w2_ref[0, cols, :])
            acc = part if acc is None else acc + part
        _store_rows_tiled(o_ref, acc)

    @pl.when(g >= na_ref[0])
    def _():
        o_ref[...] = jnp.zeros_like(o_ref)


def _expert_ffn(tile_expert, n_active, xg, w1, w3, w2):
    rows = pl.BlockSpec((TG * ROW_TILE, LANES), lambda g, te, na: (g, 0))
    w_up = pl.BlockSpec((1, D, E_FF), lambda g, te, na: (te[g], 0, 0))
    w_dn = pl.BlockSpec((1, E_FF, D), lambda g, te, na: (te[g], 0, 0))
    return pl.pallas_call(
        _expert_ffn_kernel,
        out_shape=jax.ShapeDtypeStruct((P_ROWS * ROW_TILE, LANES), F32),
        grid_spec=pltpu.PrefetchScalarGridSpec(
            num_scalar_prefetch=2, grid=(NG,),
            in_specs=[rows, w_up, w_up, w_dn],
            out_specs=rows),
        compiler_params=_params("arbitrary"),
        name="moe_expert_ffn",
    )(tile_expert, n_active, xg, w1, w3, w2)


def _combine_ln_kernel(pos_ref, y_ref, mod_ref, meta_ref, g_ref, b_ref, eo_ref, outp_ref, outs_ref,
                       buf1, buf2, sem):
    i = pl.program_id(0)

    def issue(r, carry):
        t = i * TM + r
        _row_copy(eo_ref, pos_ref[2 * t], buf1, r, sem).start()
        _row_copy(eo_ref, pos_ref[2 * t + 1], buf2, r, sem).start()
        return carry

    lax.fori_loop(0, TM, issue, 0, unroll=DMA_UNROLL)
    for buf in (buf1, buf2):
        pltpu.make_async_copy(eo_ref.at[pl.ds(0, TM * ROW_TILE)], buf, sem).wait()
    m = mod_ref[0]
    meta = meta_ref[...]
    f = meta[:, 4:5] * _load_rows_tiled(buf1, TM) + meta[:, 5:6] * _load_rows_tiled(buf2, TM)
    out = _layer_norm(ALPHA * y_ref[...] + m[5:6] * f, g_ref[...], b_ref[...])

    @pl.when(i < NP_TILES)
    def _():
        outp_ref[...] = out

    @pl.when(i >= NP_TILES)
    def _():
        outs_ref[...] = out


def _combine_ln(pos, y, mods, meta, ln_g, ln_b, eo):
    return pl.pallas_call(
        _combine_ln_kernel,
        out_shape=[jax.ShapeDtypeStruct((T_P, D), F32), jax.ShapeDtypeStruct((T_S, D), F32)],
        grid_spec=pltpu.PrefetchScalarGridSpec(
            num_scalar_prefetch=1, grid=(NT,),
            in_specs=[pl.BlockSpec((TM, D), lambda i, *_: (i, 0)),
                      pl.BlockSpec((1, 6, D), lambda i, *_: (_cond_of_tile(i), 0, 0)),
                      pl.BlockSpec((TM, LANES), lambda i, *_: (i, 0)),
                      pl.BlockSpec((1, D), lambda i, *_: (0, 0)),
                      pl.BlockSpec((1, D), lambda i, *_: (0, 0)),
                      pl.BlockSpec(memory_space=pl.ANY)],
            out_specs=[_prompt_tile_spec(D), _sample_tile_spec(D)],
            scratch_shapes=[pltpu.VMEM((TM * ROW_TILE, LANES), F32),
                            pltpu.VMEM((TM * ROW_TILE, LANES), F32),
                            pltpu.SemaphoreType.DMA(())]),
        compiler_params=_params("arbitrary"),
        name="moe_combine_ln",
    )(pos, y, mods, meta, ln_g.reshape(1, D), ln_b.reshape(1, D), eo)


def _moe_ln(y, mods, router_w, router_b, w1, w3, w2_f32, ln_g, ln_b):
    h_tiled, meta, counts = _router(y, mods, router_w, router_b)
    cnt = counts[0, :N_EXPERTS].astype(jnp.int32)
    tiles = (cnt + TG - 1) // TG
    tile_end = jnp.cumsum(tiles)
    offs = (tile_end - tiles) * TG
    e1 = meta[:, 0].astype(jnp.int32)
    e2 = meta[:, 1].astype(jnp.int32)
    pos = jnp.stack([offs[e1] + meta[:, 2].astype(jnp.int32), offs[e2] + meta[:, 3].astype(jnp.int32)], axis=1)
    pos = pos.reshape(2 * T)
    tile_expert = jnp.sum((jnp.arange(NG)[:, None] >= tile_end[None, :]).astype(jnp.int32), axis=1)
    tile_expert = jnp.minimum(tile_expert, N_EXPERTS - 1)
    n_active = tile_end[-1:].astype(jnp.int32)
    xg, w2 = _dispatch(pos, h_tiled, jnp.zeros((P_ROWS * ROW_TILE, LANES), F32),
                       w2_f32.reshape(N_EXPERTS * E_FF, D))
    eo = _expert_ffn(tile_expert, n_active, xg, w1, w3, w2.reshape(N_EXPERTS, E_FF, D))
    return _combine_ln(pos, y, mods, meta, ln_g, ln_b, eo)


def _l0_weight_layouts(w_in, w_q_up, w_kv_up):
    a, b, c = MLA_Q_LORA, MLA_Q_LORA + MLA_KV_LORA, MLA_Q_LORA + MLA_KV_LORA + MLA_ROPE
    k_rope = w_in[:, b:c]
    w_in_r = jnp.concatenate([w_in[:, :b], w_in[:, c:], k_rope, k_rope, k_rope, k_rope], axis=1)
    wq = w_q_up.reshape(MLA_Q_LORA, MLA_HEADS, MLA_NOPE + MLA_ROPE)
    w_q_up_r = jnp.concatenate([wq[:, :, :MLA_NOPE].reshape(MLA_Q_LORA, -1),
                                wq[:, :, MLA_NOPE:].reshape(MLA_Q_LORA, -1)], axis=1)
    wkv = w_kv_up.reshape(MLA_KV_LORA, MLA_HEADS, MLA_NOPE + MLA_V)
    w_kv_up_r = jnp.concatenate([wkv[:, :, :MLA_NOPE].reshape(MLA_KV_LORA, -1),
                                 wkv[:, :, MLA_NOPE:].reshape(MLA_KV_LORA, -1)], axis=1)
    return w_in_r.astype(BF16), w_q_up_r.astype(BF16), w_kv_up_r.astype(BF16)


def kernel(x_prompt, x_sample, cache_l0_mla_ckv, cache_l0_mla_krope, cache_l0_swa_k, cache_l0_swa_v,
           cache_l1_na_k, cache_l1_na_v, c, c_ctx,
           l0_ada_w, l0_ada_b, l0_w_in, l0_mla_q_norm, l0_mla_w_q_up, l0_mla_kv_norm, l0_mla_w_kv_up,
           l0_swa_sink, l0_w_out, l0_ln1_g, l0_ln1_b, l0_ffn_w1, l0_ffn_w3, l0_ffn_w2, l0_ln2_g, l0_ln2_b,
           l1_ada_w, l1_ada_b, l1_w_in, l1_na_rel_bias, l1_w_out, l1_ln1_g, l1_ln1_b,
           l1_moe_router_w, l1_moe_router_b, l1_moe_w1, l1_moe_w3, l1_moe_w2, l1_ln2_g, l1_ln2_b):
    cond = jnp.concatenate([c_ctx[None, :], c, jnp.zeros((8 - N_COND, D), F32)], axis=0)

    mods = _adaln(cond, l0_ada_w, l0_ada_b)
    w_in_r, w_q_up_r, w_kv_up_r = _l0_weight_layouts(l0_w_in, l0_mla_w_q_up, l0_mla_w_kv_up)
    y, qn, qr, ckv, kn, vm, kr4, sq, sk, sv = _l0_in(
        x_prompt.reshape(T_P, D), x_sample.reshape(T_S, D), mods, w_in_r, l0_mla_q_norm, w_q_up_r,
        l0_mla_kv_norm, w_kv_up_r, _rope_tables(MLA_ROPE, 256), _rope_tables(HEAD_DIM, 512))
    knc, vc = _kv_up(cache_l0_mla_ckv.reshape(DEC_BATCH * PAST, MLA_KV_LORA), w_kv_up_r)
    kr4c = jnp.tile(cache_l0_mla_krope.reshape(DEC_BATCH * PAST, MLA_ROPE), (1, 4))
    skc = cache_l0_swa_k.reshape(DEC_BATCH * PAST, SWA_KV_HEADS * HEAD_DIM)
    svc = cache_l0_swa_v.reshape(DEC_BATCH * PAST, SWA_KV_HEADS * HEAD_DIM)
    o_p, ffn_w1, ffn_w3, ffn_w2 = _l0_attn_prompt(l0_swa_sink, qn, qr, kn, vm, kr4, sq, sk, sv,
                                                  [l0_ffn_w1, l0_ffn_w3, l0_ffn_w2])
    o_s = _l0_attn_sample(l0_swa_sink, qn, qr, kn, vm, kr4, sq, sk, sv, knc, vc, kr4c, skc, svc)
    y = _out_ln(o_p, o_s, l0_w_out.astype(BF16), y, mods, l0_ln1_g, l0_ln1_b)
    y, moe_w1, moe_w3 = _ffn_ln(y, mods, ffn_w1, ffn_w3, ffn_w2, l0_ln2_g, l0_ln2_b,
                                l1_moe_w1.reshape(N_EXPERTS * D, E_FF), l1_moe_w3.reshape(N_EXPERTS * D, E_FF))
    new_ckv = ckv[:T_P].reshape(BATCH, SEQ, MLA_KV_LORA)
    new_krope = kr4[:T_P, :MLA_ROPE].reshape(BATCH, SEQ, MLA_ROPE)
    new_sk = sk[:T_P].reshape(BATCH, SEQ, SWA_KV_HEADS, HEAD_DIM)
    new_sv = sv[:T_P].reshape(BATCH, SEQ, SWA_KV_HEADS, HEAD_DIM)

    mods = _adaln(cond, l1_ada_w, l1_ada_b)
    q, k, v, k_heads, v_heads = _l1_in(y, mods, l1_w_in.astype(BF16))
    o_p = _l1_attn_prompt(q, k, v)
    o_s = _l1_attn_sample(q, k, v, cache_l1_na_k.reshape(DEC_BATCH, PAST, D),
                          cache_l1_na_v.reshape(DEC_BATCH, PAST, D), _na_bias_rows(l1_na_rel_bias))
    y = _out_ln(o_p, o_s, l1_w_out.astype(BF16), y, mods, l1_ln1_g, l1_ln1_b)
    y_p, y_s = _moe_ln(y, mods, l1_moe_router_w, l1_moe_router_b, moe_w1.reshape(N_EXPERTS, D, E_FF),
                       moe_w3.reshape(N_EXPERTS, D, E_FF), l1_moe_w2, l1_ln2_g, l1_ln2_b)
    new_k = k_heads.reshape(BATCH, SEQ, NA_HEADS, HEAD_DIM)
    new_v = v_heads.reshape(BATCH, SEQ, NA_HEADS, HEAD_DIM)

    return (y_p.reshape(BATCH, SEQ, D), y_s.reshape(DEC_BATCH, DEC_SEQ, D),
            new_ckv, new_krope, new_sk, new_sv, new_k, new_v)
```

```python
import functools

import jax
import jax.numpy as jnp
import numpy as np
from jax import lax
from jax.experimental import pallas as pl
from jax.experimental.pallas import tpu as pltpu

F32 = jnp.float32
BF16 = jnp.bfloat16

D = 1024
BATCH, SEQ = 32, 256
DEC_BATCH, DEC_SEQ = 2, 1024
PAST = 256
GRID_W = 64
T_P = BATCH * SEQ
T_S = DEC_BATCH * DEC_SEQ
T = T_P + T_S
N_COND = 1 + DEC_BATCH

MLA_HEADS, MLA_Q_LORA, MLA_KV_LORA, MLA_NOPE, MLA_ROPE, MLA_V = 8, 384, 256, 64, 32, 64
SWA_HEADS, SWA_KV_HEADS, SWA_WINDOW, HEAD_DIM = 8, 2, 128, 64
NA_HEADS, NA_WIN_ROWS, NA_WIN_COLS = 16, 8, 16
D_FF, N_EXPERTS, E_FF = 2816, 8, 3584
ROPE_THETA = 10000.0
LN_EPS, RMS_EPS = 1e-5, 1e-6
NEG = -1e30
ALPHA = 4.0 ** 0.25

LANES = 128
TM = 512
NT = T // TM
NP_TILES = T_P // TM
TILES_PER_SAMPLE = DEC_SEQ // TM
TG = 256
P_ROWS = 2 * T + N_EXPERTS * TG
NG = P_ROWS // TG
F_CHUNK = 896
VMEM_LIMIT = 56 * 1024 * 1024


def _params(*sem):
    return pltpu.CompilerParams(dimension_semantics=sem, vmem_limit_bytes=VMEM_LIMIT)


def _const_spec(shape, single_buffer=False):
    if single_buffer:
        return pl.BlockSpec(shape, lambda *_: (0,) * len(shape), pipeline_mode=pl.Buffered(1))
    return pl.BlockSpec(shape, lambda *_: (0,) * len(shape))


def _cast_specs(shape, steps):
    rows, width = shape
    spec = pl.BlockSpec((rows // steps, width), lambda i, *_: (jnp.minimum(i, steps - 1), 0))
    return spec, jax.ShapeDtypeStruct(shape, BF16)


def _cast_block(i, steps, src_ref, dst_ref):
    @pl.when(i < steps)
    def _():
        dst_ref[...] = src_ref[...].astype(BF16)


ROW_TILE = D // LANES


def _store_rows_tiled(ref, x):
    n = x.shape[0]
    for k in range(ROW_TILE):
        ref[pl.ds(k, n, stride=ROW_TILE), :] = x[:, k * LANES:(k + 1) * LANES]


def _load_rows_tiled(ref, n):
    return jnp.concatenate([ref[pl.ds(k, n, stride=ROW_TILE), :] for k in range(ROW_TILE)], axis=1)


def _cond_of_tile(i, rows=TM):
    per_sample = DEC_SEQ // rows
    return jnp.maximum((i - (T_P // rows - per_sample)) // per_sample, 0)


def _mod_spec(rows=TM):
    return pl.BlockSpec((1, 6, D), lambda i: (_cond_of_tile(i, rows), 0, 0))


def _row_spec(width, rows=TM):
    return pl.BlockSpec((rows, width), lambda i: (i, 0))


def _dot(a, b):
    return jnp.dot(a, b, preferred_element_type=F32)


def _dot_nt(a, b):
    return lax.dot_general(a, b, (((1,), (1,)), ((), ())), preferred_element_type=F32)


def _layer_norm(r, g, b):
    mu = jnp.mean(r, axis=-1, keepdims=True)
    d = r - mu
    var = jnp.mean(d * d, axis=-1, keepdims=True)
    return d * lax.rsqrt(var + LN_EPS) * g + b


def _rms_norm(x, g):
    return x * lax.rsqrt(jnp.mean(x * x, axis=-1, keepdims=True) + RMS_EPS) * g


def _silu(x):
    return x * jax.nn.sigmoid(x)


def _ada_kernel(c_ref, w_ref, b_ref, o_ref):
    s = _silu(c_ref[...]).astype(BF16)
    o_ref[...] = _dot(s, w_ref[...].astype(BF16)) + b_ref[...]


def _adaln(cond, ada_w, ada_b):
    nb = 1536
    out = pl.pallas_call(
        _ada_kernel,
        out_shape=jax.ShapeDtypeStruct((8, 6 * D), F32),
        grid=(6 * D // nb,),
        in_specs=[_const_spec((8, D)), pl.BlockSpec((D, nb), lambda j: (0, j)),
                  pl.BlockSpec((1, nb), lambda j: (0, j))],
        out_specs=pl.BlockSpec((8, nb), lambda j: (0, j)),
        compiler_params=_params("arbitrary"),
        name="adaln",
    )(cond, ada_w, ada_b.reshape(1, 6 * D))
    return out[:N_COND].reshape(N_COND, 6, D)


def _rope_tables(head_dim, width):
    half = head_dim // 2
    nf = half // 2
    lane = np.arange(width)
    d = lane % head_dim
    dd = d % half
    f = dd % nf
    inv = np.float32(ROPE_THETA) ** (-f.astype(np.float32) / np.float32(nf))
    t = np.arange(DEC_SEQ)
    pos = np.where((d // half)[None, :] == 0, (t // GRID_W)[:, None], (t % GRID_W)[:, None])
    ang = pos.astype(np.float32) * inv[None, :].astype(np.float32)
    cos, sin = np.cos(ang), np.sin(ang)
    first = (dd < nf)[None, :]
    zero = np.float32(0.0)
    return (jnp.asarray(cos, F32), jnp.asarray(np.where(first, -sin, zero), F32),
            jnp.asarray(np.where(first, zero, sin), F32))


def _rope(x, cos, sin_up, sin_dn, nf):
    w = x.shape[-1]
    return x * cos + pltpu.roll(x, w - nf, 1) * sin_up + pltpu.roll(x, nf, 1) * sin_dn


L0_COLS = MLA_Q_LORA + MLA_KV_LORA + 512 + 128 + 128 + 128


def _l0_in_kernel(xp_ref, xs_ref, mod_ref, win_ref, qn_ref, wq_ref, kvn_ref, wkv_ref,
                  c8_ref, su8_ref, sd8_ref, c16_ref, su16_ref, sd16_ref,
                  y_o, qnope_o, qrope_o, knope_o, vmla_o, kr4_o, sq_o, sk_o, sv_o,
                  ckv_o, krt_o, skt_o, svt_o):
    i = pl.program_id(0)
    m = mod_ref[0]
    x = jnp.where(i < NP_TILES, xp_ref[...], xs_ref[...])
    y_o[...] = x
    h = (x * (1.0 + m[1:2]) + m[0:1]).astype(BF16)
    z = _dot(h, win_ref[...])
    q_lat = z[:, 0:384]
    kv_lat = z[:, 384:640]
    sq = z[:, 640:1152]
    sk = z[:, 1152:1280]
    sv = z[:, 1280:1408]
    kr4 = z[:, 1408:1536]
    q = _dot(_rms_norm(q_lat, qn_ref[...]).astype(BF16), wq_ref[...])
    c_kv = _rms_norm(kv_lat, kvn_ref[...])
    kv = _dot(c_kv.astype(BF16), wkv_ref[...])
    qnope_o[...] = q[:, 0:512].astype(BF16)
    knope_o[...] = kv[:, 0:512].astype(BF16)
    vmla_o[...] = kv[:, 512:1024].astype(BF16)
    sv_o[...] = sv
    q_rope = q[:, 512:768]

    @pl.when(i < NP_TILES)
    def _():
        qrope_o[...] = q_rope.astype(BF16)
        kr4_o[...] = kr4
        sq_o[...] = sq.astype(BF16)
        sk_o[...] = sk
        ckv_o[...] = c_kv
        for bb in range(TM // SEQ):
            rows = slice(bb * SEQ, (bb + 1) * SEQ)
            krt_o[bb] = kr4[rows].T[:MLA_ROPE]
            skt = sk[rows].T
            svt = sv[rows].T
            for g in range(SWA_KV_HEADS):
                skt_o[bb, g] = skt[g * HEAD_DIM:(g + 1) * HEAD_DIM]
                svt_o[bb, g] = svt[g * HEAD_DIM:(g + 1) * HEAD_DIM]

    @pl.when(i >= NP_TILES)
    def _():
        c8, su8, sd8 = c8_ref[...], su8_ref[...], sd8_ref[...]
        c16, su16, sd16 = c16_ref[...], su16_ref[...], sd16_ref[...]
        qrope_o[...] = _rope(q_rope, c8, su8, sd8, 8).astype(BF16)
        kr4_o[...] = _rope(kr4, c8[:, :128], su8[:, :128], sd8[:, :128], 8)
        sq_o[...] = _rope(sq, c16, su16, sd16, 16).astype(BF16)
        sk_o[...] = _rope(sk, c16[:, :128], su16[:, :128], sd16[:, :128], 16)


def _prompt_tile_spec(width):
    return pl.BlockSpec((TM, width), lambda i, *_: (jnp.minimum(i, NP_TILES - 1), 0))


def _sample_tile_spec(width):
    return pl.BlockSpec((TM, width), lambda i, *_: (jnp.maximum(i - NP_TILES, 0), 0))


def _l0_in(x_prompt, x_sample, mods, w_in_r, q_norm, w_q_up_r, kv_norm, w_kv_up_r, tabs8, tabs16):
    def tab_spec(width):
        return pl.BlockSpec((TM, width), lambda i: (jnp.maximum(i - NP_TILES, 0) % TILES_PER_SAMPLE, 0))

    out_shape = [
        jax.ShapeDtypeStruct((T, D), F32),
        jax.ShapeDtypeStruct((T, 512), BF16),
        jax.ShapeDtypeStruct((T, 256), BF16),
        jax.ShapeDtypeStruct((T, 512), BF16),
        jax.ShapeDtypeStruct((T, 512), BF16),
        jax.ShapeDtypeStruct((T, 128), F32),
        jax.ShapeDtypeStruct((T, 512), BF16),
        jax.ShapeDtypeStruct((T, 128), F32),
        jax.ShapeDtypeStruct((T, 128), F32),
    ]
    per_tile = TM // SEQ
    prompt_block = lambda *dims: pl.BlockSpec((per_tile,) + dims,
                                              lambda i: (jnp.minimum(i, NP_TILES - 1),) + (0,) * len(dims))
    cache_shape = [
        jax.ShapeDtypeStruct((T_P, MLA_KV_LORA), F32),
        jax.ShapeDtypeStruct((BATCH, MLA_ROPE, SEQ), F32),
        jax.ShapeDtypeStruct((BATCH, SWA_KV_HEADS, HEAD_DIM, SEQ), F32),
        jax.ShapeDtypeStruct((BATCH, SWA_KV_HEADS, HEAD_DIM, SEQ), F32),
    ]
    cache_specs = [_prompt_tile_spec(MLA_KV_LORA), prompt_block(MLA_ROPE, SEQ),
                   prompt_block(SWA_KV_HEADS, HEAD_DIM, SEQ), prompt_block(SWA_KV_HEADS, HEAD_DIM, SEQ)]
    return pl.pallas_call(
        _l0_in_kernel,
        out_shape=out_shape + cache_shape,
        grid=(NT,),
        in_specs=[_prompt_tile_spec(D), _sample_tile_spec(D), _mod_spec(), _const_spec((D, L0_COLS)),
                  _const_spec((1, MLA_Q_LORA)), _const_spec((MLA_Q_LORA, 768)),
                  _const_spec((1, MLA_KV_LORA)), _const_spec((MLA_KV_LORA, 1024)),
                  tab_spec(256), tab_spec(256), tab_spec(256),
                  tab_spec(512), tab_spec(512), tab_spec(512)],
        out_specs=[_row_spec(s.shape[1]) for s in out_shape] + cache_specs,
        compiler_params=_params("arbitrary"),
        name="l0_in_proj",
    )(x_prompt, x_sample, mods, w_in_r, q_norm.reshape(1, -1), w_q_up_r, kv_norm.reshape(1, -1), w_kv_up_r,
      *tabs8, *tabs16)


def _kv_up_kernel(c_ref, w_ref, k_o, v_o):
    kv = _dot(c_ref[...].astype(BF16), w_ref[...])
    k_o[...] = kv[:, 0:512].astype(BF16)
    v_o[...] = kv[:, 512:1024].astype(BF16)


def _kv_up(ckv, w_kv_up_r):
    n = ckv.shape[0]
    return pl.pallas_call(
        _kv_up_kernel,
        out_shape=[jax.ShapeDtypeStruct((n, 512), BF16)] * 2,
        grid=(1,),
        in_specs=[_const_spec((n, MLA_KV_LORA)), _const_spec((MLA_KV_LORA, 1024))],
        out_specs=[_const_spec((n, 512))] * 2,
        compiler_params=_params("arbitrary"),
        name="l0_ctx_kv_up",
    )(ckv, w_kv_up_r)


def _lane():
    return lax.broadcasted_iota(jnp.int32, (1, LANES), 1)


def _attend(scores, values, extra_logit=None):
    m = jnp.max(scores[0], axis=1, keepdims=True)
    for s in scores[1:]:
        m = jnp.maximum(m, jnp.max(s, axis=1, keepdims=True))
    if extra_logit is not None:
        m = jnp.maximum(m, extra_logit)
    den = None
    acc = None
    for s, v in zip(scores, values):
        e = jnp.exp(s - m)
        d = jnp.sum(e, axis=1, keepdims=True)
        a = _dot(e.astype(BF16), v)
        den = d if den is None else den + d
        acc = a if acc is None else acc + a
    if extra_logit is not None:
        den = den + jnp.exp(extra_logit - m)
    return acc / den


def _head_pair(q2, keys, values, scale, masks=None, biases=None, sinks=None, q_extra=None):
    lo = _lane() < 64
    zero = jnp.zeros_like(q2)
    outs = []
    for hh in range(2):
        qm = jnp.where(lo if hh == 0 else jnp.logical_not(lo), q2, zero)
        if q_extra is not None:
            qm = jnp.concatenate([qm, q_extra[hh]], axis=1)
        scores = []
        for n, k in enumerate(keys):
            s = _dot_nt(qm, k) * scale
            if biases is not None and biases[n] is not None:
                s = s + biases[n][hh]
            if masks is not None and masks[n] is not None:
                s = jnp.where(masks[n], s, NEG)
            scores.append(s)
        outs.append(_attend(scores, values, None if sinks is None else sinks[hh]))
    return jnp.where(lo, outs[0], outs[1])


def _dup_halves(x):
    lo = _lane() < 64
    sw = pltpu.roll(x, 64, 1)
    return jnp.where(lo, x, sw), jnp.where(lo, sw, x)


MLA_SCALE = (MLA_NOPE + MLA_ROPE) ** -0.5
HD_SCALE = HEAD_DIM ** -0.5


def _mla_pairs(qn_ref, qr_ref, key_sets, o_ref):
    lane = _lane()
    for j in range(MLA_HEADS // 2):
        cols = slice(128 * j, 128 * (j + 1))
        qr = qr_ref[:, 128 * (j // 2):128 * (j // 2 + 1)]
        zero = jnp.zeros_like(qr)
        q_extra = [jnp.where((lane // MLA_ROPE) == ((2 * j + hh) % 4), qr, zero) for hh in range(2)]
        keys = [jnp.concatenate([kn[:, cols], kr4], axis=1) for kn, kr4, _ in key_sets]
        values = [v[:, cols] for _, _, v in key_sets]
        o = _head_pair(qn_ref[:, cols], keys, values, MLA_SCALE, q_extra=q_extra)
        o_ref[:, cols] = o.astype(BF16)


def _swa_pairs(sink_ref, sq_ref, key_sets, masks, o_ref):
    kd = [[a.astype(BF16) for a in _dup_halves(k)] for k, _ in key_sets]
    vd = [[a.astype(BF16) for a in _dup_halves(v)] for _, v in key_sets]
    for g in range(SWA_KV_HEADS):
        for u in range(2):
            c = 2 * g + u
            cols = slice(128 * c, 128 * (c + 1))
            sinks = [sink_ref[2 * c + hh] for hh in range(2)]
            o = _head_pair(sq_ref[:, cols], [k[g] for k in kd], [v[g] for v in vd], HD_SCALE,
                           masks=masks, sinks=sinks)
            o_ref[:, 512 + 128 * c:512 + 128 * (c + 1)] = o.astype(BF16)


def _cast_steps(rows, max_steps):
    steps = max_steps
    while rows % (16 * steps):
        steps //= 2
    return steps


def _l0_attn_prompt_kernel(cast_steps, sink_ref, qn_ref, qr_ref, kn_ref, v_ref, kr4_ref, sq_ref, sk_ref, sv_ref,
                           wf1_ref, wf2_ref, wf3_ref, o_ref, wb1_ref, wb2_ref, wb3_ref):
    _mla_pairs(qn_ref, qr_ref, [(kn_ref, kr4_ref[...].astype(BF16), v_ref)], o_ref)
    _swa_pairs(sink_ref, sq_ref, [(sk_ref[...], sv_ref[...])], None, o_ref)
    for steps, wf_ref, wb_ref in zip(cast_steps, (wf1_ref, wf2_ref, wf3_ref), (wb1_ref, wb2_ref, wb3_ref)):
        _cast_block(pl.program_id(0), steps, wf_ref, wb_ref)


def _l0_attn_prompt(sink, qn, qr, kn, vm, kr4, sq, sk, sv, weights_f32):
    spec = lambda w: pl.BlockSpec((SEQ, w), lambda b, *_: (b, 0))
    cast_steps = tuple(_cast_steps(w.shape[0], BATCH) for w in weights_f32)
    casts = [_cast_specs(w.shape, s) for w, s in zip(weights_f32, cast_steps)]
    return pl.pallas_call(
        functools.partial(_l0_attn_prompt_kernel, cast_steps),
        out_shape=[jax.ShapeDtypeStruct((T_P, D), BF16)] + [c[1] for c in casts],
        grid_spec=pltpu.PrefetchScalarGridSpec(
            num_scalar_prefetch=1, grid=(BATCH,),
            in_specs=[spec(512), spec(256), spec(512), spec(512), spec(128), spec(512), spec(128), spec(128)]
            + [c[0] for c in casts],
            out_specs=[spec(D)] + [c[0] for c in casts]),
        compiler_params=_params("arbitrary"),
        name="l0_attn_prompt",
    )(sink, qn, qr, kn, vm, kr4, sq, sk, sv, *weights_f32)


TQ_S = 256


def _l0_attn_sample_kernel(sink_ref, qn_ref, qr_ref, sq_ref, kn_ref, v_ref, kr4_ref, sk_ref, sv_ref,
                           knc_ref, vc_ref, kr4c_ref, skc_ref, svc_ref, o_ref):
    i = pl.program_id(1)
    _mla_pairs(qn_ref, qr_ref,
               [(knc_ref, kr4c_ref[...].astype(BF16), vc_ref), (kn_ref, kr4_ref[...].astype(BF16), v_ref)],
               o_ref)
    span = TQ_S + 2 * SWA_WINDOW
    start = pl.multiple_of(jnp.clip(i * TQ_S - SWA_WINDOW, 0, DEC_SEQ - span), SWA_WINDOW)
    qpos = i * TQ_S + lax.broadcasted_iota(jnp.int32, (TQ_S, span), 0)
    kpos = start + lax.broadcasted_iota(jnp.int32, (TQ_S, span), 1)
    band = jnp.abs(qpos - kpos) <= SWA_WINDOW
    keys = pl.ds(start, span)
    _swa_pairs(sink_ref, sq_ref, [(skc_ref[...], svc_ref[...]), (sk_ref[keys, :], sv_ref[keys, :])],
               [None, band], o_ref)


def _l0_attn_sample(sink, qn, qr, kn, vm, kr4, sq, sk, sv, knc, vc, kr4c, skc, svc):
    nq = DEC_SEQ // TQ_S
    qspec = lambda w: pl.BlockSpec((TQ_S, w), lambda b, i, *_: (T_P // TQ_S + b * nq + i, 0))
    kspec = lambda w: pl.BlockSpec((DEC_SEQ, w), lambda b, i, *_: (T_P // DEC_SEQ + b, 0))
    cspec = lambda w: pl.BlockSpec((PAST, w), lambda b, i, *_: (b, 0))
    return pl.pallas_call(
        _l0_attn_sample_kernel,
        out_shape=jax.ShapeDtypeStruct((T_S, D), BF16),
        grid_spec=pltpu.PrefetchScalarGridSpec(
            num_scalar_prefetch=1, grid=(DEC_BATCH, nq),
            in_specs=[qspec(512), qspec(256), qspec(512),
                      kspec(512), kspec(512), kspec(128), kspec(128), kspec(128),
                      cspec(512), cspec(512), cspec(128), cspec(128), cspec(128)],
            out_specs=pl.BlockSpec((TQ_S, D), lambda b, i, *_: (b * nq + i, 0))),
        compiler_params=_params("arbitrary", "arbitrary"),
        name="l0_attn_sample",
    )(sink, qn, qr, sq, kn, vm, kr4, sk, sv, knc, vc, kr4c, skc, svc)


def _out_ln_kernel(op_ref, os_ref, w_ref, y_ref, mod_ref, g_ref, b_ref, out_ref):
    i = pl.program_id(0)
    m = mod_ref[0]

    def finish(o):
        r = ALPHA * y_ref[...] + m[2:3] * _dot(o, w_ref[...])
        out_ref[...] = _layer_norm(r, g_ref[...], b_ref[...])

    @pl.when(i < NP_TILES)
    def _():
        finish(op_ref[...])

    @pl.when(i >= NP_TILES)
    def _():
        finish(os_ref[...])


def _out_ln(o_prompt, o_sample, w_out, y, mods, ln_g, ln_b):
    return pl.pallas_call(
        _out_ln_kernel,
        out_shape=jax.ShapeDtypeStruct((T, D), F32),
        grid=(NT,),
        in_specs=[_prompt_tile_spec(D), _sample_tile_spec(D),
                  _const_spec((D, D)), _row_spec(D), _mod_spec(),
                  _const_spec((1, D)), _const_spec((1, D))],
        out_specs=_row_spec(D),
        compiler_params=_params("arbitrary"),
        name="out_proj_ln",
    )(o_prompt, o_sample, w_out, y, mods, ln_g.reshape(1, D), ln_b.reshape(1, D))


FFN_CHUNK = D_FF // 2
FFN_TM = 256
FFN_CAST_STEPS = 32


def _ffn_ln_kernel(y_ref, mod_ref, w1_ref, w3_ref, w2_ref, g_ref, b_ref, wfa_ref, wfb_ref,
                   out_ref, wba_ref, wbb_ref):
    i = pl.program_id(0)
    _cast_block(i, FFN_CAST_STEPS, wfa_ref, wba_ref)
    _cast_block(i, FFN_CAST_STEPS, wfb_ref, wbb_ref)
    m = mod_ref[0]
    y = y_ref[...]
    h = (y * (1.0 + m[4:5]) + m[3:4]).astype(BF16)
    acc = None
    for c in range(D_FF // FFN_CHUNK):
        cols = slice(c * FFN_CHUNK, (c + 1) * FFN_CHUNK)
        a = _dot(h, w1_ref[:, cols])
        g = _dot(h, w3_ref[:, cols])
        part = _dot((_silu(a) * g).astype(BF16), w2_ref[cols, :])
        acc = part if acc is None else acc + part
    out_ref[...] = _layer_norm(ALPHA * y + m[5:6] * acc, g_ref[...], b_ref[...])


def _ffn_ln(y, mods, w1, w3, w2, ln_g, ln_b, wa_f32, wb_f32):
    wa_spec, wa_shape = _cast_specs(wa_f32.shape, FFN_CAST_STEPS)
    wb_spec, wb_shape = _cast_specs(wb_f32.shape, FFN_CAST_STEPS)
    rows = _row_spec(D, FFN_TM)
    return pl.pallas_call(
        _ffn_ln_kernel,
        out_shape=[jax.ShapeDtypeStruct((T, D), F32), wa_shape, wb_shape],
        grid=(T // FFN_TM,),
        in_specs=[rows, _mod_spec(FFN_TM), _const_spec((D, D_FF), True), _const_spec((D, D_FF), True),
                  _const_spec((D_FF, D), True), _const_spec((1, D)), _const_spec((1, D)), wa_spec, wb_spec],
        out_specs=[rows, wa_spec, wb_spec],
        compiler_params=_params("arbitrary"),
        name="ffn_ln",
    )(y, mods, w1, w3, w2, ln_g.reshape(1, D), ln_b.reshape(1, D), wa_f32, wb_f32)


def _l1_in_kernel(y_ref, mod_ref, w_ref, q_o, k_o, v_o, kh_o, vh_o):
    i = pl.program_id(0)
    m = mod_ref[0]
    h = (y_ref[...] * (1.0 + m[1:2]) + m[0:1]).astype(BF16)
    z = _dot(h, w_ref[...])
    k = z[:, D:2 * D]
    v = z[:, 2 * D:3 * D]
    q_o[...] = z[:, 0:D].astype(BF16)
    k_o[...] = k.astype(BF16)
    v_o[...] = v.astype(BF16)

    @pl.when(i < NP_TILES)
    def _():
        for bb in range(TM // SEQ):
            rows = slice(bb * SEQ, (bb + 1) * SEQ)
            for j in range(NA_HEADS // 2):
                cols = slice(j * LANES, (j + 1) * LANES)
                kt = k[rows, cols].T
                vt = v[rows, cols].T
                for hh in range(2):
                    drows = slice(hh * HEAD_DIM, (hh + 1) * HEAD_DIM)
                    kh_o[bb, 2 * j + hh] = kt[drows]
                    vh_o[bb, 2 * j + hh] = vt[drows]


def _l1_in(y, mods, w_in):
    heads = pl.BlockSpec((TM // SEQ, NA_HEADS, HEAD_DIM, SEQ), lambda i: (jnp.minimum(i, NP_TILES - 1), 0, 0, 0))
    return pl.pallas_call(
        _l1_in_kernel,
        out_shape=[jax.ShapeDtypeStruct((T, D), BF16)] * 3
        + [jax.ShapeDtypeStruct((BATCH, NA_HEADS, HEAD_DIM, SEQ), F32)] * 2,
        grid=(NT,),
        in_specs=[_row_spec(D), _mod_spec(), _const_spec((D, 3 * D))],
        out_specs=[_row_spec(D)] * 3 + [heads, heads],
        compiler_params=_params("arbitrary"),
        name="l1_in_proj",
    )(y, mods, w_in)


def _l1_attn_prompt_kernel(q_ref, k_ref, v_ref, o_ref):
    for j in range(NA_HEADS // 2):
        cols = slice(128 * j, 128 * (j + 1))
        o = _head_pair(q_ref[:, cols], [k_ref[:, cols].astype(BF16)], [v_ref[:, cols].astype(BF16)], HD_SCALE)
        o_ref[:, cols] = o.astype(BF16)


def _l1_attn_prompt(q, k, v):
    spec = pl.BlockSpec((SEQ, D), lambda b: (b, 0))
    return pl.pallas_call(
        _l1_attn_prompt_kernel,
        out_shape=jax.ShapeDtypeStruct((T_P, D), BF16),
        grid=(BATCH,),
        in_specs=[spec, spec, spec],
        out_specs=spec,
        compiler_params=_params("arbitrary"),
        name="l1_attn_prompt",
    )(q, k, v)


NA_ROWS = DEC_SEQ // GRID_W
NA_TILE_ROWS = 4
NA_TQ = NA_TILE_ROWS * GRID_W
NA_SPAN = NA_WIN_ROWS + NA_TILE_ROWS
NA_DR = 2 * NA_WIN_ROWS - 1
COL_SPAN = 2 * GRID_W - 1


def _na_span_start(t):
    first = max(0, min(t * NA_TILE_ROWS - NA_WIN_ROWS // 2, NA_ROWS - NA_WIN_ROWS))
    return min(first, NA_ROWS - NA_SPAN)


def _l1_attn_sample_kernel(q_ref, k_ref, v_ref, kc_ref, vc_ref, ext_ref, o_ref):
    kc = kc_ref[0].astype(BF16)
    vc = vc_ref[0].astype(BF16)
    lo = _lane() < GRID_W
    qcol = lax.broadcasted_iota(jnp.int32, (GRID_W, LANES), 0)
    kcol = lax.broadcasted_iota(jnp.int32, (GRID_W, LANES), 1) % GRID_W
    cs = jnp.clip(qcol - NA_WIN_COLS // 2, 0, GRID_W - NA_WIN_COLS)
    col_ok = jnp.logical_and(kcol >= cs, kcol < cs + NA_WIN_COLS)
    neg = jnp.full((GRID_W, LANES), NEG, F32)

    def bias_tile(hh, d, half):
        x = jnp.broadcast_to(ext_ref[hh, d:d + 1, :], (GRID_W, LANES))
        shift = (LANES - COL_SPAN // 2 + GRID_W * half) % LANES
        return jnp.where(col_ok, pltpu.roll(x, shift, 1, stride=1, stride_axis=0), neg)

    tiles = [[[bias_tile(hh, d, half) for half in range(2)] for d in range(NA_DR)] for hh in range(2)]

    for t in range(NA_ROWS // NA_TILE_ROWS):
        ws = _na_span_start(t)
        keys = slice(ws * GRID_W, (ws + NA_SPAN) * GRID_W)
        kw = k_ref[keys, :].astype(BF16)
        vw = v_ref[keys, :].astype(BF16)
        bias = []
        for hh in range(2):
            rows = []
            for rr in range(NA_TILE_ROWS):
                r = t * NA_TILE_ROWS + rr
                rs = max(0, min(r - NA_WIN_ROWS // 2, NA_ROWS - NA_WIN_ROWS))
                blocks = []
                for u in range(NA_SPAN // 2):
                    halves = []
                    for half in range(2):
                        kr = ws + 2 * u + half
                        ok = rs <= kr < rs + NA_WIN_ROWS
                        halves.append(tiles[hh][kr - r + NA_WIN_ROWS - 1][half] if ok else neg)
                    blocks.append(jnp.where(lo, halves[0], halves[1]))
                rows.append(jnp.concatenate(blocks, axis=1))
            bias.append(jnp.concatenate(rows, axis=0))
        qrows = slice(t * NA_TQ, (t + 1) * NA_TQ)
        o = _head_pair(q_ref[qrows, :], [kw, kc], [vw, vc], HD_SCALE, biases=[bias, None])
        o_ref[qrows, :] = o.astype(BF16)


def _l1_attn_sample(q, k, v, kc, vc, ext):
    lat = pl.BlockSpec((DEC_SEQ, LANES), lambda j, b: (T_P // DEC_SEQ + b, j))
    ctx = pl.BlockSpec((1, PAST, LANES), lambda j, b: (b, 0, j))
    return pl.pallas_call(
        _l1_attn_sample_kernel,
        out_shape=jax.ShapeDtypeStruct((T_S, D), BF16),
        grid=(NA_HEADS // 2, DEC_BATCH),
        in_specs=[lat, lat, lat, ctx, ctx, pl.BlockSpec((2, NA_DR, LANES), lambda j, b: (j, 0, 0))],
        out_specs=pl.BlockSpec((DEC_SEQ, LANES), lambda j, b: (b, j)),
        compiler_params=_params("arbitrary", "arbitrary"),
        name="l1_attn_sample",
    )(q, k, v, kc, vc, ext)


def _na_bias_rows(rel_bias):
    rb = rel_bias.astype(F32)
    n_lo = GRID_W - 1 - (NA_WIN_COLS - 1)
    n_hi = LANES - n_lo - rb.shape[-1]
    return jnp.concatenate([jnp.repeat(rb[..., :1], n_lo, axis=-1), rb,
                            jnp.repeat(rb[..., -1:], n_hi, axis=-1)], axis=-1)


def _router_kernel(y_ref, mod_ref, rw_ref, rb_ref, h_o, meta_o, cnt_o, carry_ref):
    i = pl.program_id(0)

    @pl.when(i == 0)
    def _():
        carry_ref[...] = jnp.zeros_like(carry_ref)

    m = mod_ref[0]
    h = y_ref[...] * (1.0 + m[4:5]) + m[3:4]
    _store_rows_tiled(h_o, h)
    logits = jnp.dot(h, rw_ref[...], preferred_element_type=F32, precision=lax.Precision.HIGHEST) + rb_ref[...]
    lane = lax.broadcasted_iota(jnp.int32, (TM, LANES), 1).astype(F32)
    m1 = jnp.max(logits, axis=1, keepdims=True)
    i1 = jnp.min(jnp.where(logits == m1, lane, float(LANES)), axis=1, keepdims=True)
    sel1 = lane == i1
    rest = jnp.where(sel1, -jnp.inf, logits)
    m2 = jnp.max(rest, axis=1, keepdims=True)
    i2 = jnp.min(jnp.where(rest == m2, lane, float(LANES)), axis=1, keepdims=True)
    sel2 = lane == i2
    e2 = jnp.exp(m2 - m1)
    w1 = 1.0 / (1.0 + e2)
    w2 = e2 / (1.0 + e2)
    sel = jnp.logical_or(sel1, sel2)
    rr = lax.broadcasted_iota(jnp.int32, (TM, TM), 0)
    cc = lax.broadcasted_iota(jnp.int32, (TM, TM), 1)
    tri = jnp.where(cc < rr, 1.0, 0.0).astype(BF16)
    ahead = _dot(tri, jnp.where(sel, 1.0, 0.0).astype(BF16)) + carry_ref[...]
    r1 = jnp.sum(jnp.where(sel1, ahead, 0.0), axis=1, keepdims=True)
    r2 = jnp.sum(jnp.where(sel2, ahead, 0.0), axis=1, keepdims=True)
    meta = jnp.where(lane == 0, i1, 0.0)
    meta = jnp.where(lane == 1, i2, meta)
    meta = jnp.where(lane == 2, r1, meta)
    meta = jnp.where(lane == 3, r2, meta)
    meta = jnp.where(lane == 4, w1, meta)
    meta = jnp.where(lane == 5, w2, meta)
    meta_o[...] = meta
    carry_ref[...] = carry_ref[...] + jnp.sum(jnp.where(sel, 1.0, 0.0), axis=0, keepdims=True)
    cnt_o[...] = carry_ref[...]


def _router(y, mods, router_w, router_b):
    rw = jnp.zeros((D, LANES), F32).at[:, :N_EXPERTS].set(router_w)
    rb = jnp.full((1, LANES), NEG, F32).at[0, :N_EXPERTS].set(router_b)
    return pl.pallas_call(
        _router_kernel,
        out_shape=[jax.ShapeDtypeStruct((T * ROW_TILE, LANES), F32),
                   jax.ShapeDtypeStruct((T, LANES), F32),
                   jax.ShapeDtypeStruct((1, LANES), F32)],
        grid=(NT,),
        in_specs=[_row_spec(D), _mod_spec(), _const_spec((D, LANES)), _const_spec((1, LANES))],
        out_specs=[_row_spec(LANES, TM * ROW_TILE), _row_spec(LANES), _const_spec((1, LANES))],
        scratch_shapes=[pltpu.VMEM((1, LANES), F32)],
        compiler_params=_params("arbitrary"),
        name="moe_router",
    )(y, mods, rw, rb)


DMA_UNROLL = 8


def _row_copy(src_ref, src_row, dst_ref, dst_row, sem):
    return pltpu.make_async_copy(src_ref.at[pl.ds(pl.multiple_of(src_row * ROW_TILE, ROW_TILE), ROW_TILE)],
                                 dst_ref.at[pl.ds(pl.multiple_of(dst_row * ROW_TILE, ROW_TILE), ROW_TILE)], sem)


def _dispatch_kernel(pos_ref, h_ref, xg_in_ref, wf_ref, xg_ref, wb_ref, sem):
    del xg_in_ref
    i = pl.program_id(0)
    _cast_block(i, NP_TILES, wf_ref, wb_ref)

    def issue(r, carry):
        t = i * TM + r
        _row_copy(h_ref, r, xg_ref, pos_ref[2 * t], sem).start()
        _row_copy(h_ref, r, xg_ref, pos_ref[2 * t + 1], sem).start()
        return carry

    lax.fori_loop(0, TM, issue, 0, unroll=DMA_UNROLL)
    for _ in range(2):
        pltpu.make_async_copy(h_ref, xg_ref.at[pl.ds(0, TM * ROW_TILE)], sem).wait()


def _dispatch(pos, h_tiled, xg0, w_f32):
    w_spec, wb_shape = _cast_specs(w_f32.shape, NP_TILES)
    return pl.pallas_call(
        _dispatch_kernel,
        out_shape=[jax.ShapeDtypeStruct((P_ROWS * ROW_TILE, LANES), F32), wb_shape],
        grid_spec=pltpu.PrefetchScalarGridSpec(
            num_scalar_prefetch=1, grid=(NT,),
            in_specs=[pl.BlockSpec((TM * ROW_TILE, LANES), lambda i, *_: (i, 0)),
                      pl.BlockSpec(memory_space=pl.ANY), w_spec],
            out_specs=[pl.BlockSpec(memory_space=pl.ANY), w_spec],
            scratch_shapes=[pltpu.SemaphoreType.DMA(())]),
        input_output_aliases={2: 0},
        compiler_params=_params("arbitrary"),
        name="moe_dispatch",
    )(pos, h_tiled, xg0, w_f32)


def _expert_ffn_kernel(te_ref, na_ref, x_ref, w1_ref, w3_ref, w2_ref, o_ref):
    g = pl.program_id(0)

    @pl.when(g < na_ref[0])
    def _():
        x = _load_rows_tiled(x_ref, TG).astype(BF16)
        acc = None
        for c in range(E_FF // F_CHUNK):
            cols = slice(c * F_CHUNK, (c + 1) * F_CHUNK)
            a = _dot(x, w1_ref[0, :, cols])
            b = _dot(x, w3_ref[0, :, cols])
            part = _dot((_silu(a) * b).astype(BF16), w2_ref[0, cols, :])
            acc = part if acc is None else acc + part
        _store_rows_tiled(o_ref, acc)

    @pl.when(g >= na_ref[0])
    def _():
        o_ref[...] = jnp.zeros_like(o_ref)


def _expert_ffn(tile_expert, n_active, xg, w1, w3, w2):
    rows = pl.BlockSpec((TG * ROW_TILE, LANES), lambda g, te, na: (g, 0))
    w_up = pl.BlockSpec((1, D, E_FF), lambda g, te, na: (te[g], 0, 0))
    w_dn = pl.BlockSpec((1, E_FF, D), lambda g, te, na: (te[g], 0, 0))
    return pl.pallas_call(
        _expert_ffn_kernel,
        out_shape=jax.ShapeDtypeStruct((P_ROWS * ROW_TILE, LANES), F32),
        grid_spec=pltpu.PrefetchScalarGridSpec(
            num_scalar_prefetch=2, grid=(NG,),
            in_specs=[rows, w_up, w_up, w_dn],
            out_specs=rows),
        compiler_params=_params("arbitrary"),
        name="moe_expert_ffn",
    )(tile_expert, n_active, xg, w1, w3, w2)


def _combine_ln_kernel(pos_ref, y_ref, mod_ref, meta_ref, g_ref, b_ref, eo_ref, outp_ref, outs_ref,
                       buf1, buf2, sem):
    i = pl.program_id(0)

    def issue(r, carry):
        t = i * TM + r
        _row_copy(eo_ref, pos_ref[2 * t], buf1, r, sem).start()
        _row_copy(eo_ref, pos_ref[2 * t + 1], buf2, r, sem).start()
        return carry

    lax.fori_loop(0, TM, issue, 0, unroll=DMA_UNROLL)
    for buf in (buf1, buf2):
        pltpu.make_async_copy(eo_ref.at[pl.ds(0, TM * ROW_TILE)], buf, sem).wait()
    m = mod_ref[0]
    meta = meta_ref[...]
    f = meta[:, 4:5] * _load_rows_tiled(buf1, TM) + meta[:, 5:6] * _load_rows_tiled(buf2, TM)
    out = _layer_norm(ALPHA * y_ref[...] + m[5:6] * f, g_ref[...], b_ref[...])

    @pl.when(i < NP_TILES)
    def _():
        outp_ref[...] = out

    @pl.when(i >= NP_TILES)
    def _():
        outs_ref[...] = out


def _combine_ln(pos, y, mods, meta, ln_g, ln_b, eo):
    return pl.pallas_call(
        _combine_ln_kernel,
        out_shape=[jax.ShapeDtypeStruct((T_P, D), F32), jax.ShapeDtypeStruct((T_S, D), F32)],
        grid_spec=pltpu.PrefetchScalarGridSpec(
            num_scalar_prefetch=1, grid=(NT,),
            in_specs=[pl.BlockSpec((TM, D), lambda i, *_: (i, 0)),
                      pl.BlockSpec((1, 6, D), lambda i, *_: (_cond_of_tile(i), 0, 0)),
                      pl.BlockSpec((TM, LANES), lambda i, *_: (i, 0)),
                      pl.BlockSpec((1, D), lambda i, *_: (0, 0)),
                      pl.BlockSpec((1, D), lambda i, *_: (0, 0)),
                      pl.BlockSpec(memory_space=pl.ANY)],
            out_specs=[_prompt_tile_spec(D), _sample_tile_spec(D)],
            scratch_shapes=[pltpu.VMEM((TM * ROW_TILE, LANES), F32),
                            pltpu.VMEM((TM * ROW_TILE, LANES), F32),
                            pltpu.SemaphoreType.DMA(())]),
        compiler_params=_params("arbitrary"),
        name="moe_combine_ln",
    )(pos, y, mods, meta, ln_g.reshape(1, D), ln_b.reshape(1, D), eo)


def _moe_ln(y, mods, router_w, router_b, w1, w3, w2_f32, ln_g, ln_b):
    h_tiled, meta, counts = _router(y, mods, router_w, router_b)
    cnt = counts[0, :N_EXPERTS].astype(jnp.int32)
    tiles = (cnt + TG - 1) // TG
    tile_end = jnp.cumsum(tiles)
    offs = (tile_end - tiles) * TG
    e1 = meta[:, 0].astype(jnp.int32)
    e2 = meta[:, 1].astype(jnp.int32)
    pos = jnp.stack([offs[e1] + meta[:, 2].astype(jnp.int32), offs[e2] + meta[:, 3].astype(jnp.int32)], axis=1)
    pos = pos.reshape(2 * T)
    tile_expert = jnp.sum((jnp.arange(NG)[:, None] >= tile_end[None, :]).astype(jnp.int32), axis=1)
    tile_expert = jnp.minimum(tile_expert, N_EXPERTS - 1)
    n_active = tile_end[-1:].astype(jnp.int32)
    xg, w2 = _dispatch(pos, h_tiled, jnp.zeros((P_ROWS * ROW_TILE, LANES), F32),
                       w2_f32.reshape(N_EXPERTS * E_FF, D))
    eo = _expert_ffn(tile_expert, n_active, xg, w1, w3, w2.reshape(N_EXPERTS, E_FF, D))
    return _combine_ln(pos, y, mods, meta, ln_g, ln_b, eo)


def _l0_weight_layouts(w_in, w_q_up, w_kv_up):
    a, b, c = MLA_Q_LORA, MLA_Q_LORA + MLA_KV_LORA, MLA_Q_LORA + MLA_KV_LORA + MLA_ROPE
    k_rope = w_in[:, b:c]
    w_in_r = jnp.concatenate([w_in[:, :b], w_in[:, c:], k_rope, k_rope, k_rope, k_rope], axis=1)
    wq = w_q_up.reshape(MLA_Q_LORA, MLA_HEADS, MLA_NOPE + MLA_ROPE)
    w_q_up_r = jnp.concatenate([wq[:, :, :MLA_NOPE].reshape(MLA_Q_LORA, -1),
                                wq[:, :, MLA_NOPE:].reshape(MLA_Q_LORA, -1)], axis=1)
    wkv = w_kv_up.reshape(MLA_KV_LORA, MLA_HEADS, MLA_NOPE + MLA_V)
    w_kv_up_r = jnp.concatenate([wkv[:, :, :MLA_NOPE].reshape(MLA_KV_LORA, -1),
                                 wkv[:, :, MLA_NOPE:].reshape(MLA_KV_LORA, -1)], axis=1)
    return w_in_r.astype(BF16), w_q_up_r.astype(BF16), w_kv_up_r.astype(BF16)


def kernel(x_prompt, x_sample, cache_l0_mla_ckv, cache_l0_mla_krope, cache_l0_swa_k, cache_l0_swa_v,
           cache_l1_na_k, cache_l1_na_v, c, c_ctx,
           l0_ada_w, l0_ada_b, l0_w_in, l0_mla_q_norm, l0_mla_w_q_up, l0_mla_kv_norm, l0_mla_w_kv_up,
           l0_swa_sink, l0_w_out, l0_ln1_g, l0_ln1_b, l0_ffn_w1, l0_ffn_w3, l0_ffn_w2, l0_ln2_g, l0_ln2_b,
           l1_ada_w, l1_ada_b, l1_w_in, l1_na_rel_bias, l1_w_out, l1_ln1_g, l1_ln1_b,
           l1_moe_router_w, l1_moe_router_b, l1_moe_w1, l1_moe_w3, l1_moe_w2, l1_ln2_g, l1_ln2_b):
    cond = jnp.concatenate([c_ctx[None, :], c, jnp.zeros((8 - N_COND, D), F32)], axis=0)

    mods = _adaln(cond, l0_ada_w, l0_ada_b)
    w_in_r, w_q_up_r, w_kv_up_r = _l0_weight_layouts(l0_w_in, l0_mla_w_q_up, l0_mla_w_kv_up)
    y, qn, qr, kn, vm, kr4, sq, sk, sv, ckv, krope_t, sk_t, sv_t = _l0_in(
        x_prompt.reshape(T_P, D), x_sample.reshape(T_S, D), mods, w_in_r, l0_mla_q_norm, w_q_up_r,
        l0_mla_kv_norm, w_kv_up_r, _rope_tables(MLA_ROPE, 256), _rope_tables(HEAD_DIM, 512))
    knc, vc = _kv_up(cache_l0_mla_ckv.reshape(DEC_BATCH * PAST, MLA_KV_LORA), w_kv_up_r)
    kr4c = jnp.tile(cache_l0_mla_krope.reshape(DEC_BATCH * PAST, MLA_ROPE), (1, 4))
    skc = cache_l0_swa_k.reshape(DEC_BATCH * PAST, SWA_KV_HEADS * HEAD_DIM)
    svc = cache_l0_swa_v.reshape(DEC_BATCH * PAST, SWA_KV_HEADS * HEAD_DIM)
    o_p, ffn_w1, ffn_w3, ffn_w2 = _l0_attn_prompt(l0_swa_sink, qn, qr, kn, vm, kr4, sq, sk, sv,
                                                  [l0_ffn_w1, l0_ffn_w3, l0_ffn_w2])
    o_s = _l0_attn_sample(l0_swa_sink, qn, qr, kn, vm, kr4, sq, sk, sv, knc, vc, kr4c, skc, svc)
    y = _out_ln(o_p, o_s, l0_w_out.astype(BF16), y, mods, l0_ln1_g, l0_ln1_b)
    y, moe_w1, moe_w3 = _ffn_ln(y, mods, ffn_w1, ffn_w3, ffn_w2, l0_ln2_g, l0_ln2_b,
                                l1_moe_w1.reshape(N_EXPERTS * D, E_FF), l1_moe_w3.reshape(N_EXPERTS * D, E_FF))
    new_ckv = ckv.reshape(BATCH, SEQ, MLA_KV_LORA)
    new_krope = jnp.transpose(krope_t, (0, 2, 1))
    new_sk = jnp.transpose(sk_t, (0, 3, 1, 2))
    new_sv = jnp.transpose(sv_t, (0, 3, 1, 2))

    mods = _adaln(cond, l1_ada_w, l1_ada_b)
    q, k, v, k_heads, v_heads = _l1_in(y, mods, l1_w_in.astype(BF16))
    o_p = _l1_attn_prompt(q, k, v)
    o_s = _l1_attn_sample(q, k, v, cache_l1_na_k.reshape(DEC_BATCH, PAST, D),
                          cache_l1_na_v.reshape(DEC_BATCH, PAST, D), _na_bias_rows(l1_na_rel_bias))
    y = _out_ln(o_p, o_s, l1_w_out.astype(BF16), y, mods, l1_ln1_g, l1_ln1_b)
    y_p, y_s = _moe_ln(y, mods, l1_moe_router_w, l1_moe_router_b, moe_w1.reshape(N_EXPERTS, D, E_FF),
                       moe_w3.reshape(N_EXPERTS, D, E_FF), l1_moe_w2, l1_ln2_g, l1_ln2_b)
    new_k = jnp.transpose(k_heads, (0, 3, 1, 2))
    new_v = jnp.transpose(v_heads, (0, 3, 1, 2))

    return (y_p.reshape(BATCH, SEQ, D), y_s.reshape(DEC_BATCH, DEC_SEQ, D),
            new_ckv, new_krope, new_sk, new_sv, new_k, new_v)
```

```python
import functools

import jax
import jax.numpy as jnp
import numpy as np
from jax import lax
from jax.experimental import pallas as pl
from jax.experimental.pallas import tpu as pltpu

F32 = jnp.float32
BF16 = jnp.bfloat16

D = 1024
BATCH, SEQ = 32, 256
DEC_BATCH, DEC_SEQ = 2, 1024
PAST = 256
GRID_W = 64
T_P = BATCH * SEQ
T_S = DEC_BATCH * DEC_SEQ
T = T_P + T_S
N_COND = 1 + DEC_BATCH

MLA_HEADS, MLA_Q_LORA, MLA_KV_LORA, MLA_NOPE, MLA_ROPE, MLA_V = 8, 384, 256, 64, 32, 64
SWA_HEADS, SWA_KV_HEADS, SWA_WINDOW, HEAD_DIM = 8, 2, 128, 64
NA_HEADS, NA_WIN_ROWS, NA_WIN_COLS = 16, 8, 16
D_FF, N_EXPERTS, E_FF = 2816, 8, 3584
ROPE_THETA = 10000.0
LN_EPS, RMS_EPS = 1e-5, 1e-6
NEG = -1e30
ALPHA = 4.0 ** 0.25

LANES = 128
TM = 512
NT = T // TM
NP_TILES = T_P // TM
TILES_PER_SAMPLE = DEC_SEQ // TM
TG = 256
P_ROWS = 2 * T + N_EXPERTS * TG
NG = P_ROWS // TG
F_CHUNK = 896
VMEM_LIMIT = 56 * 1024 * 1024


def _params(*sem):
    return pltpu.CompilerParams(dimension_semantics=sem, vmem_limit_bytes=VMEM_LIMIT)


def _const_spec(shape, single_buffer=False):
    if single_buffer:
        return pl.BlockSpec(shape, lambda *_: (0,) * len(shape), pipeline_mode=pl.Buffered(1))
    return pl.BlockSpec(shape, lambda *_: (0,) * len(shape))


def _cast_specs(shape, steps):
    rows, width = shape
    spec = pl.BlockSpec((rows // steps, width), lambda i, *_: (jnp.minimum(i, steps - 1), 0))
    return spec, jax.ShapeDtypeStruct(shape, BF16)


def _cast_block(i, steps, src_ref, dst_ref):
    @pl.when(i < steps)
    def _():
        dst_ref[...] = src_ref[...].astype(BF16)


ROW_TILE = D // LANES


def _store_rows_tiled(ref, x):
    n = x.shape[0]
    for k in range(ROW_TILE):
        ref[pl.ds(k, n, stride=ROW_TILE), :] = x[:, k * LANES:(k + 1) * LANES]


def _load_rows_tiled(ref, n):
    return jnp.concatenate([ref[pl.ds(k, n, stride=ROW_TILE), :] for k in range(ROW_TILE)], axis=1)


def _cond_of_tile(i, rows=TM):
    per_sample = DEC_SEQ // rows
    return jnp.maximum((i - (T_P // rows - per_sample)) // per_sample, 0)


def _mod_spec(rows=TM):
    return pl.BlockSpec((1, 6, D), lambda i: (_cond_of_tile(i, rows), 0, 0))


def _row_spec(width, rows=TM):
    return pl.BlockSpec((rows, width), lambda i: (i, 0))


def _dot(a, b):
    return jnp.dot(a, b, preferred_element_type=F32)


def _dot_nt(a, b):
    return lax.dot_general(a, b, (((1,), (1,)), ((), ())), preferred_element_type=F32)


def _layer_norm(r, g, b):
    mu = jnp.mean(r, axis=-1, keepdims=True)
    d = r - mu
    var = jnp.mean(d * d, axis=-1, keepdims=True)
    return d * lax.rsqrt(var + LN_EPS) * g + b


def _rms_norm(x, g):
    return x * lax.rsqrt(jnp.mean(x * x, axis=-1, keepdims=True) + RMS_EPS) * g


def _silu(x):
    return x * jax.nn.sigmoid(x)


def _ada_kernel(c_ref, w_ref, b_ref, o_ref):
    s = _silu(c_ref[...]).astype(BF16)
    o_ref[...] = _dot(s, w_ref[...].astype(BF16)) + b_ref[...]


def _adaln(cond, ada_w, ada_b):
    nb = 1536
    out = pl.pallas_call(
        _ada_kernel,
        out_shape=jax.ShapeDtypeStruct((8, 6 * D), F32),
        grid=(6 * D // nb,),
        in_specs=[_const_spec((8, D)), pl.BlockSpec((D, nb), lambda j: (0, j)),
                  pl.BlockSpec((1, nb), lambda j: (0, j))],
        out_specs=pl.BlockSpec((8, nb), lambda j: (0, j)),
        compiler_params=_params("arbitrary"),
        name="adaln",
    )(cond, ada_w, ada_b.reshape(1, 6 * D))
    return out[:N_COND].reshape(N_COND, 6, D)


def _rope_tables(head_dim, width):
    half = head_dim // 2
    nf = half // 2
    lane = np.arange(width)
    d = lane % head_dim
    dd = d % half
    f = dd % nf
    inv = np.float32(ROPE_THETA) ** (-f.astype(np.float32) / np.float32(nf))
    t = np.arange(DEC_SEQ)
    pos = np.where((d // half)[None, :] == 0, (t // GRID_W)[:, None], (t % GRID_W)[:, None])
    ang = pos.astype(np.float32) * inv[None, :].astype(np.float32)
    cos, sin = np.cos(ang), np.sin(ang)
    first = (dd < nf)[None, :]
    zero = np.float32(0.0)
    return (jnp.asarray(cos, F32), jnp.asarray(np.where(first, -sin, zero), F32),
            jnp.asarray(np.where(first, zero, sin), F32))


def _rope(x, cos, sin_up, sin_dn, nf):
    w = x.shape[-1]
    return x * cos + pltpu.roll(x, w - nf, 1) * sin_up + pltpu.roll(x, nf, 1) * sin_dn


L0_COLS = MLA_Q_LORA + MLA_KV_LORA + 512 + 128 + 128 + 128


def _l0_in_kernel(xp_ref, xs_ref, mod_ref, win_ref, qn_ref, wq_ref, kvn_ref, wkv_ref,
                  c8_ref, su8_ref, sd8_ref, c16_ref, su16_ref, sd16_ref,
                  y_o, qnope_o, qrope_o, knope_o, vmla_o, kr4_o, sq_o, sk_o, sv_o,
                  ckv_o, krt_o, skt_o, svt_o):
    i = pl.program_id(0)
    m = mod_ref[0]
    x = jnp.where(i < NP_TILES, xp_ref[...], xs_ref[...])
    y_o[...] = x
    h = (x * (1.0 + m[1:2]) + m[0:1]).astype(BF16)
    z = _dot(h, win_ref[...])
    q_lat = z[:, 0:384]
    kv_lat = z[:, 384:640]
    sq = z[:, 640:1152]
    sk = z[:, 1152:1280]
    sv = z[:, 1280:1408]
    kr4 = z[:, 1408:1536]
    q = _dot(_rms_norm(q_lat, qn_ref[...]).astype(BF16), wq_ref[...])
    c_kv = _rms_norm(kv_lat, kvn_ref[...])
    kv = _dot(c_kv.astype(BF16), wkv_ref[...])
    qnope_o[...] = q[:, 0:512].astype(BF16)
    knope_o[...] = kv[:, 0:512].astype(BF16)
    vmla_o[...] = kv[:, 512:1024].astype(BF16)
    sv_o[...] = sv
    q_rope = q[:, 512:768]

    @pl.when(i < NP_TILES)
    def _():
        qrope_o[...] = q_rope.astype(BF16)
        kr4_o[...] = kr4
        sq_o[...] = sq.astype(BF16)
        sk_o[...] = sk
        ckv_o[...] = c_kv
        for bb in range(TM // SEQ):
            rows = slice(bb * SEQ, (bb + 1) * SEQ)
            krt_o[bb] = kr4[rows].T[:MLA_ROPE]
            skt = sk[rows].T
            svt = sv[rows].T
            for g in range(SWA_KV_HEADS):
                skt_o[bb, g] = skt[g * HEAD_DIM:(g + 1) * HEAD_DIM]
                svt_o[bb, g] = svt[g * HEAD_DIM:(g + 1) * HEAD_DIM]

    @pl.when(i >= NP_TILES)
    def _():
        c8, su8, sd8 = c8_ref[...], su8_ref[...], sd8_ref[...]
        c16, su16, sd16 = c16_ref[...], su16_ref[...], sd16_ref[...]
        qrope_o[...] = _rope(q_rope, c8, su8, sd8, 8).astype(BF16)
        kr4_o[...] = _rope(kr4, c8[:, :128], su8[:, :128], sd8[:, :128], 8)
        sq_o[...] = _rope(sq, c16, su16, sd16, 16).astype(BF16)
        sk_o[...] = _rope(sk, c16[:, :128], su16[:, :128], sd16[:, :128], 16)


def _prompt_tile_spec(width):
    return pl.BlockSpec((TM, width), lambda i, *_: (jnp.minimum(i, NP_TILES - 1), 0))


def _sample_tile_spec(width):
    return pl.BlockSpec((TM, width), lambda i, *_: (jnp.maximum(i - NP_TILES, 0), 0))


def _l0_in(x_prompt, x_sample, mods, w_in_r, q_norm, w_q_up_r, kv_norm, w_kv_up_r, tabs8, tabs16):
    def tab_spec(width):
        return pl.BlockSpec((TM, width), lambda i: (jnp.maximum(i - NP_TILES, 0) % TILES_PER_SAMPLE, 0))

    out_shape = [
        jax.ShapeDtypeStruct((T, D), F32),
        jax.ShapeDtypeStruct((T, 512), BF16),
        jax.ShapeDtypeStruct((T, 256), BF16),
        jax.ShapeDtypeStruct((T, 512), BF16),
        jax.ShapeDtypeStruct((T, 512), BF16),
        jax.ShapeDtypeStruct((T, 128), F32),
        jax.ShapeDtypeStruct((T, 512), BF16),
        jax.ShapeDtypeStruct((T, 128), F32),
        jax.ShapeDtypeStruct((T, 128), F32),
    ]
    per_tile = TM // SEQ
    prompt_block = lambda *dims: pl.BlockSpec((per_tile,) + dims,
                                              lambda i: (jnp.minimum(i, NP_TILES - 1),) + (0,) * len(dims))
    cache_shape = [
        jax.ShapeDtypeStruct((T_P, MLA_KV_LORA), F32),
        jax.ShapeDtypeStruct((BATCH, MLA_ROPE, SEQ), F32),
        jax.ShapeDtypeStruct((BATCH, SWA_KV_HEADS, HEAD_DIM, SEQ), F32),
        jax.ShapeDtypeStruct((BATCH, SWA_KV_HEADS, HEAD_DIM, SEQ), F32),
    ]
    cache_specs = [_prompt_tile_spec(MLA_KV_LORA), prompt_block(MLA_ROPE, SEQ),
                   prompt_block(SWA_KV_HEADS, HEAD_DIM, SEQ), prompt_block(SWA_KV_HEADS, HEAD_DIM, SEQ)]
    return pl.pallas_call(
        _l0_in_kernel,
        out_shape=out_shape + cache_shape,
        grid=(NT,),
        in_specs=[_prompt_tile_spec(D), _sample_tile_spec(D), _mod_spec(), _const_spec((D, L0_COLS)),
                  _const_spec((1, MLA_Q_LORA)), _const_spec((MLA_Q_LORA, 768)),
                  _const_spec((1, MLA_KV_LORA)), _const_spec((MLA_KV_LORA, 1024)),
                  tab_spec(256), tab_spec(256), tab_spec(256),
                  tab_spec(512), tab_spec(512), tab_spec(512)],
        out_specs=[_row_spec(s.shape[1]) for s in out_shape] + cache_specs,
        compiler_params=_params("arbitrary"),
        name="l0_in_proj",
    )(x_prompt, x_sample, mods, w_in_r, q_norm.reshape(1, -1), w_q_up_r, kv_norm.reshape(1, -1), w_kv_up_r,
      *tabs8, *tabs16)


def _kv_up_kernel(c_ref, w_ref, k_o, v_o):
    kv = _dot(c_ref[...].astype(BF16), w_ref[...])
    k_o[...] = kv[:, 0:512].astype(BF16)
    v_o[...] = kv[:, 512:1024].astype(BF16)


def _kv_up(ckv, w_kv_up_r):
    n = ckv.shape[0]
    return pl.pallas_call(
        _kv_up_kernel,
        out_shape=[jax.ShapeDtypeStruct((n, 512), BF16)] * 2,
        grid=(1,),
        in_specs=[_const_spec((n, MLA_KV_LORA)), _const_spec((MLA_KV_LORA, 1024))],
        out_specs=[_const_spec((n, 512))] * 2,
        compiler_params=_params("arbitrary"),
        name="l0_ctx_kv_up",
    )(ckv, w_kv_up_r)


def _lane():
    return lax.broadcasted_iota(jnp.int32, (1, LANES), 1)


def _attend(scores, values, extra_logit=None):
    m = jnp.max(scores[0], axis=1, keepdims=True)
    for s in scores[1:]:
        m = jnp.maximum(m, jnp.max(s, axis=1, keepdims=True))
    if extra_logit is not None:
        m = jnp.maximum(m, extra_logit)
    den = None
    acc = None
    for s, v in zip(scores, values):
        e = jnp.exp(s - m)
        d = jnp.sum(e, axis=1, keepdims=True)
        a = _dot(e.astype(BF16), v)
        den = d if den is None else den + d
        acc = a if acc is None else acc + a
    if extra_logit is not None:
        den = den + jnp.exp(extra_logit - m)
    return acc / den


def _head_pair(q2, keys, values, scale, masks=None, biases=None, sinks=None, q_extra=None):
    lo = _lane() < 64
    zero = jnp.zeros_like(q2)
    outs = []
    for hh in range(2):
        qm = jnp.where(lo if hh == 0 else jnp.logical_not(lo), q2, zero)
        if q_extra is not None:
            qm = jnp.concatenate([qm, q_extra[hh]], axis=1)
        scores = []
        for n, k in enumerate(keys):
            s = _dot_nt(qm, k) * scale
            if biases is not None and biases[n] is not None:
                s = s + biases[n][hh]
            if masks is not None and masks[n] is not None:
                s = jnp.where(masks[n], s, NEG)
            scores.append(s)
        outs.append(_attend(scores, values, None if sinks is None else sinks[hh]))
    return jnp.where(lo, outs[0], outs[1])


def _dup_halves(x):
    lo = _lane() < 64
    sw = pltpu.roll(x, 64, 1)
    return jnp.where(lo, x, sw), jnp.where(lo, sw, x)


MLA_SCALE = (MLA_NOPE + MLA_ROPE) ** -0.5
HD_SCALE = HEAD_DIM ** -0.5


def _mla_pairs(qn_ref, qr_ref, key_sets, o_ref):
    lane = _lane()
    for j in range(MLA_HEADS // 2):
        cols = slice(128 * j, 128 * (j + 1))
        qr = qr_ref[:, 128 * (j // 2):128 * (j // 2 + 1)]
        zero = jnp.zeros_like(qr)
        q_extra = [jnp.where((lane // MLA_ROPE) == ((2 * j + hh) % 4), qr, zero) for hh in range(2)]
        keys = [jnp.concatenate([kn[:, cols], kr4], axis=1) for kn, kr4, _ in key_sets]
        values = [v[:, cols] for _, _, v in key_sets]
        o = _head_pair(qn_ref[:, cols], keys, values, MLA_SCALE, q_extra=q_extra)
        o_ref[:, cols] = o.astype(BF16)


def _swa_pairs(sink_ref, sq_ref, key_sets, masks, o_ref):
    kd = [[a.astype(BF16) for a in _dup_halves(k)] for k, _ in key_sets]
    vd = [[a.astype(BF16) for a in _dup_halves(v)] for _, v in key_sets]
    for g in range(SWA_KV_HEADS):
        for u in range(2):
            c = 2 * g + u
            cols = slice(128 * c, 128 * (c + 1))
            sinks = [sink_ref[2 * c + hh] for hh in range(2)]
            o = _head_pair(sq_ref[:, cols], [k[g] for k in kd], [v[g] for v in vd], HD_SCALE,
                           masks=masks, sinks=sinks)
            o_ref[:, 512 + 128 * c:512 + 128 * (c + 1)] = o.astype(BF16)


def _cast_steps(rows, max_steps):
    steps = max_steps
    while rows % (16 * steps):
        steps //= 2
    return steps


def _l0_attn_prompt_kernel(cast_steps, sink_ref, qn_ref, qr_ref, kn_ref, v_ref, kr4_ref, sq_ref, sk_ref, sv_ref,
                           wf1_ref, wf2_ref, wf3_ref, o_ref, wb1_ref, wb2_ref, wb3_ref):
    _mla_pairs(qn_ref, qr_ref, [(kn_ref, kr4_ref[...].astype(BF16), v_ref)], o_ref)
    _swa_pairs(sink_ref, sq_ref, [(sk_ref[...], sv_ref[...])], None, o_ref)
    for steps, wf_ref, wb_ref in zip(cast_steps, (wf1_ref, wf2_ref, wf3_ref), (wb1_ref, wb2_ref, wb3_ref)):
        _cast_block(pl.program_id(0), steps, wf_ref, wb_ref)


def _l0_attn_prompt(sink, qn, qr, kn, vm, kr4, sq, sk, sv, weights_f32):
    spec = lambda w: pl.BlockSpec((SEQ, w), lambda b, *_: (b, 0))
    cast_steps = tuple(_cast_steps(w.shape[0], BATCH) for w in weights_f32)
    casts = [_cast_specs(w.shape, s) for w, s in zip(weights_f32, cast_steps)]
    return pl.pallas_call(
        functools.partial(_l0_attn_prompt_kernel, cast_steps),
        out_shape=[jax.ShapeDtypeStruct((T_P, D), BF16)] + [c[1] for c in casts],
        grid_spec=pltpu.PrefetchScalarGridSpec(
            num_scalar_prefetch=1, grid=(BATCH,),
            in_specs=[spec(512), spec(256), spec(512), spec(512), spec(128), spec(512), spec(128), spec(128)]
            + [c[0] for c in casts],
            out_specs=[spec(D)] + [c[0] for c in casts]),
        compiler_params=_params("arbitrary"),
        name="l0_attn_prompt",
    )(sink, qn, qr, kn, vm, kr4, sq, sk, sv, *weights_f32)


TQ_S = 256


def _l0_attn_sample_kernel(sink_ref, qn_ref, qr_ref, sq_ref, kn_ref, v_ref, kr4_ref, sk_ref, sv_ref,
                           knc_ref, vc_ref, kr4c_ref, skc_ref, svc_ref, o_ref):
    i = pl.program_id(1)
    _mla_pairs(qn_ref, qr_ref,
               [(knc_ref, kr4c_ref[...].astype(BF16), vc_ref), (kn_ref, kr4_ref[...].astype(BF16), v_ref)],
               o_ref)
    span = TQ_S + 2 * SWA_WINDOW
    start = pl.multiple_of(jnp.clip(i * TQ_S - SWA_WINDOW, 0, DEC_SEQ - span), SWA_WINDOW)
    qpos = i * TQ_S + lax.broadcasted_iota(jnp.int32, (TQ_S, span), 0)
    kpos = start + lax.broadcasted_iota(jnp.int32, (TQ_S, span), 1)
    band = jnp.abs(qpos - kpos) <= SWA_WINDOW
    keys = pl.ds(start, span)
    _swa_pairs(sink_ref, sq_ref, [(skc_ref[...], svc_ref[...]), (sk_ref[keys, :], sv_ref[keys, :])],
               [None, band], o_ref)


def _l0_attn_sample(sink, qn, qr, kn, vm, kr4, sq, sk, sv, knc, vc, kr4c, skc, svc):
    nq = DEC_SEQ // TQ_S
    qspec = lambda w: pl.BlockSpec((TQ_S, w), lambda b, i, *_: (T_P // TQ_S + b * nq + i, 0))
    kspec = lambda w: pl.BlockSpec((DEC_SEQ, w), lambda b, i, *_: (T_P // DEC_SEQ + b, 0))
    cspec = lambda w: pl.BlockSpec((PAST, w), lambda b, i, *_: (b, 0))
    return pl.pallas_call(
        _l0_attn_sample_kernel,
        out_shape=jax.ShapeDtypeStruct((T_S, D), BF16),
        grid_spec=pltpu.PrefetchScalarGridSpec(
            num_scalar_prefetch=1, grid=(DEC_BATCH, nq),
            in_specs=[qspec(512), qspec(256), qspec(512),
                      kspec(512), kspec(512), kspec(128), kspec(128), kspec(128),
                      cspec(512), cspec(512), cspec(128), cspec(128), cspec(128)],
            out_specs=pl.BlockSpec((TQ_S, D), lambda b, i, *_: (b * nq + i, 0))),
        compiler_params=_params("arbitrary", "arbitrary"),
        name="l0_attn_sample",
    )(sink, qn, qr, sq, kn, vm, kr4, sk, sv, knc, vc, kr4c, skc, svc)


OUT_LN_ROWS = 256


def _out_ln_kernel(op_ref, os_ref, w_ref, y_ref, mod_ref, g_ref, b_ref, out_ref):
    i = pl.program_id(0)
    m = mod_ref[0]
    for s in range(TM // OUT_LN_ROWS):
        rows = slice(s * OUT_LN_ROWS, (s + 1) * OUT_LN_ROWS)
        o = jnp.where(i < NP_TILES, op_ref[rows, :], os_ref[rows, :])
        r = ALPHA * y_ref[rows, :] + m[2:3] * _dot(o, w_ref[...])
        out_ref[rows, :] = _layer_norm(r, g_ref[...], b_ref[...])


def _out_ln(o_prompt, o_sample, w_out, y, mods, ln_g, ln_b):
    return pl.pallas_call(
        _out_ln_kernel,
        out_shape=jax.ShapeDtypeStruct((T, D), F32),
        grid=(NT,),
        in_specs=[_prompt_tile_spec(D), _sample_tile_spec(D),
                  _const_spec((D, D)), _row_spec(D), _mod_spec(),
                  _const_spec((1, D)), _const_spec((1, D))],
        out_specs=_row_spec(D),
        compiler_params=_params("arbitrary"),
        name="out_proj_ln",
    )(o_prompt, o_sample, w_out, y, mods, ln_g.reshape(1, D), ln_b.reshape(1, D))


FFN_CHUNK = D_FF // 2
FFN_TM = 256
FFN_CAST_STEPS = 32


def _ffn_ln_kernel(y_ref, mod_ref, w1_ref, w3_ref, w2_ref, g_ref, b_ref, wfa_ref, wfb_ref,
                   out_ref, wba_ref, wbb_ref):
    i = pl.program_id(0)
    _cast_block(i, FFN_CAST_STEPS, wfa_ref, wba_ref)
    _cast_block(i, FFN_CAST_STEPS, wfb_ref, wbb_ref)
    m = mod_ref[0]
    y = y_ref[...]
    h = (y * (1.0 + m[4:5]) + m[3:4]).astype(BF16)
    acc = None
    for c in range(D_FF // FFN_CHUNK):
        cols = slice(c * FFN_CHUNK, (c + 1) * FFN_CHUNK)
        a = _dot(h, w1_ref[:, cols])
        g = _dot(h, w3_ref[:, cols])
        part = _dot((_silu(a) * g).astype(BF16), w2_ref[cols, :])
        acc = part if acc is None else acc + part
    out_ref[...] = _layer_norm(ALPHA * y + m[5:6] * acc, g_ref[...], b_ref[...])


def _ffn_ln(y, mods, w1, w3, w2, ln_g, ln_b, wa_f32, wb_f32):
    wa_spec, wa_shape = _cast_specs(wa_f32.shape, FFN_CAST_STEPS)
    wb_spec, wb_shape = _cast_specs(wb_f32.shape, FFN_CAST_STEPS)
    rows = _row_spec(D, FFN_TM)
    return pl.pallas_call(
        _ffn_ln_kernel,
        out_shape=[jax.ShapeDtypeStruct((T, D), F32), wa_shape, wb_shape],
        grid=(T // FFN_TM,),
        in_specs=[rows, _mod_spec(FFN_TM), _const_spec((D, D_FF), True), _const_spec((D, D_FF), True),
                  _const_spec((D_FF, D), True), _const_spec((1, D)), _const_spec((1, D)), wa_spec, wb_spec],
        out_specs=[rows, wa_spec, wb_spec],
        compiler_params=_params("arbitrary"),
        name="ffn_ln",
    )(y, mods, w1, w3, w2, ln_g.reshape(1, D), ln_b.reshape(1, D), wa_f32, wb_f32)


def _l1_in_kernel(y_ref, mod_ref, w_ref, q_o, k_o, v_o, kh_o, vh_o):
    i = pl.program_id(0)
    m = mod_ref[0]
    h = (y_ref[...] * (1.0 + m[1:2]) + m[0:1]).astype(BF16)
    z = _dot(h, w_ref[...])
    k = z[:, D:2 * D]
    v = z[:, 2 * D:3 * D]
    q_o[...] = z[:, 0:D].astype(BF16)
    k_o[...] = k.astype(BF16)
    v_o[...] = v.astype(BF16)

    @pl.when(i < NP_TILES)
    def _():
        for bb in range(TM // SEQ):
            rows = slice(bb * SEQ, (bb + 1) * SEQ)
            for j in range(NA_HEADS // 2):
                cols = slice(j * LANES, (j + 1) * LANES)
                kt = k[rows, cols].T
                vt = v[rows, cols].T
                for hh in range(2):
                    drows = slice(hh * HEAD_DIM, (hh + 1) * HEAD_DIM)
                    kh_o[bb, 2 * j + hh] = kt[drows]
                    vh_o[bb, 2 * j + hh] = vt[drows]


def _l1_in(y, mods, w_in):
    heads = pl.BlockSpec((TM // SEQ, NA_HEADS, HEAD_DIM, SEQ), lambda i: (jnp.minimum(i, NP_TILES - 1), 0, 0, 0))
    return pl.pallas_call(
        _l1_in_kernel,
        out_shape=[jax.ShapeDtypeStruct((T, D), BF16)] * 3
        + [jax.ShapeDtypeStruct((BATCH, NA_HEADS, HEAD_DIM, SEQ), F32)] * 2,
        grid=(NT,),
        in_specs=[_row_spec(D), _mod_spec(), _const_spec((D, 3 * D))],
        out_specs=[_row_spec(D)] * 3 + [heads, heads],
        compiler_params=_params("arbitrary"),
        name="l1_in_proj",
    )(y, mods, w_in)


def _l1_attn_prompt_kernel(q_ref, k_ref, v_ref, o_ref):
    for j in range(NA_HEADS // 2):
        cols = slice(128 * j, 128 * (j + 1))
        o = _head_pair(q_ref[:, cols], [k_ref[:, cols].astype(BF16)], [v_ref[:, cols].astype(BF16)], HD_SCALE)
        o_ref[:, cols] = o.astype(BF16)


def _l1_attn_prompt(q, k, v):
    spec = pl.BlockSpec((SEQ, D), lambda b: (b, 0))
    return pl.pallas_call(
        _l1_attn_prompt_kernel,
        out_shape=jax.ShapeDtypeStruct((T_P, D), BF16),
        grid=(BATCH,),
        in_specs=[spec, spec, spec],
        out_specs=spec,
        compiler_params=_params("arbitrary"),
        name="l1_attn_prompt",
    )(q, k, v)


NA_ROWS = DEC_SEQ // GRID_W
NA_TILE_ROWS = 4
NA_TQ = NA_TILE_ROWS * GRID_W
NA_SPAN = NA_WIN_ROWS + NA_TILE_ROWS
NA_DR = 2 * NA_WIN_ROWS - 1
COL_SPAN = 2 * GRID_W - 1


def _na_span_start(t):
    first = max(0, min(t * NA_TILE_ROWS - NA_WIN_ROWS // 2, NA_ROWS - NA_WIN_ROWS))
    return min(first, NA_ROWS - NA_SPAN)


def _l1_attn_sample_kernel(q_ref, k_ref, v_ref, kc_ref, vc_ref, ext_ref, o_ref):
    kc = kc_ref[0].astype(BF16)
    vc = vc_ref[0].astype(BF16)
    lo = _lane() < GRID_W
    qcol = lax.broadcasted_iota(jnp.int32, (GRID_W, LANES), 0)
    kcol = lax.broadcasted_iota(jnp.int32, (GRID_W, LANES), 1) % GRID_W
    cs = jnp.clip(qcol - NA_WIN_COLS // 2, 0, GRID_W - NA_WIN_COLS)
    col_ok = jnp.logical_and(kcol >= cs, kcol < cs + NA_WIN_COLS)
    neg = jnp.full((GRID_W, LANES), NEG, F32)

    def bias_tile(hh, d, half):
        x = jnp.broadcast_to(ext_ref[hh, d:d + 1, :], (GRID_W, LANES))
        shift = (LANES - COL_SPAN // 2 + GRID_W * half) % LANES
        return jnp.where(col_ok, pltpu.roll(x, shift, 1, stride=1, stride_axis=0), neg)

    tiles = [[[bias_tile(hh, d, half) for half in range(2)] for d in range(NA_DR)] for hh in range(2)]

    for t in range(NA_ROWS // NA_TILE_ROWS):
        ws = _na_span_start(t)
        keys = slice(ws * GRID_W, (ws + NA_SPAN) * GRID_W)
        kw = k_ref[keys, :].astype(BF16)
        vw = v_ref[keys, :].astype(BF16)
        bias = []
        for hh in range(2):
            rows = []
            for rr in range(NA_TILE_ROWS):
                r = t * NA_TILE_ROWS + rr
                rs = max(0, min(r - NA_WIN_ROWS // 2, NA_ROWS - NA_WIN_ROWS))
                blocks = []
                for u in range(NA_SPAN // 2):
                    halves = []
                    for half in range(2):
                        kr = ws + 2 * u + half
                        ok = rs <= kr < rs + NA_WIN_ROWS
                        halves.append(tiles[hh][kr - r + NA_WIN_ROWS - 1][half] if ok else neg)
                    blocks.append(jnp.where(lo, halves[0], halves[1]))
                rows.append(jnp.concatenate(blocks, axis=1))
            bias.append(jnp.concatenate(rows, axis=0))
        qrows = slice(t * NA_TQ, (t + 1) * NA_TQ)
        o = _head_pair(q_ref[qrows, :], [kw, kc], [vw, vc], HD_SCALE, biases=[bias, None])
        o_ref[qrows, :] = o.astype(BF16)


def _l1_attn_sample(q, k, v, kc, vc, ext):
    lat = pl.BlockSpec((DEC_SEQ, LANES), lambda j, b: (T_P // DEC_SEQ + b, j))
    ctx = pl.BlockSpec((1, PAST, LANES), lambda j, b: (b, 0, j))
    return pl.pallas_call(
        _l1_attn_sample_kernel,
        out_shape=jax.ShapeDtypeStruct((T_S, D), BF16),
        grid=(NA_HEADS // 2, DEC_BATCH),
        in_specs=[lat, lat, lat, ctx, ctx, pl.BlockSpec((2, NA_DR, LANES), lambda j, b: (j, 0, 0))],
        out_specs=pl.BlockSpec((DEC_SEQ, LANES), lambda j, b: (b, j)),
        compiler_params=_params("arbitrary", "arbitrary"),
        name="l1_attn_sample",
    )(q, k, v, kc, vc, ext)


def _na_bias_rows(rel_bias):
    rb = rel_bias.astype(F32)
    n_lo = GRID_W - 1 - (NA_WIN_COLS - 1)
    n_hi = LANES - n_lo - rb.shape[-1]
    return jnp.concatenate([jnp.repeat(rb[..., :1], n_lo, axis=-1), rb,
                            jnp.repeat(rb[..., -1:], n_hi, axis=-1)], axis=-1)


def _router_kernel(y_ref, mod_ref, rw_ref, rb_ref, h_o, meta_o, cnt_o, carry_ref):
    i = pl.program_id(0)

    @pl.when(i == 0)
    def _():
        carry_ref[...] = jnp.zeros_like(carry_ref)

    m = mod_ref[0]
    h = y_ref[...] * (1.0 + m[4:5]) + m[3:4]
    _store_rows_tiled(h_o, h)
    logits = jnp.dot(h, rw_ref[...], preferred_element_type=F32, precision=lax.Precision.HIGHEST) + rb_ref[...]
    lane = lax.broadcasted_iota(jnp.int32, (TM, LANES), 1).astype(F32)
    m1 = jnp.max(logits, axis=1, keepdims=True)
    i1 = jnp.min(jnp.where(logits == m1, lane, float(LANES)), axis=1, keepdims=True)
    sel1 = lane == i1
    rest = jnp.where(sel1, -jnp.inf, logits)
    m2 = jnp.max(rest, axis=1, keepdims=True)
    i2 = jnp.min(jnp.where(rest == m2, lane, float(LANES)), axis=1, keepdims=True)
    sel2 = lane == i2
    e2 = jnp.exp(m2 - m1)
    w1 = 1.0 / (1.0 + e2)
    w2 = e2 / (1.0 + e2)
    sel = jnp.logical_or(sel1, sel2)
    rr = lax.broadcasted_iota(jnp.int32, (TM, TM), 0)
    cc = lax.broadcasted_iota(jnp.int32, (TM, TM), 1)
    tri = jnp.where(cc < rr, 1.0, 0.0).astype(BF16)
    ahead = _dot(tri, jnp.where(sel, 1.0, 0.0).astype(BF16)) + carry_ref[...]
    r1 = jnp.sum(jnp.where(sel1, ahead, 0.0), axis=1, keepdims=True)
    r2 = jnp.sum(jnp.where(sel2, ahead, 0.0), axis=1, keepdims=True)
    meta = jnp.where(lane == 0, i1, 0.0)
    meta = jnp.where(lane == 1, i2, meta)
    meta = jnp.where(lane == 2, r1, meta)
    meta = jnp.where(lane == 3, r2, meta)
    meta = jnp.where(lane == 4, w1, meta)
    meta = jnp.where(lane == 5, w2, meta)
    meta_o[...] = meta
    carry_ref[...] = carry_ref[...] + jnp.sum(jnp.where(sel, 1.0, 0.0), axis=0, keepdims=True)
    cnt_o[...] = carry_ref[...]


def _router(y, mods, router_w, router_b):
    rw = jnp.zeros((D, LANES), F32).at[:, :N_EXPERTS].set(router_w)
    rb = jnp.full((1, LANES), NEG, F32).at[0, :N_EXPERTS].set(router_b)
    return pl.pallas_call(
        _router_kernel,
        out_shape=[jax.ShapeDtypeStruct((T * ROW_TILE, LANES), F32),
                   jax.ShapeDtypeStruct((T, LANES), F32),
                   jax.ShapeDtypeStruct((1, LANES), F32)],
        grid=(NT,),
        in_specs=[_row_spec(D), _mod_spec(), _const_spec((D, LANES)), _const_spec((1, LANES))],
        out_specs=[_row_spec(LANES, TM * ROW_TILE), _row_spec(LANES), _const_spec((1, LANES))],
        scratch_shapes=[pltpu.VMEM((1, LANES), F32)],
        compiler_params=_params("arbitrary"),
        name="moe_router",
    )(y, mods, rw, rb)


DMA_UNROLL = 8


def _row_copy(src_ref, src_row, dst_ref, dst_row, sem):
    return pltpu.make_async_copy(src_ref.at[pl.ds(pl.multiple_of(src_row * ROW_TILE, ROW_TILE), ROW_TILE)],
                                 dst_ref.at[pl.ds(pl.multiple_of(dst_row * ROW_TILE, ROW_TILE), ROW_TILE)], sem)


def _group_tile_copy(src_ref, dst_ref, tile, sem):
    start = pl.multiple_of(tile * (TG * ROW_TILE), TG * ROW_TILE)
    return pltpu.make_async_copy(src_ref, dst_ref.at[pl.ds(start, TG * ROW_TILE)], sem)


def _dispatch_kernel(pos_ref, last_ref, na_ref, h_ref, wf_ref, xg_ref, wb_ref, zero_ref, sem, zsem):
    i = pl.program_id(0)
    _cast_block(i, NP_TILES, wf_ref, wb_ref)

    @pl.when(i == 0)
    def _():
        zero_ref[...] = jnp.zeros_like(zero_ref)
        for e in range(N_EXPERTS):
            @pl.when(last_ref[e] >= 0)
            def _():
                _group_tile_copy(zero_ref, xg_ref, last_ref[e], zsem).start()

        def start_unused(g, carry):
            _group_tile_copy(zero_ref, xg_ref, g, zsem).start()
            return carry

        def wait_one(g, carry):
            _group_tile_copy(zero_ref, xg_ref, 0, zsem).wait()
            return carry

        lax.fori_loop(na_ref[0], NG, start_unused, 0)
        lax.fori_loop(0, na_ref[1], wait_one, 0)

    def issue(r, carry):
        t = i * TM + r
        _row_copy(h_ref, r, xg_ref, pos_ref[2 * t], sem).start(priority=0)
        _row_copy(h_ref, r, xg_ref, pos_ref[2 * t + 1], sem).start(priority=1)
        return carry

    lax.fori_loop(0, TM, issue, 0, unroll=DMA_UNROLL)
    for _ in range(2):
        pltpu.make_async_copy(h_ref, xg_ref.at[pl.ds(0, TM * ROW_TILE)], sem).wait()


def _dispatch(pos, last_tile, tile_counts, h_tiled, w_f32):
    w_spec, wb_shape = _cast_specs(w_f32.shape, NP_TILES)
    return pl.pallas_call(
        _dispatch_kernel,
        out_shape=[jax.ShapeDtypeStruct((P_ROWS * ROW_TILE, LANES), F32), wb_shape],
        grid_spec=pltpu.PrefetchScalarGridSpec(
            num_scalar_prefetch=3, grid=(NT,),
            in_specs=[pl.BlockSpec((TM * ROW_TILE, LANES), lambda i, *_: (i, 0)), w_spec],
            out_specs=[pl.BlockSpec(memory_space=pl.ANY), w_spec],
            scratch_shapes=[pltpu.VMEM((TG * ROW_TILE, LANES), F32), pltpu.SemaphoreType.DMA(()),
                            pltpu.SemaphoreType.DMA(())]),
        compiler_params=_params("arbitrary"),
        name="moe_dispatch",
    )(pos, last_tile, tile_counts, h_tiled, w_f32)


def _expert_ffn_kernel(te_ref, na_ref, x_ref, w1_ref, w3_ref, w2_ref, o_ref):
    g = pl.program_id(0)

    @pl.when(g < na_ref[0])
    def _():
        x = _load_rows_tiled(x_ref, TG).astype(BF16)
        acc = None
        for c in range(E_FF // F_CHUNK):
            cols = slice(c * F_CHUNK, (c + 1) * F_CHUNK)
            a = _dot(x, w1_ref[0, :, cols])
            b = _dot(x, w3_ref[0, :, cols])
            part = _dot((_silu(a) * b).astype(BF16), w2_ref[0, cols, :])
            acc = part if acc is None else acc + part
        _store_rows_tiled(o_ref, acc)

    @pl.when(g >= na_ref[0])
    def _():
        o_ref[...] = jnp.zeros_like(o_ref)


def _expert_ffn(tile_expert, n_active, xg, w1, w3, w2):
    rows = pl.BlockSpec((TG * ROW_TILE, LANES), lambda g, te, na: (g, 0))
    w_up = pl.BlockSpec((1, D, E_FF), lambda g, te, na: (te[g], 0, 0))
    w_dn = pl.BlockSpec((1, E_FF, D), lambda g, te, na: (te[g], 0, 0))
    return pl.pallas_call(
        _expert_ffn_kernel,
        out_shape=jax.ShapeDtypeStruct((P_ROWS * ROW_TILE, LANES), F32),
        grid_spec=pltpu.PrefetchScalarGridSpec(
            num_scalar_prefetch=2, grid=(NG,),
            in_specs=[rows, w_up, w_up, w_dn],
            out_specs=rows),
        compiler_params=_params("arbitrary"),
        name="moe_expert_ffn",
    )(tile_expert, n_active, xg, w1, w3, w2)


def _combine_ln_kernel(pos_ref, y_ref, mod_ref, meta_ref, g_ref, b_ref, eo_ref, outp_ref, outs_ref,
                       buf1, buf2, sem):
    i = pl.program_id(0)

    def issue(r, carry):
        t = i * TM + r
        _row_copy(eo_ref, pos_ref[2 * t], buf1, r, sem).start(priority=0)
        _row_copy(eo_ref, pos_ref[2 * t + 1], buf2, r, sem).start(priority=1)
        return carry

    lax.fori_loop(0, TM, issue, 0, unroll=DMA_UNROLL)
    for buf in (buf1, buf2):
        pltpu.make_async_copy(eo_ref.at[pl.ds(0, TM * ROW_TILE)], buf, sem).wait()
    m = mod_ref[0]
    meta = meta_ref[...]
    f = meta[:, 4:5] * _load_rows_tiled(buf1, TM) + meta[:, 5:6] * _load_rows_tiled(buf2, TM)
    out = _layer_norm(ALPHA * y_ref[...] + m[5:6] * f, g_ref[...], b_ref[...])

    @pl.when(i < NP_TILES)
    def _():
        outp_ref[...] = out

    @pl.when(i >= NP_TILES)
    def _():
        outs_ref[...] = out


def _combine_ln(pos, y, mods, meta, ln_g, ln_b, eo):
    return pl.pallas_call(
        _combine_ln_kernel,
        out_shape=[jax.ShapeDtypeStruct((T_P, D), F32), jax.ShapeDtypeStruct((T_S, D), F32)],
        grid_spec=pltpu.PrefetchScalarGridSpec(
            num_scalar_prefetch=1, grid=(NT,),
            in_specs=[pl.BlockSpec((TM, D), lambda i, *_: (i, 0)),
                      pl.BlockSpec((1, 6, D), lambda i, *_: (_cond_of_tile(i), 0, 0)),
                      pl.BlockSpec((TM, LANES), lambda i, *_: (i, 0)),
                      pl.BlockSpec((1, D), lambda i, *_: (0, 0)),
                      pl.BlockSpec((1, D), lambda i, *_: (0, 0)),
                      pl.BlockSpec(memory_space=pl.ANY)],
            out_specs=[_prompt_tile_spec(D), _sample_tile_spec(D)],
            scratch_shapes=[pltpu.VMEM((TM * ROW_TILE, LANES), F32),
                            pltpu.VMEM((TM * ROW_TILE, LANES), F32),
                            pltpu.SemaphoreType.DMA(())]),
        compiler_params=_params("arbitrary"),
        name="moe_combine_ln",
    )(pos, y, mods, meta, ln_g.reshape(1, D), ln_b.reshape(1, D), eo)


def _moe_ln(y, mods, router_w, router_b, w1, w3, w2_f32, ln_g, ln_b):
    h_tiled, meta, counts = _router(y, mods, router_w, router_b)
    cnt = counts[0, :N_EXPERTS].astype(jnp.int32)
    tiles = (cnt + TG - 1) // TG
    tile_end = jnp.cumsum(tiles)
    offs = (tile_end - tiles) * TG
    expert = meta[:, 0:2].astype(jnp.int32)
    chosen = expert[:, :, None] == jnp.arange(N_EXPERTS)[None, None, :]
    pos = jnp.sum(jnp.where(chosen, offs[None, None, :], 0), axis=-1) + meta[:, 2:4].astype(jnp.int32)
    pos = pos.reshape(2 * T)
    tile_expert = jnp.sum((jnp.arange(NG)[:, None] >= tile_end[None, :]).astype(jnp.int32), axis=1)
    tile_expert = jnp.minimum(tile_expert, N_EXPERTS - 1)
    n_active = tile_end[-1:].astype(jnp.int32)
    last_tile = jnp.where(tiles > 0, tile_end - 1, -1).astype(jnp.int32)
    n_zeroed = jnp.sum((tiles > 0).astype(jnp.int32)) + NG - tile_end[-1]
    tile_counts = jnp.stack([tile_end[-1], n_zeroed]).astype(jnp.int32)
    xg, w2 = _dispatch(pos, last_tile, tile_counts, h_tiled, w2_f32.reshape(N_EXPERTS * E_FF, D))
    eo = _expert_ffn(tile_expert, n_active, xg, w1, w3, w2.reshape(N_EXPERTS, E_FF, D))
    return _combine_ln(pos, y, mods, meta, ln_g, ln_b, eo)


def _l0_weight_layouts(w_in, w_q_up, w_kv_up):
    a, b, c = MLA_Q_LORA, MLA_Q_LORA + MLA_KV_LORA, MLA_Q_LORA + MLA_KV_LORA + MLA_ROPE
    k_rope = w_in[:, b:c]
    w_in_r = jnp.concatenate([w_in[:, :b], w_in[:, c:], k_rope, k_rope, k_rope, k_rope], axis=1)
    wq = w_q_up.reshape(MLA_Q_LORA, MLA_HEADS, MLA_NOPE + MLA_ROPE)
    w_q_up_r = jnp.concatenate([wq[:, :, :MLA_NOPE].reshape(MLA_Q_LORA, -1),
                                wq[:, :, MLA_NOPE:].reshape(MLA_Q_LORA, -1)], axis=1)
    wkv = w_kv_up.reshape(MLA_KV_LORA, MLA_HEADS, MLA_NOPE + MLA_V)
    w_kv_up_r = jnp.concatenate([wkv[:, :, :MLA_NOPE].reshape(MLA_KV_LORA, -1),
                                 wkv[:, :, MLA_NOPE:].reshape(MLA_KV_LORA, -1)], axis=1)
    return w_in_r.astype(BF16), w_q_up_r.astype(BF16), w_kv_up_r.astype(BF16)


def kernel(x_prompt, x_sample, cache_l0_mla_ckv, cache_l0_mla_krope, cache_l0_swa_k, cache_l0_swa_v,
           cache_l1_na_k, cache_l1_na_v, c, c_ctx,
           l0_ada_w, l0_ada_b, l0_w_in, l0_mla_q_norm, l0_mla_w_q_up, l0_mla_kv_norm, l0_mla_w_kv_up,
           l0_swa_sink, l0_w_out, l0_ln1_g, l0_ln1_b, l0_ffn_w1, l0_ffn_w3, l0_ffn_w2, l0_ln2_g, l0_ln2_b,
           l1_ada_w, l1_ada_b, l1_w_in, l1_na_rel_bias, l1_w_out, l1_ln1_g, l1_ln1_b,
           l1_moe_router_w, l1_moe_router_b, l1_moe_w1, l1_moe_w3, l1_moe_w2, l1_ln2_g, l1_ln2_b):
    cond = jnp.concatenate([c_ctx[None, :], c, jnp.zeros((8 - N_COND, D), F32)], axis=0)

    mods = _adaln(cond, l0_ada_w, l0_ada_b)
    w_in_r, w_q_up_r, w_kv_up_r = _l0_weight_layouts(l0_w_in, l0_mla_w_q_up, l0_mla_w_kv_up)
    y, qn, qr, kn, vm, kr4, sq, sk, sv, ckv, krope_t, sk_t, sv_t = _l0_in(
        x_prompt.reshape(T_P, D), x_sample.reshape(T_S, D), mods, w_in_r, l0_mla_q_norm, w_q_up_r,
        l0_mla_kv_norm, w_kv_up_r, _rope_tables(MLA_ROPE, 256), _rope_tables(HEAD_DIM, 512))
    knc, vc = _kv_up(cache_l0_mla_ckv.reshape(DEC_BATCH * PAST, MLA_KV_LORA), w_kv_up_r)
    kr4c = jnp.tile(cache_l0_mla_krope.reshape(DEC_BATCH * PAST, MLA_ROPE), (1, 4))
    skc = cache_l0_swa_k.reshape(DEC_BATCH * PAST, SWA_KV_HEADS * HEAD_DIM)
    svc = cache_l0_swa_v.reshape(DEC_BATCH * PAST, SWA_KV_HEADS * HEAD_DIM)
    o_p, ffn_w1, ffn_w3, ffn_w2 = _l0_attn_prompt(l0_swa_sink, qn, qr, kn, vm, kr4, sq, sk, sv,
                                                  [l0_ffn_w1, l0_ffn_w3, l0_ffn_w2])
    o_s = _l0_attn_sample(l0_swa_sink, qn, qr, kn, vm, kr4, sq, sk, sv, knc, vc, kr4c, skc, svc)
    y = _out_ln(o_p, o_s, l0_w_out.astype(BF16), y, mods, l0_ln1_g, l0_ln1_b)
    y, moe_w1, moe_w3 = _ffn_ln(y, mods, ffn_w1, ffn_w3, ffn_w2, l0_ln2_g, l0_ln2_b,
                                l1_moe_w1.reshape(N_EXPERTS * D, E_FF), l1_moe_w3.reshape(N_EXPERTS * D, E_FF))
    new_ckv = ckv.reshape(BATCH, SEQ, MLA_KV_LORA)
    new_krope = jnp.transpose(krope_t, (0, 2, 1))
    new_sk = jnp.transpose(sk_t, (0, 3, 1, 2))
    new_sv = jnp.transpose(sv_t, (0, 3, 1, 2))

    mods = _adaln(cond, l1_ada_w, l1_ada_b)
    q, k, v, k_heads, v_heads = _l1_in(y, mods, l1_w_in.astype(BF16))
    o_p = _l1_attn_prompt(q, k, v)
    o_s = _l1_attn_sample(q, k, v, cache_l1_na_k.reshape(DEC_BATCH, PAST, D),
                          cache_l1_na_v.reshape(DEC_BATCH, PAST, D), _na_bias_rows(l1_na_rel_bias))
    y = _out_ln(o_p, o_s, l1_w_out.astype(BF16), y, mods, l1_ln1_g, l1_ln1_b)
    y_p, y_s = _moe_ln(y, mods, l1_moe_router_w, l1_moe_router_b, moe_w1.reshape(N_EXPERTS, D, E_FF),
                       moe_w3.reshape(N_EXPERTS, D, E_FF), l1_moe_w2, l1_ln2_g, l1_ln2_b)
    new_k = jnp.transpose(k_heads, (0, 3, 1, 2))
    new_v = jnp.transpose(v_heads, (0, 3, 1, 2))

    return (y_p.reshape(BATCH, SEQ, D), y_s.reshape(DEC_BATCH, DEC_SEQ, D),
            new_ckv, new_krope, new_sk, new_sv, new_k, new_v)
```

```python
import functools

import jax
import jax.numpy as jnp
import numpy as np
from jax import lax
from jax.experimental import pallas as pl
from jax.experimental.pallas import tpu as pltpu

F32 = jnp.float32
BF16 = jnp.bfloat16

D = 1024
BATCH, SEQ = 32, 256
DEC_BATCH, DEC_SEQ = 2, 1024
PAST = 256
GRID_W = 64
T_P = BATCH * SEQ
T_S = DEC_BATCH * DEC_SEQ
T = T_P + T_S
N_COND = 1 + DEC_BATCH

MLA_HEADS, MLA_Q_LORA, MLA_KV_LORA, MLA_NOPE, MLA_ROPE, MLA_V = 8, 384, 256, 64, 32, 64
SWA_HEADS, SWA_KV_HEADS, SWA_WINDOW, HEAD_DIM = 8, 2, 128, 64
NA_HEADS, NA_WIN_ROWS, NA_WIN_COLS = 16, 8, 16
D_FF, N_EXPERTS, E_FF = 2816, 8, 3584
ROPE_THETA = 10000.0
LN_EPS, RMS_EPS = 1e-5, 1e-6
NEG = -1e30
ALPHA = 4.0 ** 0.25

LANES = 128
TM = 512
NT = T // TM
NP_TILES = T_P // TM
TILES_PER_SAMPLE = DEC_SEQ // TM
TG = 256
P_ROWS = 2 * T + N_EXPERTS * TG
NG = P_ROWS // TG
F_CHUNK = 896
VMEM_LIMIT = 56 * 1024 * 1024


def _params(*sem):
    return pltpu.CompilerParams(dimension_semantics=sem, vmem_limit_bytes=VMEM_LIMIT)


def _const_spec(shape, single_buffer=False):
    if single_buffer:
        return pl.BlockSpec(shape, lambda *_: (0,) * len(shape), pipeline_mode=pl.Buffered(1))
    return pl.BlockSpec(shape, lambda *_: (0,) * len(shape))


def _cast_specs(shape, steps):
    rows, width = shape
    spec = pl.BlockSpec((rows // steps, width), lambda i, *_: (jnp.minimum(i, steps - 1), 0))
    return spec, jax.ShapeDtypeStruct(shape, BF16)


def _cast_block(i, steps, src_ref, dst_ref):
    @pl.when(i < steps)
    def _():
        dst_ref[...] = src_ref[...].astype(BF16)


ROW_TILE = D // LANES


def _store_rows_tiled(ref, x):
    n = x.shape[0]
    for k in range(ROW_TILE):
        ref[pl.ds(k, n, stride=ROW_TILE), :] = x[:, k * LANES:(k + 1) * LANES]


def _load_rows_tiled(ref, n):
    return jnp.concatenate([ref[pl.ds(k, n, stride=ROW_TILE), :] for k in range(ROW_TILE)], axis=1)


def _cond_of_tile(i, rows=TM):
    per_sample = DEC_SEQ // rows
    return jnp.maximum((i - (T_P // rows - per_sample)) // per_sample, 0)


def _mod_spec(rows=TM):
    return pl.BlockSpec((1, 6, D), lambda i: (_cond_of_tile(i, rows), 0, 0))


def _row_spec(width, rows=TM):
    return pl.BlockSpec((rows, width), lambda i: (i, 0))


def _dot(a, b):
    return jnp.dot(a, b, preferred_element_type=F32)


def _dot_nt(a, b):
    return lax.dot_general(a, b, (((1,), (1,)), ((), ())), preferred_element_type=F32)


def _layer_norm(r, g, b):
    mu = jnp.mean(r, axis=-1, keepdims=True)
    d = r - mu
    var = jnp.mean(d * d, axis=-1, keepdims=True)
    return d * lax.rsqrt(var + LN_EPS) * g + b


def _rms_norm(x, g):
    return x * lax.rsqrt(jnp.mean(x * x, axis=-1, keepdims=True) + RMS_EPS) * g


def _silu(x):
    return x * jax.nn.sigmoid(x)


def _ada_kernel(c_ref, w_ref, b_ref, o_ref):
    s = _silu(c_ref[...]).astype(BF16)
    o_ref[...] = _dot(s, w_ref[...].astype(BF16)) + b_ref[...]


def _adaln(cond, ada_w, ada_b):
    nb = 1536
    out = pl.pallas_call(
        _ada_kernel,
        out_shape=jax.ShapeDtypeStruct((8, 6 * D), F32),
        grid=(6 * D // nb,),
        in_specs=[_const_spec((8, D)), pl.BlockSpec((D, nb), lambda j: (0, j)),
                  pl.BlockSpec((1, nb), lambda j: (0, j))],
        out_specs=pl.BlockSpec((8, nb), lambda j: (0, j)),
        compiler_params=_params("arbitrary"),
        name="adaln",
    )(cond, ada_w, ada_b.reshape(1, 6 * D))
    return out[:N_COND].reshape(N_COND, 6, D)


def _rope_tables(head_dim, width):
    half = head_dim // 2
    nf = half // 2
    lane = np.arange(width)
    d = lane % head_dim
    dd = d % half
    f = dd % nf
    inv = np.float32(ROPE_THETA) ** (-f.astype(np.float32) / np.float32(nf))
    t = np.arange(DEC_SEQ)
    pos = np.where((d // half)[None, :] == 0, (t // GRID_W)[:, None], (t % GRID_W)[:, None])
    ang = pos.astype(np.float32) * inv[None, :].astype(np.float32)
    cos, sin = np.cos(ang), np.sin(ang)
    first = (dd < nf)[None, :]
    zero = np.float32(0.0)
    return (jnp.asarray(cos, F32), jnp.asarray(np.where(first, -sin, zero), F32),
            jnp.asarray(np.where(first, zero, sin), F32))


def _rope(x, cos, sin_up, sin_dn, nf):
    w = x.shape[-1]
    return x * cos + pltpu.roll(x, w - nf, 1) * sin_up + pltpu.roll(x, nf, 1) * sin_dn


L0_COLS = MLA_Q_LORA + MLA_KV_LORA + 512 + 128 + 128 + 128


def _l0_in_kernel(xp_ref, xs_ref, mod_ref, win_ref, qn_ref, wq_ref, kvn_ref, wkv_ref,
                  c8_ref, su8_ref, sd8_ref, c16_ref, su16_ref, sd16_ref,
                  y_o, qnope_o, qrope_o, knope_o, vmla_o, kr4_o, sq_o, sk_o, sv_o,
                  ckv_o, krt_o, skt_o, svt_o):
    i = pl.program_id(0)
    m = mod_ref[0]
    x = jnp.where(i < NP_TILES, xp_ref[...], xs_ref[...])
    y_o[...] = x
    h = (x * (1.0 + m[1:2]) + m[0:1]).astype(BF16)
    z = _dot(h, win_ref[...])
    q_lat = z[:, 0:384]
    kv_lat = z[:, 384:640]
    sq = z[:, 640:1152]
    sk = z[:, 1152:1280]
    sv = z[:, 1280:1408]
    kr4 = z[:, 1408:1536]
    q = _dot(_rms_norm(q_lat, qn_ref[...]).astype(BF16), wq_ref[...])
    c_kv = _rms_norm(kv_lat, kvn_ref[...])
    kv = _dot(c_kv.astype(BF16), wkv_ref[...])
    q = q * MLA_QMUL
    sq = sq * HD_QMUL
    qnope_o[...] = q[:, 0:512].astype(BF16)
    knope_o[...] = kv[:, 0:512].astype(BF16)
    vmla_o[...] = kv[:, 512:1024].astype(BF16)
    sv_o[...] = sv
    q_rope = q[:, 512:768]

    @pl.when(i < NP_TILES)
    def _():
        qrope_o[...] = q_rope.astype(BF16)
        kr4_o[...] = kr4
        sq_o[...] = sq.astype(BF16)
        sk_o[...] = sk
        ckv_o[...] = c_kv
        for bb in range(TM // SEQ):
            rows = slice(bb * SEQ, (bb + 1) * SEQ)
            krt_o[bb] = kr4[rows].T[:MLA_ROPE]
            skt = sk[rows].T
            svt = sv[rows].T
            for g in range(SWA_KV_HEADS):
                skt_o[bb, g] = skt[g * HEAD_DIM:(g + 1) * HEAD_DIM]
                svt_o[bb, g] = svt[g * HEAD_DIM:(g + 1) * HEAD_DIM]

    @pl.when(i >= NP_TILES)
    def _():
        c8, su8, sd8 = c8_ref[...], su8_ref[...], sd8_ref[...]
        c16, su16, sd16 = c16_ref[...], su16_ref[...], sd16_ref[...]
        qrope_o[...] = _rope(q_rope, c8, su8, sd8, 8).astype(BF16)
        kr4_o[...] = _rope(kr4, c8[:, :128], su8[:, :128], sd8[:, :128], 8)
        sq_o[...] = _rope(sq, c16, su16, sd16, 16).astype(BF16)
        sk_o[...] = _rope(sk, c16[:, :128], su16[:, :128], sd16[:, :128], 16)


def _prompt_tile_spec(width):
    return pl.BlockSpec((TM, width), lambda i, *_: (jnp.minimum(i, NP_TILES - 1), 0))


def _sample_tile_spec(width):
    return pl.BlockSpec((TM, width), lambda i, *_: (jnp.maximum(i - NP_TILES, 0), 0))


def _l0_in(x_prompt, x_sample, mods, w_in_r, q_norm, w_q_up_r, kv_norm, w_kv_up_r, tabs8, tabs16):
    def tab_spec(width):
        return pl.BlockSpec((TM, width), lambda i: (jnp.maximum(i - NP_TILES, 0) % TILES_PER_SAMPLE, 0))

    out_shape = [
        jax.ShapeDtypeStruct((T, D), F32),
        jax.ShapeDtypeStruct((T, 512), BF16),
        jax.ShapeDtypeStruct((T, 256), BF16),
        jax.ShapeDtypeStruct((T, 512), BF16),
        jax.ShapeDtypeStruct((T, 512), BF16),
        jax.ShapeDtypeStruct((T, 128), F32),
        jax.ShapeDtypeStruct((T, 512), BF16),
        jax.ShapeDtypeStruct((T, 128), F32),
        jax.ShapeDtypeStruct((T, 128), F32),
    ]
    per_tile = TM // SEQ
    prompt_block = lambda *dims: pl.BlockSpec((per_tile,) + dims,
                                              lambda i: (jnp.minimum(i, NP_TILES - 1),) + (0,) * len(dims))
    cache_shape = [
        jax.ShapeDtypeStruct((T_P, MLA_KV_LORA), F32),
        jax.ShapeDtypeStruct((BATCH, MLA_ROPE, SEQ), F32),
        jax.ShapeDtypeStruct((BATCH, SWA_KV_HEADS, HEAD_DIM, SEQ), F32),
        jax.ShapeDtypeStruct((BATCH, SWA_KV_HEADS, HEAD_DIM, SEQ), F32),
    ]
    cache_specs = [_prompt_tile_spec(MLA_KV_LORA), prompt_block(MLA_ROPE, SEQ),
                   prompt_block(SWA_KV_HEADS, HEAD_DIM, SEQ), prompt_block(SWA_KV_HEADS, HEAD_DIM, SEQ)]
    return pl.pallas_call(
        _l0_in_kernel,
        out_shape=out_shape + cache_shape,
        grid=(NT,),
        in_specs=[_prompt_tile_spec(D), _sample_tile_spec(D), _mod_spec(), _const_spec((D, L0_COLS)),
                  _const_spec((1, MLA_Q_LORA)), _const_spec((MLA_Q_LORA, 768)),
                  _const_spec((1, MLA_KV_LORA)), _const_spec((MLA_KV_LORA, 1024)),
                  tab_spec(256), tab_spec(256), tab_spec(256),
                  tab_spec(512), tab_spec(512), tab_spec(512)],
        out_specs=[_row_spec(s.shape[1]) for s in out_shape] + cache_specs,
        compiler_params=_params("arbitrary"),
        name="l0_in_proj",
    )(x_prompt, x_sample, mods, w_in_r, q_norm.reshape(1, -1), w_q_up_r, kv_norm.reshape(1, -1), w_kv_up_r,
      *tabs8, *tabs16)


def _kv_up_kernel(c_ref, w_ref, k_o, v_o):
    kv = _dot(c_ref[...].astype(BF16), w_ref[...])
    k_o[...] = kv[:, 0:512].astype(BF16)
    v_o[...] = kv[:, 512:1024].astype(BF16)


def _kv_up(ckv, w_kv_up_r):
    n = ckv.shape[0]
    return pl.pallas_call(
        _kv_up_kernel,
        out_shape=[jax.ShapeDtypeStruct((n, 512), BF16)] * 2,
        grid=(1,),
        in_specs=[_const_spec((n, MLA_KV_LORA)), _const_spec((MLA_KV_LORA, 1024))],
        out_specs=[_const_spec((n, 512))] * 2,
        compiler_params=_params("arbitrary"),
        name="l0_ctx_kv_up",
    )(ckv, w_kv_up_r)


def _lane():
    return lax.broadcasted_iota(jnp.int32, (1, LANES), 1)


def _attend(scores, values, extra_logit=None):
    m = jnp.max(scores[0], axis=1, keepdims=True)
    for s in scores[1:]:
        m = jnp.maximum(m, jnp.max(s, axis=1, keepdims=True))
    if extra_logit is not None:
        m = jnp.maximum(m, extra_logit)
    den = None
    acc = None
    for s, v in zip(scores, values):
        e = jnp.exp2(s - m)
        d = jnp.sum(e, axis=1, keepdims=True)
        a = _dot(e.astype(BF16), v)
        den = d if den is None else den + d
        acc = a if acc is None else acc + a
    if extra_logit is not None:
        den = den + jnp.exp2(extra_logit - m)
    return acc / den


def _head_pair(q2, keys, values, masks=None, biases=None, sinks=None, q_extra=None):
    lo = _lane() < 64
    zero = jnp.zeros_like(q2)
    outs = []
    for hh in range(2):
        qm = jnp.where(lo if hh == 0 else jnp.logical_not(lo), q2, zero)
        if q_extra is not None:
            qm = jnp.concatenate([qm, q_extra[hh]], axis=1)
        scores = []
        for n, k in enumerate(keys):
            s = _dot_nt(qm, k)
            if biases is not None and biases[n] is not None:
                s = s + biases[n][hh]
            if masks is not None and masks[n] is not None:
                s = jnp.where(masks[n], s, NEG)
            scores.append(s)
        outs.append(_attend(scores, values, None if sinks is None else sinks[hh]))
    return jnp.where(lo, outs[0], outs[1])


def _dup_halves(x):
    lo = _lane() < 64
    sw = pltpu.roll(x, 64, 1)
    return jnp.where(lo, x, sw), jnp.where(lo, sw, x)


LOG2E = 1.4426950408889634
MLA_QMUL = (MLA_NOPE + MLA_ROPE) ** -0.5 * LOG2E
HD_QMUL = HEAD_DIM ** -0.5 * LOG2E


def _mla_pairs(qn_ref, qr_ref, key_sets, o_ref):
    lane = _lane()
    for j in range(MLA_HEADS // 2):
        cols = slice(128 * j, 128 * (j + 1))
        qr = qr_ref[:, 128 * (j // 2):128 * (j // 2 + 1)]
        zero = jnp.zeros_like(qr)
        q_extra = [jnp.where((lane // MLA_ROPE) == ((2 * j + hh) % 4), qr, zero) for hh in range(2)]
        keys = [jnp.concatenate([kn[:, cols], kr4], axis=1) for kn, kr4, _ in key_sets]
        values = [v[:, cols] for _, _, v in key_sets]
        o = _head_pair(qn_ref[:, cols], keys, values, q_extra=q_extra)
        o_ref[:, cols] = o.astype(BF16)


def _swa_pairs(sink_ref, sq_ref, key_sets, masks, o_ref):
    kd = [[a.astype(BF16) for a in _dup_halves(k)] for k, _ in key_sets]
    vd = [[a.astype(BF16) for a in _dup_halves(v)] for _, v in key_sets]
    for g in range(SWA_KV_HEADS):
        for u in range(2):
            c = 2 * g + u
            cols = slice(128 * c, 128 * (c + 1))
            sinks = [sink_ref[2 * c + hh] * LOG2E for hh in range(2)]
            o = _head_pair(sq_ref[:, cols], [k[g] for k in kd], [v[g] for v in vd],
                           masks=masks, sinks=sinks)
            o_ref[:, 512 + 128 * c:512 + 128 * (c + 1)] = o.astype(BF16)


def _cast_steps(rows, max_steps):
    steps = max_steps
    while rows % (16 * steps):
        steps //= 2
    return steps


def _l0_attn_prompt_kernel(cast_steps, sink_ref, qn_ref, qr_ref, kn_ref, v_ref, kr4_ref, sq_ref, sk_ref, sv_ref,
                           wf1_ref, wf2_ref, wf3_ref, o_ref, wb1_ref, wb2_ref, wb3_ref):
    _mla_pairs(qn_ref, qr_ref, [(kn_ref, kr4_ref[...].astype(BF16), v_ref)], o_ref)
    _swa_pairs(sink_ref, sq_ref, [(sk_ref[...], sv_ref[...])], None, o_ref)
    for steps, wf_ref, wb_ref in zip(cast_steps, (wf1_ref, wf2_ref, wf3_ref), (wb1_ref, wb2_ref, wb3_ref)):
        _cast_block(pl.program_id(0), steps, wf_ref, wb_ref)


def _l0_attn_prompt(sink, qn, qr, kn, vm, kr4, sq, sk, sv, weights_f32):
    spec = lambda w: pl.BlockSpec((SEQ, w), lambda b, *_: (b, 0))
    cast_steps = tuple(_cast_steps(w.shape[0], BATCH) for w in weights_f32)
    casts = [_cast_specs(w.shape, s) for w, s in zip(weights_f32, cast_steps)]
    return pl.pallas_call(
        functools.partial(_l0_attn_prompt_kernel, cast_steps),
        out_shape=[jax.ShapeDtypeStruct((T_P, D), BF16)] + [c[1] for c in casts],
        grid_spec=pltpu.PrefetchScalarGridSpec(
            num_scalar_prefetch=1, grid=(BATCH,),
            in_specs=[spec(512), spec(256), spec(512), spec(512), spec(128), spec(512), spec(128), spec(128)]
            + [c[0] for c in casts],
            out_specs=[spec(D)] + [c[0] for c in casts]),
        compiler_params=_params("arbitrary"),
        name="l0_attn_prompt",
    )(sink, qn, qr, kn, vm, kr4, sq, sk, sv, *weights_f32)


TQ_S = 256


def _l0_attn_sample_kernel(sink_ref, qn_ref, qr_ref, sq_ref, kn_ref, v_ref, kr4_ref, sk_ref, sv_ref,
                           knc_ref, vc_ref, kr4c_ref, skc_ref, svc_ref, o_ref):
    i = pl.program_id(1)
    _mla_pairs(qn_ref, qr_ref,
               [(knc_ref, kr4c_ref[...].astype(BF16), vc_ref), (kn_ref, kr4_ref[...].astype(BF16), v_ref)],
               o_ref)
    span = TQ_S + 2 * SWA_WINDOW
    start = pl.multiple_of(jnp.clip(i * TQ_S - SWA_WINDOW, 0, DEC_SEQ - span), SWA_WINDOW)
    qpos = i * TQ_S + lax.broadcasted_iota(jnp.int32, (TQ_S, span), 0)
    kpos = start + lax.broadcasted_iota(jnp.int32, (TQ_S, span), 1)
    band = jnp.abs(qpos - kpos) <= SWA_WINDOW
    keys = pl.ds(start, span)
    _swa_pairs(sink_ref, sq_ref, [(skc_ref[...], svc_ref[...]), (sk_ref[keys, :], sv_ref[keys, :])],
               [None, band], o_ref)


def _l0_attn_sample(sink, qn, qr, kn, vm, kr4, sq, sk, sv, knc, vc, kr4c, skc, svc):
    nq = DEC_SEQ // TQ_S
    qspec = lambda w: pl.BlockSpec((TQ_S, w), lambda b, i, *_: (T_P // TQ_S + b * nq + i, 0))
    kspec = lambda w: pl.BlockSpec((DEC_SEQ, w), lambda b, i, *_: (T_P // DEC_SEQ + b, 0))
    cspec = lambda w: pl.BlockSpec((PAST, w), lambda b, i, *_: (b, 0))
    return pl.pallas_call(
        _l0_attn_sample_kernel,
        out_shape=jax.ShapeDtypeStruct((T_S, D), BF16),
        grid_spec=pltpu.PrefetchScalarGridSpec(
            num_scalar_prefetch=1, grid=(DEC_BATCH, nq),
            in_specs=[qspec(512), qspec(256), qspec(512),
                      kspec(512), kspec(512), kspec(128), kspec(128), kspec(128),
                      cspec(512), cspec(512), cspec(128), cspec(128), cspec(128)],
            out_specs=pl.BlockSpec((TQ_S, D), lambda b, i, *_: (b * nq + i, 0))),
        compiler_params=_params("arbitrary", "arbitrary"),
        name="l0_attn_sample",
    )(sink, qn, qr, sq, kn, vm, kr4, sk, sv, knc, vc, kr4c, skc, svc)


OUT_LN_ROWS = 256


def _out_ln_kernel(op_ref, os_ref, w_ref, y_ref, mod_ref, g_ref, b_ref, out_ref):
    i = pl.program_id(0)
    m = mod_ref[0]
    for s in range(TM // OUT_LN_ROWS):
        rows = slice(s * OUT_LN_ROWS, (s + 1) * OUT_LN_ROWS)
        o = jnp.where(i < NP_TILES, op_ref[rows, :], os_ref[rows, :])
        r = ALPHA * y_ref[rows, :] + m[2:3] * _dot(o, w_ref[...])
        out_ref[rows, :] = _layer_norm(r, g_ref[...], b_ref[...])


def _out_ln(o_prompt, o_sample, w_out, y, mods, ln_g, ln_b):
    return pl.pallas_call(
        _out_ln_kernel,
        out_shape=jax.ShapeDtypeStruct((T, D), F32),
        grid=(NT,),
        in_specs=[_prompt_tile_spec(D), _sample_tile_spec(D),
                  _const_spec((D, D)), _row_spec(D), _mod_spec(),
                  _const_spec((1, D)), _const_spec((1, D))],
        out_specs=_row_spec(D),
        compiler_params=_params("arbitrary"),
        name="out_proj_ln",
    )(o_prompt, o_sample, w_out, y, mods, ln_g.reshape(1, D), ln_b.reshape(1, D))


FFN_CHUNK = D_FF // 2
FFN_TM = 256
FFN_CAST_STEPS = 32


def _ffn_ln_kernel(y_ref, mod_ref, w1_ref, w3_ref, w2_ref, g_ref, b_ref, wfa_ref, wfb_ref,
                   out_ref, wba_ref, wbb_ref):
    i = pl.program_id(0)
    _cast_block(i, FFN_CAST_STEPS, wfa_ref, wba_ref)
    _cast_block(i, FFN_CAST_STEPS, wfb_ref, wbb_ref)
    m = mod_ref[0]
    y = y_ref[...]
    h = (y * (1.0 + m[4:5]) + m[3:4]).astype(BF16)
    acc = None
    for c in range(D_FF // FFN_CHUNK):
        cols = slice(c * FFN_CHUNK, (c + 1) * FFN_CHUNK)
        a = _dot(h, w1_ref[:, cols])
        g = _dot(h, w3_ref[:, cols])
        part = _dot((_silu(a) * g).astype(BF16), w2_ref[cols, :])
        acc = part if acc is None else acc + part
    out_ref[...] = _layer_norm(ALPHA * y + m[5:6] * acc, g_ref[...], b_ref[...])


def _ffn_ln(y, mods, w1, w3, w2, ln_g, ln_b, wa_f32, wb_f32):
    wa_spec, wa_shape = _cast_specs(wa_f32.shape, FFN_CAST_STEPS)
    wb_spec, wb_shape = _cast_specs(wb_f32.shape, FFN_CAST_STEPS)
    rows = _row_spec(D, FFN_TM)
    return pl.pallas_call(
        _ffn_ln_kernel,
        out_shape=[jax.ShapeDtypeStruct((T, D), F32), wa_shape, wb_shape],
        grid=(T // FFN_TM,),
        in_specs=[rows, _mod_spec(FFN_TM), _const_spec((D, D_FF), True), _const_spec((D, D_FF), True),
                  _const_spec((D_FF, D), True), _const_spec((1, D)), _const_spec((1, D)), wa_spec, wb_spec],
        out_specs=[rows, wa_spec, wb_spec],
        compiler_params=_params("arbitrary"),
        name="ffn_ln",
    )(y, mods, w1, w3, w2, ln_g.reshape(1, D), ln_b.reshape(1, D), wa_f32, wb_f32)


def _l1_in_kernel(y_ref, mod_ref, w_ref, q_o, k_o, v_o, kh_o, vh_o):
    i = pl.program_id(0)
    m = mod_ref[0]
    h = (y_ref[...] * (1.0 + m[1:2]) + m[0:1]).astype(BF16)
    z = _dot(h, w_ref[...])
    k = z[:, D:2 * D]
    v = z[:, 2 * D:3 * D]
    q_o[...] = (z[:, 0:D] * HD_QMUL).astype(BF16)
    k_o[...] = k.astype(BF16)
    v_o[...] = v.astype(BF16)

    @pl.when(i < NP_TILES)
    def _():
        for bb in range(TM // SEQ):
            rows = slice(bb * SEQ, (bb + 1) * SEQ)
            for j in range(NA_HEADS // 2):
                cols = slice(j * LANES, (j + 1) * LANES)
                kt = k[rows, cols].T
                vt = v[rows, cols].T
                for hh in range(2):
                    drows = slice(hh * HEAD_DIM, (hh + 1) * HEAD_DIM)
                    kh_o[bb, 2 * j + hh] = kt[drows]
                    vh_o[bb, 2 * j + hh] = vt[drows]


def _l1_in(y, mods, w_in):
    heads = pl.BlockSpec((TM // SEQ, NA_HEADS, HEAD_DIM, SEQ), lambda i: (jnp.minimum(i, NP_TILES - 1), 0, 0, 0))
    return pl.pallas_call(
        _l1_in_kernel,
        out_shape=[jax.ShapeDtypeStruct((T, D), BF16)] * 3
        + [jax.ShapeDtypeStruct((BATCH, NA_HEADS, HEAD_DIM, SEQ), F32)] * 2,
        grid=(NT,),
        in_specs=[_row_spec(D), _mod_spec(), _const_spec((D, 3 * D))],
        out_specs=[_row_spec(D)] * 3 + [heads, heads],
        compiler_params=_params("arbitrary"),
        name="l1_in_proj",
    )(y, mods, w_in)


def _l1_attn_prompt_kernel(q_ref, k_ref, v_ref, o_ref):
    for j in range(NA_HEADS // 2):
        cols = slice(128 * j, 128 * (j + 1))
        o = _head_pair(q_ref[:, cols], [k_ref[:, cols].astype(BF16)], [v_ref[:, cols].astype(BF16)])
        o_ref[:, cols] = o.astype(BF16)


def _l1_attn_prompt(q, k, v):
    spec = pl.BlockSpec((SEQ, D), lambda b: (b, 0))
    return pl.pallas_call(
        _l1_attn_prompt_kernel,
        out_shape=jax.ShapeDtypeStruct((T_P, D), BF16),
        grid=(BATCH,),
        in_specs=[spec, spec, spec],
        out_specs=spec,
        compiler_params=_params("arbitrary"),
        name="l1_attn_prompt",
    )(q, k, v)


NA_ROWS = DEC_SEQ // GRID_W
NA_TILE_ROWS = 4
NA_TQ = NA_TILE_ROWS * GRID_W
NA_SPAN = NA_WIN_ROWS + NA_TILE_ROWS
NA_DR = 2 * NA_WIN_ROWS - 1
COL_SPAN = 2 * GRID_W - 1


def _na_span_start(t):
    first = max(0, min(t * NA_TILE_ROWS - NA_WIN_ROWS // 2, NA_ROWS - NA_WIN_ROWS))
    return min(first, NA_ROWS - NA_SPAN)


def _l1_attn_sample_kernel(q_ref, k_ref, v_ref, kc_ref, vc_ref, ext_ref, o_ref):
    kc = kc_ref[0].astype(BF16)
    vc = vc_ref[0].astype(BF16)
    lo = _lane() < GRID_W
    qcol = lax.broadcasted_iota(jnp.int32, (GRID_W, LANES), 0)
    kcol = lax.broadcasted_iota(jnp.int32, (GRID_W, LANES), 1) % GRID_W
    cs = jnp.clip(qcol - NA_WIN_COLS // 2, 0, GRID_W - NA_WIN_COLS)
    col_ok = jnp.logical_and(kcol >= cs, kcol < cs + NA_WIN_COLS)
    neg = jnp.full((GRID_W, LANES), NEG, F32)

    def bias_tile(hh, d, half):
        x = jnp.broadcast_to(ext_ref[hh, d:d + 1, :], (GRID_W, LANES))
        shift = (LANES - COL_SPAN // 2 + GRID_W * half) % LANES
        return jnp.where(col_ok, pltpu.roll(x, shift, 1, stride=1, stride_axis=0), neg)

    tiles = [[[bias_tile(hh, d, half) for half in range(2)] for d in range(NA_DR)] for hh in range(2)]

    for t in range(NA_ROWS // NA_TILE_ROWS):
        ws = _na_span_start(t)
        keys = slice(ws * GRID_W, (ws + NA_SPAN) * GRID_W)
        kw = k_ref[keys, :].astype(BF16)
        vw = v_ref[keys, :].astype(BF16)
        bias = []
        for hh in range(2):
            rows = []
            for rr in range(NA_TILE_ROWS):
                r = t * NA_TILE_ROWS + rr
                rs = max(0, min(r - NA_WIN_ROWS // 2, NA_ROWS - NA_WIN_ROWS))
                blocks = []
                for u in range(NA_SPAN // 2):
                    halves = []
                    for half in range(2):
                        kr = ws + 2 * u + half
                        ok = rs <= kr < rs + NA_WIN_ROWS
                        halves.append(tiles[hh][kr - r + NA_WIN_ROWS - 1][half] if ok else neg)
                    blocks.append(jnp.where(lo, halves[0], halves[1]))
                rows.append(jnp.concatenate(blocks, axis=1))
            bias.append(jnp.concatenate(rows, axis=0))
        qrows = slice(t * NA_TQ, (t + 1) * NA_TQ)
        o = _head_pair(q_ref[qrows, :], [kw, kc], [vw, vc], biases=[bias, None])
        o_ref[qrows, :] = o.astype(BF16)


def _l1_attn_sample(q, k, v, kc, vc, ext):
    lat = pl.BlockSpec((DEC_SEQ, LANES), lambda j, b: (T_P // DEC_SEQ + b, j))
    ctx = pl.BlockSpec((1, PAST, LANES), lambda j, b: (b, 0, j))
    return pl.pallas_call(
        _l1_attn_sample_kernel,
        out_shape=jax.ShapeDtypeStruct((T_S, D), BF16),
        grid=(NA_HEADS // 2, DEC_BATCH),
        in_specs=[lat, lat, lat, ctx, ctx, pl.BlockSpec((2, NA_DR, LANES), lambda j, b: (j, 0, 0))],
        out_specs=pl.BlockSpec((DEC_SEQ, LANES), lambda j, b: (b, j)),
        compiler_params=_params("arbitrary", "arbitrary"),
        name="l1_attn_sample",
    )(q, k, v, kc, vc, ext)


def _na_bias_rows(rel_bias):
    rb = rel_bias.astype(F32) * LOG2E
    n_lo = GRID_W - 1 - (NA_WIN_COLS - 1)
    n_hi = LANES - n_lo - rb.shape[-1]
    return jnp.concatenate([jnp.repeat(rb[..., :1], n_lo, axis=-1), rb,
                            jnp.repeat(rb[..., -1:], n_hi, axis=-1)], axis=-1)


def _router_kernel(y_ref, mod_ref, rw_ref, rb_ref, h_o, meta_o, cnt_o, carry_ref):
    i = pl.program_id(0)

    @pl.when(i == 0)
    def _():
        carry_ref[...] = jnp.zeros_like(carry_ref)

    m = mod_ref[0]
    h = y_ref[...] * (1.0 + m[4:5]) + m[3:4]
    _store_rows_tiled(h_o, h)
    logits = jnp.dot(h, rw_ref[...], preferred_element_type=F32, precision=lax.Precision.HIGHEST) + rb_ref[...]
    lane = lax.broadcasted_iota(jnp.int32, (TM, LANES), 1).astype(F32)
    m1 = jnp.max(logits, axis=1, keepdims=True)
    i1 = jnp.min(jnp.where(logits == m1, lane, float(LANES)), axis=1, keepdims=True)
    sel1 = lane == i1
    rest = jnp.where(sel1, -jnp.inf, logits)
    m2 = jnp.max(rest, axis=1, keepdims=True)
    i2 = jnp.min(jnp.where(rest == m2, lane, float(LANES)), axis=1, keepdims=True)
    sel2 = lane == i2
    e2 = jnp.exp(m2 - m1)
    w1 = 1.0 / (1.0 + e2)
    w2 = e2 / (1.0 + e2)
    sel = jnp.logical_or(sel1, sel2)
    rr = lax.broadcasted_iota(jnp.int32, (TM, TM), 0)
    cc = lax.broadcasted_iota(jnp.int32, (TM, TM), 1)
    tri = jnp.where(cc < rr, 1.0, 0.0).astype(BF16)
    ahead = _dot(tri, jnp.where(sel, 1.0, 0.0).astype(BF16)) + carry_ref[...]
    r1 = jnp.sum(jnp.where(sel1, ahead, 0.0), axis=1, keepdims=True)
    r2 = jnp.sum(jnp.where(sel2, ahead, 0.0), axis=1, keepdims=True)
    meta = jnp.where(lane == 0, i1, 0.0)
    meta = jnp.where(lane == 1, i2, meta)
    meta = jnp.where(lane == 2, r1, meta)
    meta = jnp.where(lane == 3, r2, meta)
    meta = jnp.where(lane == 4, w1, meta)
    meta = jnp.where(lane == 5, w2, meta)
    meta_o[...] = meta
    carry_ref[...] = carry_ref[...] + jnp.sum(jnp.where(sel, 1.0, 0.0), axis=0, keepdims=True)
    cnt_o[...] = carry_ref[...]


def _router(y, mods, router_w, router_b):
    rw = jnp.zeros((D, LANES), F32).at[:, :N_EXPERTS].set(router_w)
    rb = jnp.full((1, LANES), NEG, F32).at[0, :N_EXPERTS].set(router_b)
    return pl.pallas_call(
        _router_kernel,
        out_shape=[jax.ShapeDtypeStruct((T * ROW_TILE, LANES), F32),
                   jax.ShapeDtypeStruct((T, LANES), F32),
                   jax.ShapeDtypeStruct((1, LANES), F32)],
        grid=(NT,),
        in_specs=[_row_spec(D), _mod_spec(), _const_spec((D, LANES)), _const_spec((1, LANES))],
        out_specs=[_row_spec(LANES, TM * ROW_TILE), _row_spec(LANES), _const_spec((1, LANES))],
        scratch_shapes=[pltpu.VMEM((1, LANES), F32)],
        compiler_params=_params("arbitrary"),
        name="moe_router",
    )(y, mods, rw, rb)


DMA_UNROLL = 8


def _row_copy(src_ref, src_row, dst_ref, dst_row, sem):
    return pltpu.make_async_copy(src_ref.at[pl.ds(pl.multiple_of(src_row * ROW_TILE, ROW_TILE), ROW_TILE)],
                                 dst_ref.at[pl.ds(pl.multiple_of(dst_row * ROW_TILE, ROW_TILE), ROW_TILE)], sem)


def _group_tile_copy(src_ref, dst_ref, tile, sem):
    start = pl.multiple_of(tile * (TG * ROW_TILE), TG * ROW_TILE)
    return pltpu.make_async_copy(src_ref, dst_ref.at[pl.ds(start, TG * ROW_TILE)], sem)


def _dispatch_kernel(pos_ref, last_ref, na_ref, h_ref, wf_ref, xg_ref, wb_ref, zero_ref, sem, zsem):
    i = pl.program_id(0)
    _cast_block(i, NP_TILES, wf_ref, wb_ref)

    @pl.when(i == 0)
    def _():
        zero_ref[...] = jnp.zeros_like(zero_ref)
        for e in range(N_EXPERTS):
            @pl.when(last_ref[e] >= 0)
            def _():
                _group_tile_copy(zero_ref, xg_ref, last_ref[e], zsem).start()

        def start_unused(g, carry):
            _group_tile_copy(zero_ref, xg_ref, g, zsem).start()
            return carry

        def wait_one(g, carry):
            _group_tile_copy(zero_ref, xg_ref, 0, zsem).wait()
            return carry

        lax.fori_loop(na_ref[0], NG, start_unused, 0)
        lax.fori_loop(0, na_ref[1], wait_one, 0)

    def issue(r, carry):
        t = i * TM + r
        _row_copy(h_ref, r, xg_ref, pos_ref[2 * t], sem).start(priority=0)
        _row_copy(h_ref, r, xg_ref, pos_ref[2 * t + 1], sem).start(priority=1)
        return carry

    lax.fori_loop(0, TM, issue, 0, unroll=DMA_UNROLL)
    for _ in range(2):
        pltpu.make_async_copy(h_ref, xg_ref.at[pl.ds(0, TM * ROW_TILE)], sem).wait()


def _dispatch(pos, last_tile, tile_counts, h_tiled, w_f32):
    w_spec, wb_shape = _cast_specs(w_f32.shape, NP_TILES)
    return pl.pallas_call(
        _dispatch_kernel,
        out_shape=[jax.ShapeDtypeStruct((P_ROWS * ROW_TILE, LANES), F32), wb_shape],
        grid_spec=pltpu.PrefetchScalarGridSpec(
            num_scalar_prefetch=3, grid=(NT,),
            in_specs=[pl.BlockSpec((TM * ROW_TILE, LANES), lambda i, *_: (i, 0)), w_spec],
            out_specs=[pl.BlockSpec(memory_space=pl.ANY), w_spec],
            scratch_shapes=[pltpu.VMEM((TG * ROW_TILE, LANES), F32), pltpu.SemaphoreType.DMA(()),
                            pltpu.SemaphoreType.DMA(())]),
        compiler_params=_params("arbitrary"),
        name="moe_dispatch",
    )(pos, last_tile, tile_counts, h_tiled, w_f32)


def _expert_ffn_kernel(te_ref, na_ref, x_ref, w1_ref, w3_ref, w2_ref, o_ref):
    g = pl.program_id(0)

    @pl.when(g < na_ref[0])
    def _():
        x = _load_rows_tiled(x_ref, TG).astype(BF16)
        acc = None
        for c in range(E_FF // F_CHUNK):
            cols = slice(c * F_CHUNK, (c + 1) * F_CHUNK)
            a = _dot(x, w1_ref[0, :, cols])
            b = _dot(x, w3_ref[0, :, cols])
            part = _dot((_silu(a) * b).astype(BF16), w2_ref[0, cols, :])
            acc = part if acc is None else acc + part
        _store_rows_tiled(o_ref, acc)

    @pl.when(g >= na_ref[0])
    def _():
        o_ref[...] = jnp.zeros_like(o_ref)


def _expert_ffn(tile_expert, n_active, xg, w1, w3, w2):
    rows = pl.BlockSpec((TG * ROW_TILE, LANES), lambda g, te, na: (g, 0))
    w_up = pl.BlockSpec((1, D, E_FF), lambda g, te, na: (te[g], 0, 0))
    w_dn = pl.BlockSpec((1, E_FF, D), lambda g, te, na: (te[g], 0, 0))
    return pl.pallas_call(
        _expert_ffn_kernel,
        out_shape=jax.ShapeDtypeStruct((P_ROWS * ROW_TILE, LANES), F32),
        grid_spec=pltpu.PrefetchScalarGridSpec(
            num_scalar_prefetch=2, grid=(NG,),
            in_specs=[rows, w_up, w_up, w_dn],
            out_specs=rows),
        compiler_params=_params("arbitrary"),
        name="moe_expert_ffn",
    )(tile_expert, n_active, xg, w1, w3, w2)


def _combine_ln_kernel(pos_ref, y_ref, mod_ref, meta_ref, g_ref, b_ref, eo_ref, outp_ref, outs_ref,
                       buf1, buf2, sem):
    i = pl.program_id(0)
    slot = i % 2

    def fetch(tile, into):
        def issue(r, carry):
            t = tile * TM + r
            _row_copy(eo_ref, pos_ref[2 * t], buf1.at[into], r, sem.at[into]).start(priority=0)
            _row_copy(eo_ref, pos_ref[2 * t + 1], buf2.at[into], r, sem.at[into]).start(priority=1)
            return carry

        lax.fori_loop(0, TM, issue, 0, unroll=DMA_UNROLL)

    @pl.when(i == 0)
    def _():
        fetch(0, 0)

    @pl.when(i + 1 < NT)
    def _():
        fetch(i + 1, 1 - slot)

    for buf in (buf1, buf2):
        pltpu.make_async_copy(eo_ref.at[pl.ds(0, TM * ROW_TILE)], buf.at[slot], sem.at[slot]).wait()
    m = mod_ref[0]
    meta = meta_ref[...]
    f = (meta[:, 4:5] * _load_rows_tiled(buf1.at[slot], TM)
         + meta[:, 5:6] * _load_rows_tiled(buf2.at[slot], TM))
    out = _layer_norm(ALPHA * y_ref[...] + m[5:6] * f, g_ref[...], b_ref[...])

    @pl.when(i < NP_TILES)
    def _():
        outp_ref[...] = out

    @pl.when(i >= NP_TILES)
    def _():
        outs_ref[...] = out


def _combine_ln(pos, y, mods, meta, ln_g, ln_b, eo):
    return pl.pallas_call(
        _combine_ln_kernel,
        out_shape=[jax.ShapeDtypeStruct((T_P, D), F32), jax.ShapeDtypeStruct((T_S, D), F32)],
        grid_spec=pltpu.PrefetchScalarGridSpec(
            num_scalar_prefetch=1, grid=(NT,),
            in_specs=[pl.BlockSpec((TM, D), lambda i, *_: (i, 0)),
                      pl.BlockSpec((1, 6, D), lambda i, *_: (_cond_of_tile(i), 0, 0)),
                      pl.BlockSpec((TM, LANES), lambda i, *_: (i, 0)),
                      pl.BlockSpec((1, D), lambda i, *_: (0, 0)),
                      pl.BlockSpec((1, D), lambda i, *_: (0, 0)),
                      pl.BlockSpec(memory_space=pl.ANY)],
            out_specs=[_prompt_tile_spec(D), _sample_tile_spec(D)],
            scratch_shapes=[pltpu.VMEM((2, TM * ROW_TILE, LANES), F32),
                            pltpu.VMEM((2, TM * ROW_TILE, LANES), F32),
                            pltpu.SemaphoreType.DMA((2,))]),
        compiler_params=_params("arbitrary"),
        name="moe_combine_ln",
    )(pos, y, mods, meta, ln_g.reshape(1, D), ln_b.reshape(1, D), eo)


def _moe_ln(y, mods, router_w, router_b, w1, w3, w2_f32, ln_g, ln_b):
    h_tiled, meta, counts = _router(y, mods, router_w, router_b)
    cnt = counts[0, :N_EXPERTS].astype(jnp.int32)
    tiles = (cnt + TG - 1) // TG
    tile_end = jnp.cumsum(tiles)
    offs = (tile_end - tiles) * TG
    expert = meta[:, 0:2].astype(jnp.int32)
    chosen = expert[:, :, None] == jnp.arange(N_EXPERTS)[None, None, :]
    pos = jnp.sum(jnp.where(chosen, offs[None, None, :], 0), axis=-1) + meta[:, 2:4].astype(jnp.int32)
    pos = pos.reshape(2 * T)
    tile_expert = jnp.sum((jnp.arange(NG)[:, None] >= tile_end[None, :]).astype(jnp.int32), axis=1)
    tile_expert = jnp.minimum(tile_expert, N_EXPERTS - 1)
    n_active = tile_end[-1:].astype(jnp.int32)
    last_tile = jnp.where(tiles > 0, tile_end - 1, -1).astype(jnp.int32)
    n_zeroed = jnp.sum((tiles > 0).astype(jnp.int32)) + NG - tile_end[-1]
    tile_counts = jnp.stack([tile_end[-1], n_zeroed]).astype(jnp.int32)
    xg, w2 = _dispatch(pos, last_tile, tile_counts, h_tiled, w2_f32.reshape(N_EXPERTS * E_FF, D))
    eo = _expert_ffn(tile_expert, n_active, xg, w1, w3, w2.reshape(N_EXPERTS, E_FF, D))
    return _combine_ln(pos, y, mods, meta, ln_g, ln_b, eo)


def _l0_weight_layouts(w_in, w_q_up, w_kv_up):
    a, b, c = MLA_Q_LORA, MLA_Q_LORA + MLA_KV_LORA, MLA_Q_LORA + MLA_KV_LORA + MLA_ROPE
    k_rope = w_in[:, b:c]
    w_in_r = jnp.concatenate([w_in[:, :b], w_in[:, c:], k_rope, k_rope, k_rope, k_rope], axis=1)
    wq = w_q_up.reshape(MLA_Q_LORA, MLA_HEADS, MLA_NOPE + MLA_ROPE)
    w_q_up_r = jnp.concatenate([wq[:, :, :MLA_NOPE].reshape(MLA_Q_LORA, -1),
                                wq[:, :, MLA_NOPE:].reshape(MLA_Q_LORA, -1)], axis=1)
    wkv = w_kv_up.reshape(MLA_KV_LORA, MLA_HEADS, MLA_NOPE + MLA_V)
    w_kv_up_r = jnp.concatenate([wkv[:, :, :MLA_NOPE].reshape(MLA_KV_LORA, -1),
                                 wkv[:, :, MLA_NOPE:].reshape(MLA_KV_LORA, -1)], axis=1)
    return w_in_r.astype(BF16), w_q_up_r.astype(BF16), w_kv_up_r.astype(BF16)


def kernel(x_prompt, x_sample, cache_l0_mla_ckv, cache_l0_mla_krope, cache_l0_swa_k, cache_l0_swa_v,
           cache_l1_na_k, cache_l1_na_v, c, c_ctx,
           l0_ada_w, l0_ada_b, l0_w_in, l0_mla_q_norm, l0_mla_w_q_up, l0_mla_kv_norm, l0_mla_w_kv_up,
           l0_swa_sink, l0_w_out, l0_ln1_g, l0_ln1_b, l0_ffn_w1, l0_ffn_w3, l0_ffn_w2, l0_ln2_g, l0_ln2_b,
           l1_ada_w, l1_ada_b, l1_w_in, l1_na_rel_bias, l1_w_out, l1_ln1_g, l1_ln1_b,
           l1_moe_router_w, l1_moe_router_b, l1_moe_w1, l1_moe_w3, l1_moe_w2, l1_ln2_g, l1_ln2_b):
    cond = jnp.concatenate([c_ctx[None, :], c, jnp.zeros((8 - N_COND, D), F32)], axis=0)

    mods = _adaln(cond, l0_ada_w, l0_ada_b)
    w_in_r, w_q_up_r, w_kv_up_r = _l0_weight_layouts(l0_w_in, l0_mla_w_q_up, l0_mla_w_kv_up)
    y, qn, qr, kn, vm, kr4, sq, sk, sv, ckv, krope_t, sk_t, sv_t = _l0_in(
        x_prompt.reshape(T_P, D), x_sample.reshape(T_S, D), mods, w_in_r, l0_mla_q_norm, w_q_up_r,
        l0_mla_kv_norm, w_kv_up_r, _rope_tables(MLA_ROPE, 256), _rope_tables(HEAD_DIM, 512))
    knc, vc = _kv_up(cache_l0_mla_ckv.reshape(DEC_BATCH * PAST, MLA_KV_LORA), w_kv_up_r)
    kr4c = jnp.tile(cache_l0_mla_krope.reshape(DEC_BATCH * PAST, MLA_ROPE), (1, 4))
    skc = cache_l0_swa_k.reshape(DEC_BATCH * PAST, SWA_KV_HEADS * HEAD_DIM)
    svc = cache_l0_swa_v.reshape(DEC_BATCH * PAST, SWA_KV_HEADS * HEAD_DIM)
    o_p, ffn_w1, ffn_w3, ffn_w2 = _l0_attn_prompt(l0_swa_sink, qn, qr, kn, vm, kr4, sq, sk, sv,
                                                  [l0_ffn_w1, l0_ffn_w3, l0_ffn_w2])
    o_s = _l0_attn_sample(l0_swa_sink, qn, qr, kn, vm, kr4, sq, sk, sv, knc, vc, kr4c, skc, svc)
    y = _out_ln(o_p, o_s, l0_w_out.astype(BF16), y, mods, l0_ln1_g, l0_ln1_b)
    y, moe_w1, moe_w3 = _ffn_ln(y, mods, ffn_w1, ffn_w3, ffn_w2, l0_ln2_g, l0_ln2_b,
                                l1_moe_w1.reshape(N_EXPERTS * D, E_FF), l1_moe_w3.reshape(N_EXPERTS * D, E_FF))
    new_ckv = ckv.reshape(BATCH, SEQ, MLA_KV_LORA)
    new_krope = jnp.transpose(krope_t, (0, 2, 1))
    new_sk = jnp.transpose(sk_t, (0, 3, 1, 2))
    new_sv = jnp.transpose(sv_t, (0, 3, 1, 2))

    mods = _adaln(cond, l1_ada_w, l1_ada_b)
    q, k, v, k_heads, v_heads = _l1_in(y, mods, l1_w_in.astype(BF16))
    o_p = _l1_attn_prompt(q, k, v)
    o_s = _l1_attn_sample(q, k, v, cache_l1_na_k.reshape(DEC_BATCH, PAST, D),
                          cache_l1_na_v.reshape(DEC_BATCH, PAST, D), _na_bias_rows(l1_na_rel_bias))
    y = _out_ln(o_p, o_s, l1_w_out.astype(BF16), y, mods, l1_ln1_g, l1_ln1_b)
    y_p, y_s = _moe_ln(y, mods, l1_moe_router_w, l1_moe_router_b, moe_w1.reshape(N_EXPERTS, D, E_FF),
                       moe_w3.reshape(N_EXPERTS, D, E_FF), l1_moe_w2, l1_ln2_g, l1_ln2_b)
    new_k = jnp.transpose(k_heads, (0, 3, 1, 2))
    new_v = jnp.transpose(v_heads, (0, 3, 1, 2))

    return (y_p.reshape(BATCH, SEQ, D), y_s.reshape(DEC_BATCH, DEC_SEQ, D),
            new_ckv, new_krope, new_sk, new_sv, new_k, new_v)
```

```python
import functools

import jax
import jax.numpy as jnp
import numpy as np
from jax import lax
from jax.experimental import pallas as pl
from jax.experimental.pallas import tpu as pltpu

F32 = jnp.float32
BF16 = jnp.bfloat16

D = 1024
BATCH, SEQ = 32, 256
DEC_BATCH, DEC_SEQ = 2, 1024
PAST = 256
GRID_W = 64
T_P = BATCH * SEQ
T_S = DEC_BATCH * DEC_SEQ
T = T_P + T_S
N_COND = 1 + DEC_BATCH

MLA_HEADS, MLA_Q_LORA, MLA_KV_LORA, MLA_NOPE, MLA_ROPE, MLA_V = 8, 384, 256, 64, 32, 64
SWA_HEADS, SWA_KV_HEADS, SWA_WINDOW, HEAD_DIM = 8, 2, 128, 64
NA_HEADS, NA_WIN_ROWS, NA_WIN_COLS = 16, 8, 16
D_FF, N_EXPERTS, E_FF = 2816, 8, 3584
ROPE_THETA = 10000.0
LN_EPS, RMS_EPS = 1e-5, 1e-6
NEG = -1e30
ALPHA = 4.0 ** 0.25

LANES = 128
TM = 512
NT = T // TM
NP_TILES = T_P // TM
TILES_PER_SAMPLE = DEC_SEQ // TM
TG = 256
P_ROWS = 2 * T + N_EXPERTS * TG
NG = P_ROWS // TG
MXU_N = 256
F_CHUNK = 1792
assert F_CHUNK % MXU_N == 0 and E_FF % F_CHUNK == 0
VMEM_LIMIT = 56 * 1024 * 1024


def _params(*sem):
    return pltpu.CompilerParams(dimension_semantics=sem, vmem_limit_bytes=VMEM_LIMIT)


def _const_spec(shape, single_buffer=False):
    if single_buffer:
        return pl.BlockSpec(shape, lambda *_: (0,) * len(shape), pipeline_mode=pl.Buffered(1))
    return pl.BlockSpec(shape, lambda *_: (0,) * len(shape))


def _cast_specs(shape, steps):
    rows, width = shape
    spec = pl.BlockSpec((rows // steps, width), lambda i, *_: (jnp.minimum(i, steps - 1), 0))
    return spec, jax.ShapeDtypeStruct(shape, BF16)


def _cast_block(i, steps, src_ref, dst_ref):
    @pl.when(i < steps)
    def _():
        dst_ref[...] = src_ref[...].astype(BF16)


ROW_TILE = D // LANES


def _store_rows_tiled(ref, x):
    n = x.shape[0]
    for k in range(ROW_TILE):
        ref[pl.ds(k, n, stride=ROW_TILE), :] = x[:, k * LANES:(k + 1) * LANES]


def _load_rows_tiled(ref, n):
    return jnp.concatenate([ref[pl.ds(k, n, stride=ROW_TILE), :] for k in range(ROW_TILE)], axis=1)


def _cond_of_tile(i, rows=TM):
    per_sample = DEC_SEQ // rows
    return jnp.maximum((i - (T_P // rows - per_sample)) // per_sample, 0)


def _mod_spec(rows=TM):
    return pl.BlockSpec((1, 6, D), lambda i: (_cond_of_tile(i, rows), 0, 0))


def _row_spec(width, rows=TM):
    return pl.BlockSpec((rows, width), lambda i: (i, 0))


def _dot(a, b):
    return jnp.dot(a, b, preferred_element_type=F32)


def _dot_nt(a, b):
    return lax.dot_general(a, b, (((1,), (1,)), ((), ())), preferred_element_type=F32)


def _layer_norm(r, g, b):
    mu = jnp.mean(r, axis=-1, keepdims=True)
    d = r - mu
    var = jnp.mean(d * d, axis=-1, keepdims=True)
    return d * lax.rsqrt(var + LN_EPS) * g + b


def _rms_norm(x, g):
    return x * lax.rsqrt(jnp.mean(x * x, axis=-1, keepdims=True) + RMS_EPS) * g


def _silu(x):
    return x * jax.nn.sigmoid(x)


def _ada_kernel(c_ref, w_ref, b_ref, o_ref):
    s = _silu(c_ref[...]).astype(BF16)
    o_ref[...] = _dot(s, w_ref[...].astype(BF16)) + b_ref[...]


def _adaln(cond, ada_w, ada_b):
    nb = 1536
    out = pl.pallas_call(
        _ada_kernel,
        out_shape=jax.ShapeDtypeStruct((8, 6 * D), F32),
        grid=(6 * D // nb,),
        in_specs=[_const_spec((8, D)), pl.BlockSpec((D, nb), lambda j: (0, j)),
                  pl.BlockSpec((1, nb), lambda j: (0, j))],
        out_specs=pl.BlockSpec((8, nb), lambda j: (0, j)),
        compiler_params=_params("arbitrary"),
        name="adaln",
    )(cond, ada_w, ada_b.reshape(1, 6 * D))
    return out[:N_COND].reshape(N_COND, 6, D)


def _rope_tables(head_dim, width):
    half = head_dim // 2
    nf = half // 2
    lane = np.arange(width)
    d = lane % head_dim
    dd = d % half
    f = dd % nf
    inv = np.float32(ROPE_THETA) ** (-f.astype(np.float32) / np.float32(nf))
    t = np.arange(DEC_SEQ)
    pos = np.where((d // half)[None, :] == 0, (t // GRID_W)[:, None], (t % GRID_W)[:, None])
    ang = pos.astype(np.float32) * inv[None, :].astype(np.float32)
    cos, sin = np.cos(ang), np.sin(ang)
    first = (dd < nf)[None, :]
    zero = np.float32(0.0)
    return (jnp.asarray(cos, F32), jnp.asarray(np.where(first, -sin, zero), F32),
            jnp.asarray(np.where(first, zero, sin), F32))


def _rope(x, cos, sin_up, sin_dn, nf):
    w = x.shape[-1]
    return x * cos + pltpu.roll(x, w - nf, 1) * sin_up + pltpu.roll(x, nf, 1) * sin_dn


L0_COLS = MLA_Q_LORA + MLA_KV_LORA + 512 + 128 + 128 + 128


def _l0_in_kernel(xp_ref, xs_ref, mod_ref, win_ref, qn_ref, wq_ref, kvn_ref, wkv_ref,
                  c8_ref, su8_ref, sd8_ref, c16_ref, su16_ref, sd16_ref,
                  y_o, qnope_o, qrope_o, knope_o, vmla_o, kr4_o, sq_o, sk_o, sv_o,
                  ckv_o, krt_o, skt_o, svt_o):
    i = pl.program_id(0)
    m = mod_ref[0]
    x = jnp.where(i < NP_TILES, xp_ref[...], xs_ref[...])
    y_o[...] = x
    h = (x * (1.0 + m[1:2]) + m[0:1]).astype(BF16)
    z = _dot(h, win_ref[...])
    q_lat = z[:, 0:384]
    kv_lat = z[:, 384:640]
    sq = z[:, 640:1152]
    sk = z[:, 1152:1280]
    sv = z[:, 1280:1408]
    kr4 = z[:, 1408:1536]
    q = _dot(_rms_norm(q_lat, qn_ref[...]).astype(BF16), wq_ref[...])
    c_kv = _rms_norm(kv_lat, kvn_ref[...])
    kv = _dot(c_kv.astype(BF16), wkv_ref[...])
    q = q * MLA_QMUL
    sq = sq * HD_QMUL
    qnope_o[...] = q[:, 0:512].astype(BF16)
    knope_o[...] = kv[:, 0:512].astype(BF16)
    vmla_o[...] = kv[:, 512:1024].astype(BF16)
    sv_o[...] = sv
    q_rope = q[:, 512:768]

    @pl.when(i < NP_TILES)
    def _():
        qrope_o[...] = q_rope.astype(BF16)
        kr4_o[...] = kr4
        sq_o[...] = sq.astype(BF16)
        sk_o[...] = sk
        ckv_o[...] = c_kv
        for bb in range(TM // SEQ):
            rows = slice(bb * SEQ, (bb + 1) * SEQ)
            krt_o[bb] = kr4[rows].T[:MLA_ROPE]
            skt = sk[rows].T
            svt = sv[rows].T
            for g in range(SWA_KV_HEADS):
                skt_o[bb, g] = skt[g * HEAD_DIM:(g + 1) * HEAD_DIM]
                svt_o[bb, g] = svt[g * HEAD_DIM:(g + 1) * HEAD_DIM]

    @pl.when(i >= NP_TILES)
    def _():
        c8, su8, sd8 = c8_ref[...], su8_ref[...], sd8_ref[...]
        c16, su16, sd16 = c16_ref[...], su16_ref[...], sd16_ref[...]
        qrope_o[...] = _rope(q_rope, c8, su8, sd8, 8).astype(BF16)
        kr4_o[...] = _rope(kr4, c8[:, :128], su8[:, :128], sd8[:, :128], 8)
        sq_o[...] = _rope(sq, c16, su16, sd16, 16).astype(BF16)
        sk_o[...] = _rope(sk, c16[:, :128], su16[:, :128], sd16[:, :128], 16)


def _prompt_tile_spec(width):
    return pl.BlockSpec((TM, width), lambda i, *_: (jnp.minimum(i, NP_TILES - 1), 0))


def _sample_tile_spec(width):
    return pl.BlockSpec((TM, width), lambda i, *_: (jnp.maximum(i - NP_TILES, 0), 0))


def _l0_in(x_prompt, x_sample, mods, w_in_r, q_norm, w_q_up_r, kv_norm, w_kv_up_r, tabs8, tabs16):
    def tab_spec(width):
        return pl.BlockSpec((TM, width), lambda i: (jnp.maximum(i - NP_TILES, 0) % TILES_PER_SAMPLE, 0))

    out_shape = [
        jax.ShapeDtypeStruct((T, D), F32),
        jax.ShapeDtypeStruct((T, 512), BF16),
        jax.ShapeDtypeStruct((T, 256), BF16),
        jax.ShapeDtypeStruct((T, 512), BF16),
        jax.ShapeDtypeStruct((T, 512), BF16),
        jax.ShapeDtypeStruct((T, 128), F32),
        jax.ShapeDtypeStruct((T, 512), BF16),
        jax.ShapeDtypeStruct((T, 128), F32),
        jax.ShapeDtypeStruct((T, 128), F32),
    ]
    per_tile = TM // SEQ
    prompt_block = lambda *dims: pl.BlockSpec((per_tile,) + dims,
                                              lambda i: (jnp.minimum(i, NP_TILES - 1),) + (0,) * len(dims))
    cache_shape = [
        jax.ShapeDtypeStruct((T_P, MLA_KV_LORA), F32),
        jax.ShapeDtypeStruct((BATCH, MLA_ROPE, SEQ), F32),
        jax.ShapeDtypeStruct((BATCH, SWA_KV_HEADS, HEAD_DIM, SEQ), F32),
        jax.ShapeDtypeStruct((BATCH, SWA_KV_HEADS, HEAD_DIM, SEQ), F32),
    ]
    cache_specs = [_prompt_tile_spec(MLA_KV_LORA), prompt_block(MLA_ROPE, SEQ),
                   prompt_block(SWA_KV_HEADS, HEAD_DIM, SEQ), prompt_block(SWA_KV_HEADS, HEAD_DIM, SEQ)]
    return pl.pallas_call(
        _l0_in_kernel,
        out_shape=out_shape + cache_shape,
        grid=(NT,),
        in_specs=[_prompt_tile_spec(D), _sample_tile_spec(D), _mod_spec(), _const_spec((D, L0_COLS)),
                  _const_spec((1, MLA_Q_LORA)), _const_spec((MLA_Q_LORA, 768)),
                  _const_spec((1, MLA_KV_LORA)), _const_spec((MLA_KV_LORA, 1024)),
                  tab_spec(256), tab_spec(256), tab_spec(256),
                  tab_spec(512), tab_spec(512), tab_spec(512)],
        out_specs=[_row_spec(s.shape[1]) for s in out_shape] + cache_specs,
        compiler_params=_params("arbitrary"),
        name="l0_in_proj",
    )(x_prompt, x_sample, mods, w_in_r, q_norm.reshape(1, -1), w_q_up_r, kv_norm.reshape(1, -1), w_kv_up_r,
      *tabs8, *tabs16)


def _kv_up_kernel(c_ref, w_ref, k_o, v_o):
    kv = _dot(c_ref[...].astype(BF16), w_ref[...])
    k_o[...] = kv[:, 0:512].astype(BF16)
    v_o[...] = kv[:, 512:1024].astype(BF16)


def _kv_up(ckv, w_kv_up_r):
    n = ckv.shape[0]
    return pl.pallas_call(
        _kv_up_kernel,
        out_shape=[jax.ShapeDtypeStruct((n, 512), BF16)] * 2,
        grid=(1,),
        in_specs=[_const_spec((n, MLA_KV_LORA)), _const_spec((MLA_KV_LORA, 1024))],
        out_specs=[_const_spec((n, 512))] * 2,
        compiler_params=_params("arbitrary"),
        name="l0_ctx_kv_up",
    )(ckv, w_kv_up_r)


def _lane():
    return lax.broadcasted_iota(jnp.int32, (1, LANES), 1)


def _attend(scores, values, extra_logit=None):
    m = jnp.max(scores[0], axis=1, keepdims=True)
    for s in scores[1:]:
        m = jnp.maximum(m, jnp.max(s, axis=1, keepdims=True))
    if extra_logit is not None:
        m = jnp.maximum(m, extra_logit)
    den = None
    acc = None
    for s, v in zip(scores, values):
        e = jnp.exp2(s - m)
        d = jnp.sum(e, axis=1, keepdims=True)
        a = _dot(e.astype(BF16), v)
        den = d if den is None else den + d
        acc = a if acc is None else acc + a
    if extra_logit is not None:
        den = den + jnp.exp2(extra_logit - m)
    return acc / den


def _head_pair(q2, keys, values, masks=None, biases=None, sinks=None, q_extra=None):
    lo = _lane() < 64
    zero = jnp.zeros_like(q2)
    outs = []
    for hh in range(2):
        qm = jnp.where(lo if hh == 0 else jnp.logical_not(lo), q2, zero)
        if q_extra is not None:
            qm = jnp.concatenate([qm, q_extra[hh]], axis=1)
        scores = []
        for n, k in enumerate(keys):
            s = _dot_nt(qm, k)
            if biases is not None and biases[n] is not None:
                s = s + biases[n][hh]
            if masks is not None and masks[n] is not None:
                s = jnp.where(masks[n], s, NEG)
            scores.append(s)
        outs.append(_attend(scores, values, None if sinks is None else sinks[hh]))
    return jnp.where(lo, outs[0], outs[1])


def _dup_halves(x):
    lo = _lane() < 64
    sw = pltpu.roll(x, 64, 1)
    return jnp.where(lo, x, sw), jnp.where(lo, sw, x)


LOG2E = 1.4426950408889634
MLA_QMUL = (MLA_NOPE + MLA_ROPE) ** -0.5 * LOG2E
HD_QMUL = HEAD_DIM ** -0.5 * LOG2E


def _mla_pairs(qn_ref, qr_ref, key_sets, o_ref):
    lane = _lane()
    for j in range(MLA_HEADS // 2):
        cols = slice(128 * j, 128 * (j + 1))
        qr = qr_ref[:, 128 * (j // 2):128 * (j // 2 + 1)]
        zero = jnp.zeros_like(qr)
        q_extra = [jnp.where((lane // MLA_ROPE) == ((2 * j + hh) % 4), qr, zero) for hh in range(2)]
        keys = [jnp.concatenate([kn[:, cols], kr4], axis=1) for kn, kr4, _ in key_sets]
        values = [v[:, cols] for _, _, v in key_sets]
        o = _head_pair(qn_ref[:, cols], keys, values, q_extra=q_extra)
        o_ref[:, cols] = o.astype(BF16)


def _swa_pairs(sink_ref, sq_ref, key_sets, masks, o_ref):
    kd = [[a.astype(BF16) for a in _dup_halves(k)] for k, _ in key_sets]
    vd = [[a.astype(BF16) for a in _dup_halves(v)] for _, v in key_sets]
    for g in range(SWA_KV_HEADS):
        for u in range(2):
            c = 2 * g + u
            cols = slice(128 * c, 128 * (c + 1))
            sinks = [sink_ref[2 * c + hh] * LOG2E for hh in range(2)]
            o = _head_pair(sq_ref[:, cols], [k[g] for k in kd], [v[g] for v in vd],
                           masks=masks, sinks=sinks)
            o_ref[:, 512 + 128 * c:512 + 128 * (c + 1)] = o.astype(BF16)


def _cast_steps(rows, max_steps):
    steps = max_steps
    while rows % (16 * steps):
        steps //= 2
    return steps


def _l0_attn_prompt_kernel(cast_steps, sink_ref, qn_ref, qr_ref, kn_ref, v_ref, kr4_ref, sq_ref, sk_ref, sv_ref,
                           wf1_ref, wf2_ref, wf3_ref, o_ref, wb1_ref, wb2_ref, wb3_ref):
    _mla_pairs(qn_ref, qr_ref, [(kn_ref, kr4_ref[...].astype(BF16), v_ref)], o_ref)
    _swa_pairs(sink_ref, sq_ref, [(sk_ref[...], sv_ref[...])], None, o_ref)
    for steps, wf_ref, wb_ref in zip(cast_steps, (wf1_ref, wf2_ref, wf3_ref), (wb1_ref, wb2_ref, wb3_ref)):
        _cast_block(pl.program_id(0), steps, wf_ref, wb_ref)


def _l0_attn_prompt(sink, qn, qr, kn, vm, kr4, sq, sk, sv, weights_f32):
    spec = lambda w: pl.BlockSpec((SEQ, w), lambda b, *_: (b, 0))
    cast_steps = tuple(_cast_steps(w.shape[0], BATCH) for w in weights_f32)
    casts = [_cast_specs(w.shape, s) for w, s in zip(weights_f32, cast_steps)]
    return pl.pallas_call(
        functools.partial(_l0_attn_prompt_kernel, cast_steps),
        out_shape=[jax.ShapeDtypeStruct((T_P, D), BF16)] + [c[1] for c in casts],
        grid_spec=pltpu.PrefetchScalarGridSpec(
            num_scalar_prefetch=1, grid=(BATCH,),
            in_specs=[spec(512), spec(256), spec(512), spec(512), spec(128), spec(512), spec(128), spec(128)]
            + [c[0] for c in casts],
            out_specs=[spec(D)] + [c[0] for c in casts]),
        compiler_params=_params("arbitrary"),
        name="l0_attn_prompt",
    )(sink, qn, qr, kn, vm, kr4, sq, sk, sv, *weights_f32)


TQ_S = 256


def _l0_attn_sample_kernel(sink_ref, qn_ref, qr_ref, sq_ref, kn_ref, v_ref, kr4_ref, sk_ref, sv_ref,
                           knc_ref, vc_ref, kr4c_ref, skc_ref, svc_ref, o_ref):
    i = pl.program_id(1)
    _mla_pairs(qn_ref, qr_ref,
               [(knc_ref, kr4c_ref[...].astype(BF16), vc_ref), (kn_ref, kr4_ref[...].astype(BF16), v_ref)],
               o_ref)
    span = TQ_S + 2 * SWA_WINDOW
    start = pl.multiple_of(jnp.clip(i * TQ_S - SWA_WINDOW, 0, DEC_SEQ - span), SWA_WINDOW)
    qpos = i * TQ_S + lax.broadcasted_iota(jnp.int32, (TQ_S, span), 0)
    kpos = start + lax.broadcasted_iota(jnp.int32, (TQ_S, span), 1)
    band = jnp.abs(qpos - kpos) <= SWA_WINDOW
    keys = pl.ds(start, span)
    _swa_pairs(sink_ref, sq_ref, [(skc_ref[...], svc_ref[...]), (sk_ref[keys, :], sv_ref[keys, :])],
               [None, band], o_ref)


def _l0_attn_sample(sink, qn, qr, kn, vm, kr4, sq, sk, sv, knc, vc, kr4c, skc, svc):
    nq = DEC_SEQ // TQ_S
    qspec = lambda w: pl.BlockSpec((TQ_S, w), lambda b, i, *_: (T_P // TQ_S + b * nq + i, 0))
    kspec = lambda w: pl.BlockSpec((DEC_SEQ, w), lambda b, i, *_: (T_P // DEC_SEQ + b, 0))
    cspec = lambda w: pl.BlockSpec((PAST, w), lambda b, i, *_: (b, 0))
    return pl.pallas_call(
        _l0_attn_sample_kernel,
        out_shape=jax.ShapeDtypeStruct((T_S, D), BF16),
        grid_spec=pltpu.PrefetchScalarGridSpec(
            num_scalar_prefetch=1, grid=(DEC_BATCH, nq),
            in_specs=[qspec(512), qspec(256), qspec(512),
                      kspec(512), kspec(512), kspec(128), kspec(128), kspec(128),
                      cspec(512), cspec(512), cspec(128), cspec(128), cspec(128)],
            out_specs=pl.BlockSpec((TQ_S, D), lambda b, i, *_: (b * nq + i, 0))),
        compiler_params=_params("arbitrary", "arbitrary"),
        name="l0_attn_sample",
    )(sink, qn, qr, sq, kn, vm, kr4, sk, sv, knc, vc, kr4c, skc, svc)


OUT_LN_ROWS = 256


def _out_ln_kernel(op_ref, os_ref, w_ref, y_ref, mod_ref, g_ref, b_ref, out_ref):
    i = pl.program_id(0)
    m = mod_ref[0]
    for s in range(TM // OUT_LN_ROWS):
        rows = slice(s * OUT_LN_ROWS, (s + 1) * OUT_LN_ROWS)
        o = jnp.where(i < NP_TILES, op_ref[rows, :], os_ref[rows, :])
        r = ALPHA * y_ref[rows, :] + m[2:3] * _dot(o, w_ref[...])
        out_ref[rows, :] = _layer_norm(r, g_ref[...], b_ref[...])


def _out_ln(o_prompt, o_sample, w_out, y, mods, ln_g, ln_b):
    return pl.pallas_call(
        _out_ln_kernel,
        out_shape=jax.ShapeDtypeStruct((T, D), F32),
        grid=(NT,),
        in_specs=[_prompt_tile_spec(D), _sample_tile_spec(D),
                  _const_spec((D, D)), _row_spec(D), _mod_spec(),
                  _const_spec((1, D)), _const_spec((1, D))],
        out_specs=_row_spec(D),
        compiler_params=_params("arbitrary"),
        name="out_proj_ln",
    )(o_prompt, o_sample, w_out, y, mods, ln_g.reshape(1, D), ln_b.reshape(1, D))


FFN_CHUNK = D_FF
assert FFN_CHUNK % MXU_N == 0 and D_FF % FFN_CHUNK == 0
FFN_TM = 256
FFN_CAST_STEPS = 32


def _ffn_ln_kernel(y_ref, mod_ref, w1_ref, w3_ref, w2_ref, g_ref, b_ref, wfa_ref, wfb_ref,
                   out_ref, wba_ref, wbb_ref):
    i = pl.program_id(0)
    _cast_block(i, FFN_CAST_STEPS, wfa_ref, wba_ref)
    _cast_block(i, FFN_CAST_STEPS, wfb_ref, wbb_ref)
    m = mod_ref[0]
    y = y_ref[...]
    h = (y * (1.0 + m[4:5]) + m[3:4]).astype(BF16)
    acc = None
    for c in range(D_FF // FFN_CHUNK):
        cols = slice(c * FFN_CHUNK, (c + 1) * FFN_CHUNK)
        a = _dot(h, w1_ref[:, cols])
        g = _dot(h, w3_ref[:, cols])
        part = _dot((_silu(a) * g).astype(BF16), w2_ref[cols, :])
        acc = part if acc is None else acc + part
    out_ref[...] = _layer_norm(ALPHA * y + m[5:6] * acc, g_ref[...], b_ref[...])


def _ffn_ln(y, mods, w1, w3, w2, ln_g, ln_b, wa_f32, wb_f32):
    wa_spec, wa_shape = _cast_specs(wa_f32.shape, FFN_CAST_STEPS)
    wb_spec, wb_shape = _cast_specs(wb_f32.shape, FFN_CAST_STEPS)
    rows = _row_spec(D, FFN_TM)
    return pl.pallas_call(
        _ffn_ln_kernel,
        out_shape=[jax.ShapeDtypeStruct((T, D), F32), wa_shape, wb_shape],
        grid=(T // FFN_TM,),
        in_specs=[rows, _mod_spec(FFN_TM), _const_spec((D, D_FF), True), _const_spec((D, D_FF), True),
                  _const_spec((D_FF, D), True), _const_spec((1, D)), _const_spec((1, D)), wa_spec, wb_spec],
        out_specs=[rows, wa_spec, wb_spec],
        compiler_params=_params("arbitrary"),
        name="ffn_ln",
    )(y, mods, w1, w3, w2, ln_g.reshape(1, D), ln_b.reshape(1, D), wa_f32, wb_f32)


def _l1_in_kernel(y_ref, mod_ref, w_ref, q_o, k_o, v_o, kh_o, vh_o):
    i = pl.program_id(0)
    m = mod_ref[0]
    h = (y_ref[...] * (1.0 + m[1:2]) + m[0:1]).astype(BF16)
    z = _dot(h, w_ref[...])
    k = z[:, D:2 * D]
    v = z[:, 2 * D:3 * D]
    q_o[...] = (z[:, 0:D] * HD_QMUL).astype(BF16)
    k_o[...] = k.astype(BF16)
    v_o[...] = v.astype(BF16)

    @pl.when(i < NP_TILES)
    def _():
        for bb in range(TM // SEQ):
            rows = slice(bb * SEQ, (bb + 1) * SEQ)
            for j in range(NA_HEADS // 2):
                cols = slice(j * LANES, (j + 1) * LANES)
                kt = k[rows, cols].T
                vt = v[rows, cols].T
                for hh in range(2):
                    drows = slice(hh * HEAD_DIM, (hh + 1) * HEAD_DIM)
                    kh_o[bb, 2 * j + hh] = kt[drows]
                    vh_o[bb, 2 * j + hh] = vt[drows]


def _l1_in(y, mods, w_in):
    heads = pl.BlockSpec((TM // SEQ, NA_HEADS, HEAD_DIM, SEQ), lambda i: (jnp.minimum(i, NP_TILES - 1), 0, 0, 0))
    return pl.pallas_call(
        _l1_in_kernel,
        out_shape=[jax.ShapeDtypeStruct((T, D), BF16)] * 3
        + [jax.ShapeDtypeStruct((BATCH, NA_HEADS, HEAD_DIM, SEQ), F32)] * 2,
        grid=(NT,),
        in_specs=[_row_spec(D), _mod_spec(), _const_spec((D, 3 * D))],
        out_specs=[_row_spec(D)] * 3 + [heads, heads],
        compiler_params=_params("arbitrary"),
        name="l1_in_proj",
    )(y, mods, w_in)


def _l1_attn_prompt_kernel(q_ref, k_ref, v_ref, o_ref):
    for j in range(NA_HEADS // 2):
        cols = slice(128 * j, 128 * (j + 1))
        o = _head_pair(q_ref[:, cols], [k_ref[:, cols].astype(BF16)], [v_ref[:, cols].astype(BF16)])
        o_ref[:, cols] = o.astype(BF16)


def _l1_attn_prompt(q, k, v):
    spec = pl.BlockSpec((SEQ, D), lambda b: (b, 0))
    return pl.pallas_call(
        _l1_attn_prompt_kernel,
        out_shape=jax.ShapeDtypeStruct((T_P, D), BF16),
        grid=(BATCH,),
        in_specs=[spec, spec, spec],
        out_specs=spec,
        compiler_params=_params("arbitrary"),
        name="l1_attn_prompt",
    )(q, k, v)


NA_ROWS = DEC_SEQ // GRID_W
NA_TILE_ROWS = 4
NA_TQ = NA_TILE_ROWS * GRID_W
NA_SPAN = NA_WIN_ROWS + NA_TILE_ROWS
NA_DR = 2 * NA_WIN_ROWS - 1
COL_SPAN = 2 * GRID_W - 1


def _na_span_start(t):
    first = max(0, min(t * NA_TILE_ROWS - NA_WIN_ROWS // 2, NA_ROWS - NA_WIN_ROWS))
    return min(first, NA_ROWS - NA_SPAN)


def _l1_attn_sample_kernel(q_ref, k_ref, v_ref, kc_ref, vc_ref, ext_ref, o_ref):
    kc = kc_ref[0].astype(BF16)
    vc = vc_ref[0].astype(BF16)
    lo = _lane() < GRID_W
    qcol = lax.broadcasted_iota(jnp.int32, (GRID_W, LANES), 0)
    kcol = lax.broadcasted_iota(jnp.int32, (GRID_W, LANES), 1) % GRID_W
    cs = jnp.clip(qcol - NA_WIN_COLS // 2, 0, GRID_W - NA_WIN_COLS)
    col_ok = jnp.logical_and(kcol >= cs, kcol < cs + NA_WIN_COLS)
    neg = jnp.full((GRID_W, LANES), NEG, F32)

    def bias_tile(hh, d, half):
        x = jnp.broadcast_to(ext_ref[hh, d:d + 1, :], (GRID_W, LANES))
        shift = (LANES - COL_SPAN // 2 + GRID_W * half) % LANES
        return jnp.where(col_ok, pltpu.roll(x, shift, 1, stride=1, stride_axis=0), neg)

    tiles = [[[bias_tile(hh, d, half) for half in range(2)] for d in range(NA_DR)] for hh in range(2)]

    for t in range(NA_ROWS // NA_TILE_ROWS):
        ws = _na_span_start(t)
        keys = slice(ws * GRID_W, (ws + NA_SPAN) * GRID_W)
        kw = k_ref[keys, :].astype(BF16)
        vw = v_ref[keys, :].astype(BF16)
        bias = []
        for hh in range(2):
            rows = []
            for rr in range(NA_TILE_ROWS):
                r = t * NA_TILE_ROWS + rr
                rs = max(0, min(r - NA_WIN_ROWS // 2, NA_ROWS - NA_WIN_ROWS))
                blocks = []
                for u in range(NA_SPAN // 2):
                    halves = []
                    for half in range(2):
                        kr = ws + 2 * u + half
                        ok = rs <= kr < rs + NA_WIN_ROWS
                        halves.append(tiles[hh][kr - r + NA_WIN_ROWS - 1][half] if ok else neg)
                    blocks.append(jnp.where(lo, halves[0], halves[1]))
                rows.append(jnp.concatenate(blocks, axis=1))
            bias.append(jnp.concatenate(rows, axis=0))
        qrows = slice(t * NA_TQ, (t + 1) * NA_TQ)
        o = _head_pair(q_ref[qrows, :], [kw, kc], [vw, vc], biases=[bias, None])
        o_ref[qrows, :] = o.astype(BF16)


def _l1_attn_sample(q, k, v, kc, vc, ext):
    lat = pl.BlockSpec((DEC_SEQ, LANES), lambda j, b: (T_P // DEC_SEQ + b, j))
    ctx = pl.BlockSpec((1, PAST, LANES), lambda j, b: (b, 0, j))
    return pl.pallas_call(
        _l1_attn_sample_kernel,
        out_shape=jax.ShapeDtypeStruct((T_S, D), BF16),
        grid=(NA_HEADS // 2, DEC_BATCH),
        in_specs=[lat, lat, lat, ctx, ctx, pl.BlockSpec((2, NA_DR, LANES), lambda j, b: (j, 0, 0))],
        out_specs=pl.BlockSpec((DEC_SEQ, LANES), lambda j, b: (b, j)),
        compiler_params=_params("arbitrary", "arbitrary"),
        name="l1_attn_sample",
    )(q, k, v, kc, vc, ext)


def _na_bias_rows(rel_bias):
    rb = rel_bias.astype(F32) * LOG2E
    n_lo = GRID_W - 1 - (NA_WIN_COLS - 1)
    n_hi = LANES - n_lo - rb.shape[-1]
    return jnp.concatenate([jnp.repeat(rb[..., :1], n_lo, axis=-1), rb,
                            jnp.repeat(rb[..., -1:], n_hi, axis=-1)], axis=-1)


def _split_bf16(x):
    hi = x.astype(BF16)
    return hi, (x - hi.astype(F32)).astype(BF16)


def _router_kernel(y_ref, mod_ref, rw_ref, rb_ref, wf_ref, h_o, meta_o, cnt_o, wb_ref, carry_ref):
    i = pl.program_id(0)
    _cast_block(i, NP_TILES, wf_ref, wb_ref)

    @pl.when(i == 0)
    def _():
        carry_ref[...] = jnp.zeros_like(carry_ref)

    m = mod_ref[0]
    h = y_ref[...] * (1.0 + m[4:5]) + m[3:4]
    _store_rows_tiled(h_o, h)
    h_hi, h_lo = _split_bf16(h)
    w_hi, w_lo = _split_bf16(rw_ref[...])
    logits = _dot(jnp.concatenate([h_hi, h_hi, h_lo], axis=1),
                  jnp.concatenate([w_hi, w_lo, w_hi], axis=0)) + rb_ref[...]
    lane = lax.broadcasted_iota(jnp.int32, (TM, LANES), 1).astype(F32)
    m1 = jnp.max(logits, axis=1, keepdims=True)
    i1 = jnp.min(jnp.where(logits == m1, lane, float(LANES)), axis=1, keepdims=True)
    sel1 = lane == i1
    rest = jnp.where(sel1, -jnp.inf, logits)
    m2 = jnp.max(rest, axis=1, keepdims=True)
    i2 = jnp.min(jnp.where(rest == m2, lane, float(LANES)), axis=1, keepdims=True)
    sel2 = lane == i2
    e2 = jnp.exp(m2 - m1)
    w1 = 1.0 / (1.0 + e2)
    w2 = e2 / (1.0 + e2)
    sel = jnp.logical_or(sel1, sel2)
    rr = lax.broadcasted_iota(jnp.int32, (TM, TM), 0)
    cc = lax.broadcasted_iota(jnp.int32, (TM, TM), 1)
    tri = jnp.where(cc < rr, 1.0, 0.0).astype(BF16)
    ahead = _dot(tri, jnp.where(sel, 1.0, 0.0).astype(BF16)) + carry_ref[...]
    r1 = jnp.sum(jnp.where(sel1, ahead, 0.0), axis=1, keepdims=True)
    r2 = jnp.sum(jnp.where(sel2, ahead, 0.0), axis=1, keepdims=True)
    meta = jnp.where(lane == 0, i1, 0.0)
    meta = jnp.where(lane == 1, i2, meta)
    meta = jnp.where(lane == 2, r1, meta)
    meta = jnp.where(lane == 3, r2, meta)
    meta = jnp.where(lane == 4, w1, meta)
    meta = jnp.where(lane == 5, w2, meta)
    meta_o[...] = meta
    carry_ref[...] = carry_ref[...] + jnp.sum(jnp.where(sel, 1.0, 0.0), axis=0, keepdims=True)
    cnt_o[...] = carry_ref[...]


def _router(y, mods, router_w, router_b, w_f32):
    rw = jnp.zeros((D, LANES), F32).at[:, :N_EXPERTS].set(router_w)
    rb = jnp.full((1, LANES), NEG, F32).at[0, :N_EXPERTS].set(router_b)
    w_spec, wb_shape = _cast_specs(w_f32.shape, NP_TILES)
    return pl.pallas_call(
        _router_kernel,
        out_shape=[jax.ShapeDtypeStruct((T * ROW_TILE, LANES), F32),
                   jax.ShapeDtypeStruct((T, LANES), F32),
                   jax.ShapeDtypeStruct((1, LANES), F32),
                   wb_shape],
        grid=(NT,),
        in_specs=[_row_spec(D), _mod_spec(), _const_spec((D, LANES)), _const_spec((1, LANES)), w_spec],
        out_specs=[_row_spec(LANES, TM * ROW_TILE), _row_spec(LANES), _const_spec((1, LANES)), w_spec],
        scratch_shapes=[pltpu.VMEM((1, LANES), F32)],
        compiler_params=_params("arbitrary"),
        name="moe_router",
    )(y, mods, rw, rb, w_f32)


DMA_UNROLL = 8


def _row_copy(src_ref, src_row, dst_ref, dst_row, sem):
    return pltpu.make_async_copy(src_ref.at[pl.ds(pl.multiple_of(src_row * ROW_TILE, ROW_TILE), ROW_TILE)],
                                 dst_ref.at[pl.ds(pl.multiple_of(dst_row * ROW_TILE, ROW_TILE), ROW_TILE)], sem)


def _group_tile_copy(src_ref, dst_ref, tile, sem):
    start = pl.multiple_of(tile * (TG * ROW_TILE), TG * ROW_TILE)
    return pltpu.make_async_copy(src_ref, dst_ref.at[pl.ds(start, TG * ROW_TILE)], sem)


def _dispatch_kernel(pos_ref, last_ref, na_ref, h_ref, xg_ref, zero_ref, sem, zsem):
    i = pl.program_id(0)

    @pl.when(i == 0)
    def _():
        zero_ref[...] = jnp.zeros_like(zero_ref)
        for e in range(N_EXPERTS):
            @pl.when(last_ref[e] >= 0)
            def _():
                _group_tile_copy(zero_ref, xg_ref, last_ref[e], zsem).start()

        def start_unused(g, carry):
            _group_tile_copy(zero_ref, xg_ref, g, zsem).start()
            return carry

        def wait_one(g, carry):
            _group_tile_copy(zero_ref, xg_ref, 0, zsem).wait()
            return carry

        lax.fori_loop(na_ref[0], NG, start_unused, 0)
        lax.fori_loop(0, na_ref[1], wait_one, 0)

    def issue(r, carry):
        t = i * TM + r
        _row_copy(h_ref, r, xg_ref, pos_ref[2 * t], sem).start(priority=0)
        _row_copy(h_ref, r, xg_ref, pos_ref[2 * t + 1], sem).start(priority=1)
        return carry

    lax.fori_loop(0, TM, issue, 0, unroll=DMA_UNROLL)
    for _ in range(2):
        pltpu.make_async_copy(h_ref, xg_ref.at[pl.ds(0, TM * ROW_TILE)], sem).wait()


def _dispatch(pos, last_tile, tile_counts, h_tiled):
    return pl.pallas_call(
        _dispatch_kernel,
        out_shape=jax.ShapeDtypeStruct((P_ROWS * ROW_TILE, LANES), F32),
        grid_spec=pltpu.PrefetchScalarGridSpec(
            num_scalar_prefetch=3, grid=(NT,),
            in_specs=[pl.BlockSpec((TM * ROW_TILE, LANES), lambda i, *_: (i, 0))],
            out_specs=pl.BlockSpec(memory_space=pl.ANY),
            scratch_shapes=[pltpu.VMEM((TG * ROW_TILE, LANES), F32), pltpu.SemaphoreType.DMA(()),
                            pltpu.SemaphoreType.DMA(())]),
        compiler_params=_params("arbitrary"),
        name="moe_dispatch",
    )(pos, last_tile, tile_counts, h_tiled)


def _expert_ffn_kernel(te_ref, na_ref, x_ref, w1_ref, w3_ref, w2_ref, o_ref):
    g = pl.program_id(0)

    @pl.when(g < na_ref[0])
    def _():
        x = _load_rows_tiled(x_ref, TG).astype(BF16)
        acc = None
        for c in range(E_FF // F_CHUNK):
            cols = slice(c * F_CHUNK, (c + 1) * F_CHUNK)
            a = _dot(x, w1_ref[0, :, cols])
            b = _dot(x, w3_ref[0, :, cols])
            part = _dot((_silu(a) * b).astype(BF16), w2_ref[0, cols, :])
            acc = part if acc is None else acc + part
        _store_rows_tiled(o_ref, acc)

    @pl.when(g >= na_ref[0])
    def _():
        o_ref[...] = jnp.zeros_like(o_ref)


def _expert_ffn(tile_expert, n_active, xg, w1, w3, w2):
    rows = pl.BlockSpec((TG * ROW_TILE, LANES), lambda g, te, na: (g, 0))
    w_up = pl.BlockSpec((1, D, E_FF), lambda g, te, na: (te[g], 0, 0))
    w_dn = pl.BlockSpec((1, E_FF, D), lambda g, te, na: (te[g], 0, 0))
    return pl.pallas_call(
        _expert_ffn_kernel,
        out_shape=jax.ShapeDtypeStruct((P_ROWS * ROW_TILE, LANES), F32),
        grid_spec=pltpu.PrefetchScalarGridSpec(
            num_scalar_prefetch=2, grid=(NG,),
            in_specs=[rows, w_up, w_up, w_dn],
            out_specs=rows),
        compiler_params=_params("arbitrary"),
        name="moe_expert_ffn",
    )(tile_expert, n_active, xg, w1, w3, w2)


def _combine_ln_kernel(pos_ref, y_ref, mod_ref, meta_ref, g_ref, b_ref, eo_ref, outp_ref, outs_ref,
                       buf1, buf2, sem):
    i = pl.program_id(0)
    slot = i % 2

    def fetch(tile, into):
        def issue(r, carry):
            t = tile * TM + r
            _row_copy(eo_ref, pos_ref[2 * t], buf1.at[into], r, sem.at[into]).start(priority=0)
            _row_copy(eo_ref, pos_ref[2 * t + 1], buf2.at[into], r, sem.at[into]).start(priority=1)
            return carry

        lax.fori_loop(0, TM, issue, 0, unroll=DMA_UNROLL)

    @pl.when(i == 0)
    def _():
        fetch(0, 0)

    @pl.when(i + 1 < NT)
    def _():
        fetch(i + 1, 1 - slot)

    for buf in (buf1, buf2):
        pltpu.make_async_copy(eo_ref.at[pl.ds(0, TM * ROW_TILE)], buf.at[slot], sem.at[slot]).wait()
    m = mod_ref[0]
    meta = meta_ref[...]
    f = (meta[:, 4:5] * _load_rows_tiled(buf1.at[slot], TM)
         + meta[:, 5:6] * _load_rows_tiled(buf2.at[slot], TM))
    out = _layer_norm(ALPHA * y_ref[...] + m[5:6] * f, g_ref[...], b_ref[...])

    @pl.when(i < NP_TILES)
    def _():
        outp_ref[...] = out

    @pl.when(i >= NP_TILES)
    def _():
        outs_ref[...] = out


def _combine_ln(pos, y, mods, meta, ln_g, ln_b, eo):
    return pl.pallas_call(
        _combine_ln_kernel,
        out_shape=[jax.ShapeDtypeStruct((T_P, D), F32), jax.ShapeDtypeStruct((T_S, D), F32)],
        grid_spec=pltpu.PrefetchScalarGridSpec(
            num_scalar_prefetch=1, grid=(NT,),
            in_specs=[pl.BlockSpec((TM, D), lambda i, *_: (i, 0)),
                      pl.BlockSpec((1, 6, D), lambda i, *_: (_cond_of_tile(i), 0, 0)),
                      pl.BlockSpec((TM, LANES), lambda i, *_: (i, 0)),
                      pl.BlockSpec((1, D), lambda i, *_: (0, 0)),
                      pl.BlockSpec((1, D), lambda i, *_: (0, 0)),
                      pl.BlockSpec(memory_space=pl.ANY)],
            out_specs=[_prompt_tile_spec(D), _sample_tile_spec(D)],
            scratch_shapes=[pltpu.VMEM((2, TM * ROW_TILE, LANES), F32),
                            pltpu.VMEM((2, TM * ROW_TILE, LANES), F32),
                            pltpu.SemaphoreType.DMA((2,))]),
        compiler_params=_params("arbitrary"),
        name="moe_combine_ln",
    )(pos, y, mods, meta, ln_g.reshape(1, D), ln_b.reshape(1, D), eo)


def _moe_ln(y, mods, router_w, router_b, w1, w3, w2_f32, ln_g, ln_b):
    h_tiled, meta, counts, w2 = _router(y, mods, router_w, router_b, w2_f32.reshape(N_EXPERTS * E_FF, D))
    cnt = counts[0, :N_EXPERTS].astype(jnp.int32)
    tiles = (cnt + TG - 1) // TG
    tile_end = jnp.cumsum(tiles)
    offs = (tile_end - tiles) * TG
    expert = meta[:, 0:2].astype(jnp.int32)
    chosen = expert[:, :, None] == jnp.arange(N_EXPERTS)[None, None, :]
    pos = jnp.sum(jnp.where(chosen, offs[None, None, :], 0), axis=-1) + meta[:, 2:4].astype(jnp.int32)
    pos = pos.reshape(2 * T)
    tile_expert = jnp.sum((jnp.arange(NG)[:, None] >= tile_end[None, :]).astype(jnp.int32), axis=1)
    tile_expert = jnp.minimum(tile_expert, N_EXPERTS - 1)
    n_active = tile_end[-1:].astype(jnp.int32)
    last_tile = jnp.where(tiles > 0, tile_end - 1, -1).astype(jnp.int32)
    n_zeroed = jnp.sum((tiles > 0).astype(jnp.int32)) + NG - tile_end[-1]
    tile_counts = jnp.stack([tile_end[-1], n_zeroed]).astype(jnp.int32)
    xg = _dispatch(pos, last_tile, tile_counts, h_tiled)
    eo = _expert_ffn(tile_expert, n_active, xg, w1, w3, w2.reshape(N_EXPERTS, E_FF, D))
    return _combine_ln(pos, y, mods, meta, ln_g, ln_b, eo)


def _l0_weight_layouts(w_in, w_q_up, w_kv_up):
    a, b, c = MLA_Q_LORA, MLA_Q_LORA + MLA_KV_LORA, MLA_Q_LORA + MLA_KV_LORA + MLA_ROPE
    k_rope = w_in[:, b:c]
    w_in_r = jnp.concatenate([w_in[:, :b], w_in[:, c:], k_rope, k_rope, k_rope, k_rope], axis=1)
    wq = w_q_up.reshape(MLA_Q_LORA, MLA_HEADS, MLA_NOPE + MLA_ROPE)
    w_q_up_r = jnp.concatenate([wq[:, :, :MLA_NOPE].reshape(MLA_Q_LORA, -1),
                                wq[:, :, MLA_NOPE:].reshape(MLA_Q_LORA, -1)], axis=1)
    wkv = w_kv_up.reshape(MLA_KV_LORA, MLA_HEADS, MLA_NOPE + MLA_V)
    w_kv_up_r = jnp.concatenate([wkv[:, :, :MLA_NOPE].reshape(MLA_KV_LORA, -1),
                                 wkv[:, :, MLA_NOPE:].reshape(MLA_KV_LORA, -1)], axis=1)
    return w_in_r.astype(BF16), w_q_up_r.astype(BF16), w_kv_up_r.astype(BF16)


def kernel(x_prompt, x_sample, cache_l0_mla_ckv, cache_l0_mla_krope, cache_l0_swa_k, cache_l0_swa_v,
           cache_l1_na_k, cache_l1_na_v, c, c_ctx,
           l0_ada_w, l0_ada_b, l0_w_in, l0_mla_q_norm, l0_mla_w_q_up, l0_mla_kv_norm, l0_mla_w_kv_up,
           l0_swa_sink, l0_w_out, l0_ln1_g, l0_ln1_b, l0_ffn_w1, l0_ffn_w3, l0_ffn_w2, l0_ln2_g, l0_ln2_b,
           l1_ada_w, l1_ada_b, l1_w_in, l1_na_rel_bias, l1_w_out, l1_ln1_g, l1_ln1_b,
           l1_moe_router_w, l1_moe_router_b, l1_moe_w1, l1_moe_w3, l1_moe_w2, l1_ln2_g, l1_ln2_b):
    cond = jnp.concatenate([c_ctx[None, :], c, jnp.zeros((8 - N_COND, D), F32)], axis=0)

    mods = _adaln(cond, l0_ada_w, l0_ada_b)
    w_in_r, w_q_up_r, w_kv_up_r = _l0_weight_layouts(l0_w_in, l0_mla_w_q_up, l0_mla_w_kv_up)
    y, qn, qr, kn, vm, kr4, sq, sk, sv, ckv, krope_t, sk_t, sv_t = _l0_in(
        x_prompt.reshape(T_P, D), x_sample.reshape(T_S, D), mods, w_in_r, l0_mla_q_norm, w_q_up_r,
        l0_mla_kv_norm, w_kv_up_r, _rope_tables(MLA_ROPE, 256), _rope_tables(HEAD_DIM, 512))
    knc, vc = _kv_up(cache_l0_mla_ckv.reshape(DEC_BATCH * PAST, MLA_KV_LORA), w_kv_up_r)
    kr4c = jnp.tile(cache_l0_mla_krope.reshape(DEC_BATCH * PAST, MLA_ROPE), (1, 4))
    skc = cache_l0_swa_k.reshape(DEC_BATCH * PAST, SWA_KV_HEADS * HEAD_DIM)
    svc = cache_l0_swa_v.reshape(DEC_BATCH * PAST, SWA_KV_HEADS * HEAD_DIM)
    o_p, ffn_w1, ffn_w3, ffn_w2 = _l0_attn_prompt(l0_swa_sink, qn, qr, kn, vm, kr4, sq, sk, sv,
                                                  [l0_ffn_w1, l0_ffn_w3, l0_ffn_w2])
    o_s = _l0_attn_sample(l0_swa_sink, qn, qr, kn, vm, kr4, sq, sk, sv, knc, vc, kr4c, skc, svc)
    y = _out_ln(o_p, o_s, l0_w_out.astype(BF16), y, mods, l0_ln1_g, l0_ln1_b)
    y, moe_w1, moe_w3 = _ffn_ln(y, mods, ffn_w1, ffn_w3, ffn_w2, l0_ln2_g, l0_ln2_b,
                                l1_moe_w1.reshape(N_EXPERTS * D, E_FF), l1_moe_w3.reshape(N_EXPERTS * D, E_FF))
    new_ckv = ckv.reshape(BATCH, SEQ, MLA_KV_LORA)
    new_krope = jnp.transpose(krope_t, (0, 2, 1))
    new_sk = jnp.transpose(sk_t, (0, 3, 1, 2))
    new_sv = jnp.transpose(sv_t, (0, 3, 1, 2))

    mods = _adaln(cond, l1_ada_w, l1_ada_b)
    q, k, v, k_heads, v_heads = _l1_in(y, mods, l1_w_in.astype(BF16))
    o_p = _l1_attn_prompt(q, k, v)
    o_s = _l1_attn_sample(q, k, v, cache_l1_na_k.reshape(DEC_BATCH, PAST, D),
                          cache_l1_na_v.reshape(DEC_BATCH, PAST, D), _na_bias_rows(l1_na_rel_bias))
    y = _out_ln(o_p, o_s, l1_w_out.astype(BF16), y, mods, l1_ln1_g, l1_ln1_b)
    y_p, y_s = _moe_ln(y, mods, l1_moe_router_w, l1_moe_router_b, moe_w1.reshape(N_EXPERTS, D, E_FF),
                       moe_w3.reshape(N_EXPERTS, D, E_FF), l1_moe_w2, l1_ln2_g, l1_ln2_b)
    new_k = jnp.transpose(k_heads, (0, 3, 1, 2))
    new_v = jnp.transpose(v_heads, (0, 3, 1, 2))

    return (y_p.reshape(BATCH, SEQ, D), y_s.reshape(DEC_BATCH, DEC_SEQ, D),
            new_ckv, new_krope, new_sk, new_sv, new_k, new_v)
```

```python
import functools

import jax
import jax.numpy as jnp
import numpy as np
from jax import lax
from jax.experimental import pallas as pl
from jax.experimental.pallas import tpu as pltpu

F32 = jnp.float32
BF16 = jnp.bfloat16

D = 1024
BATCH, SEQ = 32, 256
DEC_BATCH, DEC_SEQ = 2, 1024
PAST = 256
GRID_W = 64
T_P = BATCH * SEQ
T_S = DEC_BATCH * DEC_SEQ
T = T_P + T_S
N_COND = 1 + DEC_BATCH

MLA_HEADS, MLA_Q_LORA, MLA_KV_LORA, MLA_NOPE, MLA_ROPE, MLA_V = 8, 384, 256, 64, 32, 64
SWA_HEADS, SWA_KV_HEADS, SWA_WINDOW, HEAD_DIM = 8, 2, 128, 64
NA_HEADS, NA_WIN_ROWS, NA_WIN_COLS = 16, 8, 16
D_FF, N_EXPERTS, E_FF = 2816, 8, 3584
ROPE_THETA = 10000.0
LN_EPS, RMS_EPS = 1e-5, 1e-6
NEG = -1e30
ALPHA = 4.0 ** 0.25

LANES = 128
TM = 512
NT = T // TM
NP_TILES = T_P // TM
TILES_PER_SAMPLE = DEC_SEQ // TM
TG = 256
P_ROWS = 2 * T + N_EXPERTS * TG
NG = P_ROWS // TG
MXU_N = 256
F_CHUNK = 1792
assert F_CHUNK % MXU_N == 0 and E_FF % F_CHUNK == 0
VMEM_LIMIT = 56 * 1024 * 1024


def _params(*sem):
    return pltpu.CompilerParams(dimension_semantics=sem, vmem_limit_bytes=VMEM_LIMIT)


def _const_spec(shape, single_buffer=False):
    if single_buffer:
        return pl.BlockSpec(shape, lambda *_: (0,) * len(shape), pipeline_mode=pl.Buffered(1))
    return pl.BlockSpec(shape, lambda *_: (0,) * len(shape))


def _cast_specs(shape, steps):
    rows, width = shape
    spec = pl.BlockSpec((rows // steps, width), lambda i, *_: (jnp.minimum(i, steps - 1), 0))
    return spec, jax.ShapeDtypeStruct(shape, BF16)


def _cast_block(i, steps, src_ref, dst_ref):
    @pl.when(i < steps)
    def _():
        dst_ref[...] = src_ref[...].astype(BF16)


ROW_TILE = D // LANES


def _store_rows_tiled(ref, x):
    n = x.shape[0]
    for k in range(ROW_TILE):
        ref[pl.ds(k, n, stride=ROW_TILE), :] = x[:, k * LANES:(k + 1) * LANES]


def _load_rows_tiled(ref, n):
    return jnp.concatenate([ref[pl.ds(k, n, stride=ROW_TILE), :] for k in range(ROW_TILE)], axis=1)


def _cond_of_tile(i, rows=TM):
    per_sample = DEC_SEQ // rows
    return jnp.maximum((i - (T_P // rows - per_sample)) // per_sample, 0)


def _mod_spec(rows=TM):
    return pl.BlockSpec((1, 6, D), lambda i: (_cond_of_tile(i, rows), 0, 0))


def _row_spec(width, rows=TM):
    return pl.BlockSpec((rows, width), lambda i: (i, 0))


def _dot(a, b):
    return jnp.dot(a, b, preferred_element_type=F32)


def _dot_nt(a, b):
    return lax.dot_general(a, b, (((1,), (1,)), ((), ())), preferred_element_type=F32)


def _layer_norm(r, g, b):
    mu = jnp.mean(r, axis=-1, keepdims=True)
    d = r - mu
    var = jnp.mean(d * d, axis=-1, keepdims=True)
    return d * lax.rsqrt(var + LN_EPS) * g + b


def _rms_norm(x, g):
    return x * lax.rsqrt(jnp.mean(x * x, axis=-1, keepdims=True) + RMS_EPS) * g


def _silu(x):
    return x * jax.nn.sigmoid(x)


def _ada_kernel(c_ref, w_ref, b_ref, o_ref):
    s = _silu(c_ref[...]).astype(BF16)
    o_ref[...] = _dot(s, w_ref[...].astype(BF16)) + b_ref[...]


def _adaln(cond, ada_w, ada_b):
    nb = 1536
    out = pl.pallas_call(
        _ada_kernel,
        out_shape=jax.ShapeDtypeStruct((8, 6 * D), F32),
        grid=(6 * D // nb,),
        in_specs=[_const_spec((8, D)), pl.BlockSpec((D, nb), lambda j: (0, j)),
                  pl.BlockSpec((1, nb), lambda j: (0, j))],
        out_specs=pl.BlockSpec((8, nb), lambda j: (0, j)),
        compiler_params=_params("arbitrary"),
        name="adaln",
    )(cond, ada_w, ada_b.reshape(1, 6 * D))
    return out[:N_COND].reshape(N_COND, 6, D)


def _rope_tables(head_dim, width):
    half = head_dim // 2
    nf = half // 2
    lane = np.arange(width)
    d = lane % head_dim
    dd = d % half
    f = dd % nf
    inv = np.float32(ROPE_THETA) ** (-f.astype(np.float32) / np.float32(nf))
    t = np.arange(DEC_SEQ)
    pos = np.where((d // half)[None, :] == 0, (t // GRID_W)[:, None], (t % GRID_W)[:, None])
    ang = pos.astype(np.float32) * inv[None, :].astype(np.float32)
    cos, sin = np.cos(ang), np.sin(ang)
    first = (dd < nf)[None, :]
    zero = np.float32(0.0)
    return (jnp.asarray(cos, F32), jnp.asarray(np.where(first, -sin, zero), F32),
            jnp.asarray(np.where(first, zero, sin), F32))


def _rope(x, cos, sin_up, sin_dn, nf):
    w = x.shape[-1]
    return x * cos + pltpu.roll(x, w - nf, 1) * sin_up + pltpu.roll(x, nf, 1) * sin_dn


L0_COLS = MLA_Q_LORA + MLA_KV_LORA + 512 + 128 + 128 + 128


def _l0_in_kernel(xp_ref, xs_ref, mod_ref, win_ref, qn_ref, wq_ref, kvn_ref, wkv_ref,
                  c8_ref, su8_ref, sd8_ref, c16_ref, su16_ref, sd16_ref,
                  y_o, qnope_o, qrope_o, knope_o, vmla_o, kr4_o, sq_o, sk_o, sv_o,
                  ckv_o, krt_o, skt_o, svt_o):
    i = pl.program_id(0)
    m = mod_ref[0]
    x = jnp.where(i < NP_TILES, xp_ref[...], xs_ref[...])
    y_o[...] = x
    h = (x * (1.0 + m[1:2]) + m[0:1]).astype(BF16)
    z = _dot(h, win_ref[...])
    q_lat = z[:, 0:384]
    kv_lat = z[:, 384:640]
    sq = z[:, 640:1152]
    sk = z[:, 1152:1280]
    sv = z[:, 1280:1408]
    kr4 = z[:, 1408:1536]
    q = _dot(_rms_norm(q_lat, qn_ref[...]).astype(BF16), wq_ref[...])
    c_kv = _rms_norm(kv_lat, kvn_ref[...])
    kv = _dot(c_kv.astype(BF16), wkv_ref[...])
    q = q * MLA_QMUL
    sq = sq * HD_QMUL
    qnope_o[...] = q[:, 0:512].astype(BF16)
    knope_o[...] = kv[:, 0:512].astype(BF16)
    vmla_o[...] = kv[:, 512:1024].astype(BF16)
    sv_o[...] = sv
    q_rope = q[:, 512:768]

    @pl.when(i < NP_TILES)
    def _():
        qrope_o[...] = q_rope.astype(BF16)
        kr4_o[...] = kr4
        sq_o[...] = sq.astype(BF16)
        sk_o[...] = sk
        ckv_o[...] = c_kv
        for bb in range(TM // SEQ):
            rows = slice(bb * SEQ, (bb + 1) * SEQ)
            krt_o[bb] = kr4[rows].T[:MLA_ROPE]
            skt = sk[rows].T
            svt = sv[rows].T
            for g in range(SWA_KV_HEADS):
                skt_o[bb, g] = skt[g * HEAD_DIM:(g + 1) * HEAD_DIM]
                svt_o[bb, g] = svt[g * HEAD_DIM:(g + 1) * HEAD_DIM]

    @pl.when(i >= NP_TILES)
    def _():
        c8, su8, sd8 = c8_ref[...], su8_ref[...], sd8_ref[...]
        c16, su16, sd16 = c16_ref[...], su16_ref[...], sd16_ref[...]
        qrope_o[...] = _rope(q_rope, c8, su8, sd8, 8).astype(BF16)
        kr4_o[...] = _rope(kr4, c8[:, :128], su8[:, :128], sd8[:, :128], 8)
        sq_o[...] = _rope(sq, c16, su16, sd16, 16).astype(BF16)
        sk_o[...] = _rope(sk, c16[:, :128], su16[:, :128], sd16[:, :128], 16)


def _prompt_tile_spec(width):
    return pl.BlockSpec((TM, width), lambda i, *_: (jnp.minimum(i, NP_TILES - 1), 0))


def _sample_tile_spec(width):
    return pl.BlockSpec((TM, width), lambda i, *_: (jnp.maximum(i - NP_TILES, 0), 0))


def _l0_in(x_prompt, x_sample, mods, w_in_r, q_norm, w_q_up_r, kv_norm, w_kv_up_r, tabs8, tabs16):
    def tab_spec(width):
        return pl.BlockSpec((TM, width), lambda i: (jnp.maximum(i - NP_TILES, 0) % TILES_PER_SAMPLE, 0))

    out_shape = [
        jax.ShapeDtypeStruct((T, D), F32),
        jax.ShapeDtypeStruct((T, 512), BF16),
        jax.ShapeDtypeStruct((T, 256), BF16),
        jax.ShapeDtypeStruct((T, 512), BF16),
        jax.ShapeDtypeStruct((T, 512), BF16),
        jax.ShapeDtypeStruct((T, 128), F32),
        jax.ShapeDtypeStruct((T, 512), BF16),
        jax.ShapeDtypeStruct((T, 128), F32),
        jax.ShapeDtypeStruct((T, 128), F32),
    ]
    per_tile = TM // SEQ
    prompt_block = lambda *dims: pl.BlockSpec((per_tile,) + dims,
                                              lambda i: (jnp.minimum(i, NP_TILES - 1),) + (0,) * len(dims))
    cache_shape = [
        jax.ShapeDtypeStruct((T_P, MLA_KV_LORA), F32),
        jax.ShapeDtypeStruct((BATCH, MLA_ROPE, SEQ), F32),
        jax.ShapeDtypeStruct((BATCH, SWA_KV_HEADS, HEAD_DIM, SEQ), F32),
        jax.ShapeDtypeStruct((BATCH, SWA_KV_HEADS, HEAD_DIM, SEQ), F32),
    ]
    cache_specs = [_prompt_tile_spec(MLA_KV_LORA), prompt_block(MLA_ROPE, SEQ),
                   prompt_block(SWA_KV_HEADS, HEAD_DIM, SEQ), prompt_block(SWA_KV_HEADS, HEAD_DIM, SEQ)]
    return pl.pallas_call(
        _l0_in_kernel,
        out_shape=out_shape + cache_shape,
        grid=(NT,),
        in_specs=[_prompt_tile_spec(D), _sample_tile_spec(D), _mod_spec(), _const_spec((D, L0_COLS)),
                  _const_spec((1, MLA_Q_LORA)), _const_spec((MLA_Q_LORA, 768)),
                  _const_spec((1, MLA_KV_LORA)), _const_spec((MLA_KV_LORA, 1024)),
                  tab_spec(256), tab_spec(256), tab_spec(256),
                  tab_spec(512), tab_spec(512), tab_spec(512)],
        out_specs=[_row_spec(s.shape[1]) for s in out_shape] + cache_specs,
        compiler_params=_params("arbitrary"),
        name="l0_in_proj",
    )(x_prompt, x_sample, mods, w_in_r, q_norm.reshape(1, -1), w_q_up_r, kv_norm.reshape(1, -1), w_kv_up_r,
      *tabs8, *tabs16)


def _kv_up_kernel(c_ref, w_ref, k_o, v_o):
    kv = _dot(c_ref[...].astype(BF16), w_ref[...])
    k_o[...] = kv[:, 0:512].astype(BF16)
    v_o[...] = kv[:, 512:1024].astype(BF16)


def _kv_up(ckv, w_kv_up_r):
    n = ckv.shape[0]
    return pl.pallas_call(
        _kv_up_kernel,
        out_shape=[jax.ShapeDtypeStruct((n, 512), BF16)] * 2,
        grid=(1,),
        in_specs=[_const_spec((n, MLA_KV_LORA)), _const_spec((MLA_KV_LORA, 1024))],
        out_specs=[_const_spec((n, 512))] * 2,
        compiler_params=_params("arbitrary"),
        name="l0_ctx_kv_up",
    )(ckv, w_kv_up_r)


def _lane():
    return lax.broadcasted_iota(jnp.int32, (1, LANES), 1)


def _attend(scores, values, extra_logit=None):
    m = jnp.max(scores[0], axis=1, keepdims=True)
    for s in scores[1:]:
        m = jnp.maximum(m, jnp.max(s, axis=1, keepdims=True))
    if extra_logit is not None:
        m = jnp.maximum(m, extra_logit)
    den = None
    acc = None
    for s, v in zip(scores, values):
        e = jnp.exp2(s - m)
        d = jnp.sum(e, axis=1, keepdims=True)
        a = _dot(e.astype(BF16), v)
        den = d if den is None else den + d
        acc = a if acc is None else acc + a
    if extra_logit is not None:
        den = den + jnp.exp2(extra_logit - m)
    return acc / den


def _head_pair(q2, keys, values, masks=None, biases=None, sinks=None, q_extra=None):
    lo = _lane() < 64
    zero = jnp.zeros_like(q2)
    outs = []
    for hh in range(2):
        qm = jnp.where(lo if hh == 0 else jnp.logical_not(lo), q2, zero)
        if q_extra is not None:
            qm = jnp.concatenate([qm, q_extra[hh]], axis=1)
        scores = []
        for n, k in enumerate(keys):
            s = _dot_nt(qm, k)
            if biases is not None and biases[n] is not None:
                s = s + biases[n][hh]
            if masks is not None and masks[n] is not None:
                s = jnp.where(masks[n], s, NEG)
            scores.append(s)
        outs.append(_attend(scores, values, None if sinks is None else sinks[hh]))
    return jnp.where(lo, outs[0], outs[1])


def _dup_halves(x):
    lo = _lane() < 64
    sw = pltpu.roll(x, 64, 1)
    return jnp.where(lo, x, sw), jnp.where(lo, sw, x)


LOG2E = 1.4426950408889634
MLA_QMUL = (MLA_NOPE + MLA_ROPE) ** -0.5 * LOG2E
HD_QMUL = HEAD_DIM ** -0.5 * LOG2E


def _mla_pairs(qn_ref, qr_ref, key_sets, o_ref):
    lane = _lane()
    for j in range(MLA_HEADS // 2):
        cols = slice(128 * j, 128 * (j + 1))
        qr = qr_ref[:, 128 * (j // 2):128 * (j // 2 + 1)]
        zero = jnp.zeros_like(qr)
        q_extra = [jnp.where((lane // MLA_ROPE) == ((2 * j + hh) % 4), qr, zero) for hh in range(2)]
        keys = [jnp.concatenate([kn[:, cols], kr4], axis=1) for kn, kr4, _ in key_sets]
        values = [v[:, cols] for _, _, v in key_sets]
        o = _head_pair(qn_ref[:, cols], keys, values, q_extra=q_extra)
        o_ref[:, cols] = o.astype(BF16)


def _swa_pairs(sink_ref, sq_ref, key_sets, masks, o_ref):
    kd = [[a.astype(BF16) for a in _dup_halves(k)] for k, _ in key_sets]
    vd = [[a.astype(BF16) for a in _dup_halves(v)] for _, v in key_sets]
    for g in range(SWA_KV_HEADS):
        for u in range(2):
            c = 2 * g + u
            cols = slice(128 * c, 128 * (c + 1))
            sinks = [sink_ref[2 * c + hh] * LOG2E for hh in range(2)]
            o = _head_pair(sq_ref[:, cols], [k[g] for k in kd], [v[g] for v in vd],
                           masks=masks, sinks=sinks)
            o_ref[:, 512 + 128 * c:512 + 128 * (c + 1)] = o.astype(BF16)


def _cast_steps(rows, max_steps):
    steps = max_steps
    while rows % (16 * steps):
        steps //= 2
    return steps


def _proj_residual_ln(o, w_ref, y, m, g_ref, b_ref):
    return _layer_norm(ALPHA * y + m[2:3] * _dot(o, w_ref[...]), g_ref[...], b_ref[...])


def _prompt_epilogue_specs():
    return [pl.BlockSpec((SEQ, D), lambda b, *_: (b, 0)), pl.BlockSpec((1, 6, D), lambda b, *_: (0, 0, 0)),
            pl.BlockSpec((D, D), lambda b, *_: (0, 0)), pl.BlockSpec((1, D), lambda b, *_: (0, 0)),
            pl.BlockSpec((1, D), lambda b, *_: (0, 0))]


def _l0_attn_prompt_kernel(cast_steps, sink_ref, qn_ref, qr_ref, kn_ref, v_ref, kr4_ref, sq_ref, sk_ref, sv_ref,
                           y_ref, mod_ref, w_ref, g_ref, b_ref, wf1_ref, wf2_ref, wf3_ref,
                           y1_o, wb1_ref, wb2_ref, wb3_ref, o_scr):
    _mla_pairs(qn_ref, qr_ref, [(kn_ref, kr4_ref[...].astype(BF16), v_ref)], o_scr)
    _swa_pairs(sink_ref, sq_ref, [(sk_ref[...], sv_ref[...])], None, o_scr)
    y1_o[...] = _proj_residual_ln(o_scr[...], w_ref, y_ref[...], mod_ref[0], g_ref, b_ref)
    for steps, wf_ref, wb_ref in zip(cast_steps, (wf1_ref, wf2_ref, wf3_ref), (wb1_ref, wb2_ref, wb3_ref)):
        _cast_block(pl.program_id(0), steps, wf_ref, wb_ref)


def _l0_attn_prompt(sink, qn, qr, kn, vm, kr4, sq, sk, sv, y, mods, w_out, ln_g, ln_b, weights_f32):
    spec = lambda w: pl.BlockSpec((SEQ, w), lambda b, *_: (b, 0))
    cast_steps = tuple(_cast_steps(w.shape[0], BATCH) for w in weights_f32)
    casts = [_cast_specs(w.shape, s) for w, s in zip(weights_f32, cast_steps)]
    return pl.pallas_call(
        functools.partial(_l0_attn_prompt_kernel, cast_steps),
        out_shape=[jax.ShapeDtypeStruct((T_P, D), F32)] + [c[1] for c in casts],
        grid_spec=pltpu.PrefetchScalarGridSpec(
            num_scalar_prefetch=1, grid=(BATCH,),
            in_specs=[spec(512), spec(256), spec(512), spec(512), spec(128), spec(512), spec(128), spec(128)]
            + _prompt_epilogue_specs() + [c[0] for c in casts],
            out_specs=[spec(D)] + [c[0] for c in casts],
            scratch_shapes=[pltpu.VMEM((SEQ, D), BF16)]),
        compiler_params=_params("arbitrary"),
        name="l0_attn_prompt",
    )(sink, qn, qr, kn, vm, kr4, sq, sk, sv, y, mods, w_out, ln_g.reshape(1, D), ln_b.reshape(1, D), *weights_f32)


TQ_S = 256


def _l0_attn_sample_kernel(sink_ref, qn_ref, qr_ref, sq_ref, kn_ref, v_ref, kr4_ref, sk_ref, sv_ref,
                           knc_ref, vc_ref, kr4c_ref, skc_ref, svc_ref, o_ref):
    i = pl.program_id(1)
    _mla_pairs(qn_ref, qr_ref,
               [(knc_ref, kr4c_ref[...].astype(BF16), vc_ref), (kn_ref, kr4_ref[...].astype(BF16), v_ref)],
               o_ref)
    span = TQ_S + 2 * SWA_WINDOW
    start = pl.multiple_of(jnp.clip(i * TQ_S - SWA_WINDOW, 0, DEC_SEQ - span), SWA_WINDOW)
    qpos = i * TQ_S + lax.broadcasted_iota(jnp.int32, (TQ_S, span), 0)
    kpos = start + lax.broadcasted_iota(jnp.int32, (TQ_S, span), 1)
    band = jnp.abs(qpos - kpos) <= SWA_WINDOW
    keys = pl.ds(start, span)
    _swa_pairs(sink_ref, sq_ref, [(skc_ref[...], svc_ref[...]), (sk_ref[keys, :], sv_ref[keys, :])],
               [None, band], o_ref)


def _l0_attn_sample(sink, qn, qr, kn, vm, kr4, sq, sk, sv, knc, vc, kr4c, skc, svc):
    nq = DEC_SEQ // TQ_S
    qspec = lambda w: pl.BlockSpec((TQ_S, w), lambda b, i, *_: (T_P // TQ_S + b * nq + i, 0))
    kspec = lambda w: pl.BlockSpec((DEC_SEQ, w), lambda b, i, *_: (T_P // DEC_SEQ + b, 0))
    cspec = lambda w: pl.BlockSpec((PAST, w), lambda b, i, *_: (b, 0))
    return pl.pallas_call(
        _l0_attn_sample_kernel,
        out_shape=jax.ShapeDtypeStruct((T_S, D), BF16),
        grid_spec=pltpu.PrefetchScalarGridSpec(
            num_scalar_prefetch=1, grid=(DEC_BATCH, nq),
            in_specs=[qspec(512), qspec(256), qspec(512),
                      kspec(512), kspec(512), kspec(128), kspec(128), kspec(128),
                      cspec(512), cspec(512), cspec(128), cspec(128), cspec(128)],
            out_specs=pl.BlockSpec((TQ_S, D), lambda b, i, *_: (b * nq + i, 0))),
        compiler_params=_params("arbitrary", "arbitrary"),
        name="l0_attn_sample",
    )(sink, qn, qr, sq, kn, vm, kr4, sk, sv, knc, vc, kr4c, skc, svc)


def _out_ln_sample_kernel(o_ref, w_ref, y_ref, mod_ref, g_ref, b_ref, out_ref):
    out_ref[...] = _proj_residual_ln(o_ref[...], w_ref, y_ref[...], mod_ref[0], g_ref, b_ref)


def _out_ln_sample(o_sample, w_out, y, mods, ln_g, ln_b):
    return pl.pallas_call(
        _out_ln_sample_kernel,
        out_shape=jax.ShapeDtypeStruct((T_S, D), F32),
        grid=(T_S // TM,),
        in_specs=[_row_spec(D), _const_spec((D, D)),
                  pl.BlockSpec((TM, D), lambda i: (NP_TILES + i, 0)),
                  pl.BlockSpec((1, 6, D), lambda i: (1 + i // TILES_PER_SAMPLE, 0, 0)),
                  _const_spec((1, D)), _const_spec((1, D))],
        out_specs=_row_spec(D),
        compiler_params=_params("arbitrary"),
        name="out_proj_ln_sample",
    )(o_sample, w_out, y, mods, ln_g.reshape(1, D), ln_b.reshape(1, D))


def _two_part_specs(width, rows=TM):
    n_prompt = T_P // rows
    return [pl.BlockSpec((rows, width), lambda i, *_: (jnp.minimum(i, n_prompt - 1), 0)),
            pl.BlockSpec((rows, width), lambda i, *_: (jnp.maximum(i - n_prompt, 0), 0))]


def _two_part_rows(i, p_ref, s_ref, rows=TM):
    return jnp.where(i < T_P // rows, p_ref[...], s_ref[...])


FFN_CHUNK = D_FF
assert FFN_CHUNK % MXU_N == 0 and D_FF % FFN_CHUNK == 0
FFN_TM = 256
FFN_CAST_STEPS = 32


def _ffn_ln_kernel(yp_ref, ys_ref, mod_ref, w1_ref, w3_ref, w2_ref, g_ref, b_ref, wfa_ref, wfb_ref,
                   out_ref, wba_ref, wbb_ref):
    i = pl.program_id(0)
    _cast_block(i, FFN_CAST_STEPS, wfa_ref, wba_ref)
    _cast_block(i, FFN_CAST_STEPS, wfb_ref, wbb_ref)
    m = mod_ref[0]
    y = _two_part_rows(i, yp_ref, ys_ref, FFN_TM)
    h = (y * (1.0 + m[4:5]) + m[3:4]).astype(BF16)
    acc = None
    for c in range(D_FF // FFN_CHUNK):
        cols = slice(c * FFN_CHUNK, (c + 1) * FFN_CHUNK)
        a = _dot(h, w1_ref[:, cols])
        g = _dot(h, w3_ref[:, cols])
        part = _dot((_silu(a) * g).astype(BF16), w2_ref[cols, :])
        acc = part if acc is None else acc + part
    out_ref[...] = _layer_norm(ALPHA * y + m[5:6] * acc, g_ref[...], b_ref[...])


def _ffn_ln(y_prompt, y_sample, mods, w1, w3, w2, ln_g, ln_b, wa_f32, wb_f32):
    wa_spec, wa_shape = _cast_specs(wa_f32.shape, FFN_CAST_STEPS)
    wb_spec, wb_shape = _cast_specs(wb_f32.shape, FFN_CAST_STEPS)
    rows = _row_spec(D, FFN_TM)
    return pl.pallas_call(
        _ffn_ln_kernel,
        out_shape=[jax.ShapeDtypeStruct((T, D), F32), wa_shape, wb_shape],
        grid=(T // FFN_TM,),
        in_specs=_two_part_specs(D, FFN_TM)
        + [_mod_spec(FFN_TM), _const_spec((D, D_FF), True), _const_spec((D, D_FF), True),
                  _const_spec((D_FF, D), True), _const_spec((1, D)), _const_spec((1, D)), wa_spec, wb_spec],
        out_specs=[rows, wa_spec, wb_spec],
        compiler_params=_params("arbitrary"),
        name="ffn_ln",
    )(y_prompt, y_sample, mods, w1, w3, w2, ln_g.reshape(1, D), ln_b.reshape(1, D), wa_f32, wb_f32)


def _l1_in_kernel(y_ref, mod_ref, w_ref, q_o, k_o, v_o, kh_o, vh_o):
    i = pl.program_id(0)
    m = mod_ref[0]
    h = (y_ref[...] * (1.0 + m[1:2]) + m[0:1]).astype(BF16)
    z = _dot(h, w_ref[...])
    k = z[:, D:2 * D]
    v = z[:, 2 * D:3 * D]
    q_o[...] = (z[:, 0:D] * HD_QMUL).astype(BF16)
    k_o[...] = k.astype(BF16)
    v_o[...] = v.astype(BF16)

    @pl.when(i < NP_TILES)
    def _():
        for bb in range(TM // SEQ):
            rows = slice(bb * SEQ, (bb + 1) * SEQ)
            for j in range(NA_HEADS // 2):
                cols = slice(j * LANES, (j + 1) * LANES)
                kt = k[rows, cols].T
                vt = v[rows, cols].T
                for hh in range(2):
                    drows = slice(hh * HEAD_DIM, (hh + 1) * HEAD_DIM)
                    kh_o[bb, 2 * j + hh] = kt[drows]
                    vh_o[bb, 2 * j + hh] = vt[drows]


def _l1_in(y, mods, w_in):
    heads = pl.BlockSpec((TM // SEQ, NA_HEADS, HEAD_DIM, SEQ), lambda i: (jnp.minimum(i, NP_TILES - 1), 0, 0, 0))
    return pl.pallas_call(
        _l1_in_kernel,
        out_shape=[jax.ShapeDtypeStruct((T, D), BF16)] * 3
        + [jax.ShapeDtypeStruct((BATCH, NA_HEADS, HEAD_DIM, SEQ), F32)] * 2,
        grid=(NT,),
        in_specs=[_row_spec(D), _mod_spec(), _const_spec((D, 3 * D))],
        out_specs=[_row_spec(D)] * 3 + [heads, heads],
        compiler_params=_params("arbitrary"),
        name="l1_in_proj",
    )(y, mods, w_in)


def _l1_attn_prompt_kernel(q_ref, k_ref, v_ref, y_ref, mod_ref, w_ref, g_ref, b_ref, y1_o, o_scr):
    for j in range(NA_HEADS // 2):
        cols = slice(128 * j, 128 * (j + 1))
        o = _head_pair(q_ref[:, cols], [k_ref[:, cols].astype(BF16)], [v_ref[:, cols].astype(BF16)])
        o_scr[:, cols] = o.astype(BF16)
    y1_o[...] = _proj_residual_ln(o_scr[...], w_ref, y_ref[...], mod_ref[0], g_ref, b_ref)


def _l1_attn_prompt(q, k, v, y, mods, w_out, ln_g, ln_b):
    spec = pl.BlockSpec((SEQ, D), lambda b: (b, 0))
    return pl.pallas_call(
        _l1_attn_prompt_kernel,
        out_shape=jax.ShapeDtypeStruct((T_P, D), F32),
        grid=(BATCH,),
        in_specs=[spec, spec, spec] + _prompt_epilogue_specs(),
        out_specs=spec,
        scratch_shapes=[pltpu.VMEM((SEQ, D), BF16)],
        compiler_params=_params("arbitrary"),
        name="l1_attn_prompt",
    )(q, k, v, y, mods, w_out, ln_g.reshape(1, D), ln_b.reshape(1, D))


NA_ROWS = DEC_SEQ // GRID_W
NA_TILE_ROWS = 4
NA_TQ = NA_TILE_ROWS * GRID_W
NA_SPAN = NA_WIN_ROWS + NA_TILE_ROWS
NA_DR = 2 * NA_WIN_ROWS - 1
COL_SPAN = 2 * GRID_W - 1


def _na_span_start(t):
    first = max(0, min(t * NA_TILE_ROWS - NA_WIN_ROWS // 2, NA_ROWS - NA_WIN_ROWS))
    return min(first, NA_ROWS - NA_SPAN)


def _l1_attn_sample_kernel(q_ref, k_ref, v_ref, kc_ref, vc_ref, ext_ref, o_ref):
    kc = kc_ref[0].astype(BF16)
    vc = vc_ref[0].astype(BF16)
    lo = _lane() < GRID_W
    qcol = lax.broadcasted_iota(jnp.int32, (GRID_W, LANES), 0)
    kcol = lax.broadcasted_iota(jnp.int32, (GRID_W, LANES), 1) % GRID_W
    cs = jnp.clip(qcol - NA_WIN_COLS // 2, 0, GRID_W - NA_WIN_COLS)
    col_ok = jnp.logical_and(kcol >= cs, kcol < cs + NA_WIN_COLS)
    neg = jnp.full((GRID_W, LANES), NEG, F32)

    def bias_tile(hh, d, half):
        x = jnp.broadcast_to(ext_ref[hh, d:d + 1, :], (GRID_W, LANES))
        shift = (LANES - COL_SPAN // 2 + GRID_W * half) % LANES
        return jnp.where(col_ok, pltpu.roll(x, shift, 1, stride=1, stride_axis=0), neg)

    tiles = [[[bias_tile(hh, d, half) for half in range(2)] for d in range(NA_DR)] for hh in range(2)]

    for t in range(NA_ROWS // NA_TILE_ROWS):
        ws = _na_span_start(t)
        keys = slice(ws * GRID_W, (ws + NA_SPAN) * GRID_W)
        kw = k_ref[keys, :].astype(BF16)
        vw = v_ref[keys, :].astype(BF16)
        bias = []
        for hh in range(2):
            rows = []
            for rr in range(NA_TILE_ROWS):
                r = t * NA_TILE_ROWS + rr
                rs = max(0, min(r - NA_WIN_ROWS // 2, NA_ROWS - NA_WIN_ROWS))
                blocks = []
                for u in range(NA_SPAN // 2):
                    halves = []
                    for half in range(2):
                        kr = ws + 2 * u + half
                        ok = rs <= kr < rs + NA_WIN_ROWS
                        halves.append(tiles[hh][kr - r + NA_WIN_ROWS - 1][half] if ok else neg)
                    blocks.append(jnp.where(lo, halves[0], halves[1]))
                rows.append(jnp.concatenate(blocks, axis=1))
            bias.append(jnp.concatenate(rows, axis=0))
        qrows = slice(t * NA_TQ, (t + 1) * NA_TQ)
        o = _head_pair(q_ref[qrows, :], [kw, kc], [vw, vc], biases=[bias, None])
        o_ref[qrows, :] = o.astype(BF16)


def _l1_attn_sample(q, k, v, kc, vc, ext):
    lat = pl.BlockSpec((DEC_SEQ, LANES), lambda j, b: (T_P // DEC_SEQ + b, j))
    ctx = pl.BlockSpec((1, PAST, LANES), lambda j, b: (b, 0, j))
    return pl.pallas_call(
        _l1_attn_sample_kernel,
        out_shape=jax.ShapeDtypeStruct((T_S, D), BF16),
        grid=(NA_HEADS // 2, DEC_BATCH),
        in_specs=[lat, lat, lat, ctx, ctx, pl.BlockSpec((2, NA_DR, LANES), lambda j, b: (j, 0, 0))],
        out_specs=pl.BlockSpec((DEC_SEQ, LANES), lambda j, b: (b, j)),
        compiler_params=_params("arbitrary", "arbitrary"),
        name="l1_attn_sample",
    )(q, k, v, kc, vc, ext)


def _na_bias_rows(rel_bias):
    rb = rel_bias.astype(F32) * LOG2E
    n_lo = GRID_W - 1 - (NA_WIN_COLS - 1)
    n_hi = LANES - n_lo - rb.shape[-1]
    return jnp.concatenate([jnp.repeat(rb[..., :1], n_lo, axis=-1), rb,
                            jnp.repeat(rb[..., -1:], n_hi, axis=-1)], axis=-1)


def _split_bf16(x):
    hi = x.astype(BF16)
    return hi, (x - hi.astype(F32)).astype(BF16)


def _router_kernel(yp_ref, ys_ref, mod_ref, rw_ref, rb_ref, wf_ref, h_o, meta_o, cnt_o, wb_ref, carry_ref):
    i = pl.program_id(0)
    _cast_block(i, NP_TILES, wf_ref, wb_ref)

    @pl.when(i == 0)
    def _():
        carry_ref[...] = jnp.zeros_like(carry_ref)

    m = mod_ref[0]
    h = _two_part_rows(i, yp_ref, ys_ref) * (1.0 + m[4:5]) + m[3:4]
    _store_rows_tiled(h_o, h)
    h_hi, h_lo = _split_bf16(h)
    w_hi, w_lo = _split_bf16(rw_ref[...])
    logits = _dot(jnp.concatenate([h_hi, h_hi, h_lo], axis=1),
                  jnp.concatenate([w_hi, w_lo, w_hi], axis=0)) + rb_ref[...]
    lane = lax.broadcasted_iota(jnp.int32, (TM, LANES), 1).astype(F32)
    m1 = jnp.max(logits, axis=1, keepdims=True)
    i1 = jnp.min(jnp.where(logits == m1, lane, float(LANES)), axis=1, keepdims=True)
    sel1 = lane == i1
    rest = jnp.where(sel1, -jnp.inf, logits)
    m2 = jnp.max(rest, axis=1, keepdims=True)
    i2 = jnp.min(jnp.where(rest == m2, lane, float(LANES)), axis=1, keepdims=True)
    sel2 = lane == i2
    e2 = jnp.exp(m2 - m1)
    w1 = 1.0 / (1.0 + e2)
    w2 = e2 / (1.0 + e2)
    sel = jnp.logical_or(sel1, sel2)
    rr = lax.broadcasted_iota(jnp.int32, (TM, TM), 0)
    cc = lax.broadcasted_iota(jnp.int32, (TM, TM), 1)
    tri = jnp.where(cc < rr, 1.0, 0.0).astype(BF16)
    ahead = _dot(tri, jnp.where(sel, 1.0, 0.0).astype(BF16)) + carry_ref[...]
    r1 = jnp.sum(jnp.where(sel1, ahead, 0.0), axis=1, keepdims=True)
    r2 = jnp.sum(jnp.where(sel2, ahead, 0.0), axis=1, keepdims=True)
    meta = jnp.where(lane == 0, i1, 0.0)
    meta = jnp.where(lane == 1, i2, meta)
    meta = jnp.where(lane == 2, r1, meta)
    meta = jnp.where(lane == 3, r2, meta)
    meta = jnp.where(lane == 4, w1, meta)
    meta = jnp.where(lane == 5, w2, meta)
    meta_o[...] = meta
    carry_ref[...] = carry_ref[...] + jnp.sum(jnp.where(sel, 1.0, 0.0), axis=0, keepdims=True)
    cnt_o[...] = carry_ref[...]


def _router(y_prompt, y_sample, mods, router_w, router_b, w_f32):
    rw = jnp.zeros((D, LANES), F32).at[:, :N_EXPERTS].set(router_w)
    rb = jnp.full((1, LANES), NEG, F32).at[0, :N_EXPERTS].set(router_b)
    w_spec, wb_shape = _cast_specs(w_f32.shape, NP_TILES)
    return pl.pallas_call(
        _router_kernel,
        out_shape=[jax.ShapeDtypeStruct((T * ROW_TILE, LANES), F32),
                   jax.ShapeDtypeStruct((T, LANES), F32),
                   jax.ShapeDtypeStruct((1, LANES), F32),
                   wb_shape],
        grid=(NT,),
        in_specs=_two_part_specs(D) + [_mod_spec(), _const_spec((D, LANES)), _const_spec((1, LANES)), w_spec],
        out_specs=[_row_spec(LANES, TM * ROW_TILE), _row_spec(LANES), _const_spec((1, LANES)), w_spec],
        scratch_shapes=[pltpu.VMEM((1, LANES), F32)],
        compiler_params=_params("arbitrary"),
        name="moe_router",
    )(y_prompt, y_sample, mods, rw, rb, w_f32)


DMA_UNROLL = 8


def _row_copy(src_ref, src_row, dst_ref, dst_row, sem):
    return pltpu.make_async_copy(src_ref.at[pl.ds(pl.multiple_of(src_row * ROW_TILE, ROW_TILE), ROW_TILE)],
                                 dst_ref.at[pl.ds(pl.multiple_of(dst_row * ROW_TILE, ROW_TILE), ROW_TILE)], sem)


def _group_tile_copy(src_ref, dst_ref, tile, sem):
    start = pl.multiple_of(tile * (TG * ROW_TILE), TG * ROW_TILE)
    return pltpu.make_async_copy(src_ref, dst_ref.at[pl.ds(start, TG * ROW_TILE)], sem)


def _dispatch_kernel(pos_ref, last_ref, na_ref, h_ref, xg_ref, zero_ref, sem, zsem):
    i = pl.program_id(0)

    @pl.when(i == 0)
    def _():
        zero_ref[...] = jnp.zeros_like(zero_ref)
        for e in range(N_EXPERTS):
            @pl.when(last_ref[e] >= 0)
            def _():
                _group_tile_copy(zero_ref, xg_ref, last_ref[e], zsem).start()

        def start_unused(g, carry):
            _group_tile_copy(zero_ref, xg_ref, g, zsem).start()
            return carry

        def wait_one(g, carry):
            _group_tile_copy(zero_ref, xg_ref, 0, zsem).wait()
            return carry

        lax.fori_loop(na_ref[0], NG, start_unused, 0)
        lax.fori_loop(0, na_ref[1], wait_one, 0)

    def issue(r, carry):
        t = i * TM + r
        _row_copy(h_ref, r, xg_ref, pos_ref[2 * t], sem).start(priority=0)
        _row_copy(h_ref, r, xg_ref, pos_ref[2 * t + 1], sem).start(priority=1)
        return carry

    lax.fori_loop(0, TM, issue, 0, unroll=DMA_UNROLL)
    for _ in range(2):
        pltpu.make_async_copy(h_ref, xg_ref.at[pl.ds(0, TM * ROW_TILE)], sem).wait()


def _dispatch(pos, last_tile, tile_counts, h_tiled):
    return pl.pallas_call(
        _dispatch_kernel,
        out_shape=jax.ShapeDtypeStruct((P_ROWS * ROW_TILE, LANES), F32),
        grid_spec=pltpu.PrefetchScalarGridSpec(
            num_scalar_prefetch=3, grid=(NT,),
            in_specs=[pl.BlockSpec((TM * ROW_TILE, LANES), lambda i, *_: (i, 0))],
            out_specs=pl.BlockSpec(memory_space=pl.ANY),
            scratch_shapes=[pltpu.VMEM((TG * ROW_TILE, LANES), F32), pltpu.SemaphoreType.DMA(()),
                            pltpu.SemaphoreType.DMA(())]),
        compiler_params=_params("arbitrary"),
        name="moe_dispatch",
    )(pos, last_tile, tile_counts, h_tiled)


def _expert_ffn_kernel(te_ref, na_ref, x_ref, w1_ref, w3_ref, w2_ref, o_ref):
    g = pl.program_id(0)

    @pl.when(g < na_ref[0])
    def _():
        x = _load_rows_tiled(x_ref, TG).astype(BF16)
        acc = None
        for c in range(E_FF // F_CHUNK):
            cols = slice(c * F_CHUNK, (c + 1) * F_CHUNK)
            a = _dot(x, w1_ref[0, :, cols])
            b = _dot(x, w3_ref[0, :, cols])
            part = _dot((_silu(a) * b).astype(BF16), w2_ref[0, cols, :])
            acc = part if acc is None else acc + part
        _store_rows_tiled(o_ref, acc)

    @pl.when(g >= na_ref[0])
    def _():
        o_ref[...] = jnp.zeros_like(o_ref)


def _expert_ffn(tile_expert, n_active, xg, w1, w3, w2):
    rows = pl.BlockSpec((TG * ROW_TILE, LANES), lambda g, te, na: (g, 0))
    w_up = pl.BlockSpec((1, D, E_FF), lambda g, te, na: (te[g], 0, 0))
    w_dn = pl.BlockSpec((1, E_FF, D), lambda g, te, na: (te[g], 0, 0))
    return pl.pallas_call(
        _expert_ffn_kernel,
        out_shape=jax.ShapeDtypeStruct((P_ROWS * ROW_TILE, LANES), F32),
        grid_spec=pltpu.PrefetchScalarGridSpec(
            num_scalar_prefetch=2, grid=(NG,),
            in_specs=[rows, w_up, w_up, w_dn],
            out_specs=rows),
        compiler_params=_params("arbitrary"),
        name="moe_expert_ffn",
    )(tile_expert, n_active, xg, w1, w3, w2)


def _combine_ln_kernel(pos_ref, yp_ref, ys_ref, mod_ref, meta_ref, g_ref, b_ref, eo_ref, outp_ref, outs_ref,
                       buf1, buf2, sem):
    i = pl.program_id(0)
    slot = i % 2

    def fetch(tile, into):
        def issue(r, carry):
            t = tile * TM + r
            _row_copy(eo_ref, pos_ref[2 * t], buf1.at[into], r, sem.at[into]).start(priority=0)
            _row_copy(eo_ref, pos_ref[2 * t + 1], buf2.at[into], r, sem.at[into]).start(priority=1)
            return carry

        lax.fori_loop(0, TM, issue, 0, unroll=DMA_UNROLL)

    @pl.when(i == 0)
    def _():
        fetch(0, 0)

    @pl.when(i + 1 < NT)
    def _():
        fetch(i + 1, 1 - slot)

    for buf in (buf1, buf2):
        pltpu.make_async_copy(eo_ref.at[pl.ds(0, TM * ROW_TILE)], buf.at[slot], sem.at[slot]).wait()
    m = mod_ref[0]
    meta = meta_ref[...]
    f = (meta[:, 4:5] * _load_rows_tiled(buf1.at[slot], TM)
         + meta[:, 5:6] * _load_rows_tiled(buf2.at[slot], TM))
    out = _layer_norm(ALPHA * _two_part_rows(i, yp_ref, ys_ref) + m[5:6] * f, g_ref[...], b_ref[...])

    @pl.when(i < NP_TILES)
    def _():
        outp_ref[...] = out

    @pl.when(i >= NP_TILES)
    def _():
        outs_ref[...] = out


def _combine_ln(pos, y_prompt, y_sample, mods, meta, ln_g, ln_b, eo):
    return pl.pallas_call(
        _combine_ln_kernel,
        out_shape=[jax.ShapeDtypeStruct((T_P, D), F32), jax.ShapeDtypeStruct((T_S, D), F32)],
        grid_spec=pltpu.PrefetchScalarGridSpec(
            num_scalar_prefetch=1, grid=(NT,),
            in_specs=_two_part_specs(D)
            + [pl.BlockSpec((1, 6, D), lambda i, *_: (_cond_of_tile(i), 0, 0)),
                      pl.BlockSpec((TM, LANES), lambda i, *_: (i, 0)),
                      pl.BlockSpec((1, D), lambda i, *_: (0, 0)),
                      pl.BlockSpec((1, D), lambda i, *_: (0, 0)),
                      pl.BlockSpec(memory_space=pl.ANY)],
            out_specs=[_prompt_tile_spec(D), _sample_tile_spec(D)],
            scratch_shapes=[pltpu.VMEM((2, TM * ROW_TILE, LANES), F32),
                            pltpu.VMEM((2, TM * ROW_TILE, LANES), F32),
                            pltpu.SemaphoreType.DMA((2,))]),
        compiler_params=_params("arbitrary"),
        name="moe_combine_ln",
    )(pos, y_prompt, y_sample, mods, meta, ln_g.reshape(1, D), ln_b.reshape(1, D), eo)


def _moe_ln(y_prompt, y_sample, mods, router_w, router_b, w1, w3, w2_f32, ln_g, ln_b):
    h_tiled, meta, counts, w2 = _router(y_prompt, y_sample, mods, router_w, router_b,
                                        w2_f32.reshape(N_EXPERTS * E_FF, D))
    cnt = counts[0, :N_EXPERTS].astype(jnp.int32)
    tiles = (cnt + TG - 1) // TG
    tile_end = jnp.cumsum(tiles)
    offs = (tile_end - tiles) * TG
    expert = meta[:, 0:2].astype(jnp.int32)
    chosen = expert[:, :, None] == jnp.arange(N_EXPERTS)[None, None, :]
    pos = jnp.sum(jnp.where(chosen, offs[None, None, :], 0), axis=-1) + meta[:, 2:4].astype(jnp.int32)
    pos = pos.reshape(2 * T)
    tile_expert = jnp.sum((jnp.arange(NG)[:, None] >= tile_end[None, :]).astype(jnp.int32), axis=1)
    tile_expert = jnp.minimum(tile_expert, N_EXPERTS - 1)
    n_active = tile_end[-1:].astype(jnp.int32)
    last_tile = jnp.where(tiles > 0, tile_end - 1, -1).astype(jnp.int32)
    n_zeroed = jnp.sum((tiles > 0).astype(jnp.int32)) + NG - tile_end[-1]
    tile_counts = jnp.stack([tile_end[-1], n_zeroed]).astype(jnp.int32)
    xg = _dispatch(pos, last_tile, tile_counts, h_tiled)
    eo = _expert_ffn(tile_expert, n_active, xg, w1, w3, w2.reshape(N_EXPERTS, E_FF, D))
    return _combine_ln(pos, y_prompt, y_sample, mods, meta, ln_g, ln_b, eo)


def _l0_weight_layouts(w_in, w_q_up, w_kv_up):
    a, b, c = MLA_Q_LORA, MLA_Q_LORA + MLA_KV_LORA, MLA_Q_LORA + MLA_KV_LORA + MLA_ROPE
    k_rope = w_in[:, b:c]
    w_in_r = jnp.concatenate([w_in[:, :b], w_in[:, c:], k_rope, k_rope, k_rope, k_rope], axis=1)
    wq = w_q_up.reshape(MLA_Q_LORA, MLA_HEADS, MLA_NOPE + MLA_ROPE)
    w_q_up_r = jnp.concatenate([wq[:, :, :MLA_NOPE].reshape(MLA_Q_LORA, -1),
                                wq[:, :, MLA_NOPE:].reshape(MLA_Q_LORA, -1)], axis=1)
    wkv = w_kv_up.reshape(MLA_KV_LORA, MLA_HEADS, MLA_NOPE + MLA_V)
    w_kv_up_r = jnp.concatenate([wkv[:, :, :MLA_NOPE].reshape(MLA_KV_LORA, -1),
                                 wkv[:, :, MLA_NOPE:].reshape(MLA_KV_LORA, -1)], axis=1)
    return w_in_r.astype(BF16), w_q_up_r.astype(BF16), w_kv_up_r.astype(BF16)


def kernel(x_prompt, x_sample, cache_l0_mla_ckv, cache_l0_mla_krope, cache_l0_swa_k, cache_l0_swa_v,
           cache_l1_na_k, cache_l1_na_v, c, c_ctx,
           l0_ada_w, l0_ada_b, l0_w_in, l0_mla_q_norm, l0_mla_w_q_up, l0_mla_kv_norm, l0_mla_w_kv_up,
           l0_swa_sink, l0_w_out, l0_ln1_g, l0_ln1_b, l0_ffn_w1, l0_ffn_w3, l0_ffn_w2, l0_ln2_g, l0_ln2_b,
           l1_ada_w, l1_ada_b, l1_w_in, l1_na_rel_bias, l1_w_out, l1_ln1_g, l1_ln1_b,
           l1_moe_router_w, l1_moe_router_b, l1_moe_w1, l1_moe_w3, l1_moe_w2, l1_ln2_g, l1_ln2_b):
    cond = jnp.concatenate([c_ctx[None, :], c, jnp.zeros((8 - N_COND, D), F32)], axis=0)

    mods = _adaln(cond, l0_ada_w, l0_ada_b)
    w_in_r, w_q_up_r, w_kv_up_r = _l0_weight_layouts(l0_w_in, l0_mla_w_q_up, l0_mla_w_kv_up)
    y, qn, qr, kn, vm, kr4, sq, sk, sv, ckv, krope_t, sk_t, sv_t = _l0_in(
        x_prompt.reshape(T_P, D), x_sample.reshape(T_S, D), mods, w_in_r, l0_mla_q_norm, w_q_up_r,
        l0_mla_kv_norm, w_kv_up_r, _rope_tables(MLA_ROPE, 256), _rope_tables(HEAD_DIM, 512))
    knc, vc = _kv_up(cache_l0_mla_ckv.reshape(DEC_BATCH * PAST, MLA_KV_LORA), w_kv_up_r)
    kr4c = jnp.tile(cache_l0_mla_krope.reshape(DEC_BATCH * PAST, MLA_ROPE), (1, 4))
    skc = cache_l0_swa_k.reshape(DEC_BATCH * PAST, SWA_KV_HEADS * HEAD_DIM)
    svc = cache_l0_swa_v.reshape(DEC_BATCH * PAST, SWA_KV_HEADS * HEAD_DIM)
    w_out = l0_w_out.astype(BF16)
    y_p, ffn_w1, ffn_w3, ffn_w2 = _l0_attn_prompt(l0_swa_sink, qn, qr, kn, vm, kr4, sq, sk, sv,
                                                  y, mods, w_out, l0_ln1_g, l0_ln1_b,
                                                  [l0_ffn_w1, l0_ffn_w3, l0_ffn_w2])
    o_s = _l0_attn_sample(l0_swa_sink, qn, qr, kn, vm, kr4, sq, sk, sv, knc, vc, kr4c, skc, svc)
    y_s = _out_ln_sample(o_s, w_out, y, mods, l0_ln1_g, l0_ln1_b)
    y, moe_w1, moe_w3 = _ffn_ln(y_p, y_s, mods, ffn_w1, ffn_w3, ffn_w2, l0_ln2_g, l0_ln2_b,
                                l1_moe_w1.reshape(N_EXPERTS * D, E_FF), l1_moe_w3.reshape(N_EXPERTS * D, E_FF))
    new_ckv = ckv.reshape(BATCH, SEQ, MLA_KV_LORA)
    new_krope = jnp.transpose(krope_t, (0, 2, 1))
    new_sk = jnp.transpose(sk_t, (0, 3, 1, 2))
    new_sv = jnp.transpose(sv_t, (0, 3, 1, 2))

    mods = _adaln(cond, l1_ada_w, l1_ada_b)
    q, k, v, k_heads, v_heads = _l1_in(y, mods, l1_w_in.astype(BF16))
    w_out = l1_w_out.astype(BF16)
    y_p = _l1_attn_prompt(q, k, v, y, mods, w_out, l1_ln1_g, l1_ln1_b)
    o_s = _l1_attn_sample(q, k, v, cache_l1_na_k.reshape(DEC_BATCH, PAST, D),
                          cache_l1_na_v.reshape(DEC_BATCH, PAST, D), _na_bias_rows(l1_na_rel_bias))
    y_s = _out_ln_sample(o_s, w_out, y, mods, l1_ln1_g, l1_ln1_b)
    y_p, y_s = _moe_ln(y_p, y_s, mods, l1_moe_router_w, l1_moe_router_b, moe_w1.reshape(N_EXPERTS, D, E_FF),
                       moe_w3.reshape(N_EXPERTS, D, E_FF), l1_moe_w2, l1_ln2_g, l1_ln2_b)
    new_k = jnp.transpose(k_heads, (0, 3, 1, 2))
    new_v = jnp.transpose(v_heads, (0, 3, 1, 2))

    return (y_p.reshape(BATCH, SEQ, D), y_s.reshape(DEC_BATCH, DEC_SEQ, D),
            new_ckv, new_krope, new_sk, new_sv, new_k, new_v)
```

```python
import functools

import jax
import jax.numpy as jnp
import numpy as np
from jax import lax
from jax.experimental import pallas as pl
from jax.experimental.pallas import tpu as pltpu

F32 = jnp.float32
BF16 = jnp.bfloat16

D = 1024
BATCH, SEQ = 32, 256
DEC_BATCH, DEC_SEQ = 2, 1024
PAST = 256
GRID_W = 64
T_P = BATCH * SEQ
T_S = DEC_BATCH * DEC_SEQ
T = T_P + T_S
N_COND = 1 + DEC_BATCH

MLA_HEADS, MLA_Q_LORA, MLA_KV_LORA, MLA_NOPE, MLA_ROPE, MLA_V = 8, 384, 256, 64, 32, 64
SWA_HEADS, SWA_KV_HEADS, SWA_WINDOW, HEAD_DIM = 8, 2, 128, 64
NA_HEADS, NA_WIN_ROWS, NA_WIN_COLS = 16, 8, 16
D_FF, N_EXPERTS, E_FF = 2816, 8, 3584
ROPE_THETA = 10000.0
LN_EPS, RMS_EPS = 1e-5, 1e-6
NEG = -1e30
ALPHA = 4.0 ** 0.25

LANES = 128
TM = 512
NT = T // TM
NP_TILES = T_P // TM
TILES_PER_SAMPLE = DEC_SEQ // TM
TG = 256
P_ROWS = 2 * T + N_EXPERTS * TG
NG = P_ROWS // TG
MXU_N = 256
F_CHUNK = 1792
assert F_CHUNK % MXU_N == 0 and E_FF % F_CHUNK == 0
VMEM_LIMIT = 56 * 1024 * 1024


def _params(*sem):
    return pltpu.CompilerParams(dimension_semantics=sem, vmem_limit_bytes=VMEM_LIMIT)


def _const_spec(shape, single_buffer=False):
    if single_buffer:
        return pl.BlockSpec(shape, lambda *_: (0,) * len(shape), pipeline_mode=pl.Buffered(1))
    return pl.BlockSpec(shape, lambda *_: (0,) * len(shape))


def _cast_specs(shape, steps):
    rows, width = shape
    spec = pl.BlockSpec((rows // steps, width), lambda i, *_: (jnp.minimum(i, steps - 1), 0))
    return spec, jax.ShapeDtypeStruct(shape, BF16)


def _cast_block(i, steps, src_ref, dst_ref):
    @pl.when(i < steps)
    def _():
        dst_ref[...] = src_ref[...].astype(BF16)


ROW_TILE = D // LANES


def _store_rows_tiled(ref, x):
    n = x.shape[0]
    for k in range(ROW_TILE):
        ref[pl.ds(k, n, stride=ROW_TILE), :] = x[:, k * LANES:(k + 1) * LANES]


def _load_rows_tiled(ref, n):
    return jnp.concatenate([ref[pl.ds(k, n, stride=ROW_TILE), :] for k in range(ROW_TILE)], axis=1)


def _cond_of_tile(i, rows=TM):
    per_sample = DEC_SEQ // rows
    return jnp.maximum((i - (T_P // rows - per_sample)) // per_sample, 0)


def _mod_spec(rows=TM):
    return pl.BlockSpec((1, 6, D), lambda i: (_cond_of_tile(i, rows), 0, 0))


def _row_spec(width, rows=TM):
    return pl.BlockSpec((rows, width), lambda i: (i, 0))


def _dot(a, b):
    return jnp.dot(a, b, preferred_element_type=F32)


def _dot_nt(a, b):
    return lax.dot_general(a, b, (((1,), (1,)), ((), ())), preferred_element_type=F32)


def _layer_norm(r, g, b):
    mu = jnp.mean(r, axis=-1, keepdims=True)
    d = r - mu
    var = jnp.mean(d * d, axis=-1, keepdims=True)
    return d * lax.rsqrt(var + LN_EPS) * g + b


def _rms_norm(x, g):
    return x * lax.rsqrt(jnp.mean(x * x, axis=-1, keepdims=True) + RMS_EPS) * g


def _silu(x):
    return x * jax.nn.sigmoid(x)


def _ada_kernel(c_ref, w_ref, b_ref, o_ref):
    s = _silu(c_ref[...]).astype(BF16)
    o_ref[...] = _dot(s, w_ref[...].astype(BF16)) + b_ref[...]


def _adaln(cond, ada_w, ada_b):
    nb = 1536
    out = pl.pallas_call(
        _ada_kernel,
        out_shape=jax.ShapeDtypeStruct((8, 6 * D), F32),
        grid=(6 * D // nb,),
        in_specs=[_const_spec((8, D)), pl.BlockSpec((D, nb), lambda j: (0, j)),
                  pl.BlockSpec((1, nb), lambda j: (0, j))],
        out_specs=pl.BlockSpec((8, nb), lambda j: (0, j)),
        compiler_params=_params("arbitrary"),
        name="adaln",
    )(cond, ada_w, ada_b.reshape(1, 6 * D))
    return out[:N_COND].reshape(N_COND, 6, D)


def _rope_tables(head_dim, width):
    half = head_dim // 2
    nf = half // 2
    lane = np.arange(width)
    d = lane % head_dim
    dd = d % half
    f = dd % nf
    inv = np.float32(ROPE_THETA) ** (-f.astype(np.float32) / np.float32(nf))
    t = np.arange(DEC_SEQ)
    pos = np.where((d // half)[None, :] == 0, (t // GRID_W)[:, None], (t % GRID_W)[:, None])
    ang = pos.astype(np.float32) * inv[None, :].astype(np.float32)
    cos, sin = np.cos(ang), np.sin(ang)
    first = (dd < nf)[None, :]
    zero = np.float32(0.0)
    return (jnp.asarray(cos, F32), jnp.asarray(np.where(first, -sin, zero), F32),
            jnp.asarray(np.where(first, zero, sin), F32))


def _rope(x, cos, sin_up, sin_dn, nf):
    w = x.shape[-1]
    return x * cos + pltpu.roll(x, w - nf, 1) * sin_up + pltpu.roll(x, nf, 1) * sin_dn


L0_COLS = MLA_Q_LORA + MLA_KV_LORA + 512 + 128 + 128 + 128


def _l0_in_kernel(xp_ref, xs_ref, mod_ref, win_ref, qn_ref, wq_ref, kvn_ref, wkv_ref,
                  c8_ref, su8_ref, sd8_ref, c16_ref, su16_ref, sd16_ref,
                  y_o, qnope_o, qrope_o, knope_o, vmla_o, kr4_o, sq_o, sk_o, sv_o,
                  ckv_o, krt_o, skt_o, svt_o):
    i = pl.program_id(0)
    m = mod_ref[0]
    x = jnp.where(i < NP_TILES, xp_ref[...], xs_ref[...])
    y_o[...] = x
    h = (x * (1.0 + m[1:2]) + m[0:1]).astype(BF16)
    z = _dot(h, win_ref[...])
    q_lat = z[:, 0:384]
    kv_lat = z[:, 384:640]
    sq = z[:, 640:1152]
    sk = z[:, 1152:1280]
    sv = z[:, 1280:1408]
    kr4 = z[:, 1408:1536]
    q = _dot(_rms_norm(q_lat, qn_ref[...]).astype(BF16), wq_ref[...])
    c_kv = _rms_norm(kv_lat, kvn_ref[...])
    kv = _dot(c_kv.astype(BF16), wkv_ref[...])
    q = q * MLA_QMUL
    sq = sq * HD_QMUL
    qnope_o[...] = q[:, 0:512].astype(BF16)
    knope_o[...] = kv[:, 0:512].astype(BF16)
    vmla_o[...] = kv[:, 512:1024].astype(BF16)
    sv_o[...] = sv
    q_rope = q[:, 512:768]

    @pl.when(i < NP_TILES)
    def _():
        qrope_o[...] = q_rope.astype(BF16)
        kr4_o[...] = kr4
        sq_o[...] = sq.astype(BF16)
        sk_o[...] = sk
        ckv_o[...] = c_kv
        for bb in range(TM // SEQ):
            rows = slice(bb * SEQ, (bb + 1) * SEQ)
            krt_o[bb] = kr4[rows].T[:MLA_ROPE]
            skt = sk[rows].T
            svt = sv[rows].T
            for g in range(SWA_KV_HEADS):
                skt_o[bb, g] = skt[g * HEAD_DIM:(g + 1) * HEAD_DIM]
                svt_o[bb, g] = svt[g * HEAD_DIM:(g + 1) * HEAD_DIM]

    @pl.when(i >= NP_TILES)
    def _():
        c8, su8, sd8 = c8_ref[...], su8_ref[...], sd8_ref[...]
        c16, su16, sd16 = c16_ref[...], su16_ref[...], sd16_ref[...]
        qrope_o[...] = _rope(q_rope, c8, su8, sd8, 8).astype(BF16)
        kr4_o[...] = _rope(kr4, c8[:, :128], su8[:, :128], sd8[:, :128], 8)
        sq_o[...] = _rope(sq, c16, su16, sd16, 16).astype(BF16)
        sk_o[...] = _rope(sk, c16[:, :128], su16[:, :128], sd16[:, :128], 16)


def _prompt_tile_spec(width):
    return pl.BlockSpec((TM, width), lambda i, *_: (jnp.minimum(i, NP_TILES - 1), 0))


def _sample_tile_spec(width):
    return pl.BlockSpec((TM, width), lambda i, *_: (jnp.maximum(i - NP_TILES, 0), 0))


def _l0_in(x_prompt, x_sample, mods, w_in_r, q_norm, w_q_up_r, kv_norm, w_kv_up_r, tabs8, tabs16):
    def tab_spec(width):
        return pl.BlockSpec((TM, width), lambda i: (jnp.maximum(i - NP_TILES, 0) % TILES_PER_SAMPLE, 0))

    out_shape = [
        jax.ShapeDtypeStruct((T, D), F32),
        jax.ShapeDtypeStruct((T, 512), BF16),
        jax.ShapeDtypeStruct((T, 256), BF16),
        jax.ShapeDtypeStruct((T, 512), BF16),
        jax.ShapeDtypeStruct((T, 512), BF16),
        jax.ShapeDtypeStruct((T, 128), F32),
        jax.ShapeDtypeStruct((T, 512), BF16),
        jax.ShapeDtypeStruct((T, 128), F32),
        jax.ShapeDtypeStruct((T, 128), F32),
    ]
    per_tile = TM // SEQ
    prompt_block = lambda *dims: pl.BlockSpec((per_tile,) + dims,
                                              lambda i: (jnp.minimum(i, NP_TILES - 1),) + (0,) * len(dims))
    cache_shape = [
        jax.ShapeDtypeStruct((T_P, MLA_KV_LORA), F32),
        jax.ShapeDtypeStruct((BATCH, MLA_ROPE, SEQ), F32),
        jax.ShapeDtypeStruct((BATCH, SWA_KV_HEADS, HEAD_DIM, SEQ), F32),
        jax.ShapeDtypeStruct((BATCH, SWA_KV_HEADS, HEAD_DIM, SEQ), F32),
    ]
    cache_specs = [_prompt_tile_spec(MLA_KV_LORA), prompt_block(MLA_ROPE, SEQ),
                   prompt_block(SWA_KV_HEADS, HEAD_DIM, SEQ), prompt_block(SWA_KV_HEADS, HEAD_DIM, SEQ)]
    return pl.pallas_call(
        _l0_in_kernel,
        out_shape=out_shape + cache_shape,
        grid=(NT,),
        in_specs=[_prompt_tile_spec(D), _sample_tile_spec(D), _mod_spec(), _const_spec((D, L0_COLS)),
                  _const_spec((1, MLA_Q_LORA)), _const_spec((MLA_Q_LORA, 768)),
                  _const_spec((1, MLA_KV_LORA)), _const_spec((MLA_KV_LORA, 1024)),
                  tab_spec(256), tab_spec(256), tab_spec(256),
                  tab_spec(512), tab_spec(512), tab_spec(512)],
        out_specs=[_row_spec(s.shape[1]) for s in out_shape] + cache_specs,
        compiler_params=_params("arbitrary"),
        name="l0_in_proj",
    )(x_prompt, x_sample, mods, w_in_r, q_norm.reshape(1, -1), w_q_up_r, kv_norm.reshape(1, -1), w_kv_up_r,
      *tabs8, *tabs16)


def _kv_up_kernel(c_ref, w_ref, k_o, v_o):
    kv = _dot(c_ref[...].astype(BF16), w_ref[...])
    k_o[...] = kv[:, 0:512].astype(BF16)
    v_o[...] = kv[:, 512:1024].astype(BF16)


def _kv_up(ckv, w_kv_up_r):
    n = ckv.shape[0]
    return pl.pallas_call(
        _kv_up_kernel,
        out_shape=[jax.ShapeDtypeStruct((n, 512), BF16)] * 2,
        grid=(1,),
        in_specs=[_const_spec((n, MLA_KV_LORA)), _const_spec((MLA_KV_LORA, 1024))],
        out_specs=[_const_spec((n, 512))] * 2,
        compiler_params=_params("arbitrary"),
        name="l0_ctx_kv_up",
    )(ckv, w_kv_up_r)


def _lane():
    return lax.broadcasted_iota(jnp.int32, (1, LANES), 1)


def _attend(scores, values, extra_logit=None):
    m = jnp.max(scores[0], axis=1, keepdims=True)
    for s in scores[1:]:
        m = jnp.maximum(m, jnp.max(s, axis=1, keepdims=True))
    if extra_logit is not None:
        m = jnp.maximum(m, extra_logit)
    den = None
    acc = None
    for s, v in zip(scores, values):
        e = jnp.exp2(s - m)
        d = jnp.sum(e, axis=1, keepdims=True)
        a = _dot(e.astype(BF16), v)
        den = d if den is None else den + d
        acc = a if acc is None else acc + a
    if extra_logit is not None:
        den = den + jnp.exp2(extra_logit - m)
    return acc / den


def _head_pair(q2, keys, values, masks=None, biases=None, sinks=None, q_extra=None):
    lo = _lane() < 64
    zero = jnp.zeros_like(q2)
    outs = []
    for hh in range(2):
        qm = jnp.where(lo if hh == 0 else jnp.logical_not(lo), q2, zero)
        if q_extra is not None:
            qm = jnp.concatenate([qm, q_extra[hh]], axis=1)
        scores = []
        for n, k in enumerate(keys):
            s = _dot_nt(qm, k)
            if biases is not None and biases[n] is not None:
                s = s + biases[n][hh]
            if masks is not None and masks[n] is not None:
                s = jnp.where(masks[n], s, NEG)
            scores.append(s)
        outs.append(_attend(scores, values, None if sinks is None else sinks[hh]))
    return jnp.where(lo, outs[0], outs[1])


def _dup_halves(x):
    lo = _lane() < 64
    sw = pltpu.roll(x, 64, 1)
    return jnp.where(lo, x, sw), jnp.where(lo, sw, x)


LOG2E = 1.4426950408889634
MLA_QMUL = (MLA_NOPE + MLA_ROPE) ** -0.5 * LOG2E
HD_QMUL = HEAD_DIM ** -0.5 * LOG2E


def _mla_pairs(qn_ref, qr_ref, key_sets, o_ref):
    lane = _lane()
    for j in range(MLA_HEADS // 2):
        cols = slice(128 * j, 128 * (j + 1))
        qr = qr_ref[:, 128 * (j // 2):128 * (j // 2 + 1)]
        zero = jnp.zeros_like(qr)
        q_extra = [jnp.where((lane // MLA_ROPE) == ((2 * j + hh) % 4), qr, zero) for hh in range(2)]
        keys = [jnp.concatenate([kn[:, cols], kr4], axis=1) for kn, kr4, _ in key_sets]
        values = [v[:, cols] for _, _, v in key_sets]
        o = _head_pair(qn_ref[:, cols], keys, values, q_extra=q_extra)
        o_ref[:, cols] = o.astype(BF16)


def _swa_pairs(sink_ref, sq_ref, key_sets, masks, o_ref):
    kd = [[a.astype(BF16) for a in _dup_halves(k)] for k, _ in key_sets]
    vd = [[a.astype(BF16) for a in _dup_halves(v)] for _, v in key_sets]
    for g in range(SWA_KV_HEADS):
        for u in range(2):
            c = 2 * g + u
            cols = slice(128 * c, 128 * (c + 1))
            sinks = [sink_ref[2 * c + hh] * LOG2E for hh in range(2)]
            o = _head_pair(sq_ref[:, cols], [k[g] for k in kd], [v[g] for v in vd],
                           masks=masks, sinks=sinks)
            o_ref[:, 512 + 128 * c:512 + 128 * (c + 1)] = o.astype(BF16)


def _cast_steps(rows, max_steps):
    steps = max_steps
    while rows % (16 * steps):
        steps //= 2
    return steps


def _proj_residual_ln(o, w_ref, y, m, g_ref, b_ref):
    return _layer_norm(ALPHA * y + m[2:3] * _dot(o, w_ref[...]), g_ref[...], b_ref[...])


def _prompt_epilogue_specs():
    return [pl.BlockSpec((SEQ, D), lambda b, *_: (b, 0)), pl.BlockSpec((1, 6, D), lambda b, *_: (0, 0, 0)),
            pl.BlockSpec((D, D), lambda b, *_: (0, 0)), pl.BlockSpec((1, D), lambda b, *_: (0, 0)),
            pl.BlockSpec((1, D), lambda b, *_: (0, 0))]


def _l0_attn_prompt_kernel(cast_steps, sink_ref, qn_ref, qr_ref, kn_ref, v_ref, kr4_ref, sq_ref, sk_ref, sv_ref,
                           y_ref, mod_ref, w_ref, g_ref, b_ref, wf1_ref, wf2_ref, wf3_ref,
                           y1_o, wb1_ref, wb2_ref, wb3_ref, o_scr):
    _mla_pairs(qn_ref, qr_ref, [(kn_ref, kr4_ref[...].astype(BF16), v_ref)], o_scr)
    _swa_pairs(sink_ref, sq_ref, [(sk_ref[...], sv_ref[...])], None, o_scr)
    y1_o[...] = _proj_residual_ln(o_scr[...], w_ref, y_ref[...], mod_ref[0], g_ref, b_ref)
    for steps, wf_ref, wb_ref in zip(cast_steps, (wf1_ref, wf2_ref, wf3_ref), (wb1_ref, wb2_ref, wb3_ref)):
        _cast_block(pl.program_id(0), steps, wf_ref, wb_ref)


def _l0_attn_prompt(sink, qn, qr, kn, vm, kr4, sq, sk, sv, y, mods, w_out, ln_g, ln_b, weights_f32):
    spec = lambda w: pl.BlockSpec((SEQ, w), lambda b, *_: (b, 0))
    cast_steps = tuple(_cast_steps(w.shape[0], BATCH) for w in weights_f32)
    casts = [_cast_specs(w.shape, s) for w, s in zip(weights_f32, cast_steps)]
    return pl.pallas_call(
        functools.partial(_l0_attn_prompt_kernel, cast_steps),
        out_shape=[jax.ShapeDtypeStruct((T_P, D), F32)] + [c[1] for c in casts],
        grid_spec=pltpu.PrefetchScalarGridSpec(
            num_scalar_prefetch=1, grid=(BATCH,),
            in_specs=[spec(512), spec(256), spec(512), spec(512), spec(128), spec(512), spec(128), spec(128)]
            + _prompt_epilogue_specs() + [c[0] for c in casts],
            out_specs=[spec(D)] + [c[0] for c in casts],
            scratch_shapes=[pltpu.VMEM((SEQ, D), BF16)]),
        compiler_params=_params("arbitrary"),
        name="l0_attn_prompt",
    )(sink, qn, qr, kn, vm, kr4, sq, sk, sv, y, mods, w_out, ln_g.reshape(1, D), ln_b.reshape(1, D), *weights_f32)


TQ_S = 256


def _l0_attn_sample_kernel(sink_ref, qn_ref, qr_ref, sq_ref, kn_ref, v_ref, kr4_ref, sk_ref, sv_ref,
                           knc_ref, vc_ref, kr4c_ref, skc_ref, svc_ref, wf_ref, o_ref, wb_ref):
    i = pl.program_id(1)
    wb_ref[...] = wf_ref[...].astype(BF16)
    _mla_pairs(qn_ref, qr_ref,
               [(knc_ref, kr4c_ref[...].astype(BF16), vc_ref), (kn_ref, kr4_ref[...].astype(BF16), v_ref)],
               o_ref)
    span = TQ_S + 2 * SWA_WINDOW
    start = pl.multiple_of(jnp.clip(i * TQ_S - SWA_WINDOW, 0, DEC_SEQ - span), SWA_WINDOW)
    qpos = i * TQ_S + lax.broadcasted_iota(jnp.int32, (TQ_S, span), 0)
    kpos = start + lax.broadcasted_iota(jnp.int32, (TQ_S, span), 1)
    band = jnp.abs(qpos - kpos) <= SWA_WINDOW
    keys = pl.ds(start, span)
    _swa_pairs(sink_ref, sq_ref, [(skc_ref[...], svc_ref[...]), (sk_ref[keys, :], sv_ref[keys, :])],
               [None, band], o_ref)


def _l0_attn_sample(sink, qn, qr, kn, vm, kr4, sq, sk, sv, knc, vc, kr4c, skc, svc, w_f32):
    nq = DEC_SEQ // TQ_S
    qspec = lambda w: pl.BlockSpec((TQ_S, w), lambda b, i, *_: (T_P // TQ_S + b * nq + i, 0))
    kspec = lambda w: pl.BlockSpec((DEC_SEQ, w), lambda b, i, *_: (T_P // DEC_SEQ + b, 0))
    cspec = lambda w: pl.BlockSpec((PAST, w), lambda b, i, *_: (b, 0))
    w_spec = pl.BlockSpec((w_f32.shape[0] // (DEC_BATCH * nq), w_f32.shape[1]), lambda b, i, *_: (b * nq + i, 0))
    return pl.pallas_call(
        _l0_attn_sample_kernel,
        out_shape=[jax.ShapeDtypeStruct((T_S, D), BF16), jax.ShapeDtypeStruct(w_f32.shape, BF16)],
        grid_spec=pltpu.PrefetchScalarGridSpec(
            num_scalar_prefetch=1, grid=(DEC_BATCH, nq),
            in_specs=[qspec(512), qspec(256), qspec(512),
                      kspec(512), kspec(512), kspec(128), kspec(128), kspec(128),
                      cspec(512), cspec(512), cspec(128), cspec(128), cspec(128), w_spec],
            out_specs=[pl.BlockSpec((TQ_S, D), lambda b, i, *_: (b * nq + i, 0)), w_spec]),
        compiler_params=_params("arbitrary", "arbitrary"),
        name="l0_attn_sample",
    )(sink, qn, qr, sq, kn, vm, kr4, sk, sv, knc, vc, kr4c, skc, svc, w_f32)


def _out_ln_sample_kernel(o_ref, w_ref, y_ref, mod_ref, g_ref, b_ref, out_ref):
    out_ref[...] = _proj_residual_ln(o_ref[...], w_ref, y_ref[...], mod_ref[0], g_ref, b_ref)


def _out_ln_sample(o_sample, w_out, y, mods, ln_g, ln_b):
    return pl.pallas_call(
        _out_ln_sample_kernel,
        out_shape=jax.ShapeDtypeStruct((T_S, D), F32),
        grid=(T_S // TM,),
        in_specs=[_row_spec(D), _const_spec((D, D)),
                  pl.BlockSpec((TM, D), lambda i: (NP_TILES + i, 0)),
                  pl.BlockSpec((1, 6, D), lambda i: (1 + i // TILES_PER_SAMPLE, 0, 0)),
                  _const_spec((1, D)), _const_spec((1, D))],
        out_specs=_row_spec(D),
        compiler_params=_params("arbitrary"),
        name="out_proj_ln_sample",
    )(o_sample, w_out, y, mods, ln_g.reshape(1, D), ln_b.reshape(1, D))


def _two_part_specs(width, rows=TM):
    n_prompt = T_P // rows
    return [pl.BlockSpec((rows, width), lambda i, *_: (jnp.minimum(i, n_prompt - 1), 0)),
            pl.BlockSpec((rows, width), lambda i, *_: (jnp.maximum(i - n_prompt, 0), 0))]


def _two_part_rows(i, p_ref, s_ref, rows=TM):
    return jnp.where(i < T_P // rows, p_ref[...], s_ref[...])


FFN_CHUNK = D_FF
assert FFN_CHUNK % MXU_N == 0 and D_FF % FFN_CHUNK == 0
FFN_TM = 256
FFN_CAST_STEPS = 32


def _ffn_ln_kernel(yp_ref, ys_ref, mod_ref, w1_ref, w3_ref, w2_ref, g_ref, b_ref, wfa_ref, wfb_ref,
                   out_ref, wba_ref, wbb_ref):
    i = pl.program_id(0)
    _cast_block(i, FFN_CAST_STEPS, wfa_ref, wba_ref)
    _cast_block(i, FFN_CAST_STEPS, wfb_ref, wbb_ref)
    m = mod_ref[0]
    y = _two_part_rows(i, yp_ref, ys_ref, FFN_TM)
    h = (y * (1.0 + m[4:5]) + m[3:4]).astype(BF16)
    acc = None
    for c in range(D_FF // FFN_CHUNK):
        cols = slice(c * FFN_CHUNK, (c + 1) * FFN_CHUNK)
        a = _dot(h, w1_ref[:, cols])
        g = _dot(h, w3_ref[:, cols])
        part = _dot((_silu(a) * g).astype(BF16), w2_ref[cols, :])
        acc = part if acc is None else acc + part
    out_ref[...] = _layer_norm(ALPHA * y + m[5:6] * acc, g_ref[...], b_ref[...])


def _ffn_ln(y_prompt, y_sample, mods, w1, w3, w2, ln_g, ln_b, wa_f32, wb_f32):
    wa_spec, wa_shape = _cast_specs(wa_f32.shape, FFN_CAST_STEPS)
    wb_spec, wb_shape = _cast_specs(wb_f32.shape, FFN_CAST_STEPS)
    rows = _row_spec(D, FFN_TM)
    return pl.pallas_call(
        _ffn_ln_kernel,
        out_shape=[jax.ShapeDtypeStruct((T, D), F32), wa_shape, wb_shape],
        grid=(T // FFN_TM,),
        in_specs=_two_part_specs(D, FFN_TM)
        + [_mod_spec(FFN_TM), _const_spec((D, D_FF), True), _const_spec((D, D_FF), True),
                  _const_spec((D_FF, D), True), _const_spec((1, D)), _const_spec((1, D)), wa_spec, wb_spec],
        out_specs=[rows, wa_spec, wb_spec],
        compiler_params=_params("arbitrary"),
        name="ffn_ln",
    )(y_prompt, y_sample, mods, w1, w3, w2, ln_g.reshape(1, D), ln_b.reshape(1, D), wa_f32, wb_f32)


def _l1_in_kernel(y_ref, mod_ref, w_ref, q_o, k_o, v_o, kh_o, vh_o):
    i = pl.program_id(0)
    m = mod_ref[0]
    h = (y_ref[...] * (1.0 + m[1:2]) + m[0:1]).astype(BF16)
    z = _dot(h, w_ref[...])
    k = z[:, D:2 * D]
    v = z[:, 2 * D:3 * D]
    q_o[...] = (z[:, 0:D] * HD_QMUL).astype(BF16)
    k_o[...] = k.astype(BF16)
    v_o[...] = v.astype(BF16)

    @pl.when(i < NP_TILES)
    def _():
        for bb in range(TM // SEQ):
            rows = slice(bb * SEQ, (bb + 1) * SEQ)
            for j in range(NA_HEADS // 2):
                cols = slice(j * LANES, (j + 1) * LANES)
                kt = k[rows, cols].T
                vt = v[rows, cols].T
                for hh in range(2):
                    drows = slice(hh * HEAD_DIM, (hh + 1) * HEAD_DIM)
                    kh_o[bb, 2 * j + hh] = kt[drows]
                    vh_o[bb, 2 * j + hh] = vt[drows]


def _l1_in(y, mods, w_in):
    heads = pl.BlockSpec((TM // SEQ, NA_HEADS, HEAD_DIM, SEQ), lambda i: (jnp.minimum(i, NP_TILES - 1), 0, 0, 0))
    return pl.pallas_call(
        _l1_in_kernel,
        out_shape=[jax.ShapeDtypeStruct((T, D), BF16)] * 3
        + [jax.ShapeDtypeStruct((BATCH, NA_HEADS, HEAD_DIM, SEQ), F32)] * 2,
        grid=(NT,),
        in_specs=[_row_spec(D), _mod_spec(), _const_spec((D, 3 * D))],
        out_specs=[_row_spec(D)] * 3 + [heads, heads],
        compiler_params=_params("arbitrary"),
        name="l1_in_proj",
    )(y, mods, w_in)


def _l1_attn_prompt_kernel(q_ref, k_ref, v_ref, y_ref, mod_ref, w_ref, g_ref, b_ref, y1_o, o_scr):
    for j in range(NA_HEADS // 2):
        cols = slice(128 * j, 128 * (j + 1))
        o = _head_pair(q_ref[:, cols], [k_ref[:, cols].astype(BF16)], [v_ref[:, cols].astype(BF16)])
        o_scr[:, cols] = o.astype(BF16)
    y1_o[...] = _proj_residual_ln(o_scr[...], w_ref, y_ref[...], mod_ref[0], g_ref, b_ref)


def _l1_attn_prompt(q, k, v, y, mods, w_out, ln_g, ln_b):
    spec = pl.BlockSpec((SEQ, D), lambda b: (b, 0))
    return pl.pallas_call(
        _l1_attn_prompt_kernel,
        out_shape=jax.ShapeDtypeStruct((T_P, D), F32),
        grid=(BATCH,),
        in_specs=[spec, spec, spec] + _prompt_epilogue_specs(),
        out_specs=spec,
        scratch_shapes=[pltpu.VMEM((SEQ, D), BF16)],
        compiler_params=_params("arbitrary"),
        name="l1_attn_prompt",
    )(q, k, v, y, mods, w_out, ln_g.reshape(1, D), ln_b.reshape(1, D))


NA_ROWS = DEC_SEQ // GRID_W
NA_TILE_ROWS = 4
NA_TQ = NA_TILE_ROWS * GRID_W
NA_DR = 2 * NA_WIN_ROWS - 1
COL_SPAN = 2 * GRID_W - 1


def _na_window_start(r):
    return max(0, min(r - NA_WIN_ROWS // 2, NA_ROWS - NA_WIN_ROWS))


def _na_span(t):
    first = _na_window_start(t * NA_TILE_ROWS)
    last = _na_window_start((t + 1) * NA_TILE_ROWS - 1) + NA_WIN_ROWS
    n = last - first + (last - first) % 2
    return min(first, NA_ROWS - n), n


def _l1_attn_sample_kernel(q_ref, k_ref, v_ref, kc_ref, vc_ref, ext_ref, o_ref):
    kc = kc_ref[0].astype(BF16)
    vc = vc_ref[0].astype(BF16)
    lo = _lane() < GRID_W
    qcol = lax.broadcasted_iota(jnp.int32, (GRID_W, LANES), 0)
    kcol = lax.broadcasted_iota(jnp.int32, (GRID_W, LANES), 1) % GRID_W
    cs = jnp.clip(qcol - NA_WIN_COLS // 2, 0, GRID_W - NA_WIN_COLS)
    col_ok = jnp.logical_and(kcol >= cs, kcol < cs + NA_WIN_COLS)
    neg = jnp.full((GRID_W, LANES), NEG, F32)

    def bias_tile(hh, d, half):
        x = jnp.broadcast_to(ext_ref[hh, d:d + 1, :], (GRID_W, LANES))
        shift = (LANES - COL_SPAN // 2 + GRID_W * half) % LANES
        return jnp.where(col_ok, pltpu.roll(x, shift, 1, stride=1, stride_axis=0), neg)

    tiles = [[[bias_tile(hh, d, half) for half in range(2)] for d in range(NA_DR)] for hh in range(2)]

    for t in range(NA_ROWS // NA_TILE_ROWS):
        ws, span = _na_span(t)
        keys = slice(ws * GRID_W, (ws + span) * GRID_W)
        kw = k_ref[keys, :].astype(BF16)
        vw = v_ref[keys, :].astype(BF16)
        bias = []
        for hh in range(2):
            rows = []
            for rr in range(NA_TILE_ROWS):
                r = t * NA_TILE_ROWS + rr
                rs = _na_window_start(r)
                blocks = []
                for u in range(span // 2):
                    halves = []
                    for half in range(2):
                        kr = ws + 2 * u + half
                        ok = rs <= kr < rs + NA_WIN_ROWS
                        halves.append(tiles[hh][kr - r + NA_WIN_ROWS - 1][half] if ok else neg)
                    blocks.append(jnp.where(lo, halves[0], halves[1]))
                rows.append(jnp.concatenate(blocks, axis=1))
            bias.append(jnp.concatenate(rows, axis=0))
        qrows = slice(t * NA_TQ, (t + 1) * NA_TQ)
        o = _head_pair(q_ref[qrows, :], [kw, kc], [vw, vc], biases=[bias, None])
        o_ref[qrows, :] = o.astype(BF16)


def _l1_attn_sample(q, k, v, kc, vc, ext):
    lat = pl.BlockSpec((DEC_SEQ, LANES), lambda j, b: (T_P // DEC_SEQ + b, j))
    ctx = pl.BlockSpec((1, PAST, LANES), lambda j, b: (b, 0, j))
    return pl.pallas_call(
        _l1_attn_sample_kernel,
        out_shape=jax.ShapeDtypeStruct((T_S, D), BF16),
        grid=(NA_HEADS // 2, DEC_BATCH),
        in_specs=[lat, lat, lat, ctx, ctx, pl.BlockSpec((2, NA_DR, LANES), lambda j, b: (j, 0, 0))],
        out_specs=pl.BlockSpec((DEC_SEQ, LANES), lambda j, b: (b, j)),
        compiler_params=_params("arbitrary", "arbitrary"),
        name="l1_attn_sample",
    )(q, k, v, kc, vc, ext)


def _na_bias_rows(rel_bias):
    rb = rel_bias.astype(F32) * LOG2E
    n_lo = GRID_W - 1 - (NA_WIN_COLS - 1)
    n_hi = LANES - n_lo - rb.shape[-1]
    return jnp.concatenate([jnp.repeat(rb[..., :1], n_lo, axis=-1), rb,
                            jnp.repeat(rb[..., -1:], n_hi, axis=-1)], axis=-1)


def _split_bf16(x):
    hi = x.astype(BF16)
    return hi, (x - hi.astype(F32)).astype(BF16)


def _router_kernel(yp_ref, ys_ref, mod_ref, rw_ref, rb_ref, wf_ref, h_o, meta_o, cnt_o, wb_ref, carry_ref):
    i = pl.program_id(0)
    _cast_block(i, NP_TILES, wf_ref, wb_ref)

    @pl.when(i == 0)
    def _():
        carry_ref[...] = jnp.zeros_like(carry_ref)

    m = mod_ref[0]
    h = _two_part_rows(i, yp_ref, ys_ref) * (1.0 + m[4:5]) + m[3:4]
    _store_rows_tiled(h_o, h)
    h_hi, h_lo = _split_bf16(h)
    w_hi, w_lo = _split_bf16(rw_ref[...])
    logits = _dot(jnp.concatenate([h_hi, h_hi, h_lo], axis=1),
                  jnp.concatenate([w_hi, w_lo, w_hi], axis=0)) + rb_ref[...]
    lane = lax.broadcasted_iota(jnp.int32, (TM, LANES), 1).astype(F32)
    m1 = jnp.max(logits, axis=1, keepdims=True)
    i1 = jnp.min(jnp.where(logits == m1, lane, float(LANES)), axis=1, keepdims=True)
    sel1 = lane == i1
    rest = jnp.where(sel1, -jnp.inf, logits)
    m2 = jnp.max(rest, axis=1, keepdims=True)
    i2 = jnp.min(jnp.where(rest == m2, lane, float(LANES)), axis=1, keepdims=True)
    sel2 = lane == i2
    e2 = jnp.exp(m2 - m1)
    w1 = 1.0 / (1.0 + e2)
    w2 = e2 / (1.0 + e2)
    sel = jnp.logical_or(sel1, sel2)
    rr = lax.broadcasted_iota(jnp.int32, (TM, TM), 0)
    cc = lax.broadcasted_iota(jnp.int32, (TM, TM), 1)
    tri = jnp.where(cc < rr, 1.0, 0.0).astype(BF16)
    ahead = _dot(tri, jnp.where(sel, 1.0, 0.0).astype(BF16)) + carry_ref[...]
    r1 = jnp.sum(jnp.where(sel1, ahead, 0.0), axis=1, keepdims=True)
    r2 = jnp.sum(jnp.where(sel2, ahead, 0.0), axis=1, keepdims=True)
    meta = jnp.where(lane == 0, i1, 0.0)
    meta = jnp.where(lane == 1, i2, meta)
    meta = jnp.where(lane == 2, r1, meta)
    meta = jnp.where(lane == 3, r2, meta)
    meta = jnp.where(lane == 4, w1, meta)
    meta = jnp.where(lane == 5, w2, meta)
    meta_o[...] = meta
    carry_ref[...] = carry_ref[...] + jnp.sum(jnp.where(sel, 1.0, 0.0), axis=0, keepdims=True)
    cnt_o[...] = carry_ref[...]


def _router(y_prompt, y_sample, mods, router_w, router_b, w_f32):
    rw = jnp.zeros((D, LANES), F32).at[:, :N_EXPERTS].set(router_w)
    rb = jnp.full((1, LANES), NEG, F32).at[0, :N_EXPERTS].set(router_b)
    w_spec, wb_shape = _cast_specs(w_f32.shape, NP_TILES)
    return pl.pallas_call(
        _router_kernel,
        out_shape=[jax.ShapeDtypeStruct((T * ROW_TILE, LANES), F32),
                   jax.ShapeDtypeStruct((T, LANES), F32),
                   jax.ShapeDtypeStruct((1, LANES), F32),
                   wb_shape],
        grid=(NT,),
        in_specs=_two_part_specs(D) + [_mod_spec(), _const_spec((D, LANES)), _const_spec((1, LANES)), w_spec],
        out_specs=[_row_spec(LANES, TM * ROW_TILE), _row_spec(LANES), _const_spec((1, LANES)), w_spec],
        scratch_shapes=[pltpu.VMEM((1, LANES), F32)],
        compiler_params=_params("arbitrary"),
        name="moe_router",
    )(y_prompt, y_sample, mods, rw, rb, w_f32)


DMA_UNROLL = 8
DISPATCH_TM = 2048


def _row_copy(src_ref, src_row, dst_ref, dst_row, sem):
    return pltpu.make_async_copy(src_ref.at[pl.ds(pl.multiple_of(src_row * ROW_TILE, ROW_TILE), ROW_TILE)],
                                 dst_ref.at[pl.ds(pl.multiple_of(dst_row * ROW_TILE, ROW_TILE), ROW_TILE)], sem)


def _group_tile_copy(src_ref, dst_ref, tile, sem):
    start = pl.multiple_of(tile * (TG * ROW_TILE), TG * ROW_TILE)
    return pltpu.make_async_copy(src_ref, dst_ref.at[pl.ds(start, TG * ROW_TILE)], sem)


def _dispatch_kernel(pos_ref, last_ref, na_ref, h_ref, xg_ref, zero_ref, sem, zsem):
    i = pl.program_id(0)

    @pl.when(i == 0)
    def _():
        zero_ref[...] = jnp.zeros_like(zero_ref)
        for e in range(N_EXPERTS):
            @pl.when(last_ref[e] >= 0)
            def _():
                _group_tile_copy(zero_ref, xg_ref, last_ref[e], zsem).start()

        def start_unused(g, carry):
            _group_tile_copy(zero_ref, xg_ref, g, zsem).start()
            return carry

        def wait_one(g, carry):
            _group_tile_copy(zero_ref, xg_ref, 0, zsem).wait()
            return carry

        lax.fori_loop(na_ref[0], NG, start_unused, 0)
        lax.fori_loop(0, na_ref[1], wait_one, 0)

    def issue(r, carry):
        t = i * DISPATCH_TM + r
        _row_copy(h_ref, r, xg_ref, pos_ref[2 * t], sem).start(priority=0)
        _row_copy(h_ref, r, xg_ref, pos_ref[2 * t + 1], sem).start(priority=1)
        return carry

    lax.fori_loop(0, DISPATCH_TM, issue, 0, unroll=DMA_UNROLL)
    for _ in range(2):
        pltpu.make_async_copy(h_ref, xg_ref.at[pl.ds(0, DISPATCH_TM * ROW_TILE)], sem).wait()


def _dispatch(pos, last_tile, tile_counts, h_tiled):
    return pl.pallas_call(
        _dispatch_kernel,
        out_shape=jax.ShapeDtypeStruct((P_ROWS * ROW_TILE, LANES), F32),
        grid_spec=pltpu.PrefetchScalarGridSpec(
            num_scalar_prefetch=3, grid=(T // DISPATCH_TM,),
            in_specs=[pl.BlockSpec((DISPATCH_TM * ROW_TILE, LANES), lambda i, *_: (i, 0))],
            out_specs=pl.BlockSpec(memory_space=pl.ANY),
            scratch_shapes=[pltpu.VMEM((TG * ROW_TILE, LANES), F32), pltpu.SemaphoreType.DMA(()),
                            pltpu.SemaphoreType.DMA(())]),
        compiler_params=_params("arbitrary"),
        name="moe_dispatch",
    )(pos, last_tile, tile_counts, h_tiled)


def _expert_ffn_kernel(te_ref, na_ref, x_ref, w1_ref, w3_ref, w2_ref, o_ref):
    g = pl.program_id(0)

    @pl.when(g < na_ref[0])
    def _():
        x = _load_rows_tiled(x_ref, TG).astype(BF16)
        acc = None
        for c in range(E_FF // F_CHUNK):
            cols = slice(c * F_CHUNK, (c + 1) * F_CHUNK)
            a = _dot(x, w1_ref[0, :, cols])
            b = _dot(x, w3_ref[0, :, cols])
            part = _dot((_silu(a) * b).astype(BF16), w2_ref[0, cols, :])
            acc = part if acc is None else acc + part
        _store_rows_tiled(o_ref, acc)

    @pl.when(g >= na_ref[0])
    def _():
        o_ref[...] = jnp.zeros_like(o_ref)


def _expert_ffn(tile_expert, n_active, xg, w1, w3, w2):
    rows = pl.BlockSpec((TG * ROW_TILE, LANES), lambda g, te, na: (g, 0))
    w_up = pl.BlockSpec((1, D, E_FF), lambda g, te, na: (te[g], 0, 0))
    w_dn = pl.BlockSpec((1, E_FF, D), lambda g, te, na: (te[g], 0, 0))
    return pl.pallas_call(
        _expert_ffn_kernel,
        out_shape=jax.ShapeDtypeStruct((P_ROWS * ROW_TILE, LANES), F32),
        grid_spec=pltpu.PrefetchScalarGridSpec(
            num_scalar_prefetch=2, grid=(NG,),
            in_specs=[rows, w_up, w_up, w_dn],
            out_specs=rows),
        compiler_params=_params("arbitrary"),
        name="moe_expert_ffn",
    )(tile_expert, n_active, xg, w1, w3, w2)


def _combine_ln_kernel(pos_ref, yp_ref, ys_ref, mod_ref, meta_ref, g_ref, b_ref, eo_ref, outp_ref, outs_ref,
                       buf1, buf2, sem):
    i = pl.program_id(0)
    slot = i % 2

    def fetch(tile, into):
        def issue(r, carry):
            t = tile * TM + r
            _row_copy(eo_ref, pos_ref[2 * t], buf1.at[into], r, sem.at[into]).start(priority=0)
            _row_copy(eo_ref, pos_ref[2 * t + 1], buf2.at[into], r, sem.at[into]).start(priority=1)
            return carry

        lax.fori_loop(0, TM, issue, 0, unroll=DMA_UNROLL)

    @pl.when(i == 0)
    def _():
        fetch(0, 0)

    @pl.when(i + 1 < NT)
    def _():
        fetch(i + 1, 1 - slot)

    for buf in (buf1, buf2):
        pltpu.make_async_copy(eo_ref.at[pl.ds(0, TM * ROW_TILE)], buf.at[slot], sem.at[slot]).wait()
    m = mod_ref[0]
    meta = meta_ref[...]
    f = (meta[:, 4:5] * _load_rows_tiled(buf1.at[slot], TM)
         + meta[:, 5:6] * _load_rows_tiled(buf2.at[slot], TM))
    out = _layer_norm(ALPHA * _two_part_rows(i, yp_ref, ys_ref) + m[5:6] * f, g_ref[...], b_ref[...])

    @pl.when(i < NP_TILES)
    def _():
        outp_ref[...] = out

    @pl.when(i >= NP_TILES)
    def _():
        outs_ref[...] = out


def _combine_ln(pos, y_prompt, y_sample, mods, meta, ln_g, ln_b, eo):
    return pl.pallas_call(
        _combine_ln_kernel,
        out_shape=[jax.ShapeDtypeStruct((T_P, D), F32), jax.ShapeDtypeStruct((T_S, D), F32)],
        grid_spec=pltpu.PrefetchScalarGridSpec(
            num_scalar_prefetch=1, grid=(NT,),
            in_specs=_two_part_specs(D)
            + [pl.BlockSpec((1, 6, D), lambda i, *_: (_cond_of_tile(i), 0, 0)),
                      pl.BlockSpec((TM, LANES), lambda i, *_: (i, 0)),
                      pl.BlockSpec((1, D), lambda i, *_: (0, 0)),
                      pl.BlockSpec((1, D), lambda i, *_: (0, 0)),
                      pl.BlockSpec(memory_space=pl.ANY)],
            out_specs=[_prompt_tile_spec(D), _sample_tile_spec(D)],
            scratch_shapes=[pltpu.VMEM((2, TM * ROW_TILE, LANES), F32),
                            pltpu.VMEM((2, TM * ROW_TILE, LANES), F32),
                            pltpu.SemaphoreType.DMA((2,))]),
        compiler_params=_params("arbitrary"),
        name="moe_combine_ln",
    )(pos, y_prompt, y_sample, mods, meta, ln_g.reshape(1, D), ln_b.reshape(1, D), eo)


def _moe_ln(y_prompt, y_sample, mods, router_w, router_b, w1, w3, w2_f32, ln_g, ln_b):
    h_tiled, meta, counts, w2 = _router(y_prompt, y_sample, mods, router_w, router_b,
                                        w2_f32.reshape(N_EXPERTS * E_FF, D))
    cnt = counts[0, :N_EXPERTS].astype(jnp.int32)
    tiles = (cnt + TG - 1) // TG
    tile_end = jnp.cumsum(tiles)
    offs = (tile_end - tiles) * TG
    expert = meta[:, 0:2].astype(jnp.int32)
    chosen = expert[:, :, None] == jnp.arange(N_EXPERTS)[None, None, :]
    pos = jnp.sum(jnp.where(chosen, offs[None, None, :], 0), axis=-1) + meta[:, 2:4].astype(jnp.int32)
    pos = pos.reshape(2 * T)
    tile_expert = jnp.sum((jnp.arange(NG)[:, None] >= tile_end[None, :]).astype(jnp.int32), axis=1)
    tile_expert = jnp.minimum(tile_expert, N_EXPERTS - 1)
    n_active = tile_end[-1:].astype(jnp.int32)
    last_tile = jnp.where(tiles > 0, tile_end - 1, -1).astype(jnp.int32)
    n_zeroed = jnp.sum((tiles > 0).astype(jnp.int32)) + NG - tile_end[-1]
    tile_counts = jnp.stack([tile_end[-1], n_zeroed]).astype(jnp.int32)
    xg = _dispatch(pos, last_tile, tile_counts, h_tiled)
    eo = _expert_ffn(tile_expert, n_active, xg, w1, w3, w2.reshape(N_EXPERTS, E_FF, D))
    return _combine_ln(pos, y_prompt, y_sample, mods, meta, ln_g, ln_b, eo)


def _l0_weight_layouts(w_in, w_q_up, w_kv_up):
    a, b, c = MLA_Q_LORA, MLA_Q_LORA + MLA_KV_LORA, MLA_Q_LORA + MLA_KV_LORA + MLA_ROPE
    k_rope = w_in[:, b:c]
    w_in_r = jnp.concatenate([w_in[:, :b], w_in[:, c:], k_rope, k_rope, k_rope, k_rope], axis=1)
    wq = w_q_up.reshape(MLA_Q_LORA, MLA_HEADS, MLA_NOPE + MLA_ROPE)
    w_q_up_r = jnp.concatenate([wq[:, :, :MLA_NOPE].reshape(MLA_Q_LORA, -1),
                                wq[:, :, MLA_NOPE:].reshape(MLA_Q_LORA, -1)], axis=1)
    wkv = w_kv_up.reshape(MLA_KV_LORA, MLA_HEADS, MLA_NOPE + MLA_V)
    w_kv_up_r = jnp.concatenate([wkv[:, :, :MLA_NOPE].reshape(MLA_KV_LORA, -1),
                                 wkv[:, :, MLA_NOPE:].reshape(MLA_KV_LORA, -1)], axis=1)
    return w_in_r.astype(BF16), w_q_up_r.astype(BF16), w_kv_up_r.astype(BF16)


def kernel(x_prompt, x_sample, cache_l0_mla_ckv, cache_l0_mla_krope, cache_l0_swa_k, cache_l0_swa_v,
           cache_l1_na_k, cache_l1_na_v, c, c_ctx,
           l0_ada_w, l0_ada_b, l0_w_in, l0_mla_q_norm, l0_mla_w_q_up, l0_mla_kv_norm, l0_mla_w_kv_up,
           l0_swa_sink, l0_w_out, l0_ln1_g, l0_ln1_b, l0_ffn_w1, l0_ffn_w3, l0_ffn_w2, l0_ln2_g, l0_ln2_b,
           l1_ada_w, l1_ada_b, l1_w_in, l1_na_rel_bias, l1_w_out, l1_ln1_g, l1_ln1_b,
           l1_moe_router_w, l1_moe_router_b, l1_moe_w1, l1_moe_w3, l1_moe_w2, l1_ln2_g, l1_ln2_b):
    cond = jnp.concatenate([c_ctx[None, :], c, jnp.zeros((8 - N_COND, D), F32)], axis=0)

    mods = _adaln(cond, l0_ada_w, l0_ada_b)
    w_in_r, w_q_up_r, w_kv_up_r = _l0_weight_layouts(l0_w_in, l0_mla_w_q_up, l0_mla_w_kv_up)
    y, qn, qr, kn, vm, kr4, sq, sk, sv, ckv, krope_t, sk_t, sv_t = _l0_in(
        x_prompt.reshape(T_P, D), x_sample.reshape(T_S, D), mods, w_in_r, l0_mla_q_norm, w_q_up_r,
        l0_mla_kv_norm, w_kv_up_r, _rope_tables(MLA_ROPE, 256), _rope_tables(HEAD_DIM, 512))
    knc, vc = _kv_up(cache_l0_mla_ckv.reshape(DEC_BATCH * PAST, MLA_KV_LORA), w_kv_up_r)
    kr4c = jnp.tile(cache_l0_mla_krope.reshape(DEC_BATCH * PAST, MLA_ROPE), (1, 4))
    skc = cache_l0_swa_k.reshape(DEC_BATCH * PAST, SWA_KV_HEADS * HEAD_DIM)
    svc = cache_l0_swa_v.reshape(DEC_BATCH * PAST, SWA_KV_HEADS * HEAD_DIM)
    w_out = l0_w_out.astype(BF16)
    y_p, ffn_w1, ffn_w3, ffn_w2 = _l0_attn_prompt(l0_swa_sink, qn, qr, kn, vm, kr4, sq, sk, sv,
                                                  y, mods, w_out, l0_ln1_g, l0_ln1_b,
                                                  [l0_ffn_w1, l0_ffn_w3, l0_ffn_w2])
    o_s, w_in_l1 = _l0_attn_sample(l0_swa_sink, qn, qr, kn, vm, kr4, sq, sk, sv, knc, vc, kr4c, skc, svc,
                                   l1_w_in)
    y_s = _out_ln_sample(o_s, w_out, y, mods, l0_ln1_g, l0_ln1_b)
    y, moe_w1, moe_w3 = _ffn_ln(y_p, y_s, mods, ffn_w1, ffn_w3, ffn_w2, l0_ln2_g, l0_ln2_b,
                                l1_moe_w1.reshape(N_EXPERTS * D, E_FF), l1_moe_w3.reshape(N_EXPERTS * D, E_FF))
    new_ckv = ckv.reshape(BATCH, SEQ, MLA_KV_LORA)
    new_krope = jnp.transpose(krope_t, (0, 2, 1))
    new_sk = jnp.transpose(sk_t, (0, 3, 1, 2))
    new_sv = jnp.transpose(sv_t, (0, 3, 1, 2))

    mods = _adaln(cond, l1_ada_w, l1_ada_b)
    q, k, v, k_heads, v_heads = _l1_in(y, mods, w_in_l1)
    w_out = l1_w_out.astype(BF16)
    y_p = _l1_attn_prompt(q, k, v, y, mods, w_out, l1_ln1_g, l1_ln1_b)
    o_s = _l1_attn_sample(q, k, v, cache_l1_na_k.reshape(DEC_BATCH, PAST, D),
                          cache_l1_na_v.reshape(DEC_BATCH, PAST, D), _na_bias_rows(l1_na_rel_bias))
    y_s = _out_ln_sample(o_s, w_out, y, mods, l1_ln1_g, l1_ln1_b)
    y_p, y_s = _moe_ln(y_p, y_s, mods, l1_moe_router_w, l1_moe_router_b, moe_w1.reshape(N_EXPERTS, D, E_FF),
                       moe_w3.reshape(N_EXPERTS, D, E_FF), l1_moe_w2, l1_ln2_g, l1_ln2_b)
    new_k = jnp.transpose(k_heads, (0, 3, 1, 2))
    new_v = jnp.transpose(v_heads, (0, 3, 1, 2))

    return (y_p.reshape(BATCH, SEQ, D), y_s.reshape(DEC_BATCH, DEC_SEQ, D),
            new_ckv, new_krope, new_sk, new_sv, new_k, new_v)
```

```python
import functools

import jax
import jax.numpy as jnp
import numpy as np
from jax import lax
from jax.experimental import pallas as pl
from jax.experimental.pallas import tpu as pltpu

F32 = jnp.float32
BF16 = jnp.bfloat16

D = 1024
BATCH, SEQ = 32, 256
DEC_BATCH, DEC_SEQ = 2, 1024
PAST = 256
GRID_W = 64
T_P = BATCH * SEQ
T_S = DEC_BATCH * DEC_SEQ
T = T_P + T_S
N_COND = 1 + DEC_BATCH

MLA_HEADS, MLA_Q_LORA, MLA_KV_LORA, MLA_NOPE, MLA_ROPE, MLA_V = 8, 384, 256, 64, 32, 64
SWA_HEADS, SWA_KV_HEADS, SWA_WINDOW, HEAD_DIM = 8, 2, 128, 64
NA_HEADS, NA_WIN_ROWS, NA_WIN_COLS = 16, 8, 16
D_FF, N_EXPERTS, E_FF = 2816, 8, 3584
ROPE_THETA = 10000.0
LN_EPS, RMS_EPS = 1e-5, 1e-6
NEG = -1e30
ALPHA = 4.0 ** 0.25

LANES = 128
TM = 512
NT = T // TM
NP_TILES = T_P // TM
TILES_PER_SAMPLE = DEC_SEQ // TM
TG = 256
P_ROWS = 2 * T + N_EXPERTS * TG
NG = P_ROWS // TG
MXU_N = 256
F_CHUNK = 1792
assert F_CHUNK % MXU_N == 0 and E_FF % F_CHUNK == 0
VMEM_LIMIT = 56 * 1024 * 1024


def _params(*sem):
    return pltpu.CompilerParams(dimension_semantics=sem, vmem_limit_bytes=VMEM_LIMIT)


def _const_spec(shape, single_buffer=False):
    if single_buffer:
        return pl.BlockSpec(shape, lambda *_: (0,) * len(shape), pipeline_mode=pl.Buffered(1))
    return pl.BlockSpec(shape, lambda *_: (0,) * len(shape))


def _cast_specs(shape, steps):
    rows, width = shape
    spec = pl.BlockSpec((rows // steps, width), lambda i, *_: (jnp.minimum(i, steps - 1), 0))
    return spec, jax.ShapeDtypeStruct(shape, BF16)


def _cast_block(i, steps, src_ref, dst_ref):
    @pl.when(i < steps)
    def _():
        dst_ref[...] = src_ref[...].astype(BF16)


ROW_TILE = D // LANES


def _store_rows_tiled(ref, x):
    n = x.shape[0]
    for k in range(ROW_TILE):
        ref[pl.ds(k, n, stride=ROW_TILE), :] = x[:, k * LANES:(k + 1) * LANES]


def _load_rows_tiled(ref, n):
    return jnp.concatenate([ref[pl.ds(k, n, stride=ROW_TILE), :] for k in range(ROW_TILE)], axis=1)


def _cond_of_tile(i, rows=TM):
    per_sample = DEC_SEQ // rows
    return jnp.maximum((i - (T_P // rows - per_sample)) // per_sample, 0)


def _mod_spec(rows=TM):
    return pl.BlockSpec((1, 6, D), lambda i: (_cond_of_tile(i, rows), 0, 0))


def _row_spec(width, rows=TM):
    return pl.BlockSpec((rows, width), lambda i: (i, 0))


def _dot(a, b):
    return jnp.dot(a, b, preferred_element_type=F32)


def _dot_nt(a, b):
    return lax.dot_general(a, b, (((1,), (1,)), ((), ())), preferred_element_type=F32)


def _layer_norm(r, g, b):
    mu = jnp.mean(r, axis=-1, keepdims=True)
    d = r - mu
    var = jnp.mean(d * d, axis=-1, keepdims=True)
    return d * lax.rsqrt(var + LN_EPS) * g + b


def _rms_norm(x, g):
    return x * lax.rsqrt(jnp.mean(x * x, axis=-1, keepdims=True) + RMS_EPS) * g


def _silu(x):
    return x * jax.nn.sigmoid(x)


def _ada_kernel(c_ref, w_ref, b_ref, o_ref):
    s = _silu(c_ref[...]).astype(BF16)
    o_ref[...] = _dot(s, w_ref[...].astype(BF16)) + b_ref[...]


def _adaln(cond, ada_w, ada_b):
    nb = 1536
    out = pl.pallas_call(
        _ada_kernel,
        out_shape=jax.ShapeDtypeStruct((8, 6 * D), F32),
        grid=(6 * D // nb,),
        in_specs=[_const_spec((8, D)), pl.BlockSpec((D, nb), lambda j: (0, j)),
                  pl.BlockSpec((1, nb), lambda j: (0, j))],
        out_specs=pl.BlockSpec((8, nb), lambda j: (0, j)),
        compiler_params=_params("arbitrary"),
        name="adaln",
    )(cond, ada_w, ada_b.reshape(1, 6 * D))
    return out[:N_COND].reshape(N_COND, 6, D)


def _rope_tables(head_dim, width):
    half = head_dim // 2
    nf = half // 2
    lane = np.arange(width)
    d = lane % head_dim
    dd = d % half
    f = dd % nf
    inv = np.float32(ROPE_THETA) ** (-f.astype(np.float32) / np.float32(nf))
    t = np.arange(DEC_SEQ)
    pos = np.where((d // half)[None, :] == 0, (t // GRID_W)[:, None], (t % GRID_W)[:, None])
    ang = pos.astype(np.float32) * inv[None, :].astype(np.float32)
    cos, sin = np.cos(ang), np.sin(ang)
    first = (dd < nf)[None, :]
    zero = np.float32(0.0)
    return (jnp.asarray(cos, F32), jnp.asarray(np.where(first, -sin, zero), F32),
            jnp.asarray(np.where(first, zero, sin), F32))


def _rope(x, cos, sin_up, sin_dn, nf):
    w = x.shape[-1]
    return x * cos + pltpu.roll(x, w - nf, 1) * sin_up + pltpu.roll(x, nf, 1) * sin_dn


L0_COLS = MLA_Q_LORA + MLA_KV_LORA + 512 + 128 + 128 + 128


def _l0_in_kernel(xp_ref, xs_ref, mod_ref, win_ref, qn_ref, wq_ref, kvn_ref, wkv_ref,
                  c8_ref, su8_ref, sd8_ref, c16_ref, su16_ref, sd16_ref,
                  y_o, qnope_o, qrope_o, knope_o, vmla_o, kr4_o, sq_o, sk_o, sv_o,
                  ckv_o, krt_o, skt_o, svt_o):
    i = pl.program_id(0)
    m = mod_ref[0]
    x = jnp.where(i < NP_TILES, xp_ref[...], xs_ref[...])
    y_o[...] = x
    h = (x * (1.0 + m[1:2]) + m[0:1]).astype(BF16)
    z = _dot(h, win_ref[...])
    q_lat = z[:, 0:384]
    kv_lat = z[:, 384:640]
    sq = z[:, 640:1152]
    sk = z[:, 1152:1280]
    sv = z[:, 1280:1408]
    kr4 = z[:, 1408:1536]
    q = _dot(_rms_norm(q_lat, qn_ref[...]).astype(BF16), wq_ref[...])
    c_kv = _rms_norm(kv_lat, kvn_ref[...])
    kv = _dot(c_kv.astype(BF16), wkv_ref[...])
    q = q * MLA_QMUL
    sq = sq * HD_QMUL
    qnope_o[...] = q[:, 0:512].astype(BF16)
    knope_o[...] = kv[:, 0:512].astype(BF16)
    vmla_o[...] = kv[:, 512:1024].astype(BF16)
    sv_o[...] = sv
    q_rope = q[:, 512:768]

    @pl.when(i < NP_TILES)
    def _():
        qrope_o[...] = q_rope.astype(BF16)
        kr4_o[...] = kr4
        sq_o[...] = sq.astype(BF16)
        sk_o[...] = sk
        ckv_o[...] = c_kv
        for bb in range(TM // SEQ):
            rows = slice(bb * SEQ, (bb + 1) * SEQ)
            krt_o[bb] = kr4[rows].T[:MLA_ROPE]
            skt = sk[rows].T
            svt = sv[rows].T
            for g in range(SWA_KV_HEADS):
                skt_o[bb, g] = skt[g * HEAD_DIM:(g + 1) * HEAD_DIM]
                svt_o[bb, g] = svt[g * HEAD_DIM:(g + 1) * HEAD_DIM]

    @pl.when(i >= NP_TILES)
    def _():
        c8, su8, sd8 = c8_ref[...], su8_ref[...], sd8_ref[...]
        c16, su16, sd16 = c16_ref[...], su16_ref[...], sd16_ref[...]
        qrope_o[...] = _rope(q_rope, c8, su8, sd8, 8).astype(BF16)
        kr4_o[...] = _rope(kr4, c8[:, :128], su8[:, :128], sd8[:, :128], 8)
        sq_o[...] = _rope(sq, c16, su16, sd16, 16).astype(BF16)
        sk_o[...] = _rope(sk, c16[:, :128], su16[:, :128], sd16[:, :128], 16)


def _prompt_tile_spec(width):
    return pl.BlockSpec((TM, width), lambda i, *_: (jnp.minimum(i, NP_TILES - 1), 0))


def _sample_tile_spec(width):
    return pl.BlockSpec((TM, width), lambda i, *_: (jnp.maximum(i - NP_TILES, 0), 0))


def _l0_in(x_prompt, x_sample, mods, w_in_r, q_norm, w_q_up_r, kv_norm, w_kv_up_r, tabs8, tabs16):
    def tab_spec(width):
        return pl.BlockSpec((TM, width), lambda i: (jnp.maximum(i - NP_TILES, 0) % TILES_PER_SAMPLE, 0))

    out_shape = [
        jax.ShapeDtypeStruct((T, D), F32),
        jax.ShapeDtypeStruct((T, 512), BF16),
        jax.ShapeDtypeStruct((T, 256), BF16),
        jax.ShapeDtypeStruct((T, 512), BF16),
        jax.ShapeDtypeStruct((T, 512), BF16),
        jax.ShapeDtypeStruct((T, 128), F32),
        jax.ShapeDtypeStruct((T, 512), BF16),
        jax.ShapeDtypeStruct((T, 128), F32),
        jax.ShapeDtypeStruct((T, 128), F32),
    ]
    per_tile = TM // SEQ
    prompt_block = lambda *dims: pl.BlockSpec((per_tile,) + dims,
                                              lambda i: (jnp.minimum(i, NP_TILES - 1),) + (0,) * len(dims))
    cache_shape = [
        jax.ShapeDtypeStruct((T_P, MLA_KV_LORA), F32),
        jax.ShapeDtypeStruct((BATCH, MLA_ROPE, SEQ), F32),
        jax.ShapeDtypeStruct((BATCH, SWA_KV_HEADS, HEAD_DIM, SEQ), F32),
        jax.ShapeDtypeStruct((BATCH, SWA_KV_HEADS, HEAD_DIM, SEQ), F32),
    ]
    cache_specs = [_prompt_tile_spec(MLA_KV_LORA), prompt_block(MLA_ROPE, SEQ),
                   prompt_block(SWA_KV_HEADS, HEAD_DIM, SEQ), prompt_block(SWA_KV_HEADS, HEAD_DIM, SEQ)]
    return pl.pallas_call(
        _l0_in_kernel,
        out_shape=out_shape + cache_shape,
        grid=(NT,),
        in_specs=[_prompt_tile_spec(D), _sample_tile_spec(D), _mod_spec(), _const_spec((D, L0_COLS)),
                  _const_spec((1, MLA_Q_LORA)), _const_spec((MLA_Q_LORA, 768)),
                  _const_spec((1, MLA_KV_LORA)), _const_spec((MLA_KV_LORA, 1024)),
                  tab_spec(256), tab_spec(256), tab_spec(256),
                  tab_spec(512), tab_spec(512), tab_spec(512)],
        out_specs=[_row_spec(s.shape[1]) for s in out_shape] + cache_specs,
        compiler_params=_params("arbitrary"),
        name="l0_in_proj",
    )(x_prompt, x_sample, mods, w_in_r, q_norm.reshape(1, -1), w_q_up_r, kv_norm.reshape(1, -1), w_kv_up_r,
      *tabs8, *tabs16)


def _kv_up_kernel(c_ref, w_ref, k_o, v_o):
    kv = _dot(c_ref[...].astype(BF16), w_ref[...])
    k_o[...] = kv[:, 0:512].astype(BF16)
    v_o[...] = kv[:, 512:1024].astype(BF16)


def _kv_up(ckv, w_kv_up_r):
    n = ckv.shape[0]
    return pl.pallas_call(
        _kv_up_kernel,
        out_shape=[jax.ShapeDtypeStruct((n, 512), BF16)] * 2,
        grid=(1,),
        in_specs=[_const_spec((n, MLA_KV_LORA)), _const_spec((MLA_KV_LORA, 1024))],
        out_specs=[_const_spec((n, 512))] * 2,
        compiler_params=_params("arbitrary"),
        name="l0_ctx_kv_up",
    )(ckv, w_kv_up_r)


def _lane():
    return lax.broadcasted_iota(jnp.int32, (1, LANES), 1)


def _softmax_weights(scores, extra_logit=None):
    m = jnp.max(scores[0], axis=1, keepdims=True)
    for s in scores[1:]:
        m = jnp.maximum(m, jnp.max(s, axis=1, keepdims=True))
    if extra_logit is not None:
        m = jnp.maximum(m, extra_logit)
    weights = [jnp.exp2(s - m) for s in scores]
    den = None
    for e in weights:
        d = jnp.sum(e, axis=1, keepdims=True)
        den = d if den is None else den + d
    if extra_logit is not None:
        den = den + jnp.exp2(extra_logit - m)
    return [e.astype(BF16) for e in weights], den


def _head_pair(q2, keys, values, masks=None, biases=None, sinks=None, q_extra=None, joint_values=False):
    lo = _lane() < 64
    half = [lo, jnp.logical_not(lo)]
    zero = jnp.zeros_like(q2)
    weights, dens, outs = [], [], []
    for hh in range(2):
        qm = jnp.where(half[hh], q2, zero)
        if q_extra is not None:
            qm = jnp.concatenate([qm, q_extra[hh]], axis=1)
        scores = []
        for n, k in enumerate(keys):
            s = _dot_nt(qm, k)
            if biases is not None and biases[n] is not None:
                s = s + biases[n][hh]
            if masks is not None and masks[n] is not None:
                s = jnp.where(masks[n], s, NEG)
            scores.append(s)
        w, den = _softmax_weights(scores, None if sinks is None else sinks[hh])
        if joint_values:
            weights += w
            dens.append(den)
        else:
            acc = None
            for e, v in zip(w, values):
                a = _dot(e, v)
                acc = a if acc is None else acc + a
            outs.append(acc / den)
    if not joint_values:
        return jnp.where(lo, outs[0], outs[1])
    vals = [jnp.where(half[hh], v, jnp.zeros_like(v)) for hh in range(2) for v in values]
    acc = _dot(jnp.concatenate(weights, axis=1), jnp.concatenate(vals, axis=0))
    return acc / jnp.where(lo, dens[0], dens[1])


def _dup_halves(x):
    lo = _lane() < 64
    sw = pltpu.roll(x, 64, 1)
    return jnp.where(lo, x, sw), jnp.where(lo, sw, x)


LOG2E = 1.4426950408889634
MLA_QMUL = (MLA_NOPE + MLA_ROPE) ** -0.5 * LOG2E
HD_QMUL = HEAD_DIM ** -0.5 * LOG2E


def _mla_pairs(qn_ref, qr_ref, key_sets, o_ref, joint_values=False):
    lane = _lane()
    for j in range(MLA_HEADS // 2):
        cols = slice(128 * j, 128 * (j + 1))
        qr = qr_ref[:, 128 * (j // 2):128 * (j // 2 + 1)]
        zero = jnp.zeros_like(qr)
        q_extra = [jnp.where((lane // MLA_ROPE) == ((2 * j + hh) % 4), qr, zero) for hh in range(2)]
        keys = [jnp.concatenate([kn[:, cols], kr4], axis=1) for kn, kr4, _ in key_sets]
        values = [v[:, cols] for _, _, v in key_sets]
        o = _head_pair(qn_ref[:, cols], keys, values, q_extra=q_extra, joint_values=joint_values)
        o_ref[:, cols] = o.astype(BF16)


def _swa_pairs(sink_ref, sq_ref, key_sets, masks, o_ref, joint_values=False):
    kd = [[a.astype(BF16) for a in _dup_halves(k)] for k, _ in key_sets]
    vd = [[a.astype(BF16) for a in _dup_halves(v)] for _, v in key_sets]
    for g in range(SWA_KV_HEADS):
        for u in range(2):
            c = 2 * g + u
            cols = slice(128 * c, 128 * (c + 1))
            sinks = [sink_ref[2 * c + hh] * LOG2E for hh in range(2)]
            o = _head_pair(sq_ref[:, cols], [k[g] for k in kd], [v[g] for v in vd],
                           masks=masks, sinks=sinks, joint_values=joint_values)
            o_ref[:, 512 + 128 * c:512 + 128 * (c + 1)] = o.astype(BF16)


def _cast_steps(rows, max_steps):
    steps = max_steps
    while rows % (16 * steps):
        steps //= 2
    return steps


def _proj_residual_ln(o, w_ref, y, m, g_ref, b_ref):
    return _layer_norm(ALPHA * y + m[2:3] * _dot(o, w_ref[...]), g_ref[...], b_ref[...])


PROMPT_PER_STEP = 2
PROMPT_ROWS = PROMPT_PER_STEP * SEQ
PROMPT_STEPS = BATCH // PROMPT_PER_STEP


def _prompt_rows_spec(width):
    return pl.BlockSpec((PROMPT_ROWS, width), lambda s, *_: (s, 0))


def _prompt_epilogue_specs():
    return [_prompt_rows_spec(D), pl.BlockSpec((1, 6, D), lambda s, *_: (0, 0, 0)),
            pl.BlockSpec((D, D), lambda s, *_: (0, 0)), pl.BlockSpec((1, D), lambda s, *_: (0, 0)),
            pl.BlockSpec((1, D), lambda s, *_: (0, 0))]


def _l0_attn_prompt_kernel(cast_steps, sink_ref, qn_ref, qr_ref, kn_ref, v_ref, kr4_ref, sq_ref, sk_ref, sv_ref,
                           y_ref, mod_ref, w_ref, g_ref, b_ref, wf1_ref, wf2_ref, wf3_ref,
                           y1_o, wb1_ref, wb2_ref, wb3_ref, o_scr):
    for bb in range(PROMPT_PER_STEP):
        rows = pl.ds(bb * SEQ, SEQ)
        o = o_scr.at[rows]
        _mla_pairs(qn_ref.at[rows], qr_ref.at[rows],
                   [(kn_ref.at[rows], kr4_ref[rows, :].astype(BF16), v_ref.at[rows])], o)
        _swa_pairs(sink_ref, sq_ref.at[rows], [(sk_ref[rows, :], sv_ref[rows, :])], None, o)
        y1_o[rows, :] = _proj_residual_ln(o_scr[rows, :], w_ref, y_ref[rows, :], mod_ref[0], g_ref, b_ref)
    for steps, wf_ref, wb_ref in zip(cast_steps, (wf1_ref, wf2_ref, wf3_ref), (wb1_ref, wb2_ref, wb3_ref)):
        _cast_block(pl.program_id(0), steps, wf_ref, wb_ref)


def _l0_attn_prompt(sink, qn, qr, kn, vm, kr4, sq, sk, sv, y, mods, w_out, ln_g, ln_b, weights_f32):
    spec = _prompt_rows_spec
    cast_steps = tuple(_cast_steps(w.shape[0], PROMPT_STEPS) for w in weights_f32)
    casts = [_cast_specs(w.shape, s) for w, s in zip(weights_f32, cast_steps)]
    return pl.pallas_call(
        functools.partial(_l0_attn_prompt_kernel, cast_steps),
        out_shape=[jax.ShapeDtypeStruct((T_P, D), F32)] + [c[1] for c in casts],
        grid_spec=pltpu.PrefetchScalarGridSpec(
            num_scalar_prefetch=1, grid=(PROMPT_STEPS,),
            in_specs=[spec(512), spec(256), spec(512), spec(512), spec(128), spec(512), spec(128), spec(128)]
            + _prompt_epilogue_specs() + [c[0] for c in casts],
            out_specs=[spec(D)] + [c[0] for c in casts],
            scratch_shapes=[pltpu.VMEM((PROMPT_ROWS, D), BF16)]),
        compiler_params=_params("arbitrary"),
        name="l0_attn_prompt",
    )(sink, qn, qr, kn, vm, kr4, sq, sk, sv, y, mods, w_out, ln_g.reshape(1, D), ln_b.reshape(1, D), *weights_f32)


TQ_S = 256


def _l0_attn_sample_kernel(sink_ref, qn_ref, qr_ref, sq_ref, kn_ref, v_ref, kr4_ref, sk_ref, sv_ref,
                           knc_ref, vc_ref, kr4c_ref, skc_ref, svc_ref, wf_ref, o_ref, wb_ref):
    i = pl.program_id(1)
    wb_ref[...] = wf_ref[...].astype(BF16)
    _mla_pairs(qn_ref, qr_ref,
               [(knc_ref, kr4c_ref[...].astype(BF16), vc_ref), (kn_ref, kr4_ref[...].astype(BF16), v_ref)],
               o_ref, joint_values=True)
    span = TQ_S + 2 * SWA_WINDOW
    start = pl.multiple_of(jnp.clip(i * TQ_S - SWA_WINDOW, 0, DEC_SEQ - span), SWA_WINDOW)
    qpos = i * TQ_S + lax.broadcasted_iota(jnp.int32, (TQ_S, span), 0)
    kpos = start + lax.broadcasted_iota(jnp.int32, (TQ_S, span), 1)
    band = jnp.abs(qpos - kpos) <= SWA_WINDOW
    keys = pl.ds(start, span)
    _swa_pairs(sink_ref, sq_ref, [(skc_ref[...], svc_ref[...]), (sk_ref[keys, :], sv_ref[keys, :])],
               [None, band], o_ref, joint_values=True)


def _l0_attn_sample(sink, qn, qr, kn, vm, kr4, sq, sk, sv, knc, vc, kr4c, skc, svc, w_f32):
    nq = DEC_SEQ // TQ_S
    qspec = lambda w: pl.BlockSpec((TQ_S, w), lambda b, i, *_: (T_P // TQ_S + b * nq + i, 0))
    kspec = lambda w: pl.BlockSpec((DEC_SEQ, w), lambda b, i, *_: (T_P // DEC_SEQ + b, 0))
    cspec = lambda w: pl.BlockSpec((PAST, w), lambda b, i, *_: (b, 0))
    w_spec = pl.BlockSpec((w_f32.shape[0] // (DEC_BATCH * nq), w_f32.shape[1]), lambda b, i, *_: (b * nq + i, 0))
    return pl.pallas_call(
        _l0_attn_sample_kernel,
        out_shape=[jax.ShapeDtypeStruct((T_S, D), BF16), jax.ShapeDtypeStruct(w_f32.shape, BF16)],
        grid_spec=pltpu.PrefetchScalarGridSpec(
            num_scalar_prefetch=1, grid=(DEC_BATCH, nq),
            in_specs=[qspec(512), qspec(256), qspec(512),
                      kspec(512), kspec(512), kspec(128), kspec(128), kspec(128),
                      cspec(512), cspec(512), cspec(128), cspec(128), cspec(128), w_spec],
            out_specs=[pl.BlockSpec((TQ_S, D), lambda b, i, *_: (b * nq + i, 0)), w_spec]),
        compiler_params=_params("arbitrary", "arbitrary"),
        name="l0_attn_sample",
    )(sink, qn, qr, sq, kn, vm, kr4, sk, sv, knc, vc, kr4c, skc, svc, w_f32)


def _out_ln_sample_kernel(o_ref, w_ref, y_ref, mod_ref, g_ref, b_ref, out_ref):
    out_ref[...] = _proj_residual_ln(o_ref[...], w_ref, y_ref[...], mod_ref[0], g_ref, b_ref)


def _out_ln_sample(o_sample, w_out, y, mods, ln_g, ln_b):
    return pl.pallas_call(
        _out_ln_sample_kernel,
        out_shape=jax.ShapeDtypeStruct((T_S, D), F32),
        grid=(T_S // TM,),
        in_specs=[_row_spec(D), _const_spec((D, D)),
                  pl.BlockSpec((TM, D), lambda i: (NP_TILES + i, 0)),
                  pl.BlockSpec((1, 6, D), lambda i: (1 + i // TILES_PER_SAMPLE, 0, 0)),
                  _const_spec((1, D)), _const_spec((1, D))],
        out_specs=_row_spec(D),
        compiler_params=_params("arbitrary"),
        name="out_proj_ln_sample",
    )(o_sample, w_out, y, mods, ln_g.reshape(1, D), ln_b.reshape(1, D))


def _two_part_specs(width, rows=TM):
    n_prompt = T_P // rows
    return [pl.BlockSpec((rows, width), lambda i, *_: (jnp.minimum(i, n_prompt - 1), 0)),
            pl.BlockSpec((rows, width), lambda i, *_: (jnp.maximum(i - n_prompt, 0), 0))]


def _two_part_rows(i, p_ref, s_ref, rows=TM):
    return jnp.where(i < T_P // rows, p_ref[...], s_ref[...])


FFN_CHUNK = D_FF
assert FFN_CHUNK % MXU_N == 0 and D_FF % FFN_CHUNK == 0
FFN_TM = 256
FFN_CAST_STEPS = 32


def _ffn_ln_kernel(yp_ref, ys_ref, mod_ref, w1_ref, w3_ref, w2_ref, g_ref, b_ref, wfa_ref, wfb_ref,
                   out_ref, wba_ref, wbb_ref):
    i = pl.program_id(0)
    _cast_block(i, FFN_CAST_STEPS, wfa_ref, wba_ref)
    _cast_block(i, FFN_CAST_STEPS, wfb_ref, wbb_ref)
    m = mod_ref[0]
    y = _two_part_rows(i, yp_ref, ys_ref, FFN_TM)
    h = (y * (1.0 + m[4:5]) + m[3:4]).astype(BF16)
    acc = None
    for c in range(D_FF // FFN_CHUNK):
        cols = slice(c * FFN_CHUNK, (c + 1) * FFN_CHUNK)
        a = _dot(h, w1_ref[:, cols])
        g = _dot(h, w3_ref[:, cols])
        part = _dot((_silu(a) * g).astype(BF16), w2_ref[cols, :])
        acc = part if acc is None else acc + part
    out_ref[...] = _layer_norm(ALPHA * y + m[5:6] * acc, g_ref[...], b_ref[...])


def _ffn_ln(y_prompt, y_sample, mods, w1, w3, w2, ln_g, ln_b, wa_f32, wb_f32):
    wa_spec, wa_shape = _cast_specs(wa_f32.shape, FFN_CAST_STEPS)
    wb_spec, wb_shape = _cast_specs(wb_f32.shape, FFN_CAST_STEPS)
    rows = _row_spec(D, FFN_TM)
    return pl.pallas_call(
        _ffn_ln_kernel,
        out_shape=[jax.ShapeDtypeStruct((T, D), F32), wa_shape, wb_shape],
        grid=(T // FFN_TM,),
        in_specs=_two_part_specs(D, FFN_TM)
        + [_mod_spec(FFN_TM), _const_spec((D, D_FF), True), _const_spec((D, D_FF), True),
                  _const_spec((D_FF, D), True), _const_spec((1, D)), _const_spec((1, D)), wa_spec, wb_spec],
        out_specs=[rows, wa_spec, wb_spec],
        compiler_params=_params("arbitrary"),
        name="ffn_ln",
    )(y_prompt, y_sample, mods, w1, w3, w2, ln_g.reshape(1, D), ln_b.reshape(1, D), wa_f32, wb_f32)


def _l1_in_kernel(y_ref, mod_ref, w_ref, q_o, k_o, v_o, kh_o, vh_o):
    i = pl.program_id(0)
    m = mod_ref[0]
    h = (y_ref[...] * (1.0 + m[1:2]) + m[0:1]).astype(BF16)
    z = _dot(h, w_ref[...])
    k = z[:, D:2 * D]
    v = z[:, 2 * D:3 * D]
    q_o[...] = (z[:, 0:D] * HD_QMUL).astype(BF16)
    k_o[...] = k.astype(BF16)
    v_o[...] = v.astype(BF16)

    @pl.when(i < NP_TILES)
    def _():
        for bb in range(TM // SEQ):
            rows = slice(bb * SEQ, (bb + 1) * SEQ)
            for j in range(NA_HEADS // 2):
                cols = slice(j * LANES, (j + 1) * LANES)
                kt = k[rows, cols].T
                vt = v[rows, cols].T
                for hh in range(2):
                    drows = slice(hh * HEAD_DIM, (hh + 1) * HEAD_DIM)
                    kh_o[bb, 2 * j + hh] = kt[drows]
                    vh_o[bb, 2 * j + hh] = vt[drows]


def _l1_in(y, mods, w_in):
    heads = pl.BlockSpec((TM // SEQ, NA_HEADS, HEAD_DIM, SEQ), lambda i: (jnp.minimum(i, NP_TILES - 1), 0, 0, 0))
    return pl.pallas_call(
        _l1_in_kernel,
        out_shape=[jax.ShapeDtypeStruct((T, D), BF16)] * 3
        + [jax.ShapeDtypeStruct((BATCH, NA_HEADS, HEAD_DIM, SEQ), F32)] * 2,
        grid=(NT,),
        in_specs=[_row_spec(D), _mod_spec(), _const_spec((D, 3 * D))],
        out_specs=[_row_spec(D)] * 3 + [heads, heads],
        compiler_params=_params("arbitrary"),
        name="l1_in_proj",
    )(y, mods, w_in)


def _l1_attn_prompt_kernel(q_ref, k_ref, v_ref, y_ref, mod_ref, w_ref, g_ref, b_ref, y1_o, o_scr):
    for bb in range(PROMPT_PER_STEP):
        rows = pl.ds(bb * SEQ, SEQ)
        for j in range(NA_HEADS // 2):
            cols = slice(128 * j, 128 * (j + 1))
            o = _head_pair(q_ref[rows, cols], [k_ref[rows, cols]], [v_ref[rows, cols]])
            o_scr[rows, cols] = o.astype(BF16)
        y1_o[rows, :] = _proj_residual_ln(o_scr[rows, :], w_ref, y_ref[rows, :], mod_ref[0], g_ref, b_ref)


def _l1_attn_prompt(q, k, v, y, mods, w_out, ln_g, ln_b):
    spec = _prompt_rows_spec(D)
    return pl.pallas_call(
        _l1_attn_prompt_kernel,
        out_shape=jax.ShapeDtypeStruct((T_P, D), F32),
        grid=(PROMPT_STEPS,),
        in_specs=[spec, spec, spec] + _prompt_epilogue_specs(),
        out_specs=spec,
        scratch_shapes=[pltpu.VMEM((PROMPT_ROWS, D), BF16)],
        compiler_params=_params("arbitrary"),
        name="l1_attn_prompt",
    )(q, k, v, y, mods, w_out, ln_g.reshape(1, D), ln_b.reshape(1, D))


NA_ROWS = DEC_SEQ // GRID_W
NA_TILE_ROWS = 4
NA_TQ = NA_TILE_ROWS * GRID_W
NA_DR = 2 * NA_WIN_ROWS - 1
COL_SPAN = 2 * GRID_W - 1


def _na_window_start(r):
    return max(0, min(r - NA_WIN_ROWS // 2, NA_ROWS - NA_WIN_ROWS))


def _na_span(t):
    first = _na_window_start(t * NA_TILE_ROWS)
    last = _na_window_start((t + 1) * NA_TILE_ROWS - 1) + NA_WIN_ROWS
    n = last - first + (last - first) % 2
    return min(first, NA_ROWS - n), n


def _l1_attn_sample_kernel(q_ref, k_ref, v_ref, kc_ref, vc_ref, ext_ref, wf_ref, o_ref, wb_ref):
    wb_ref[...] = wf_ref[...].astype(BF16)
    kc = kc_ref[0].astype(BF16)
    vc = vc_ref[0].astype(BF16)
    lo = _lane() < GRID_W
    qcol = lax.broadcasted_iota(jnp.int32, (GRID_W, LANES), 0)
    kcol = lax.broadcasted_iota(jnp.int32, (GRID_W, LANES), 1) % GRID_W
    cs = jnp.clip(qcol - NA_WIN_COLS // 2, 0, GRID_W - NA_WIN_COLS)
    col_ok = jnp.logical_and(kcol >= cs, kcol < cs + NA_WIN_COLS)
    neg = jnp.full((GRID_W, LANES), NEG, F32)

    def bias_tile(hh, d, half):
        x = jnp.broadcast_to(ext_ref[hh, d:d + 1, :], (GRID_W, LANES))
        shift = (LANES - COL_SPAN // 2 + GRID_W * half) % LANES
        return jnp.where(col_ok, pltpu.roll(x, shift, 1, stride=1, stride_axis=0), neg)

    tiles = [[[bias_tile(hh, d, half) for half in range(2)] for d in range(NA_DR)] for hh in range(2)]

    for t in range(NA_ROWS // NA_TILE_ROWS):
        ws, span = _na_span(t)
        keys = slice(ws * GRID_W, (ws + span) * GRID_W)
        kw = k_ref[keys, :].astype(BF16)
        vw = v_ref[keys, :].astype(BF16)
        bias = []
        for hh in range(2):
            rows = []
            for rr in range(NA_TILE_ROWS):
                r = t * NA_TILE_ROWS + rr
                rs = _na_window_start(r)
                blocks = []
                for u in range(span // 2):
                    halves = []
                    for half in range(2):
                        kr = ws + 2 * u + half
                        ok = rs <= kr < rs + NA_WIN_ROWS
                        halves.append(tiles[hh][kr - r + NA_WIN_ROWS - 1][half] if ok else neg)
                    blocks.append(jnp.where(lo, halves[0], halves[1]))
                rows.append(jnp.concatenate(blocks, axis=1))
            bias.append(jnp.concatenate(rows, axis=0))
        qrows = slice(t * NA_TQ, (t + 1) * NA_TQ)
        o = _head_pair(q_ref[qrows, :], [kw, kc], [vw, vc], biases=[bias, None], joint_values=True)
        o_ref[qrows, :] = o.astype(BF16)


def _l1_attn_sample(q, k, v, kc, vc, ext, w_f32):
    steps = (NA_HEADS // 2) * DEC_BATCH
    w_spec = pl.BlockSpec((w_f32.shape[0] // steps, w_f32.shape[1]), lambda j, b: (j * DEC_BATCH + b, 0))
    lat = pl.BlockSpec((DEC_SEQ, LANES), lambda j, b: (T_P // DEC_SEQ + b, j))
    ctx = pl.BlockSpec((1, PAST, LANES), lambda j, b: (b, 0, j))
    return pl.pallas_call(
        _l1_attn_sample_kernel,
        out_shape=[jax.ShapeDtypeStruct((T_S, D), BF16), jax.ShapeDtypeStruct(w_f32.shape, BF16)],
        grid=(NA_HEADS // 2, DEC_BATCH),
        in_specs=[lat, lat, lat, ctx, ctx, pl.BlockSpec((2, NA_DR, LANES), lambda j, b: (j, 0, 0)), w_spec],
        out_specs=[pl.BlockSpec((DEC_SEQ, LANES), lambda j, b: (b, j)), w_spec],
        compiler_params=_params("arbitrary", "arbitrary"),
        name="l1_attn_sample",
    )(q, k, v, kc, vc, ext, w_f32)


def _na_bias_rows(rel_bias):
    rb = rel_bias.astype(F32) * LOG2E
    n_lo = GRID_W - 1 - (NA_WIN_COLS - 1)
    n_hi = LANES - n_lo - rb.shape[-1]
    return jnp.concatenate([jnp.repeat(rb[..., :1], n_lo, axis=-1), rb,
                            jnp.repeat(rb[..., -1:], n_hi, axis=-1)], axis=-1)


def _split_bf16(x):
    hi = x.astype(BF16)
    return hi, (x - hi.astype(F32)).astype(BF16)


def _router_kernel(yp_ref, ys_ref, mod_ref, rw_ref, rb_ref, h_o, meta_o, cnt_o, carry_ref):
    i = pl.program_id(0)

    @pl.when(i == 0)
    def _():
        carry_ref[...] = jnp.zeros_like(carry_ref)

    m = mod_ref[0]
    h = _two_part_rows(i, yp_ref, ys_ref) * (1.0 + m[4:5]) + m[3:4]
    _store_rows_tiled(h_o, h)
    h_hi, h_lo = _split_bf16(h)
    w_hi, w_lo = _split_bf16(rw_ref[...])
    logits = _dot(jnp.concatenate([h_hi, h_hi, h_lo], axis=1),
                  jnp.concatenate([w_hi, w_lo, w_hi], axis=0)) + rb_ref[...]
    lane = lax.broadcasted_iota(jnp.int32, (TM, LANES), 1).astype(F32)
    m1 = jnp.max(logits, axis=1, keepdims=True)
    i1 = jnp.min(jnp.where(logits == m1, lane, float(LANES)), axis=1, keepdims=True)
    sel1 = lane == i1
    rest = jnp.where(sel1, -jnp.inf, logits)
    m2 = jnp.max(rest, axis=1, keepdims=True)
    i2 = jnp.min(jnp.where(rest == m2, lane, float(LANES)), axis=1, keepdims=True)
    sel2 = lane == i2
    e2 = jnp.exp(m2 - m1)
    w1 = 1.0 / (1.0 + e2)
    w2 = e2 / (1.0 + e2)
    sel = jnp.logical_or(sel1, sel2)
    rr = lax.broadcasted_iota(jnp.int32, (TM, TM), 0)
    cc = lax.broadcasted_iota(jnp.int32, (TM, TM), 1)
    tri = jnp.where(cc < rr, 1.0, 0.0).astype(BF16)
    ahead = _dot(tri, jnp.where(sel, 1.0, 0.0).astype(BF16)) + carry_ref[...]
    r1 = jnp.sum(jnp.where(sel1, ahead, 0.0), axis=1, keepdims=True)
    r2 = jnp.sum(jnp.where(sel2, ahead, 0.0), axis=1, keepdims=True)
    meta = jnp.where(lane == 0, i1, 0.0)
    meta = jnp.where(lane == 1, i2, meta)
    meta = jnp.where(lane == 2, r1, meta)
    meta = jnp.where(lane == 3, r2, meta)
    meta = jnp.where(lane == 4, w1, meta)
    meta = jnp.where(lane == 5, w2, meta)
    meta_o[...] = meta
    carry_ref[...] = carry_ref[...] + jnp.sum(jnp.where(sel, 1.0, 0.0), axis=0, keepdims=True)
    cnt_o[...] = carry_ref[...]


def _router(y_prompt, y_sample, mods, router_w, router_b):
    rw = jnp.zeros((D, LANES), F32).at[:, :N_EXPERTS].set(router_w)
    rb = jnp.full((1, LANES), -jnp.inf, F32).at[0, :N_EXPERTS].set(router_b)
    return pl.pallas_call(
        _router_kernel,
        out_shape=[jax.ShapeDtypeStruct((T * ROW_TILE, LANES), F32),
                   jax.ShapeDtypeStruct((T, LANES), F32),
                   jax.ShapeDtypeStruct((1, LANES), F32)],
        grid=(NT,),
        in_specs=_two_part_specs(D) + [_mod_spec(), _const_spec((D, LANES)), _const_spec((1, LANES))],
        out_specs=[_row_spec(LANES, TM * ROW_TILE), _row_spec(LANES), _const_spec((1, LANES))],
        scratch_shapes=[pltpu.VMEM((1, LANES), F32)],
        compiler_params=_params("arbitrary"),
        name="moe_router",
    )(y_prompt, y_sample, mods, rw, rb)


DMA_UNROLL = 8
DISPATCH_TM = 2048


def _row_copy(src_ref, src_row, dst_ref, dst_row, sem):
    return pltpu.make_async_copy(src_ref.at[pl.ds(pl.multiple_of(src_row * ROW_TILE, ROW_TILE), ROW_TILE)],
                                 dst_ref.at[pl.ds(pl.multiple_of(dst_row * ROW_TILE, ROW_TILE), ROW_TILE)], sem)


def _group_tile_copy(src_ref, dst_ref, tile, sem):
    start = pl.multiple_of(tile * (TG * ROW_TILE), TG * ROW_TILE)
    return pltpu.make_async_copy(src_ref, dst_ref.at[pl.ds(start, TG * ROW_TILE)], sem)


def _dispatch_kernel(pos_ref, last_ref, na_ref, h_ref, xg_ref, zero_ref, sem, zsem):
    i = pl.program_id(0)

    @pl.when(i == 0)
    def _():
        zero_ref[...] = jnp.zeros_like(zero_ref)
        for e in range(N_EXPERTS):
            @pl.when(last_ref[e] >= 0)
            def _():
                _group_tile_copy(zero_ref, xg_ref, last_ref[e], zsem).start()

        def start_unused(g, carry):
            _group_tile_copy(zero_ref, xg_ref, g, zsem).start()
            return carry

        def wait_one(g, carry):
            _group_tile_copy(zero_ref, xg_ref, 0, zsem).wait()
            return carry

        lax.fori_loop(na_ref[0], NG, start_unused, 0)
        lax.fori_loop(0, na_ref[1], wait_one, 0)

    def issue(r, carry):
        t = i * DISPATCH_TM + r
        _row_copy(h_ref, r, xg_ref, pos_ref[2 * t], sem).start(priority=0)
        _row_copy(h_ref, r, xg_ref, pos_ref[2 * t + 1], sem).start(priority=1)
        return carry

    lax.fori_loop(0, DISPATCH_TM, issue, 0, unroll=DMA_UNROLL)
    for _ in range(2):
        pltpu.make_async_copy(h_ref, xg_ref.at[pl.ds(0, DISPATCH_TM * ROW_TILE)], sem).wait()


def _dispatch(pos, last_tile, tile_counts, h_tiled):
    return pl.pallas_call(
        _dispatch_kernel,
        out_shape=jax.ShapeDtypeStruct((P_ROWS * ROW_TILE, LANES), F32),
        grid_spec=pltpu.PrefetchScalarGridSpec(
            num_scalar_prefetch=3, grid=(T // DISPATCH_TM,),
            in_specs=[pl.BlockSpec((DISPATCH_TM * ROW_TILE, LANES), lambda i, *_: (i, 0))],
            out_specs=pl.BlockSpec(memory_space=pl.ANY),
            scratch_shapes=[pltpu.VMEM((TG * ROW_TILE, LANES), F32), pltpu.SemaphoreType.DMA(()),
                            pltpu.SemaphoreType.DMA(())]),
        compiler_params=_params("arbitrary"),
        name="moe_dispatch",
    )(pos, last_tile, tile_counts, h_tiled)


def _expert_ffn_kernel(te_ref, na_ref, x_ref, w1_ref, w3_ref, w2_ref, o_ref):
    g = pl.program_id(0)

    @pl.when(g < na_ref[0])
    def _():
        x = _load_rows_tiled(x_ref, TG).astype(BF16)
        acc = None
        for c in range(E_FF // F_CHUNK):
            cols = slice(c * F_CHUNK, (c + 1) * F_CHUNK)
            a = _dot(x, w1_ref[0, :, cols])
            b = _dot(x, w3_ref[0, :, cols])
            part = _dot((_silu(a) * b).astype(BF16), w2_ref[0, cols, :])
            acc = part if acc is None else acc + part
        _store_rows_tiled(o_ref, acc)

    @pl.when(g >= na_ref[0])
    def _():
        o_ref[...] = jnp.zeros_like(o_ref)


def _expert_ffn(tile_expert, n_active, xg, w1, w3, w2):
    rows = pl.BlockSpec((TG * ROW_TILE, LANES), lambda g, te, na: (g, 0))
    w_up = pl.BlockSpec((1, D, E_FF), lambda g, te, na: (te[g], 0, 0))
    w_dn = pl.BlockSpec((1, E_FF, D), lambda g, te, na: (te[g], 0, 0))
    return pl.pallas_call(
        _expert_ffn_kernel,
        out_shape=jax.ShapeDtypeStruct((P_ROWS * ROW_TILE, LANES), F32),
        grid_spec=pltpu.PrefetchScalarGridSpec(
            num_scalar_prefetch=2, grid=(NG,),
            in_specs=[rows, w_up, w_up, w_dn],
            out_specs=rows),
        compiler_params=_params("arbitrary"),
        name="moe_expert_ffn",
    )(tile_expert, n_active, xg, w1, w3, w2)


def _combine_ln_kernel(pos_ref, yp_ref, ys_ref, mod_ref, meta_ref, g_ref, b_ref, eo_ref, outp_ref, outs_ref,
                       buf1, buf2, sem):
    i = pl.program_id(0)
    slot = i % 2

    def fetch(tile, into):
        def issue(r, carry):
            t = tile * TM + r
            _row_copy(eo_ref, pos_ref[2 * t], buf1.at[into], r, sem.at[into]).start(priority=0)
            _row_copy(eo_ref, pos_ref[2 * t + 1], buf2.at[into], r, sem.at[into]).start(priority=1)
            return carry

        lax.fori_loop(0, TM, issue, 0, unroll=DMA_UNROLL)

    @pl.when(i == 0)
    def _():
        fetch(0, 0)

    @pl.when(i + 1 < NT)
    def _():
        fetch(i + 1, 1 - slot)

    for buf in (buf1, buf2):
        pltpu.make_async_copy(eo_ref.at[pl.ds(0, TM * ROW_TILE)], buf.at[slot], sem.at[slot]).wait()
    m = mod_ref[0]
    meta = meta_ref[...]
    f = (meta[:, 4:5] * _load_rows_tiled(buf1.at[slot], TM)
         + meta[:, 5:6] * _load_rows_tiled(buf2.at[slot], TM))
    out = _layer_norm(ALPHA * _two_part_rows(i, yp_ref, ys_ref) + m[5:6] * f, g_ref[...], b_ref[...])

    @pl.when(i < NP_TILES)
    def _():
        outp_ref[...] = out

    @pl.when(i >= NP_TILES)
    def _():
        outs_ref[...] = out


def _combine_ln(pos, y_prompt, y_sample, mods, meta, ln_g, ln_b, eo):
    return pl.pallas_call(
        _combine_ln_kernel,
        out_shape=[jax.ShapeDtypeStruct((T_P, D), F32), jax.ShapeDtypeStruct((T_S, D), F32)],
        grid_spec=pltpu.PrefetchScalarGridSpec(
            num_scalar_prefetch=1, grid=(NT,),
            in_specs=_two_part_specs(D)
            + [pl.BlockSpec((1, 6, D), lambda i, *_: (_cond_of_tile(i), 0, 0)),
                      pl.BlockSpec((TM, LANES), lambda i, *_: (i, 0)),
                      pl.BlockSpec((1, D), lambda i, *_: (0, 0)),
                      pl.BlockSpec((1, D), lambda i, *_: (0, 0)),
                      pl.BlockSpec(memory_space=pl.ANY)],
            out_specs=[_prompt_tile_spec(D), _sample_tile_spec(D)],
            scratch_shapes=[pltpu.VMEM((2, TM * ROW_TILE, LANES), F32),
                            pltpu.VMEM((2, TM * ROW_TILE, LANES), F32),
                            pltpu.SemaphoreType.DMA((2,))]),
        compiler_params=_params("arbitrary"),
        name="moe_combine_ln",
    )(pos, y_prompt, y_sample, mods, meta, ln_g.reshape(1, D), ln_b.reshape(1, D), eo)


def _moe_ln(y_prompt, y_sample, mods, router_w, router_b, w1, w3, w2, ln_g, ln_b):
    h_tiled, meta, counts = _router(y_prompt, y_sample, mods, router_w, router_b)
    cnt = counts[0, :N_EXPERTS].astype(jnp.int32)
    tiles = (cnt + TG - 1) // TG
    tile_end = jnp.cumsum(tiles)
    offs = (tile_end - tiles) * TG
    expert = meta[:, 0:2].astype(jnp.int32)
    chosen = expert[:, :, None] == jnp.arange(N_EXPERTS)[None, None, :]
    pos = jnp.sum(jnp.where(chosen, offs[None, None, :], 0), axis=-1) + meta[:, 2:4].astype(jnp.int32)
    pos = pos.reshape(2 * T)
    tile_expert = jnp.sum((jnp.arange(NG)[:, None] >= tile_end[None, :]).astype(jnp.int32), axis=1)
    tile_expert = jnp.minimum(tile_expert, N_EXPERTS - 1)
    n_active = tile_end[-1:].astype(jnp.int32)
    last_tile = jnp.where(tiles > 0, tile_end - 1, -1).astype(jnp.int32)
    n_zeroed = jnp.sum((tiles > 0).astype(jnp.int32)) + NG - tile_end[-1]
    tile_counts = jnp.stack([tile_end[-1], n_zeroed]).astype(jnp.int32)
    xg = _dispatch(pos, last_tile, tile_counts, h_tiled)
    eo = _expert_ffn(tile_expert, n_active, xg, w1, w3, w2)
    return _combine_ln(pos, y_prompt, y_sample, mods, meta, ln_g, ln_b, eo)


def _l0_weight_layouts(w_in, w_q_up, w_kv_up):
    a, b, c = MLA_Q_LORA, MLA_Q_LORA + MLA_KV_LORA, MLA_Q_LORA + MLA_KV_LORA + MLA_ROPE
    k_rope = w_in[:, b:c]
    w_in_r = jnp.concatenate([w_in[:, :b], w_in[:, c:], k_rope, k_rope, k_rope, k_rope], axis=1)
    wq = w_q_up.reshape(MLA_Q_LORA, MLA_HEADS, MLA_NOPE + MLA_ROPE)
    w_q_up_r = jnp.concatenate([wq[:, :, :MLA_NOPE].reshape(MLA_Q_LORA, -1),
                                wq[:, :, MLA_NOPE:].reshape(MLA_Q_LORA, -1)], axis=1)
    wkv = w_kv_up.reshape(MLA_KV_LORA, MLA_HEADS, MLA_NOPE + MLA_V)
    w_kv_up_r = jnp.concatenate([wkv[:, :, :MLA_NOPE].reshape(MLA_KV_LORA, -1),
                                 wkv[:, :, MLA_NOPE:].reshape(MLA_KV_LORA, -1)], axis=1)
    return w_in_r.astype(BF16), w_q_up_r.astype(BF16), w_kv_up_r.astype(BF16)


def kernel(x_prompt, x_sample, cache_l0_mla_ckv, cache_l0_mla_krope, cache_l0_swa_k, cache_l0_swa_v,
           cache_l1_na_k, cache_l1_na_v, c, c_ctx,
           l0_ada_w, l0_ada_b, l0_w_in, l0_mla_q_norm, l0_mla_w_q_up, l0_mla_kv_norm, l0_mla_w_kv_up,
           l0_swa_sink, l0_w_out, l0_ln1_g, l0_ln1_b, l0_ffn_w1, l0_ffn_w3, l0_ffn_w2, l0_ln2_g, l0_ln2_b,
           l1_ada_w, l1_ada_b, l1_w_in, l1_na_rel_bias, l1_w_out, l1_ln1_g, l1_ln1_b,
           l1_moe_router_w, l1_moe_router_b, l1_moe_w1, l1_moe_w3, l1_moe_w2, l1_ln2_g, l1_ln2_b):
    cond = jnp.concatenate([c_ctx[None, :], c, jnp.zeros((8 - N_COND, D), F32)], axis=0)

    mods = _adaln(cond, l0_ada_w, l0_ada_b)
    w_in_r, w_q_up_r, w_kv_up_r = _l0_weight_layouts(l0_w_in, l0_mla_w_q_up, l0_mla_w_kv_up)
    y, qn, qr, kn, vm, kr4, sq, sk, sv, ckv, krope_t, sk_t, sv_t = _l0_in(
        x_prompt.reshape(T_P, D), x_sample.reshape(T_S, D), mods, w_in_r, l0_mla_q_norm, w_q_up_r,
        l0_mla_kv_norm, w_kv_up_r, _rope_tables(MLA_ROPE, 256), _rope_tables(HEAD_DIM, 512))
    knc, vc = _kv_up(cache_l0_mla_ckv.reshape(DEC_BATCH * PAST, MLA_KV_LORA), w_kv_up_r)
    kr4c = jnp.tile(cache_l0_mla_krope.reshape(DEC_BATCH * PAST, MLA_ROPE), (1, 4))
    skc = cache_l0_swa_k.reshape(DEC_BATCH * PAST, SWA_KV_HEADS * HEAD_DIM)
    svc = cache_l0_swa_v.reshape(DEC_BATCH * PAST, SWA_KV_HEADS * HEAD_DIM)
    w_out = l0_w_out.astype(BF16)
    y_p, ffn_w1, ffn_w3, ffn_w2 = _l0_attn_prompt(l0_swa_sink, qn, qr, kn, vm, kr4, sq, sk, sv,
                                                  y, mods, w_out, l0_ln1_g, l0_ln1_b,
                                                  [l0_ffn_w1, l0_ffn_w3, l0_ffn_w2])
    o_s, w_in_l1 = _l0_attn_sample(l0_swa_sink, qn, qr, kn, vm, kr4, sq, sk, sv, knc, vc, kr4c, skc, svc,
                                   l1_w_in)
    y_s = _out_ln_sample(o_s, w_out, y, mods, l0_ln1_g, l0_ln1_b)
    y, moe_w1, moe_w3 = _ffn_ln(y_p, y_s, mods, ffn_w1, ffn_w3, ffn_w2, l0_ln2_g, l0_ln2_b,
                                l1_moe_w1.reshape(N_EXPERTS * D, E_FF), l1_moe_w3.reshape(N_EXPERTS * D, E_FF))
    new_ckv = ckv.reshape(BATCH, SEQ, MLA_KV_LORA)
    new_krope = jnp.transpose(krope_t, (0, 2, 1))
    new_sk = jnp.transpose(sk_t, (0, 3, 1, 2))
    new_sv = jnp.transpose(sv_t, (0, 3, 1, 2))

    mods = _adaln(cond, l1_ada_w, l1_ada_b)
    q, k, v, k_heads, v_heads = _l1_in(y, mods, w_in_l1)
    w_out = l1_w_out.astype(BF16)
    y_p = _l1_attn_prompt(q, k, v, y, mods, w_out, l1_ln1_g, l1_ln1_b)
    o_s, moe_w2 = _l1_attn_sample(q, k, v, cache_l1_na_k.reshape(DEC_BATCH, PAST, D),
                                  cache_l1_na_v.reshape(DEC_BATCH, PAST, D), _na_bias_rows(l1_na_rel_bias),
                                  l1_moe_w2.reshape(N_EXPERTS * E_FF, D))
    y_s = _out_ln_sample(o_s, w_out, y, mods, l1_ln1_g, l1_ln1_b)
    y_p, y_s = _moe_ln(y_p, y_s, mods, l1_moe_router_w, l1_moe_router_b, moe_w1.reshape(N_EXPERTS, D, E_FF),
                       moe_w3.reshape(N_EXPERTS, D, E_FF), moe_w2.reshape(N_EXPERTS, E_FF, D),
                       l1_ln2_g, l1_ln2_b)
    new_k = jnp.transpose(k_heads, (0, 3, 1, 2))
    new_v = jnp.transpose(v_heads, (0, 3, 1, 2))

    return (y_p.reshape(BATCH, SEQ, D), y_s.reshape(DEC_BATCH, DEC_SEQ, D),
            new_ckv, new_krope, new_sk, new_sv, new_k, new_v)
```

```python
import functools

import jax
import jax.numpy as jnp
import numpy as np
from jax import lax
from jax.experimental import pallas as pl
from jax.experimental.pallas import tpu as pltpu

F32 = jnp.float32
BF16 = jnp.bfloat16

D = 1024
BATCH, SEQ = 32, 256
DEC_BATCH, DEC_SEQ = 2, 1024
PAST = 256
GRID_W = 64
T_P = BATCH * SEQ
T_S = DEC_BATCH * DEC_SEQ
T = T_P + T_S
N_COND = 1 + DEC_BATCH

MLA_HEADS, MLA_Q_LORA, MLA_KV_LORA, MLA_NOPE, MLA_ROPE, MLA_V = 8, 384, 256, 64, 32, 64
SWA_HEADS, SWA_KV_HEADS, SWA_WINDOW, HEAD_DIM = 8, 2, 128, 64
NA_HEADS, NA_WIN_ROWS, NA_WIN_COLS = 16, 8, 16
D_FF, N_EXPERTS, E_FF = 2816, 8, 3584
ROPE_THETA = 10000.0
LN_EPS, RMS_EPS = 1e-5, 1e-6
NEG = -1e30
ALPHA = 4.0 ** 0.25

LANES = 128
TM = 512
NT = T // TM
NP_TILES = T_P // TM
TILES_PER_SAMPLE = DEC_SEQ // TM
TG = 256
P_ROWS = 2 * T + N_EXPERTS * TG
NG = P_ROWS // TG
MXU_N = 256
F_CHUNK = 1792
assert F_CHUNK % MXU_N == 0 and E_FF % F_CHUNK == 0
VMEM_LIMIT = 56 * 1024 * 1024


def _params(*sem):
    return pltpu.CompilerParams(dimension_semantics=sem, vmem_limit_bytes=VMEM_LIMIT)


def _const_spec(shape, single_buffer=False):
    if single_buffer:
        return pl.BlockSpec(shape, lambda *_: (0,) * len(shape), pipeline_mode=pl.Buffered(1))
    return pl.BlockSpec(shape, lambda *_: (0,) * len(shape))


def _cast_specs(shape, steps):
    rows, width = shape
    spec = pl.BlockSpec((rows // steps, width), lambda i, *_: (jnp.minimum(i, steps - 1), 0))
    return spec, jax.ShapeDtypeStruct(shape, BF16)


def _cast_block(i, steps, src_ref, dst_ref):
    @pl.when(i < steps)
    def _():
        dst_ref[...] = src_ref[...].astype(BF16)


ROW_TILE = D // LANES


def _store_rows_tiled(ref, x):
    n = x.shape[0]
    for k in range(ROW_TILE):
        ref[pl.ds(k, n, stride=ROW_TILE), :] = x[:, k * LANES:(k + 1) * LANES]


def _load_rows_tiled(ref, n):
    return jnp.concatenate([ref[pl.ds(k, n, stride=ROW_TILE), :] for k in range(ROW_TILE)], axis=1)


def _cond_of_tile(i, rows=TM):
    per_sample = DEC_SEQ // rows
    return jnp.maximum((i - (T_P // rows - per_sample)) // per_sample, 0)


def _mod_spec(rows=TM):
    return pl.BlockSpec((1, 6, D), lambda i: (_cond_of_tile(i, rows), 0, 0))


def _row_spec(width, rows=TM):
    return pl.BlockSpec((rows, width), lambda i: (i, 0))


def _dot(a, b):
    return jnp.dot(a, b, preferred_element_type=F32)


def _dot_nt(a, b):
    return lax.dot_general(a, b, (((1,), (1,)), ((), ())), preferred_element_type=F32)


def _layer_norm(r, g, b):
    mu = jnp.mean(r, axis=-1, keepdims=True)
    d = r - mu
    var = jnp.mean(d * d, axis=-1, keepdims=True)
    return d * lax.rsqrt(var + LN_EPS) * g + b


def _rms_norm(x, g):
    return x * lax.rsqrt(jnp.mean(x * x, axis=-1, keepdims=True) + RMS_EPS) * g


def _silu(x):
    return x * jax.nn.sigmoid(x)


def _ada_kernel(c_ref, w_ref, b_ref, o_ref):
    s = _silu(c_ref[...]).astype(BF16)
    o_ref[...] = _dot(s, w_ref[...].astype(BF16)) + b_ref[...]


def _adaln(cond, ada_w, ada_b):
    nb = 1536
    out = pl.pallas_call(
        _ada_kernel,
        out_shape=jax.ShapeDtypeStruct((8, 6 * D), F32),
        grid=(6 * D // nb,),
        in_specs=[_const_spec((8, D)), pl.BlockSpec((D, nb), lambda j: (0, j)),
                  pl.BlockSpec((1, nb), lambda j: (0, j))],
        out_specs=pl.BlockSpec((8, nb), lambda j: (0, j)),
        compiler_params=_params("arbitrary"),
        name="adaln",
    )(cond, ada_w, ada_b.reshape(1, 6 * D))
    return out[:N_COND].reshape(N_COND, 6, D)


def _rope_tables(head_dim, width):
    half = head_dim // 2
    nf = half // 2
    lane = np.arange(width)
    d = lane % head_dim
    dd = d % half
    f = dd % nf
    inv = np.float32(ROPE_THETA) ** (-f.astype(np.float32) / np.float32(nf))
    t = np.arange(DEC_SEQ)
    pos = np.where((d // half)[None, :] == 0, (t // GRID_W)[:, None], (t % GRID_W)[:, None])
    ang = pos.astype(np.float32) * inv[None, :].astype(np.float32)
    cos, sin = np.cos(ang), np.sin(ang)
    first = (dd < nf)[None, :]
    zero = np.float32(0.0)
    return (jnp.asarray(cos, F32), jnp.asarray(np.where(first, -sin, zero), F32),
            jnp.asarray(np.where(first, zero, sin), F32))


def _rope(x, cos, sin_up, sin_dn, nf):
    w = x.shape[-1]
    return x * cos + pltpu.roll(x, w - nf, 1) * sin_up + pltpu.roll(x, nf, 1) * sin_dn


L0_COLS = MLA_Q_LORA + MLA_KV_LORA + 512 + 128 + 128 + 128


def _l0_in_kernel(xp_ref, xs_ref, mod_ref, win_ref, qn_ref, wq_ref, kvn_ref, wkv_ref,
                  c8_ref, su8_ref, sd8_ref, c16_ref, su16_ref, sd16_ref,
                  y_o, qnope_o, qrope_o, knope_o, vmla_o, kr4_o, sq_o, sk_o, sv_o,
                  ckv_o, krt_o, skt_o, svt_o):
    i = pl.program_id(0)
    m = mod_ref[0]
    x = jnp.where(i < NP_TILES, xp_ref[...], xs_ref[...])
    y_o[...] = x
    h = (x * (1.0 + m[1:2]) + m[0:1]).astype(BF16)
    z = _dot(h, win_ref[...])
    q_lat = z[:, 0:384]
    kv_lat = z[:, 384:640]
    sq = z[:, 640:1152]
    sk = z[:, 1152:1280]
    sv = z[:, 1280:1408]
    kr4 = z[:, 1408:1536]
    q = _dot(_rms_norm(q_lat, qn_ref[...]).astype(BF16), wq_ref[...])
    c_kv = _rms_norm(kv_lat, kvn_ref[...])
    kv = _dot(c_kv.astype(BF16), wkv_ref[...])
    q = q * MLA_QMUL
    sq = sq * HD_QMUL
    qnope_o[...] = q[:, 0:512].astype(BF16)
    knope_o[...] = kv[:, 0:512].astype(BF16)
    vmla_o[...] = kv[:, 512:1024].astype(BF16)
    sv_o[...] = sv
    q_rope = q[:, 512:768]

    @pl.when(i < NP_TILES)
    def _():
        qrope_o[...] = q_rope.astype(BF16)
        kr4_o[...] = kr4
        sq_o[...] = sq.astype(BF16)
        sk_o[...] = sk
        ckv_o[...] = c_kv
        for bb in range(TM // SEQ):
            rows = slice(bb * SEQ, (bb + 1) * SEQ)
            krt_o[bb] = kr4[rows].T[:MLA_ROPE]
            skt = sk[rows].T
            svt = sv[rows].T
            for g in range(SWA_KV_HEADS):
                skt_o[bb, g] = skt[g * HEAD_DIM:(g + 1) * HEAD_DIM]
                svt_o[bb, g] = svt[g * HEAD_DIM:(g + 1) * HEAD_DIM]

    @pl.when(i >= NP_TILES)
    def _():
        c8, su8, sd8 = c8_ref[...], su8_ref[...], sd8_ref[...]
        c16, su16, sd16 = c16_ref[...], su16_ref[...], sd16_ref[...]
        qrope_o[...] = _rope(q_rope, c8, su8, sd8, 8).astype(BF16)
        kr4_o[...] = _rope(kr4, c8[:, :128], su8[:, :128], sd8[:, :128], 8)
        sq_o[...] = _rope(sq, c16, su16, sd16, 16).astype(BF16)
        sk_o[...] = _rope(sk, c16[:, :128], su16[:, :128], sd16[:, :128], 16)


def _prompt_tile_spec(width):
    return pl.BlockSpec((TM, width), lambda i, *_: (jnp.minimum(i, NP_TILES - 1), 0))


def _sample_tile_spec(width):
    return pl.BlockSpec((TM, width), lambda i, *_: (jnp.maximum(i - NP_TILES, 0), 0))


def _l0_in(x_prompt, x_sample, mods, w_in_r, q_norm, w_q_up_r, kv_norm, w_kv_up_r, tabs8, tabs16):
    def tab_spec(width):
        return pl.BlockSpec((TM, width), lambda i: (jnp.maximum(i - NP_TILES, 0) % TILES_PER_SAMPLE, 0))

    out_shape = [
        jax.ShapeDtypeStruct((T, D), F32),
        jax.ShapeDtypeStruct((T, 512), BF16),
        jax.ShapeDtypeStruct((T, 256), BF16),
        jax.ShapeDtypeStruct((T, 512), BF16),
        jax.ShapeDtypeStruct((T, 512), BF16),
        jax.ShapeDtypeStruct((T, 128), F32),
        jax.ShapeDtypeStruct((T, 512), BF16),
        jax.ShapeDtypeStruct((T, 128), F32),
        jax.ShapeDtypeStruct((T, 128), F32),
    ]
    per_tile = TM // SEQ
    prompt_block = lambda *dims: pl.BlockSpec((per_tile,) + dims,
                                              lambda i: (jnp.minimum(i, NP_TILES - 1),) + (0,) * len(dims))
    cache_shape = [
        jax.ShapeDtypeStruct((T_P, MLA_KV_LORA), F32),
        jax.ShapeDtypeStruct((BATCH, MLA_ROPE, SEQ), F32),
        jax.ShapeDtypeStruct((BATCH, SWA_KV_HEADS, HEAD_DIM, SEQ), F32),
        jax.ShapeDtypeStruct((BATCH, SWA_KV_HEADS, HEAD_DIM, SEQ), F32),
    ]
    cache_specs = [_prompt_tile_spec(MLA_KV_LORA), prompt_block(MLA_ROPE, SEQ),
                   prompt_block(SWA_KV_HEADS, HEAD_DIM, SEQ), prompt_block(SWA_KV_HEADS, HEAD_DIM, SEQ)]
    return pl.pallas_call(
        _l0_in_kernel,
        out_shape=out_shape + cache_shape,
        grid=(NT,),
        in_specs=[_prompt_tile_spec(D), _sample_tile_spec(D), _mod_spec(), _const_spec((D, L0_COLS)),
                  _const_spec((1, MLA_Q_LORA)), _const_spec((MLA_Q_LORA, 768)),
                  _const_spec((1, MLA_KV_LORA)), _const_spec((MLA_KV_LORA, 1024)),
                  tab_spec(256), tab_spec(256), tab_spec(256),
                  tab_spec(512), tab_spec(512), tab_spec(512)],
        out_specs=[_row_spec(s.shape[1]) for s in out_shape] + cache_specs,
        compiler_params=_params("arbitrary"),
        name="l0_in_proj",
    )(x_prompt, x_sample, mods, w_in_r, q_norm.reshape(1, -1), w_q_up_r, kv_norm.reshape(1, -1), w_kv_up_r,
      *tabs8, *tabs16)


def _kv_up_kernel(c_ref, w_ref, k_o, v_o):
    kv = _dot(c_ref[...].astype(BF16), w_ref[...])
    k_o[...] = kv[:, 0:512].astype(BF16)
    v_o[...] = kv[:, 512:1024].astype(BF16)


def _kv_up(ckv, w_kv_up_r):
    n = ckv.shape[0]
    return pl.pallas_call(
        _kv_up_kernel,
        out_shape=[jax.ShapeDtypeStruct((n, 512), BF16)] * 2,
        grid=(1,),
        in_specs=[_const_spec((n, MLA_KV_LORA)), _const_spec((MLA_KV_LORA, 1024))],
        out_specs=[_const_spec((n, 512))] * 2,
        compiler_params=_params("arbitrary"),
        name="l0_ctx_kv_up",
    )(ckv, w_kv_up_r)


def _lane():
    return lax.broadcasted_iota(jnp.int32, (1, LANES), 1)


def _softmax_weights(scores, extra_logit=None):
    m = jnp.max(scores[0], axis=1, keepdims=True)
    for s in scores[1:]:
        m = jnp.maximum(m, jnp.max(s, axis=1, keepdims=True))
    if extra_logit is not None:
        m = jnp.maximum(m, extra_logit)
    weights = [jnp.exp2(s - m) for s in scores]
    den = None
    for e in weights:
        d = jnp.sum(e, axis=1, keepdims=True)
        den = d if den is None else den + d
    if extra_logit is not None:
        den = den + jnp.exp2(extra_logit - m)
    return [e.astype(BF16) for e in weights], den


def _head_pair(q2, keys, values, masks=None, biases=None, sinks=None, q_extra=None, joint_values=False):
    lo = _lane() < 64
    half = [lo, jnp.logical_not(lo)]
    zero = jnp.zeros_like(q2)
    weights, dens, outs = [], [], []
    for hh in range(2):
        qm = jnp.where(half[hh], q2, zero)
        if q_extra is not None:
            qm = jnp.concatenate([qm, q_extra[hh]], axis=1)
        scores = []
        for n, k in enumerate(keys):
            s = _dot_nt(qm, k)
            if biases is not None and biases[n] is not None:
                s = s + biases[n][hh]
            if masks is not None and masks[n] is not None:
                s = jnp.where(masks[n], s, NEG)
            scores.append(s)
        w, den = _softmax_weights(scores, None if sinks is None else sinks[hh])
        if joint_values:
            weights += w
            dens.append(den)
        else:
            acc = None
            for e, v in zip(w, values):
                a = _dot(e, v)
                acc = a if acc is None else acc + a
            outs.append(acc / den)
    if not joint_values:
        return jnp.where(lo, outs[0], outs[1])
    vals = [jnp.where(half[hh], v, jnp.zeros_like(v)) for hh in range(2) for v in values]
    acc = _dot(jnp.concatenate(weights, axis=1), jnp.concatenate(vals, axis=0))
    return acc / jnp.where(lo, dens[0], dens[1])


def _dup_halves(x):
    lo = _lane() < 64
    sw = pltpu.roll(x, 64, 1)
    return jnp.where(lo, x, sw), jnp.where(lo, sw, x)


LOG2E = 1.4426950408889634
MLA_QMUL = (MLA_NOPE + MLA_ROPE) ** -0.5 * LOG2E
HD_QMUL = HEAD_DIM ** -0.5 * LOG2E


def _mla_pairs(qn_ref, qr_ref, key_sets, o_ref, joint_values=False):
    lane = _lane()
    for j in range(MLA_HEADS // 2):
        cols = slice(128 * j, 128 * (j + 1))
        qr = qr_ref[:, 128 * (j // 2):128 * (j // 2 + 1)]
        zero = jnp.zeros_like(qr)
        q_extra = [jnp.where((lane // MLA_ROPE) == ((2 * j + hh) % 4), qr, zero) for hh in range(2)]
        keys = [jnp.concatenate([kn[:, cols], kr4], axis=1) for kn, kr4, _ in key_sets]
        values = [v[:, cols] for _, _, v in key_sets]
        o = _head_pair(qn_ref[:, cols], keys, values, q_extra=q_extra, joint_values=joint_values)
        o_ref[:, cols] = o.astype(BF16)


def _swa_pairs(sink_ref, sq_ref, key_sets, masks, o_ref, joint_values=False):
    kd = [[a.astype(BF16) for a in _dup_halves(k)] for k, _ in key_sets]
    vd = [[a.astype(BF16) for a in _dup_halves(v)] for _, v in key_sets]
    for g in range(SWA_KV_HEADS):
        for u in range(2):
            c = 2 * g + u
            cols = slice(128 * c, 128 * (c + 1))
            sinks = [sink_ref[2 * c + hh] * LOG2E for hh in range(2)]
            o = _head_pair(sq_ref[:, cols], [k[g] for k in kd], [v[g] for v in vd],
                           masks=masks, sinks=sinks, joint_values=joint_values)
            o_ref[:, 512 + 128 * c:512 + 128 * (c + 1)] = o.astype(BF16)


def _cast_steps(rows, max_steps):
    steps = max_steps
    while rows % (16 * steps):
        steps //= 2
    return steps


def _proj_residual_ln(o, w_ref, y, m, g_ref, b_ref):
    return _layer_norm(ALPHA * y + m[2:3] * _dot(o, w_ref[...]), g_ref[...], b_ref[...])


PROMPT_PER_STEP = 2
PROMPT_ROWS = PROMPT_PER_STEP * SEQ
PROMPT_STEPS = BATCH // PROMPT_PER_STEP


def _prompt_rows_spec(width):
    return pl.BlockSpec((PROMPT_ROWS, width), lambda s, *_: (s, 0))


def _prompt_epilogue_specs():
    return [_prompt_rows_spec(D), pl.BlockSpec((1, 6, D), lambda s, *_: (0, 0, 0)),
            pl.BlockSpec((D, D), lambda s, *_: (0, 0)), pl.BlockSpec((1, D), lambda s, *_: (0, 0)),
            pl.BlockSpec((1, D), lambda s, *_: (0, 0))]


def _l0_attn_prompt_kernel(cast_steps, sink_ref, qn_ref, qr_ref, kn_ref, v_ref, kr4_ref, sq_ref, sk_ref, sv_ref,
                           y_ref, mod_ref, w_ref, g_ref, b_ref, wf1_ref, wf2_ref, wf3_ref,
                           y1_o, wb1_ref, wb2_ref, wb3_ref, o_scr):
    for bb in range(PROMPT_PER_STEP):
        rows = pl.ds(bb * SEQ, SEQ)
        o = o_scr.at[rows]
        _mla_pairs(qn_ref.at[rows], qr_ref.at[rows],
                   [(kn_ref.at[rows], kr4_ref[rows, :].astype(BF16), v_ref.at[rows])], o)
        _swa_pairs(sink_ref, sq_ref.at[rows], [(sk_ref[rows, :], sv_ref[rows, :])], None, o)
        y1_o[rows, :] = _proj_residual_ln(o_scr[rows, :], w_ref, y_ref[rows, :], mod_ref[0], g_ref, b_ref)
    for steps, wf_ref, wb_ref in zip(cast_steps, (wf1_ref, wf2_ref, wf3_ref), (wb1_ref, wb2_ref, wb3_ref)):
        _cast_block(pl.program_id(0), steps, wf_ref, wb_ref)


def _l0_attn_prompt(sink, qn, qr, kn, vm, kr4, sq, sk, sv, y, mods, w_out, ln_g, ln_b, weights_f32):
    spec = _prompt_rows_spec
    cast_steps = tuple(_cast_steps(w.shape[0], PROMPT_STEPS) for w in weights_f32)
    casts = [_cast_specs(w.shape, s) for w, s in zip(weights_f32, cast_steps)]
    return pl.pallas_call(
        functools.partial(_l0_attn_prompt_kernel, cast_steps),
        out_shape=[jax.ShapeDtypeStruct((T_P, D), F32)] + [c[1] for c in casts],
        grid_spec=pltpu.PrefetchScalarGridSpec(
            num_scalar_prefetch=1, grid=(PROMPT_STEPS,),
            in_specs=[spec(512), spec(256), spec(512), spec(512), spec(128), spec(512), spec(128), spec(128)]
            + _prompt_epilogue_specs() + [c[0] for c in casts],
            out_specs=[spec(D)] + [c[0] for c in casts],
            scratch_shapes=[pltpu.VMEM((PROMPT_ROWS, D), BF16)]),
        compiler_params=_params("arbitrary"),
        name="l0_attn_prompt",
    )(sink, qn, qr, kn, vm, kr4, sq, sk, sv, y, mods, w_out, ln_g.reshape(1, D), ln_b.reshape(1, D), *weights_f32)


TQ_S = 256


def _l0_attn_sample_kernel(sink_ref, qn_ref, qr_ref, sq_ref, kn_ref, v_ref, kr4_ref, sk_ref, sv_ref,
                           knc_ref, vc_ref, kr4c_ref, skc_ref, svc_ref, wf_ref, o_ref, wb_ref):
    i = pl.program_id(1)
    wb_ref[...] = wf_ref[...].astype(BF16)
    _mla_pairs(qn_ref, qr_ref,
               [(knc_ref, kr4c_ref[...].astype(BF16), vc_ref), (kn_ref, kr4_ref[...].astype(BF16), v_ref)],
               o_ref, joint_values=True)
    span = TQ_S + 2 * SWA_WINDOW
    start = pl.multiple_of(jnp.clip(i * TQ_S - SWA_WINDOW, 0, DEC_SEQ - span), SWA_WINDOW)
    qpos = i * TQ_S + lax.broadcasted_iota(jnp.int32, (TQ_S, span), 0)
    kpos = start + lax.broadcasted_iota(jnp.int32, (TQ_S, span), 1)
    band = jnp.abs(qpos - kpos) <= SWA_WINDOW
    keys = pl.ds(start, span)
    _swa_pairs(sink_ref, sq_ref, [(skc_ref[...], svc_ref[...]), (sk_ref[keys, :], sv_ref[keys, :])],
               [None, band], o_ref, joint_values=True)


def _l0_attn_sample(sink, qn, qr, kn, vm, kr4, sq, sk, sv, knc, vc, kr4c, skc, svc, w_f32):
    nq = DEC_SEQ // TQ_S
    qspec = lambda w: pl.BlockSpec((TQ_S, w), lambda b, i, *_: (T_P // TQ_S + b * nq + i, 0))
    kspec = lambda w: pl.BlockSpec((DEC_SEQ, w), lambda b, i, *_: (T_P // DEC_SEQ + b, 0))
    cspec = lambda w: pl.BlockSpec((PAST, w), lambda b, i, *_: (b, 0))
    w_spec = pl.BlockSpec((w_f32.shape[0] // (DEC_BATCH * nq), w_f32.shape[1]), lambda b, i, *_: (b * nq + i, 0))
    return pl.pallas_call(
        _l0_attn_sample_kernel,
        out_shape=[jax.ShapeDtypeStruct((T_S, D), BF16), jax.ShapeDtypeStruct(w_f32.shape, BF16)],
        grid_spec=pltpu.PrefetchScalarGridSpec(
            num_scalar_prefetch=1, grid=(DEC_BATCH, nq),
            in_specs=[qspec(512), qspec(256), qspec(512),
                      kspec(512), kspec(512), kspec(128), kspec(128), kspec(128),
                      cspec(512), cspec(512), cspec(128), cspec(128), cspec(128), w_spec],
            out_specs=[pl.BlockSpec((TQ_S, D), lambda b, i, *_: (b * nq + i, 0)), w_spec]),
        compiler_params=_params("arbitrary", "arbitrary"),
        name="l0_attn_sample",
    )(sink, qn, qr, sq, kn, vm, kr4, sk, sv, knc, vc, kr4c, skc, svc, w_f32)


def _out_ln_sample_kernel(o_ref, w_ref, y_ref, mod_ref, g_ref, b_ref, out_ref):
    out_ref[...] = _proj_residual_ln(o_ref[...], w_ref, y_ref[...], mod_ref[0], g_ref, b_ref)


def _out_ln_sample(o_sample, w_out, y, mods, ln_g, ln_b):
    return pl.pallas_call(
        _out_ln_sample_kernel,
        out_shape=jax.ShapeDtypeStruct((T_S, D), F32),
        grid=(T_S // TM,),
        in_specs=[_row_spec(D), _const_spec((D, D)),
                  pl.BlockSpec((TM, D), lambda i: (NP_TILES + i, 0)),
                  pl.BlockSpec((1, 6, D), lambda i: (1 + i // TILES_PER_SAMPLE, 0, 0)),
                  _const_spec((1, D)), _const_spec((1, D))],
        out_specs=_row_spec(D),
        compiler_params=_params("arbitrary"),
        name="out_proj_ln_sample",
    )(o_sample, w_out, y, mods, ln_g.reshape(1, D), ln_b.reshape(1, D))


def _two_part_specs(width, rows=TM):
    n_prompt = T_P // rows
    return [pl.BlockSpec((rows, width), lambda i, *_: (jnp.minimum(i, n_prompt - 1), 0)),
            pl.BlockSpec((rows, width), lambda i, *_: (jnp.maximum(i - n_prompt, 0), 0))]


def _two_part_rows(i, p_ref, s_ref, rows=TM):
    return jnp.where(i < T_P // rows, p_ref[...], s_ref[...])


FFN_CHUNK = D_FF
assert FFN_CHUNK % MXU_N == 0 and D_FF % FFN_CHUNK == 0
FFN_TM = 256
FFN_CAST_STEPS = 32


def _ffn_ln_kernel(yp_ref, ys_ref, mod_ref, w1_ref, w3_ref, w2_ref, g_ref, b_ref, wfa_ref, wfb_ref,
                   out_ref, wba_ref, wbb_ref):
    i = pl.program_id(0)
    _cast_block(i, FFN_CAST_STEPS, wfa_ref, wba_ref)
    _cast_block(i, FFN_CAST_STEPS, wfb_ref, wbb_ref)
    m = mod_ref[0]
    y = _two_part_rows(i, yp_ref, ys_ref, FFN_TM)
    h = (y * (1.0 + m[4:5]) + m[3:4]).astype(BF16)
    acc = None
    for c in range(D_FF // FFN_CHUNK):
        cols = slice(c * FFN_CHUNK, (c + 1) * FFN_CHUNK)
        a = _dot(h, w1_ref[:, cols])
        g = _dot(h, w3_ref[:, cols])
        part = _dot((_silu(a) * g).astype(BF16), w2_ref[cols, :])
        acc = part if acc is None else acc + part
    out_ref[...] = _layer_norm(ALPHA * y + m[5:6] * acc, g_ref[...], b_ref[...])


def _ffn_ln(y_prompt, y_sample, mods, w1, w3, w2, ln_g, ln_b, wa_f32, wb_f32):
    wa_spec, wa_shape = _cast_specs(wa_f32.shape, FFN_CAST_STEPS)
    wb_spec, wb_shape = _cast_specs(wb_f32.shape, FFN_CAST_STEPS)
    rows = _row_spec(D, FFN_TM)
    return pl.pallas_call(
        _ffn_ln_kernel,
        out_shape=[jax.ShapeDtypeStruct((T, D), F32), wa_shape, wb_shape],
        grid=(T // FFN_TM,),
        in_specs=_two_part_specs(D, FFN_TM)
        + [_mod_spec(FFN_TM), _const_spec((D, D_FF), True), _const_spec((D, D_FF), True),
                  _const_spec((D_FF, D), True), _const_spec((1, D)), _const_spec((1, D)), wa_spec, wb_spec],
        out_specs=[rows, wa_spec, wb_spec],
        compiler_params=_params("arbitrary"),
        name="ffn_ln",
    )(y_prompt, y_sample, mods, w1, w3, w2, ln_g.reshape(1, D), ln_b.reshape(1, D), wa_f32, wb_f32)


def _l1_in_kernel(y_ref, mod_ref, w_ref, q_o, k_o, v_o, *head_outs):
    m = mod_ref[0]
    for bb in range(TM // SEQ):
        rows = slice(bb * SEQ, (bb + 1) * SEQ)
        h = (y_ref[rows, :] * (1.0 + m[1:2]) + m[0:1]).astype(BF16)
        z = _dot(h, w_ref[...])
        k = z[:, D:2 * D]
        v = z[:, 2 * D:3 * D]
        q_o[rows, :] = (z[:, 0:D] * HD_QMUL).astype(BF16)
        k_o[rows, :] = k.astype(BF16)
        v_o[rows, :] = v.astype(BF16)
        if head_outs:
            kh_o, vh_o = head_outs
            for j in range(NA_HEADS // 2):
                cols = slice(j * LANES, (j + 1) * LANES)
                kt = k[:, cols].T
                vt = v[:, cols].T
                for hh in range(2):
                    drows = slice(hh * HEAD_DIM, (hh + 1) * HEAD_DIM)
                    kh_o[bb, 2 * j + hh] = kt[drows]
                    vh_o[bb, 2 * j + hh] = vt[drows]


def _l1_in(y, mods, w_in):
    heads = pl.BlockSpec((TM // SEQ, NA_HEADS, HEAD_DIM, SEQ), lambda i: (i, 0, 0, 0))
    qkv_p = pl.pallas_call(
        _l1_in_kernel,
        out_shape=[jax.ShapeDtypeStruct((T_P, D), BF16)] * 3
        + [jax.ShapeDtypeStruct((BATCH, NA_HEADS, HEAD_DIM, SEQ), F32)] * 2,
        grid=(NP_TILES,),
        in_specs=[_row_spec(D), pl.BlockSpec((1, 6, D), lambda i: (0, 0, 0)), _const_spec((D, 3 * D))],
        out_specs=[_row_spec(D)] * 3 + [heads, heads],
        compiler_params=_params("arbitrary"),
        name="l1_in_proj_prompt",
    )(y, mods, w_in)
    qkv_s = pl.pallas_call(
        _l1_in_kernel,
        out_shape=[jax.ShapeDtypeStruct((T_S, D), BF16)] * 3,
        grid=(T_S // TM,),
        in_specs=[pl.BlockSpec((TM, D), lambda i: (NP_TILES + i, 0)),
                  pl.BlockSpec((1, 6, D), lambda i: (1 + i // TILES_PER_SAMPLE, 0, 0)), _const_spec((D, 3 * D))],
        out_specs=[_row_spec(D)] * 3,
        compiler_params=_params("arbitrary"),
        name="l1_in_proj_sample",
    )(y, mods, w_in)
    return qkv_p, qkv_s


def _l1_attn_prompt_kernel(q_ref, k_ref, v_ref, y_ref, mod_ref, w_ref, g_ref, b_ref, y1_o, o_scr):
    for bb in range(PROMPT_PER_STEP):
        rows = pl.ds(bb * SEQ, SEQ)
        for j in range(NA_HEADS // 2):
            cols = slice(128 * j, 128 * (j + 1))
            o = _head_pair(q_ref[rows, cols], [k_ref[rows, cols]], [v_ref[rows, cols]])
            o_scr[rows, cols] = o.astype(BF16)
        y1_o[rows, :] = _proj_residual_ln(o_scr[rows, :], w_ref, y_ref[rows, :], mod_ref[0], g_ref, b_ref)


def _l1_attn_prompt(q, k, v, y, mods, w_out, ln_g, ln_b):
    spec = _prompt_rows_spec(D)
    return pl.pallas_call(
        _l1_attn_prompt_kernel,
        out_shape=jax.ShapeDtypeStruct((T_P, D), F32),
        grid=(PROMPT_STEPS,),
        in_specs=[spec, spec, spec] + _prompt_epilogue_specs(),
        out_specs=spec,
        scratch_shapes=[pltpu.VMEM((PROMPT_ROWS, D), BF16)],
        compiler_params=_params("arbitrary"),
        name="l1_attn_prompt",
    )(q, k, v, y, mods, w_out, ln_g.reshape(1, D), ln_b.reshape(1, D))


NA_ROWS = DEC_SEQ // GRID_W
NA_TILE_ROWS = 4
NA_TQ = NA_TILE_ROWS * GRID_W
NA_DR = 2 * NA_WIN_ROWS - 1
COL_SPAN = 2 * GRID_W - 1


def _na_window_start(r):
    return max(0, min(r - NA_WIN_ROWS // 2, NA_ROWS - NA_WIN_ROWS))


def _na_span(t):
    first = _na_window_start(t * NA_TILE_ROWS)
    last = _na_window_start((t + 1) * NA_TILE_ROWS - 1) + NA_WIN_ROWS
    n = last - first + (last - first) % 2
    return min(first, NA_ROWS - n), n


def _l1_attn_sample_kernel(q_ref, k_ref, v_ref, kc_ref, vc_ref, ext_ref, wf_ref, o_ref, wb_ref):
    wb_ref[...] = wf_ref[...].astype(BF16)
    kc = kc_ref[0].astype(BF16)
    vc = vc_ref[0].astype(BF16)
    lo = _lane() < GRID_W
    qcol = lax.broadcasted_iota(jnp.int32, (GRID_W, LANES), 0)
    kcol = lax.broadcasted_iota(jnp.int32, (GRID_W, LANES), 1) % GRID_W
    cs = jnp.clip(qcol - NA_WIN_COLS // 2, 0, GRID_W - NA_WIN_COLS)
    col_ok = jnp.logical_and(kcol >= cs, kcol < cs + NA_WIN_COLS)
    neg = jnp.full((GRID_W, LANES), NEG, F32)

    def bias_tile(hh, d, half):
        x = jnp.broadcast_to(ext_ref[hh, d:d + 1, :], (GRID_W, LANES))
        shift = (LANES - COL_SPAN // 2 + GRID_W * half) % LANES
        return jnp.where(col_ok, pltpu.roll(x, shift, 1, stride=1, stride_axis=0), neg)

    tiles = [[[bias_tile(hh, d, half) for half in range(2)] for d in range(NA_DR)] for hh in range(2)]

    for t in range(NA_ROWS // NA_TILE_ROWS):
        ws, span = _na_span(t)
        keys = slice(ws * GRID_W, (ws + span) * GRID_W)
        kw = k_ref[keys, :].astype(BF16)
        vw = v_ref[keys, :].astype(BF16)
        bias = []
        for hh in range(2):
            rows = []
            for rr in range(NA_TILE_ROWS):
                r = t * NA_TILE_ROWS + rr
                rs = _na_window_start(r)
                blocks = []
                for u in range(span // 2):
                    halves = []
                    for half in range(2):
                        kr = ws + 2 * u + half
                        ok = rs <= kr < rs + NA_WIN_ROWS
                        halves.append(tiles[hh][kr - r + NA_WIN_ROWS - 1][half] if ok else neg)
                    blocks.append(jnp.where(lo, halves[0], halves[1]))
                rows.append(jnp.concatenate(blocks, axis=1))
            bias.append(jnp.concatenate(rows, axis=0))
        qrows = slice(t * NA_TQ, (t + 1) * NA_TQ)
        o = _head_pair(q_ref[qrows, :], [kw, kc], [vw, vc], biases=[bias, None], joint_values=True)
        o_ref[qrows, :] = o.astype(BF16)


def _l1_attn_sample(q, k, v, kc, vc, ext, w_f32):
    steps = (NA_HEADS // 2) * DEC_BATCH
    w_spec = pl.BlockSpec((w_f32.shape[0] // steps, w_f32.shape[1]), lambda j, b: (j * DEC_BATCH + b, 0))
    lat = pl.BlockSpec((DEC_SEQ, LANES), lambda j, b: (b, j))
    ctx = pl.BlockSpec((1, PAST, LANES), lambda j, b: (b, 0, j))
    return pl.pallas_call(
        _l1_attn_sample_kernel,
        out_shape=[jax.ShapeDtypeStruct((T_S, D), BF16), jax.ShapeDtypeStruct(w_f32.shape, BF16)],
        grid=(NA_HEADS // 2, DEC_BATCH),
        in_specs=[lat, lat, lat, ctx, ctx, pl.BlockSpec((2, NA_DR, LANES), lambda j, b: (j, 0, 0)), w_spec],
        out_specs=[lat, w_spec],
        compiler_params=_params("arbitrary", "arbitrary"),
        name="l1_attn_sample",
    )(q, k, v, kc, vc, ext, w_f32)


def _na_bias_rows(rel_bias):
    rb = rel_bias.astype(F32) * LOG2E
    n_lo = GRID_W - 1 - (NA_WIN_COLS - 1)
    n_hi = LANES - n_lo - rb.shape[-1]
    return jnp.concatenate([jnp.repeat(rb[..., :1], n_lo, axis=-1), rb,
                            jnp.repeat(rb[..., -1:], n_hi, axis=-1)], axis=-1)


def _split_bf16(x):
    hi = x.astype(BF16)
    return hi, (x - hi.astype(F32)).astype(BF16)


def _router_kernel(yp_ref, ys_ref, mod_ref, rw_ref, rb_ref, h_o, meta_o, cnt_o, carry_ref):
    i = pl.program_id(0)

    @pl.when(i == 0)
    def _():
        carry_ref[...] = jnp.zeros_like(carry_ref)

    m = mod_ref[0]
    h = _two_part_rows(i, yp_ref, ys_ref) * (1.0 + m[4:5]) + m[3:4]
    _store_rows_tiled(h_o, h)
    h_hi, h_lo = _split_bf16(h)
    w_hi, w_lo = _split_bf16(rw_ref[...])
    logits = _dot(jnp.concatenate([h_hi, h_hi, h_lo], axis=1),
                  jnp.concatenate([w_hi, w_lo, w_hi], axis=0)) + rb_ref[...]
    lane = lax.broadcasted_iota(jnp.int32, (TM, LANES), 1).astype(F32)
    m1 = jnp.max(logits, axis=1, keepdims=True)
    i1 = jnp.min(jnp.where(logits == m1, lane, float(LANES)), axis=1, keepdims=True)
    sel1 = lane == i1
    rest = jnp.where(sel1, -jnp.inf, logits)
    m2 = jnp.max(rest, axis=1, keepdims=True)
    i2 = jnp.min(jnp.where(rest == m2, lane, float(LANES)), axis=1, keepdims=True)
    sel2 = lane == i2
    e2 = jnp.exp(m2 - m1)
    w1 = 1.0 / (1.0 + e2)
    w2 = e2 / (1.0 + e2)
    sel = jnp.logical_or(sel1, sel2)
    rr = lax.broadcasted_iota(jnp.int32, (TM, TM), 0)
    cc = lax.broadcasted_iota(jnp.int32, (TM, TM), 1)
    tri = jnp.where(cc < rr, 1.0, 0.0).astype(BF16)
    ahead = _dot(tri, jnp.where(sel, 1.0, 0.0).astype(BF16)) + carry_ref[...]
    r1 = jnp.sum(jnp.where(sel1, ahead, 0.0), axis=1, keepdims=True)
    r2 = jnp.sum(jnp.where(sel2, ahead, 0.0), axis=1, keepdims=True)
    meta = jnp.where(lane == 0, i1, 0.0)
    meta = jnp.where(lane == 1, i2, meta)
    meta = jnp.where(lane == 2, r1, meta)
    meta = jnp.where(lane == 3, r2, meta)
    meta = jnp.where(lane == 4, w1, meta)
    meta = jnp.where(lane == 5, w2, meta)
    meta_o[...] = meta
    carry_ref[...] = carry_ref[...] + jnp.sum(jnp.where(sel, 1.0, 0.0), axis=0, keepdims=True)
    cnt_o[...] = carry_ref[...]


def _router(y_prompt, y_sample, mods, router_w, router_b):
    rw = jnp.zeros((D, LANES), F32).at[:, :N_EXPERTS].set(router_w)
    rb = jnp.full((1, LANES), -jnp.inf, F32).at[0, :N_EXPERTS].set(router_b)
    return pl.pallas_call(
        _router_kernel,
        out_shape=[jax.ShapeDtypeStruct((T * ROW_TILE, LANES), F32),
                   jax.ShapeDtypeStruct((T, LANES), F32),
                   jax.ShapeDtypeStruct((1, LANES), F32)],
        grid=(NT,),
        in_specs=_two_part_specs(D) + [_mod_spec(), _const_spec((D, LANES)), _const_spec((1, LANES))],
        out_specs=[_row_spec(LANES, TM * ROW_TILE), _row_spec(LANES), _const_spec((1, LANES))],
        scratch_shapes=[pltpu.VMEM((1, LANES), F32)],
        compiler_params=_params("arbitrary"),
        name="moe_router",
    )(y_prompt, y_sample, mods, rw, rb)


DMA_UNROLL = 8
DISPATCH_TM = 2048


def _row_copy(src_ref, src_row, dst_ref, dst_row, sem):
    return pltpu.make_async_copy(src_ref.at[pl.ds(pl.multiple_of(src_row * ROW_TILE, ROW_TILE), ROW_TILE)],
                                 dst_ref.at[pl.ds(pl.multiple_of(dst_row * ROW_TILE, ROW_TILE), ROW_TILE)], sem)


def _group_tile_copy(src_ref, dst_ref, tile, sem):
    start = pl.multiple_of(tile * (TG * ROW_TILE), TG * ROW_TILE)
    return pltpu.make_async_copy(src_ref, dst_ref.at[pl.ds(start, TG * ROW_TILE)], sem)


def _dispatch_kernel(pos_ref, last_ref, na_ref, h_ref, xg_ref, zero_ref, sem, zsem):
    i = pl.program_id(0)

    @pl.when(i == 0)
    def _():
        zero_ref[...] = jnp.zeros_like(zero_ref)
        for e in range(N_EXPERTS):
            @pl.when(last_ref[e] >= 0)
            def _():
                _group_tile_copy(zero_ref, xg_ref, last_ref[e], zsem).start()

        def start_unused(g, carry):
            _group_tile_copy(zero_ref, xg_ref, g, zsem).start()
            return carry

        def wait_one(g, carry):
            _group_tile_copy(zero_ref, xg_ref, 0, zsem).wait()
            return carry

        lax.fori_loop(na_ref[0], NG, start_unused, 0)
        lax.fori_loop(0, na_ref[1], wait_one, 0)

    def issue(r, carry):
        t = i * DISPATCH_TM + r
        _row_copy(h_ref, r, xg_ref, pos_ref[2 * t], sem).start(priority=0)
        _row_copy(h_ref, r, xg_ref, pos_ref[2 * t + 1], sem).start(priority=1)
        return carry

    lax.fori_loop(0, DISPATCH_TM, issue, 0, unroll=DMA_UNROLL)
    for _ in range(2):
        pltpu.make_async_copy(h_ref, xg_ref.at[pl.ds(0, DISPATCH_TM * ROW_TILE)], sem).wait()


def _dispatch(pos, last_tile, tile_counts, h_tiled):
    return pl.pallas_call(
        _dispatch_kernel,
        out_shape=jax.ShapeDtypeStruct((P_ROWS * ROW_TILE, LANES), F32),
        grid_spec=pltpu.PrefetchScalarGridSpec(
            num_scalar_prefetch=3, grid=(T // DISPATCH_TM,),
            in_specs=[pl.BlockSpec((DISPATCH_TM * ROW_TILE, LANES), lambda i, *_: (i, 0))],
            out_specs=pl.BlockSpec(memory_space=pl.ANY),
            scratch_shapes=[pltpu.VMEM((TG * ROW_TILE, LANES), F32), pltpu.SemaphoreType.DMA(()),
                            pltpu.SemaphoreType.DMA(())]),
        compiler_params=_params("arbitrary"),
        name="moe_dispatch",
    )(pos, last_tile, tile_counts, h_tiled)


def _expert_ffn_kernel(te_ref, na_ref, x_ref, w1_ref, w3_ref, w2_ref, o_ref):
    g = pl.program_id(0)

    @pl.when(g < na_ref[0])
    def _():
        x = _load_rows_tiled(x_ref, TG).astype(BF16)
        acc = None
        for c in range(E_FF // F_CHUNK):
            cols = slice(c * F_CHUNK, (c + 1) * F_CHUNK)
            a = _dot(x, w1_ref[0, :, cols])
            b = _dot(x, w3_ref[0, :, cols])
            part = _dot((_silu(a) * b).astype(BF16), w2_ref[0, cols, :])
            acc = part if acc is None else acc + part
        _store_rows_tiled(o_ref, acc)

    @pl.when(g >= na_ref[0])
    def _():
        o_ref[...] = jnp.zeros_like(o_ref)


def _expert_ffn(tile_expert, n_active, xg, w1, w3, w2):
    rows = pl.BlockSpec((TG * ROW_TILE, LANES), lambda g, te, na: (g, 0))
    w_up = pl.BlockSpec((1, D, E_FF), lambda g, te, na: (te[g], 0, 0))
    w_dn = pl.BlockSpec((1, E_FF, D), lambda g, te, na: (te[g], 0, 0))
    return pl.pallas_call(
        _expert_ffn_kernel,
        out_shape=jax.ShapeDtypeStruct((P_ROWS * ROW_TILE, LANES), F32),
        grid_spec=pltpu.PrefetchScalarGridSpec(
            num_scalar_prefetch=2, grid=(NG,),
            in_specs=[rows, w_up, w_up, w_dn],
            out_specs=rows),
        compiler_params=_params("arbitrary"),
        name="moe_expert_ffn",
    )(tile_expert, n_active, xg, w1, w3, w2)


def _combine_ln_kernel(pos_ref, yp_ref, ys_ref, mod_ref, meta_ref, g_ref, b_ref, eo_ref, outp_ref, outs_ref,
                       buf1, buf2, sem):
    i = pl.program_id(0)
    slot = i % 2

    def fetch(tile, into):
        def issue(r, carry):
            t = tile * TM + r
            _row_copy(eo_ref, pos_ref[2 * t], buf1.at[into], r, sem.at[into]).start(priority=0)
            _row_copy(eo_ref, pos_ref[2 * t + 1], buf2.at[into], r, sem.at[into]).start(priority=1)
            return carry

        lax.fori_loop(0, TM, issue, 0, unroll=DMA_UNROLL)

    @pl.when(i == 0)
    def _():
        fetch(0, 0)

    @pl.when(i + 1 < NT)
    def _():
        fetch(i + 1, 1 - slot)

    for buf in (buf1, buf2):
        pltpu.make_async_copy(eo_ref.at[pl.ds(0, TM * ROW_TILE)], buf.at[slot], sem.at[slot]).wait()
    m = mod_ref[0]
    meta = meta_ref[...]
    f = (meta[:, 4:5] * _load_rows_tiled(buf1.at[slot], TM)
         + meta[:, 5:6] * _load_rows_tiled(buf2.at[slot], TM))
    out = _layer_norm(ALPHA * _two_part_rows(i, yp_ref, ys_ref) + m[5:6] * f, g_ref[...], b_ref[...])

    @pl.when(i < NP_TILES)
    def _():
        outp_ref[...] = out

    @pl.when(i >= NP_TILES)
    def _():
        outs_ref[...] = out


def _combine_ln(pos, y_prompt, y_sample, mods, meta, ln_g, ln_b, eo):
    return pl.pallas_call(
        _combine_ln_kernel,
        out_shape=[jax.ShapeDtypeStruct((T_P, D), F32), jax.ShapeDtypeStruct((T_S, D), F32)],
        grid_spec=pltpu.PrefetchScalarGridSpec(
            num_scalar_prefetch=1, grid=(NT,),
            in_specs=_two_part_specs(D)
            + [pl.BlockSpec((1, 6, D), lambda i, *_: (_cond_of_tile(i), 0, 0)),
                      pl.BlockSpec((TM, LANES), lambda i, *_: (i, 0)),
                      pl.BlockSpec((1, D), lambda i, *_: (0, 0)),
                      pl.BlockSpec((1, D), lambda i, *_: (0, 0)),
                      pl.BlockSpec(memory_space=pl.ANY)],
            out_specs=[_prompt_tile_spec(D), _sample_tile_spec(D)],
            scratch_shapes=[pltpu.VMEM((2, TM * ROW_TILE, LANES), F32),
                            pltpu.VMEM((2, TM * ROW_TILE, LANES), F32),
                            pltpu.SemaphoreType.DMA((2,))]),
        compiler_params=_params("arbitrary"),
        name="moe_combine_ln",
    )(pos, y_prompt, y_sample, mods, meta, ln_g.reshape(1, D), ln_b.reshape(1, D), eo)


def _moe_ln(y_prompt, y_sample, mods, router_w, router_b, w1, w3, w2, ln_g, ln_b):
    h_tiled, meta, counts = _router(y_prompt, y_sample, mods, router_w, router_b)
    cnt = counts[0, :N_EXPERTS].astype(jnp.int32)
    tiles = (cnt + TG - 1) // TG
    tile_end = jnp.cumsum(tiles)
    offs = (tile_end - tiles) * TG
    expert = meta[:, 0:2].astype(jnp.int32)
    chosen = expert[:, :, None] == jnp.arange(N_EXPERTS)[None, None, :]
    pos = jnp.sum(jnp.where(chosen, offs[None, None, :], 0), axis=-1) + meta[:, 2:4].astype(jnp.int32)
    pos = pos.reshape(2 * T)
    tile_expert = jnp.sum((jnp.arange(NG)[:, None] >= tile_end[None, :]).astype(jnp.int32), axis=1)
    tile_expert = jnp.minimum(tile_expert, N_EXPERTS - 1)
    n_active = tile_end[-1:].astype(jnp.int32)
    last_tile = jnp.where(tiles > 0, tile_end - 1, -1).astype(jnp.int32)
    n_zeroed = jnp.sum((tiles > 0).astype(jnp.int32)) + NG - tile_end[-1]
    tile_counts = jnp.stack([tile_end[-1], n_zeroed]).astype(jnp.int32)
    xg = _dispatch(pos, last_tile, tile_counts, h_tiled)
    eo = _expert_ffn(tile_expert, n_active, xg, w1, w3, w2)
    return _combine_ln(pos, y_prompt, y_sample, mods, meta, ln_g, ln_b, eo)


def _l0_weight_layouts(w_in, w_q_up, w_kv_up):
    a, b, c = MLA_Q_LORA, MLA_Q_LORA + MLA_KV_LORA, MLA_Q_LORA + MLA_KV_LORA + MLA_ROPE
    k_rope = w_in[:, b:c]
    w_in_r = jnp.concatenate([w_in[:, :b], w_in[:, c:], k_rope, k_rope, k_rope, k_rope], axis=1)
    wq = w_q_up.reshape(MLA_Q_LORA, MLA_HEADS, MLA_NOPE + MLA_ROPE)
    w_q_up_r = jnp.concatenate([wq[:, :, :MLA_NOPE].reshape(MLA_Q_LORA, -1),
                                wq[:, :, MLA_NOPE:].reshape(MLA_Q_LORA, -1)], axis=1)
    wkv = w_kv_up.reshape(MLA_KV_LORA, MLA_HEADS, MLA_NOPE + MLA_V)
    w_kv_up_r = jnp.concatenate([wkv[:, :, :MLA_NOPE].reshape(MLA_KV_LORA, -1),
                                 wkv[:, :, MLA_NOPE:].reshape(MLA_KV_LORA, -1)], axis=1)
    return w_in_r.astype(BF16), w_q_up_r.astype(BF16), w_kv_up_r.astype(BF16)


def kernel(x_prompt, x_sample, cache_l0_mla_ckv, cache_l0_mla_krope, cache_l0_swa_k, cache_l0_swa_v,
           cache_l1_na_k, cache_l1_na_v, c, c_ctx,
           l0_ada_w, l0_ada_b, l0_w_in, l0_mla_q_norm, l0_mla_w_q_up, l0_mla_kv_norm, l0_mla_w_kv_up,
           l0_swa_sink, l0_w_out, l0_ln1_g, l0_ln1_b, l0_ffn_w1, l0_ffn_w3, l0_ffn_w2, l0_ln2_g, l0_ln2_b,
           l1_ada_w, l1_ada_b, l1_w_in, l1_na_rel_bias, l1_w_out, l1_ln1_g, l1_ln1_b,
           l1_moe_router_w, l1_moe_router_b, l1_moe_w1, l1_moe_w3, l1_moe_w2, l1_ln2_g, l1_ln2_b):
    cond = jnp.concatenate([c_ctx[None, :], c, jnp.zeros((8 - N_COND, D), F32)], axis=0)

    mods = _adaln(cond, l0_ada_w, l0_ada_b)
    w_in_r, w_q_up_r, w_kv_up_r = _l0_weight_layouts(l0_w_in, l0_mla_w_q_up, l0_mla_w_kv_up)
    y, qn, qr, kn, vm, kr4, sq, sk, sv, ckv, krope_t, sk_t, sv_t = _l0_in(
        x_prompt.reshape(T_P, D), x_sample.reshape(T_S, D), mods, w_in_r, l0_mla_q_norm, w_q_up_r,
        l0_mla_kv_norm, w_kv_up_r, _rope_tables(MLA_ROPE, 256), _rope_tables(HEAD_DIM, 512))
    knc, vc = _kv_up(cache_l0_mla_ckv.reshape(DEC_BATCH * PAST, MLA_KV_LORA), w_kv_up_r)
    kr4c = jnp.tile(cache_l0_mla_krope.reshape(DEC_BATCH * PAST, MLA_ROPE), (1, 4))
    skc = cache_l0_swa_k.reshape(DEC_BATCH * PAST, SWA_KV_HEADS * HEAD_DIM)
    svc = cache_l0_swa_v.reshape(DEC_BATCH * PAST, SWA_KV_HEADS * HEAD_DIM)
    w_out = l0_w_out.astype(BF16)
    y_p, ffn_w1, ffn_w3, ffn_w2 = _l0_attn_prompt(l0_swa_sink, qn, qr, kn, vm, kr4, sq, sk, sv,
                                                  y, mods, w_out, l0_ln1_g, l0_ln1_b,
                                                  [l0_ffn_w1, l0_ffn_w3, l0_ffn_w2])
    o_s, w_in_l1 = _l0_attn_sample(l0_swa_sink, qn, qr, kn, vm, kr4, sq, sk, sv, knc, vc, kr4c, skc, svc,
                                   l1_w_in)
    y_s = _out_ln_sample(o_s, w_out, y, mods, l0_ln1_g, l0_ln1_b)
    y, moe_w1, moe_w3 = _ffn_ln(y_p, y_s, mods, ffn_w1, ffn_w3, ffn_w2, l0_ln2_g, l0_ln2_b,
                                l1_moe_w1.reshape(N_EXPERTS * D, E_FF), l1_moe_w3.reshape(N_EXPERTS * D, E_FF))
    new_ckv = ckv.reshape(BATCH, SEQ, MLA_KV_LORA)
    new_krope = jnp.transpose(krope_t, (0, 2, 1))
    new_sk = jnp.transpose(sk_t, (0, 3, 1, 2))
    new_sv = jnp.transpose(sv_t, (0, 3, 1, 2))

    mods = _adaln(cond, l1_ada_w, l1_ada_b)
    (q_p, k_p, v_p, k_heads, v_heads), (q_s, k_s, v_s) = _l1_in(y, mods, w_in_l1)
    w_out = l1_w_out.astype(BF16)
    y_p = _l1_attn_prompt(q_p, k_p, v_p, y, mods, w_out, l1_ln1_g, l1_ln1_b)
    o_s, moe_w2 = _l1_attn_sample(q_s, k_s, v_s, cache_l1_na_k.reshape(DEC_BATCH, PAST, D),
                                  cache_l1_na_v.reshape(DEC_BATCH, PAST, D), _na_bias_rows(l1_na_rel_bias),
                                  l1_moe_w2.reshape(N_EXPERTS * E_FF, D))
    y_s = _out_ln_sample(o_s, w_out, y, mods, l1_ln1_g, l1_ln1_b)
    y_p, y_s = _moe_ln(y_p, y_s, mods, l1_moe_router_w, l1_moe_router_b, moe_w1.reshape(N_EXPERTS, D, E_FF),
                       moe_w3.reshape(N_EXPERTS, D, E_FF), moe_w2.reshape(N_EXPERTS, E_FF, D),
                       l1_ln2_g, l1_ln2_b)
    new_k = jnp.transpose(k_heads, (0, 3, 1, 2))
    new_v = jnp.transpose(v_heads, (0, 3, 1, 2))

    return (y_p.reshape(BATCH, SEQ, D), y_s.reshape(DEC_BATCH, DEC_SEQ, D),
            new_ckv, new_krope, new_sk, new_sv, new_k, new_v)
```
